```python
import jax, jax.numpy as jnp
from jax import lax
import numpy as np

D_MODEL = 2048
BATCH = 4
SEQ = 2048
DEPTH = 1

CHUNK = 64
N_PAST_CHUNKS = 8
BAND = (N_PAST_CHUNKS + 1) * CHUNK
ATTN_WIDTH = D_MODEL // 2
ATTN_HEAD_DIM = 64
ATTN_HEADS = ATTN_WIDTH // ATTN_HEAD_DIM
MAX_REL = 4 * CHUNK
REC_WIDTH = D_MODEL - ATTN_WIDTH
REC_HEAD_DIM = 128
REC_HEADS = REC_WIDTH // REC_HEAD_DIM
MIX_WIDTH = ATTN_WIDTH + REC_WIDTH
IN_PROJ_WIDTH = 3 * ATTN_WIDTH + 4 * REC_WIDTH
D_FF = ((8 * D_MODEL // 3 + 255) // 256) * 256
ALPHA = (2 * DEPTH) ** 0.25
BETA = (8 * DEPTH) ** -0.25
EPS = 1e-5
N_MOD = 6

kernel_name = "hybrid_chunkattn_hgrn2_deepnorm_adaln"


def _layernorm(x, g=None, b=None):
    xf = x.astype(jnp.float32)
    mu = jnp.mean(xf, axis=-1, keepdims=True)
    var = jnp.mean(jnp.square(xf - mu), axis=-1, keepdims=True)
    y = (xf - mu) * lax.rsqrt(var + EPS)
    if g is not None:
        y = y * g.astype(jnp.float32) + b.astype(jnp.float32)
    return y.astype(x.dtype)


def _rmsnorm(x, g):
    xf = x.astype(jnp.float32)
    y = xf * lax.rsqrt(jnp.mean(jnp.square(xf), axis=-1, keepdims=True) + EPS)
    return y * g.astype(jnp.float32)


def _chunk_attention(q, k, v, rel_bias):
    B, T, H, Dh = q.shape
    n_chunks = T // CHUNK
    pad = N_PAST_CHUNKS * CHUNK
    k_pad = jnp.pad(k, ((0, 0), (pad, 0), (0, 0), (0, 0)))
    v_pad = jnp.pad(v, ((0, 0), (pad, 0), (0, 0), (0, 0)))
    rel = jnp.arange(CHUNK)[:, None] + pad - jnp.arange(BAND)[None, :]
    idx = jnp.clip(rel, -MAX_REL, MAX_REL) + MAX_REL
    bias = rel_bias[:, idx].astype(jnp.float32)
    q_chunks = q.reshape(B, n_chunks, CHUNK, H, Dh).transpose(1, 0, 2, 3, 4)
    scale = Dh ** -0.5
    band_pos = jnp.arange(BAND)

    def one_chunk(args):
        n, qc = args
        kb = lax.dynamic_slice_in_dim(k_pad, n * CHUNK, BAND, axis=1)
        vb = lax.dynamic_slice_in_dim(v_pad, n * CHUNK, BAND, axis=1)
        s = jnp.einsum('bthd,bjhd->bhtj', qc, kb).astype(jnp.float32) * scale + bias
        valid = band_pos >= (N_PAST_CHUNKS - n) * CHUNK
        s = jnp.where(valid, s, -jnp.inf)
        p = jax.nn.softmax(s, axis=-1).astype(vb.dtype)
        return jnp.einsum('bhtj,bjhd->bthd', p, vb)

    out = lax.map(one_chunk, (jnp.arange(n_chunks), q_chunks))
    return out.transpose(1, 0, 2, 3, 4).reshape(B, T, H, Dh)


def _hgrn2(q, f_logit, i, lower_bound):
    B, T, H, Dk = q.shape
    lb = lower_bound.reshape(H, Dk).astype(jnp.float32)
    f = lb + (1.0 - lb) * jax.nn.sigmoid(f_logit.astype(jnp.float32))
    log_f = jnp.log(f)
    k = 1.0 - f
    q = jax.nn.silu(q.astype(jnp.float32))
    i = i.astype(jnp.float32)
    Dv = i.shape[-1]
    n_chunks = T // CHUNK

    def to_chunks(a):
        return a.reshape(B, n_chunks, CHUNK, H, a.shape[-1]).transpose(1, 0, 3, 2, 4)

    causal = jnp.tril(jnp.ones((CHUNK, CHUNK), dtype=bool))[:, :, None]

    def step(S, inp):
        qc, kc, ic, gc = inp
        b = jnp.cumsum(gc, axis=2)
        diff = b[:, :, :, None, :] - b[:, :, None, :, :]
        decay = jnp.exp(jnp.where(causal, diff, -jnp.inf))
        scores = jnp.einsum('bhtd,bhtsd,bhsd->bhts', qc, decay, kc)
        o = (jnp.einsum('bhts,bhse->bhte', scores, ic)
             + jnp.einsum('bhtd,bhde->bhte', qc * jnp.exp(b), S))
        b_last = b[:, :, -1:, :]
        S = (jnp.exp(b_last[:, :, 0, :, None]) * S
             + jnp.einsum('bhsd,bhse->bhde', kc * jnp.exp(b_last - b), ic))
        return S, o

    S0 = jnp.zeros((B, H, Dk, Dv), jnp.float32)
    _, o = lax.scan(step, S0, (to_chunks(q), to_chunks(k), to_chunks(i), to_chunks(log_f)))
    return o.transpose(1, 0, 3, 2, 4).reshape(B, T, H, Dv)


def _token_mixer(h, w_in, rel_bias, attn_gain, lower_bound, gnorm_gain, w_o):
    B, T, _ = h.shape
    proj = h @ w_in
    splits = [ATTN_WIDTH, 2 * ATTN_WIDTH, 3 * ATTN_WIDTH,
              3 * ATTN_WIDTH + REC_WIDTH, 3 * ATTN_WIDTH + 2 * REC_WIDTH,
              3 * ATTN_WIDTH + 3 * REC_WIDTH]
    q_a, k_a, v_a, q_b, f_b, i_b, g_b = jnp.split(proj, splits, axis=-1)
    heads_a = lambda a: a.reshape(B, T, ATTN_HEADS, ATTN_HEAD_DIM)
    heads_b = lambda a: a.reshape(B, T, REC_HEADS, REC_HEAD_DIM)
    o_a = _chunk_attention(heads_a(q_a), heads_a(k_a), heads_a(v_a), rel_bias)
    o_a = _rmsnorm(o_a, attn_gain.reshape(ATTN_HEADS, ATTN_HEAD_DIM)).reshape(B, T, ATTN_WIDTH)
    o_b = _hgrn2(heads_b(q_b), heads_b(f_b), heads_b(i_b), lower_bound)
    o_b = _rmsnorm(o_b, gnorm_gain).reshape(B, T, REC_WIDTH)
    o_b = o_b * jax.nn.silu(g_b.astype(jnp.float32))
    out = jnp.concatenate([o_a, o_b], axis=-1).astype(h.dtype)
    return out @ w_o


def _swiglu(h, w_ffn_in, w_ffn_out):
    gate, up = jnp.split(h @ w_ffn_in, 2, axis=-1)
    return (jax.nn.silu(gate) * up) @ w_ffn_out


def setup_inputs(seed: int = 0) -> dict:
    key = jax.random.key(seed)
    ks = jax.random.split(key, 20)
    f32 = jnp.float32
    nrm = lambda k, shape, s: jax.random.normal(k, shape, f32) * s
    return {
        "x": nrm(ks[0], (BATCH, SEQ, D_MODEL), 1.0),
        "c": nrm(ks[1], (BATCH, D_MODEL), 1.0),
        "w_ada": nrm(ks[2], (DEPTH, D_MODEL, N_MOD * D_MODEL), 0.5 * D_MODEL ** -0.5),
        "b_ada": nrm(ks[3], (DEPTH, N_MOD * D_MODEL), 0.01),
        "w_in": nrm(ks[4], (DEPTH, D_MODEL, IN_PROJ_WIDTH), D_MODEL ** -0.5),
        "rel_bias": nrm(ks[5], (DEPTH, ATTN_HEADS, 2 * MAX_REL + 1), 0.1),
        "attn_norm_g": 1.0 + nrm(ks[6], (DEPTH, ATTN_WIDTH), 0.02),
        "lb_logits": nrm(ks[7], (DEPTH + 1, REC_WIDTH), 0.1),
        "gnorm_g": 1.0 + nrm(ks[8], (DEPTH, REC_HEAD_DIM), 0.02),
        "w_o": nrm(ks[9], (DEPTH, MIX_WIDTH, D_MODEL), BETA * MIX_WIDTH ** -0.5),
        "ln1_g": 1.0 + nrm(ks[10], (DEPTH, D_MODEL), 0.02),
        "ln1_b": nrm(ks[11], (DEPTH, D_MODEL), 0.01),
        "w_ffn_in": nrm(ks[12], (DEPTH, D_MODEL, 2 * D_FF), D_MODEL ** -0.5),
        "w_ffn_out": nrm(ks[13], (DEPTH, D_FF, D_MODEL), BETA * D_FF ** -0.5),
        "ln2_g": 1.0 + nrm(ks[14], (DEPTH, D_MODEL), 0.02),
        "ln2_b": nrm(ks[15], (DEPTH, D_MODEL), 0.01),
    }


def reference(x, c, w_ada, b_ada, w_in, rel_bias, attn_norm_g, lb_logits, gnorm_g, w_o,
              ln1_g, ln1_b, w_ffn_in, w_ffn_out, ln2_g, ln2_b):
    lower_bounds = jnp.cumsum(jax.nn.softmax(lb_logits.astype(jnp.float32), axis=0), axis=0)
    c_act = jax.nn.silu(c)
    for layer in range(DEPTH):
        mod = c_act @ w_ada[layer] + b_ada[layer]
        shift1, scale1, gate1, shift2, scale2, gate2 = [m[:, None, :] for m in jnp.split(mod, N_MOD, axis=-1)]
        h = _layernorm(x) * (1.0 + scale1) + shift1
        mix = _token_mixer(h, w_in[layer], rel_bias[layer], attn_norm_g[layer],
                           lower_bounds[layer], gnorm_g[layer], w_o[layer])
        x = _layernorm(ALPHA * x + gate1 * mix, ln1_g[layer], ln1_b[layer])
        h = _layernorm(x) * (1.0 + scale2) + shift2
        x = _layernorm(ALPHA * x + gate2 * _swiglu(h, w_ffn_in[layer], w_ffn_out[layer]),
                       ln2_g[layer], ln2_b[layer])
    return x
```

```python
import functools

import jax
import jax.numpy as jnp
from jax import lax
from jax.experimental import pallas as pl
from jax.experimental.pallas import tpu as pltpu

F32 = jnp.float32
BF16 = jnp.bfloat16

CHUNK = 64
N_PAST_CHUNKS = 8
BAND = (N_PAST_CHUNKS + 1) * CHUNK
ATTN_HEAD_DIM = 64
REC_HEAD_DIM = 128
N_MOD = 6
EPS = 1e-5
LANES = 128
SUBLANES = 8
BIAS_W = BAND + CHUNK

MIB = 1024 * 1024


def _cparams(sem, vmem_mib):
    return pltpu.CompilerParams(dimension_semantics=sem, vmem_limit_bytes=vmem_mib * MIB)


def _sigmoid(x):
    return 1.0 / (1.0 + jnp.exp(-x))


def _silu(x):
    return x * _sigmoid(x)


def _ln_rows(x):
    mu = jnp.mean(x, axis=-1, keepdims=True)
    xc = x - mu
    var = jnp.mean(xc * xc, axis=-1, keepdims=True)
    return xc * lax.rsqrt(var + EPS)


def _mod_kernel(ct_ref, w_ref, b_ref, o_ref):
    ct = ct_ref[...]
    cat = _silu(ct)
    w = w_ref[...]
    rows = [jnp.sum(w * cat[:, b:b + 1], axis=0, keepdims=True) for b in range(ct.shape[1])]
    o_ref[...] = jnp.concatenate(rows, axis=0) + b_ref[...]


def _mod(c, w_ada, b_ada, tn=512):
    B, D = c.shape
    N = w_ada.shape[1]
    return pl.pallas_call(
        _mod_kernel,
        out_shape=jax.ShapeDtypeStruct((B, N), F32),
        grid=(N // tn,),
        in_specs=[pl.BlockSpec((D, B), lambda j: (0, 0)),
                  pl.BlockSpec((D, tn), lambda j: (0, j)),
                  pl.BlockSpec((1, tn), lambda j: (0, j))],
        out_specs=pl.BlockSpec((B, tn), lambda j: (0, j)),
        compiler_params=_cparams(("parallel",), 32),
        name="adaln_mod",
    )(c.T, w_ada, b_ada.reshape(1, N))


def _ln_mod_kernel(x_ref, mod_ref, o_ref, *, shift_row):
    y = _ln_rows(x_ref[0])
    shift = mod_ref[0, shift_row:shift_row + 1, :]
    scale = mod_ref[0, shift_row + 1:shift_row + 2, :]
    o_ref[0] = (y * (1.0 + scale) + shift).astype(o_ref.dtype)


def _ln_mod(x, mod3, shift_row, tm=512):
    B, T, D = x.shape
    return pl.pallas_call(
        functools.partial(_ln_mod_kernel, shift_row=shift_row),
        out_shape=jax.ShapeDtypeStruct((B, T, D), BF16),
        grid=(B, T // tm),
        in_specs=[pl.BlockSpec((1, tm, D), lambda b, i: (b, i, 0)),
                  pl.BlockSpec((1, N_MOD, D), lambda b, i: (b, 0, 0))],
        out_specs=pl.BlockSpec((1, tm, D), lambda b, i: (b, i, 0)),
        compiler_params=_cparams(("parallel", "parallel"), 32),
        name="ln_modulate",
    )(x, mod3)


def _matmul_kernel(a_ref, w_ref, o_ref):
    o_ref[...] = jnp.dot(a_ref[...], w_ref[...], preferred_element_type=F32).astype(o_ref.dtype)


def _matmul(a, w, col_block0, n_out, out_dtype, tm=1024, tn=1024):
    M, K = a.shape
    return pl.pallas_call(
        _matmul_kernel,
        out_shape=jax.ShapeDtypeStruct((M, n_out), out_dtype),
        grid=(M // tm, n_out // tn),
        in_specs=[pl.BlockSpec((tm, K), lambda i, j: (i, 0)),
                  pl.BlockSpec((K, tn), lambda i, j: (0, j + col_block0))],
        out_specs=pl.BlockSpec((tm, tn), lambda i, j: (i, j)),
        compiler_params=_cparams(("parallel", "arbitrary"), 40),
        name="in_proj",
    )(a, w)


def _attn_kernel(q_ref, k_ref, v_ref, bias_ref, gain_ref, o_ref, kpad, vpad, *, n_chunks):
    pad = N_PAST_CHUNKS * CHUNK
    T = n_chunks * CHUNK
    zeros = jnp.zeros((pad, LANES), BF16)
    kpad[0:pad, :] = zeros
    vpad[0:pad, :] = zeros
    kpad[pad:pad + T, :] = k_ref[0]
    vpad[pad:pad + T, :] = v_ref[0]

    lane = lax.broadcasted_iota(jnp.int32, (CHUNK, LANES), 1)
    head0 = lane < ATTN_HEAD_DIM
    biases = []
    for hh in range(2):
        g = jnp.broadcast_to(bias_ref[0, hh:hh + 1, :], (CHUNK, BIAS_W))
        biases.append(pltpu.roll(g, BIAS_W - CHUNK, 1, stride=1, stride_axis=0)[:, :BAND])
    band_pos = lax.broadcasted_iota(jnp.int32, (CHUNK, BAND), 1)
    gain = gain_ref[0]
    scale = ATTN_HEAD_DIM ** -0.5

    def chunk_body(n, carry):
        row0 = pl.multiple_of(n * CHUNK, CHUNK)
        q = q_ref[0, pl.ds(row0, CHUNK), :] * scale
        kb = kpad[pl.ds(row0, BAND), :]
        vb = vpad[pl.ds(row0, BAND), :]
        valid = band_pos >= (N_PAST_CHUNKS - n) * CHUNK
        outs = []
        for hh in range(2):
            qh = jnp.where(head0 if hh == 0 else ~head0, q, jnp.zeros_like(q))
            s = lax.dot_general(qh, kb, (((1,), (1,)), ((), ())), preferred_element_type=F32)
            s = jnp.where(valid, s + biases[hh], -jnp.inf)
            m = jnp.max(s, axis=-1, keepdims=True)
            p = jnp.exp(s - m)
            l = jnp.sum(p, axis=-1, keepdims=True)
            pv = jnp.dot(p.astype(BF16), vb, preferred_element_type=F32)
            outs.append(pv / l)
        o = jnp.where(head0, outs[0], outs[1])
        o2 = o * o
        ms0 = jnp.sum(jnp.where(head0, o2, 0.0), axis=-1, keepdims=True) / ATTN_HEAD_DIM
        ms1 = jnp.sum(jnp.where(head0, 0.0, o2), axis=-1, keepdims=True) / ATTN_HEAD_DIM
        y = o * lax.rsqrt(jnp.where(head0, ms0, ms1) + EPS) * gain
        o_ref[0, pl.ds(row0, CHUNK), :] = y.astype(o_ref.dtype)
        return carry

    lax.fori_loop(0, n_chunks, chunk_body, 0)


def _attention(proj_a, bias_vec, attn_gain):
    B, T, W3 = proj_a.shape
    W = W3 // 3
    n_pairs = W // LANES
    return pl.pallas_call(
        functools.partial(_attn_kernel, n_chunks=T // CHUNK),
        out_shape=jax.ShapeDtypeStruct((B, T, W), BF16),
        grid=(B, n_pairs),
        in_specs=[pl.BlockSpec((1, T, LANES), lambda b, h: (b, 0, h)),
                  pl.BlockSpec((1, T, LANES), lambda b, h: (b, 0, n_pairs + h)),
                  pl.BlockSpec((1, T, LANES), lambda b, h: (b, 0, 2 * n_pairs + h)),
                  pl.BlockSpec((1, 2, BIAS_W), lambda b, h: (h, 0, 0)),
                  pl.BlockSpec((1, 1, LANES), lambda b, h: (h, 0, 0))],
        out_specs=pl.BlockSpec((1, T, LANES), lambda b, h: (b, 0, h)),
        scratch_shapes=[pltpu.VMEM((T + N_PAST_CHUNKS * CHUNK, LANES), BF16),
                        pltpu.VMEM((T + N_PAST_CHUNKS * CHUNK, LANES), BF16)],
        compiler_params=_cparams(("parallel", "parallel"), 32),
        name="chunk_attention",
    )(proj_a, proj_a, proj_a, bias_vec, attn_gain)


HG_LEVELS = (32, 16, 8)
HG_DIAG = 8


def _split3(x):
    hi = x.astype(BF16)
    r1 = x - hi.astype(F32)
    mid = r1.astype(BF16)
    lo = (r1 - mid.astype(F32)).astype(BF16)
    return hi, mid, lo


def _hgrn_kernel(q_ref, f_ref, i_ref, g_ref, lbl_ref, gn_ref, o_ref, *, n_chunks):
    C = CHUNK
    Dk = REC_HEAD_DIM
    lbl = lbl_ref[0]
    e = jnp.exp(lbl - jnp.max(lbl, axis=0, keepdims=True))
    lb = e[0:1, :] / jnp.sum(e, axis=0, keepdims=True)
    gn = gn_ref[...]

    r = lax.broadcasted_iota(jnp.int32, (C, C), 0)
    s = lax.broadcasted_iota(jnp.int32, (C, C), 1)
    sel_parts = [s <= r]
    for m in HG_LEVELS:
        sel_parts.append(s <= (r // (2 * m)) * (2 * m) + (m - 1))
    sel_parts.append(s <= C - 1)
    sel = jnp.concatenate([p.astype(BF16) for p in sel_parts], axis=0)

    row = lax.broadcasted_iota(jnp.int32, (C, Dk), 0)
    uppers = [((row // m) % 2) == 1 for m in HG_LEVELS]
    same_blk = [(r // (2 * m)) == (s // (2 * m)) for m in HG_LEVELS]
    sub = lax.broadcasted_iota(jnp.int32, (C // SUBLANES, SUBLANES, Dk), 1)

    def roll8(x, d):
        return pltpu.roll(x.reshape(C // SUBLANES, SUBLANES, Dk), d, 1)

    def chunk_body(n, st):
        row0 = pl.multiple_of(n * C, C)
        rows = pl.ds(row0, C)
        f = lb + (1.0 - lb) * _sigmoid(f_ref[0, rows, :])
        logf = jnp.log(f)
        kk = 1.0 - f
        qq = _silu(q_ref[0, rows, :])
        ii = i_ref[0, rows, :]
        ii_bf = ii.astype(BF16)

        hi, mid, lo = _split3(logf)
        bb = jnp.dot(sel, jnp.concatenate([hi, mid, lo], axis=1), preferred_element_type=F32)
        bb = bb[:, 0:Dk] + bb[:, Dk:2 * Dk] + bb[:, 2 * Dk:3 * Dk]
        b = bb[0:C]
        b_last = bb[(len(HG_LEVELS) + 1) * C:(len(HG_LEVELS) + 2) * C]

        a = jnp.zeros((C, C), F32)
        for li, m in enumerate(HG_LEVELS):
            d = b - bb[(li + 1) * C:(li + 2) * C]
            up = uppers[li]
            ex = jnp.exp(jnp.where(up, d, -d))
            ql = jnp.where(up, qq * ex, 0.0).astype(BF16)
            kl = jnp.where(up, 0.0, kk * ex).astype(BF16)
            al = lax.dot_general(ql, kl, (((1,), (1,)), ((), ())), preferred_element_type=F32)
            a = a + jnp.where(same_blk[li], al, 0.0)
        o = jnp.dot(a.astype(BF16), ii_bf, preferred_element_type=F32)

        k3 = kk.reshape(C // SUBLANES, SUBLANES, Dk)
        b3 = b.reshape(C // SUBLANES, SUBLANES, Dk)
        q3 = qq.reshape(C // SUBLANES, SUBLANES, Dk)
        i3 = ii.reshape(C // SUBLANES, SUBLANES, Dk)
        o3 = jnp.sum(q3 * k3, axis=-1, keepdims=True) * i3
        for dlt in range(1, HG_DIAG):
            w = q3 * roll8(kk, dlt) * jnp.exp(b3 - roll8(b, dlt))
            w = jnp.where(sub >= dlt, w, 0.0)
            o3 = o3 + jnp.sum(w, axis=-1, keepdims=True) * roll8(ii, dlt)
        o = o + o3.reshape(C, Dk)

        qe = (qq * jnp.exp(b)).astype(BF16)
        o = o + lax.dot_general(qe, st.astype(BF16), (((1,), (1,)), ((), ())), preferred_element_type=F32)
        ke = (kk * jnp.exp(b_last - b)).astype(BF16)
        st = st * jnp.exp(b_last[0:1, :]) + lax.dot_general(
            ii_bf, ke, (((0,), (0,)), ((), ())), preferred_element_type=F32)

        ms = jnp.mean(o * o, axis=-1, keepdims=True)
        y = o * lax.rsqrt(ms + EPS) * gn
        y = y * _silu(g_ref[0, rows, :])
        o_ref[0, rows, :] = y.astype(o_ref.dtype)
        return st

    lax.fori_loop(0, n_chunks, chunk_body, jnp.zeros((Dk, Dk), F32))


def _hgrn(proj_b, lb_logits_h, gnorm_g):
    B, T, W4 = proj_b.shape
    W = W4 // 4
    H = W // REC_HEAD_DIM
    n_slots = lb_logits_h.shape[1]
    blk = lambda off: pl.BlockSpec((1, T, REC_HEAD_DIM), lambda b, h, off=off: (b, 0, off * H + h))
    return pl.pallas_call(
        functools.partial(_hgrn_kernel, n_chunks=T // CHUNK),
        out_shape=jax.ShapeDtypeStruct((B, T, W), BF16),
        grid=(B, H),
        in_specs=[blk(0), blk(1), blk(2), blk(3),
                  pl.BlockSpec((1, n_slots, REC_HEAD_DIM), lambda b, h: (h, 0, 0)),
                  pl.BlockSpec((1, REC_HEAD_DIM), lambda b, h: (0, 0))],
        out_specs=pl.BlockSpec((1, T, REC_HEAD_DIM), lambda b, h: (b, 0, h)),
        compiler_params=_cparams(("parallel", "parallel"), 32),
        name="hgrn2",
    )(proj_b, proj_b, proj_b, proj_b, lb_logits_h, gnorm_g)


def _outproj_kernel(oa_ref, ob_ref, w_ref, x_ref, mod_ref, g_ref, b_ref, x1_ref, h2_ref, *, alpha):
    wa = oa_ref.shape[-1]
    mix = jnp.dot(oa_ref[0], w_ref[0:wa, :], preferred_element_type=F32)
    mix = mix + jnp.dot(ob_ref[0], w_ref[wa:, :], preferred_element_type=F32)
    gate1 = mod_ref[0, 2:3, :]
    x1 = _ln_rows(alpha * x_ref[0] + gate1 * mix) * g_ref[...] + b_ref[...]
    x1_ref[0] = x1
    shift2 = mod_ref[0, 3:4, :]
    scale2 = mod_ref[0, 4:5, :]
    h2_ref[0] = (_ln_rows(x1) * (1.0 + scale2) + shift2).astype(h2_ref.dtype)


def _outproj(o_a, o_b, w_o, x, mod3, ln_g, ln_b, alpha, tm=256):
    B, T, D = x.shape
    Wa, Wb = o_a.shape[-1], o_b.shape[-1]
    return pl.pallas_call(
        functools.partial(_outproj_kernel, alpha=alpha),
        out_shape=(jax.ShapeDtypeStruct((B, T, D), F32), jax.ShapeDtypeStruct((B, T, D), BF16)),
        grid=(B, T // tm),
        in_specs=[pl.BlockSpec((1, tm, Wa), lambda b, i: (b, i, 0)),
                  pl.BlockSpec((1, tm, Wb), lambda b, i: (b, i, 0)),
                  pl.BlockSpec((Wa + Wb, D), lambda b, i: (0, 0)),
                  pl.BlockSpec((1, tm, D), lambda b, i: (b, i, 0)),
                  pl.BlockSpec((1, N_MOD, D), lambda b, i: (b, 0, 0)),
                  pl.BlockSpec((1, D), lambda b, i: (0, 0)),
                  pl.BlockSpec((1, D), lambda b, i: (0, 0))],
        out_specs=(pl.BlockSpec((1, tm, D), lambda b, i: (b, i, 0)),
                   pl.BlockSpec((1, tm, D), lambda b, i: (b, i, 0))),
        compiler_params=_cparams(("parallel", "parallel"), 48),
        name="out_proj_ln1",
    )(o_a, o_b, w_o, x, mod3, ln_g, ln_b)


def _ffn_kernel(h_ref, wg_ref, wu_ref, wo_ref, x_ref, mod_ref, g_ref, b_ref, o_ref, *, alpha):
    f = pl.program_id(2)
    h = h_ref[0]
    gate = jnp.dot(h, wg_ref[...], preferred_element_type=F32)
    up = jnp.dot(h, wu_ref[...], preferred_element_type=F32)
    act = (_silu(gate) * up).astype(BF16)
    part = jnp.dot(act, wo_ref[...], preferred_element_type=F32)

    @pl.when(f == 0)
    def _():
        o_ref[0] = part

    @pl.when(f > 0)
    def _():
        o_ref[0] += part

    @pl.when(f == pl.num_programs(2) - 1)
    def _():
        gate2 = mod_ref[0, 5:6, :]
        o_ref[0] = _ln_rows(alpha * x_ref[0] + gate2 * o_ref[0]) * g_ref[...] + b_ref[...]


def _ffn(h2, w_in, w_out, x1, mod3, ln_g, ln_b, alpha, tm=512, tf=512):
    B, T, D = x1.shape
    F = w_out.shape[0]
    nf = F // tf
    return pl.pallas_call(
        functools.partial(_ffn_kernel, alpha=alpha),
        out_shape=jax.ShapeDtypeStruct((B, T, D), F32),
        grid=(B, T // tm, nf),
        in_specs=[pl.BlockSpec((1, tm, D), lambda b, i, f: (b, i, 0)),
                  pl.BlockSpec((D, tf), lambda b, i, f: (0, f)),
                  pl.BlockSpec((D, tf), lambda b, i, f: (0, nf + f)),
                  pl.BlockSpec((tf, D), lambda b, i, f: (f, 0)),
                  pl.BlockSpec((1, tm, D), lambda b, i, f: (b, i, 0)),
                  pl.BlockSpec((1, N_MOD, D), lambda b, i, f: (b, 0, 0)),
                  pl.BlockSpec((1, D), lambda b, i, f: (0, 0)),
                  pl.BlockSpec((1, D), lambda b, i, f: (0, 0))],
        out_specs=pl.BlockSpec((1, tm, D), lambda b, i, f: (b, i, 0)),
        compiler_params=_cparams(("parallel", "parallel", "arbitrary"), 48),
        name="swiglu_ffn_ln2",
    )(h2, w_in, w_in, w_out, x1, mod3, ln_g, ln_b)


def _bias_vectors(rel_bias):
    H, n_rel = rel_bias.shape
    max_rel = (n_rel - 1) // 2
    u = jnp.arange(BIAS_W)
    idx = jnp.clip(BAND - u, -max_rel, max_rel) + max_rel
    return rel_bias[:, idx]


def kernel(x, c, w_ada, b_ada, w_in, rel_bias, attn_norm_g, lb_logits, gnorm_g, w_o,
           ln1_g, ln1_b, w_ffn_in, w_ffn_out, ln2_g, ln2_b):
    B, T, D = x.shape
    depth = w_ada.shape[0]
    alpha = (2 * depth) ** 0.25
    attn_w = attn_norm_g.shape[1]
    rec_w = lb_logits.shape[1]
    n_slots = lb_logits.shape[0]
    rec_heads = rec_w // REC_HEAD_DIM
    assert depth == 1 and n_slots == depth + 1
    for layer in range(depth):
        mod3 = _mod(c, w_ada[layer], b_ada[layer]).reshape(B, N_MOD, D)
        h1 = _ln_mod(x, mod3, 0).reshape(B * T, D)
        w_in_bf = w_in[layer].astype(BF16)
        proj_a = _matmul(h1, w_in_bf, 0, 3 * attn_w, BF16).reshape(B, T, 3 * attn_w)
        proj_b = _matmul(h1, w_in_bf, 3 * attn_w // 1024, 4 * rec_w, F32).reshape(B, T, 4 * rec_w)
        bias_vec = _bias_vectors(rel_bias[layer]).reshape(-1, 2, BIAS_W)
        o_a = _attention(proj_a, bias_vec, attn_norm_g[layer].reshape(-1, 1, LANES))
        lbl = lb_logits.reshape(n_slots, rec_heads, REC_HEAD_DIM).transpose(1, 0, 2)
        o_b = _hgrn(proj_b, lbl, gnorm_g[layer].reshape(1, REC_HEAD_DIM))
        x, h2 = _outproj(o_a, o_b, w_o[layer].astype(BF16), x, mod3,
                         ln1_g[layer].reshape(1, D), ln1_b[layer].reshape(1, D), alpha)
        x = _ffn(h2, w_ffn_in[layer].astype(BF16), w_ffn_out[layer].astype(BF16), x, mod3,
                 ln2_g[layer].reshape(1, D), ln2_b[layer].reshape(1, D), alpha)
    return x
```

```python
import functools

import jax
import jax.numpy as jnp
from jax import lax
from jax.experimental import pallas as pl
from jax.experimental.pallas import tpu as pltpu

F32 = jnp.float32
BF16 = jnp.bfloat16

CHUNK = 64
N_PAST_CHUNKS = 8
BAND = (N_PAST_CHUNKS + 1) * CHUNK
ATTN_HEAD_DIM = 64
REC_HEAD_DIM = 128
N_MOD = 6
EPS = 1e-5
LANES = 128
SUBLANES = 8
QBLK = 2 * CHUNK
KBLK = BAND + CHUNK
BIAS_W = KBLK + QBLK
N_PAD_STEPS = N_PAST_CHUNKS * CHUNK // QBLK

MIB = 1024 * 1024


def _cparams(sem, vmem_mib):
    return pltpu.CompilerParams(dimension_semantics=sem, vmem_limit_bytes=vmem_mib * MIB)


def _sigmoid(x):
    return 1.0 / (1.0 + jnp.exp(-x))


def _silu(x):
    return x * _sigmoid(x)


def _ln_rows(x):
    mu = jnp.mean(x, axis=-1, keepdims=True)
    xc = x - mu
    var = jnp.mean(xc * xc, axis=-1, keepdims=True)
    return xc * lax.rsqrt(var + EPS)


def _mod_kernel(ct_ref, w_ref, b_ref, o_ref):
    ct = ct_ref[...]
    cat = _silu(ct)
    w = w_ref[...]
    rows = [jnp.sum(w * cat[:, b:b + 1], axis=0, keepdims=True) for b in range(ct.shape[1])]
    o_ref[...] = jnp.concatenate(rows, axis=0) + b_ref[...]


def _mod(c, w_ada, b_ada, tn=512):
    B, D = c.shape
    N = w_ada.shape[1]
    return pl.pallas_call(
        _mod_kernel,
        out_shape=jax.ShapeDtypeStruct((B, N), F32),
        grid=(N // tn,),
        in_specs=[pl.BlockSpec((D, B), lambda j: (0, 0)),
                  pl.BlockSpec((D, tn), lambda j: (0, j)),
                  pl.BlockSpec((1, tn), lambda j: (0, j))],
        out_specs=pl.BlockSpec((B, tn), lambda j: (0, j)),
        compiler_params=_cparams(("parallel",), 32),
        name="adaln_mod",
    )(c.T, w_ada, b_ada.reshape(1, N))


def _ln_mod_kernel(x_ref, mod_ref, o_ref, *, shift_row):
    y = _ln_rows(x_ref[0])
    shift = mod_ref[0, shift_row:shift_row + 1, :]
    scale = mod_ref[0, shift_row + 1:shift_row + 2, :]
    o_ref[0] = (y * (1.0 + scale) + shift).astype(o_ref.dtype)


def _ln_mod(x, mod3, shift_row, tm=512):
    B, T, D = x.shape
    return pl.pallas_call(
        functools.partial(_ln_mod_kernel, shift_row=shift_row),
        out_shape=jax.ShapeDtypeStruct((B, T, D), BF16),
        grid=(B, T // tm),
        in_specs=[pl.BlockSpec((1, tm, D), lambda b, i: (b, i, 0)),
                  pl.BlockSpec((1, N_MOD, D), lambda b, i: (b, 0, 0))],
        out_specs=pl.BlockSpec((1, tm, D), lambda b, i: (b, i, 0)),
        compiler_params=_cparams(("parallel", "parallel"), 32),
        name="ln_modulate",
    )(x, mod3)


def _matmul_kernel(a_ref, w_ref, o_ref):
    o_ref[...] = jnp.dot(a_ref[...], w_ref[...], preferred_element_type=F32).astype(o_ref.dtype)


def _matmul(a, w, col_block0, n_out, out_dtype, tm=1024, tn=1024):
    M, K = a.shape
    return pl.pallas_call(
        _matmul_kernel,
        out_shape=jax.ShapeDtypeStruct((M, n_out), out_dtype),
        grid=(M // tm, n_out // tn),
        in_specs=[pl.BlockSpec((tm, K), lambda i, j: (i, 0)),
                  pl.BlockSpec((K, tn), lambda i, j: (0, j + col_block0))],
        out_specs=pl.BlockSpec((tm, tn), lambda i, j: (i, j)),
        compiler_params=_cparams(("parallel", "arbitrary"), 40),
        name="in_proj",
    )(a, w)


def _attn_kernel(q_ref, k_ref, v_ref, bias_ref, gain_ref, o_ref, kpad, vpad, tab, s_a, s_b, *, n_chunks):
    pad = N_PAST_CHUNKS * CHUNK
    T = n_chunks * CHUNK
    n_steps = T // QBLK
    zeros = jnp.zeros((pad, LANES), BF16)
    kpad[0:pad, :] = zeros
    vpad[0:pad, :] = zeros
    kpad[pad:pad + T, :] = k_ref[0]
    vpad[pad:pad + T, :] = v_ref[0]

    col = lax.broadcasted_iota(jnp.int32, (QBLK, KBLK), 1)
    rowi = lax.broadcasted_iota(jnp.int32, (QBLK, KBLK), 0)
    in_band = ((rowi < CHUNK) & (col < BAND)) | ((rowi >= CHUNK) & (col >= CHUNK))
    for hh in range(2):
        g = jnp.broadcast_to(bias_ref[0, hh:hh + 1, :], (QBLK, BIAS_W))
        t = pltpu.roll(g, BIAS_W - QBLK, 1, stride=1, stride_axis=0)[:, :KBLK]
        t = jnp.where(in_band, t, -jnp.inf)
        for v in range(N_PAD_STEPS + 1):
            first_real = pad - v * QBLK
            tv = jnp.where(col >= first_real, t, -jnp.inf) if first_real > 0 else t
            tab[v, hh * QBLK:(hh + 1) * QBLK, :] = tv

    lane = lax.broadcasted_iota(jnp.int32, (QBLK, LANES), 1)
    head0 = lane < ATTN_HEAD_DIM
    gain = gain_ref[0]
    scale = ATTN_HEAD_DIM ** -0.5
    nt = (((1,), (1,)), ((), ()))

    def scores(m, dst):
        row0 = pl.multiple_of(m * QBLK, QBLK)
        q = q_ref[0, pl.ds(row0, QBLK), :] * scale
        zq = jnp.zeros_like(q)
        qs = jnp.concatenate([jnp.where(head0, q, zq), jnp.where(head0, zq, q)], axis=0)
        s = lax.dot_general(qs, kpad[pl.ds(row0, KBLK), :], nt, preferred_element_type=F32)
        dst[...] = s + tab[jnp.minimum(m, N_PAD_STEPS)]

    def finish(m, src):
        row0 = pl.multiple_of(m * QBLK, QBLK)
        s = src[...]
        p = jnp.exp(s - jnp.max(s, axis=-1, keepdims=True))
        l = jnp.sum(p, axis=-1, keepdims=True)
        pv = jnp.dot(p.astype(BF16), vpad[pl.ds(row0, KBLK), :], preferred_element_type=F32)
        pv = pv * (1.0 / l)
        o = jnp.where(head0, pv[0:QBLK], pv[QBLK:2 * QBLK])
        o2 = o * o
        ms0 = jnp.sum(jnp.where(head0, o2, 0.0), axis=-1, keepdims=True) / ATTN_HEAD_DIM
        ms1 = jnp.sum(jnp.where(head0, 0.0, o2), axis=-1, keepdims=True) / ATTN_HEAD_DIM
        y = o * lax.rsqrt(jnp.where(head0, ms0, ms1) + EPS) * gain
        o_ref[0, pl.ds(row0, QBLK), :] = y.astype(o_ref.dtype)

    scores(0, s_a)

    def body(i, carry):
        scores(2 * i + 1, s_b)
        finish(2 * i, s_a)
        scores(2 * i + 2, s_a)
        finish(2 * i + 1, s_b)
        return carry

    lax.fori_loop(0, n_steps // 2 - 1, body, 0)
    scores(n_steps - 1, s_b)
    finish(n_steps - 2, s_a)
    finish(n_steps - 1, s_b)


def _attention(proj_a, bias_vec, attn_gain):
    B, T, W3 = proj_a.shape
    W = W3 // 3
    n_pairs = W // LANES
    assert T % (2 * QBLK) == 0
    return pl.pallas_call(
        functools.partial(_attn_kernel, n_chunks=T // CHUNK),
        out_shape=jax.ShapeDtypeStruct((B, T, W), BF16),
        grid=(B, n_pairs),
        in_specs=[pl.BlockSpec((1, T, LANES), lambda b, h: (b, 0, h)),
                  pl.BlockSpec((1, T, LANES), lambda b, h: (b, 0, n_pairs + h)),
                  pl.BlockSpec((1, T, LANES), lambda b, h: (b, 0, 2 * n_pairs + h)),
                  pl.BlockSpec((1, 2, BIAS_W), lambda b, h: (h, 0, 0)),
                  pl.BlockSpec((1, 1, LANES), lambda b, h: (h, 0, 0))],
        out_specs=pl.BlockSpec((1, T, LANES), lambda b, h: (b, 0, h)),
        scratch_shapes=[pltpu.VMEM((T + N_PAST_CHUNKS * CHUNK, LANES), BF16),
                        pltpu.VMEM((T + N_PAST_CHUNKS * CHUNK, LANES), BF16),
                        pltpu.VMEM((N_PAD_STEPS + 1, 2 * QBLK, KBLK), F32),
                        pltpu.VMEM((2 * QBLK, KBLK), F32),
                        pltpu.VMEM((2 * QBLK, KBLK), F32)],
        compiler_params=_cparams(("parallel", "parallel"), 32),
        name="chunk_attention",
    )(proj_a, proj_a, proj_a, bias_vec, attn_gain)


HG_CHUNK = 256
HG_LEVELS = (128, 64, 32, 16, 8)
HG_DIAG = SUBLANES


def _split3(x):
    hi = x.astype(BF16)
    r1 = x - hi.astype(F32)
    mid = r1.astype(BF16)
    lo = (r1 - mid.astype(F32)).astype(BF16)
    return hi, mid, lo


def _hgrn_kernel(q_ref, f_ref, i_ref, g_ref, lbl_ref, gn_ref, o_ref, *, n_steps):
    C = HG_CHUNK
    H2 = C // 2
    Dk = REC_HEAD_DIM
    nt = (((1,), (1,)), ((), ()))
    lbl = lbl_ref[0]
    e = jnp.exp(lbl - jnp.max(lbl, axis=0, keepdims=True))
    lb = e[0:1, :] / jnp.sum(e, axis=0, keepdims=True)
    gn = gn_ref[...]

    r = lax.broadcasted_iota(jnp.int32, (C, C), 0)
    s = lax.broadcasted_iota(jnp.int32, (C, C), 1)
    tril = (s <= r).astype(BF16)
    rh = lax.broadcasted_iota(jnp.int32, (H2, H2), 0)
    sh = lax.broadcasted_iota(jnp.int32, (H2, H2), 1)
    lvl_mask = {m: ((rh // (2 * m)) == (sh // (2 * m))) & (((rh // m) % 2) == 1) & (((sh // m) % 2) == 0)
                for m in HG_LEVELS[1:]}
    sub = lax.broadcasted_iota(jnp.int32, (C // SUBLANES, SUBLANES, Dk), 1)

    def roll8(x, d):
        return pltpu.roll(x.reshape(C // SUBLANES, SUBLANES, Dk), d, 1)

    def step(n, st):
        row0 = pl.multiple_of(n * C, C)
        rows = pl.ds(row0, C)
        f = lb + (1.0 - lb) * _sigmoid(f_ref[0, rows, :])
        logf = jnp.log(f)
        kk = 1.0 - f
        qq = _silu(q_ref[0, rows, :])
        ii = i_ref[0, rows, :]
        ii_bf = ii.astype(BF16)

        hi, mid, lo = _split3(logf)
        bb = jnp.dot(tril, jnp.concatenate([hi, mid, lo], axis=1), preferred_element_type=F32)
        b = bb[:, 0:Dk] + bb[:, Dk:2 * Dk] + bb[:, 2 * Dk:3 * Dk]
        b_last = b[C - 1:C, :]

        def level_z(m):
            bm = jnp.concatenate(
                [jnp.broadcast_to(b[p + m - 1:p + m, :], (2 * m, Dk)) for p in range(0, C, 2 * m)], axis=0)
            src = jnp.concatenate(
                [kk[p:p + m] if (p // m) % 2 == 0 else qq[p:p + m] for p in range(0, C, m)], axis=0)
            return (src * jnp.exp(-jnp.abs(b - bm))).astype(BF16)

        z = level_z(HG_LEVELS[0])
        a_lo = lax.dot_general(z[H2:], z[:H2], nt, preferred_element_type=F32)
        a_d0 = jnp.zeros((H2, H2), F32)
        a_d1 = jnp.zeros((H2, H2), F32)
        for m in HG_LEVELS[1:]:
            z = level_z(m)
            g0 = lax.dot_general(z[:H2], z[:H2], nt, preferred_element_type=F32)
            g1 = lax.dot_general(z[H2:], z[H2:], nt, preferred_element_type=F32)
            a_d0 = jnp.where(lvl_mask[m], g0, a_d0)
            a_d1 = jnp.where(lvl_mask[m], g1, a_d1)
        o_top = jnp.dot(a_d0.astype(BF16), ii_bf[:H2], preferred_element_type=F32)
        o_bot = jnp.dot(jnp.concatenate([a_lo, a_d1], axis=1).astype(BF16), ii_bf, preferred_element_type=F32)
        o = jnp.concatenate([o_top, o_bot], axis=0)

        k3 = kk.reshape(C // SUBLANES, SUBLANES, Dk)
        b3 = b.reshape(C // SUBLANES, SUBLANES, Dk)
        q3 = qq.reshape(C // SUBLANES, SUBLANES, Dk)
        i3 = ii.reshape(C // SUBLANES, SUBLANES, Dk)
        o3 = jnp.sum(q3 * k3, axis=-1, keepdims=True) * i3
        for dlt in range(1, HG_DIAG):
            w = q3 * roll8(kk, dlt) * jnp.exp(b3 - roll8(b, dlt))
            w = jnp.where(sub >= dlt, w, 0.0)
            o3 = o3 + jnp.sum(w, axis=-1, keepdims=True) * roll8(ii, dlt)
        o = o + o3.reshape(C, Dk)

        qe = (qq * jnp.exp(b)).astype(BF16)
        o = o + lax.dot_general(qe, st.astype(BF16), nt, preferred_element_type=F32)
        ke = (kk * jnp.exp(b_last - b)).astype(BF16)
        st = st * jnp.exp(b_last) + lax.dot_general(
            ii_bf, ke, (((0,), (0,)), ((), ())), preferred_element_type=F32)

        ms = jnp.mean(o * o, axis=-1, keepdims=True)
        y = o * lax.rsqrt(ms + EPS) * gn
        y = y * _silu(g_ref[0, rows, :])
        o_ref[0, rows, :] = y.astype(o_ref.dtype)
        return st

    lax.fori_loop(0, n_steps, step, jnp.zeros((Dk, Dk), F32))


def _hgrn(proj_b, lb_logits_h, gnorm_g):
    B, T, W4 = proj_b.shape
    W = W4 // 4
    H = W // REC_HEAD_DIM
    n_slots = lb_logits_h.shape[1]
    blk = lambda off: pl.BlockSpec((1, T, REC_HEAD_DIM), lambda b, h, off=off: (b, 0, off * H + h))
    return pl.pallas_call(
        functools.partial(_hgrn_kernel, n_steps=T // HG_CHUNK),
        out_shape=jax.ShapeDtypeStruct((B, T, W), BF16),
        grid=(B, H),
        in_specs=[blk(0), blk(1), blk(2), blk(3),
                  pl.BlockSpec((1, n_slots, REC_HEAD_DIM), lambda b, h: (h, 0, 0)),
                  pl.BlockSpec((1, REC_HEAD_DIM), lambda b, h: (0, 0))],
        out_specs=pl.BlockSpec((1, T, REC_HEAD_DIM), lambda b, h: (b, 0, h)),
        compiler_params=_cparams(("parallel", "parallel"), 32),
        name="hgrn2",
    )(proj_b, proj_b, proj_b, proj_b, lb_logits_h, gnorm_g)


def _outproj_kernel(oa_ref, ob_ref, w_ref, x_ref, mod_ref, g_ref, b_ref, x1_ref, h2_ref, *, alpha):
    wa = oa_ref.shape[-1]
    mix = jnp.dot(oa_ref[0], w_ref[0:wa, :], preferred_element_type=F32)
    mix = mix + jnp.dot(ob_ref[0], w_ref[wa:, :], preferred_element_type=F32)
    gate1 = mod_ref[0, 2:3, :]
    x1 = _ln_rows(alpha * x_ref[0] + gate1 * mix) * g_ref[...] + b_ref[...]
    x1_ref[0] = x1
    shift2 = mod_ref[0, 3:4, :]
    scale2 = mod_ref[0, 4:5, :]
    h2_ref[0] = (_ln_rows(x1) * (1.0 + scale2) + shift2).astype(h2_ref.dtype)


def _outproj(o_a, o_b, w_o, x, mod3, ln_g, ln_b, alpha, tm=256):
    B, T, D = x.shape
    Wa, Wb = o_a.shape[-1], o_b.shape[-1]
    return pl.pallas_call(
        functools.partial(_outproj_kernel, alpha=alpha),
        out_shape=(jax.ShapeDtypeStruct((B, T, D), F32), jax.ShapeDtypeStruct((B, T, D), BF16)),
        grid=(B, T // tm),
        in_specs=[pl.BlockSpec((1, tm, Wa), lambda b, i: (b, i, 0)),
                  pl.BlockSpec((1, tm, Wb), lambda b, i: (b, i, 0)),
                  pl.BlockSpec((Wa + Wb, D), lambda b, i: (0, 0)),
                  pl.BlockSpec((1, tm, D), lambda b, i: (b, i, 0)),
                  pl.BlockSpec((1, N_MOD, D), lambda b, i: (b, 0, 0)),
                  pl.BlockSpec((1, D), lambda b, i: (0, 0)),
                  pl.BlockSpec((1, D), lambda b, i: (0, 0))],
        out_specs=(pl.BlockSpec((1, tm, D), lambda b, i: (b, i, 0)),
                   pl.BlockSpec((1, tm, D), lambda b, i: (b, i, 0))),
        compiler_params=_cparams(("parallel", "parallel"), 48),
        name="out_proj_ln1",
    )(o_a, o_b, w_o, x, mod3, ln_g, ln_b)


def _ffn_kernel(h_ref, wg_ref, wu_ref, wo_ref, x_ref, mod_ref, g_ref, b_ref, o_ref, *, alpha):
    f = pl.program_id(2)
    h = h_ref[0]
    gate = jnp.dot(h, wg_ref[...], preferred_element_type=F32)
    up = jnp.dot(h, wu_ref[...], preferred_element_type=F32)
    act = (_silu(gate) * up).astype(BF16)
    part = jnp.dot(act, wo_ref[...], preferred_element_type=F32)

    @pl.when(f == 0)
    def _():
        o_ref[0] = part

    @pl.when(f > 0)
    def _():
        o_ref[0] += part

    @pl.when(f == pl.num_programs(2) - 1)
    def _():
        gate2 = mod_ref[0, 5:6, :]
        o_ref[0] = _ln_rows(alpha * x_ref[0] + gate2 * o_ref[0]) * g_ref[...] + b_ref[...]


def _ffn(h2, w_in, w_out, x1, mod3, ln_g, ln_b, alpha, tm=512, tf=512):
    B, T, D = x1.shape
    F = w_out.shape[0]
    nf = F // tf
    return pl.pallas_call(
        functools.partial(_ffn_kernel, alpha=alpha),
        out_shape=jax.ShapeDtypeStruct((B, T, D), F32),
        grid=(B, T // tm, nf),
        in_specs=[pl.BlockSpec((1, tm, D), lambda b, i, f: (b, i, 0)),
                  pl.BlockSpec((D, tf), lambda b, i, f: (0, f)),
                  pl.BlockSpec((D, tf), lambda b, i, f: (0, nf + f)),
                  pl.BlockSpec((tf, D), lambda b, i, f: (f, 0)),
                  pl.BlockSpec((1, tm, D), lambda b, i, f: (b, i, 0)),
                  pl.BlockSpec((1, N_MOD, D), lambda b, i, f: (b, 0, 0)),
                  pl.BlockSpec((1, D), lambda b, i, f: (0, 0)),
                  pl.BlockSpec((1, D), lambda b, i, f: (0, 0))],
        out_specs=pl.BlockSpec((1, tm, D), lambda b, i, f: (b, i, 0)),
        compiler_params=_cparams(("parallel", "parallel", "arbitrary"), 48),
        name="swiglu_ffn_ln2",
    )(h2, w_in, w_in, w_out, x1, mod3, ln_g, ln_b)


def _bias_vectors(rel_bias):
    H, n_rel = rel_bias.shape
    max_rel = (n_rel - 1) // 2
    u = jnp.arange(BIAS_W)
    idx = jnp.clip(KBLK - u, -max_rel, max_rel) + max_rel
    return rel_bias[:, idx]


def kernel(x, c, w_ada, b_ada, w_in, rel_bias, attn_norm_g, lb_logits, gnorm_g, w_o,
           ln1_g, ln1_b, w_ffn_in, w_ffn_out, ln2_g, ln2_b):
    B, T, D = x.shape
    depth = w_ada.shape[0]
    alpha = (2 * depth) ** 0.25
    attn_w = attn_norm_g.shape[1]
    rec_w = lb_logits.shape[1]
    n_slots = lb_logits.shape[0]
    rec_heads = rec_w // REC_HEAD_DIM
    assert depth == 1 and n_slots == depth + 1
    for layer in range(depth):
        mod3 = _mod(c, w_ada[layer], b_ada[layer]).reshape(B, N_MOD, D)
        h1 = _ln_mod(x, mod3, 0).reshape(B * T, D)
        w_in_bf = w_in[layer].astype(BF16)
        proj_a = _matmul(h1, w_in_bf, 0, 3 * attn_w, BF16).reshape(B, T, 3 * attn_w)
        proj_b = _matmul(h1, w_in_bf, 3 * attn_w // 1024, 4 * rec_w, F32).reshape(B, T, 4 * rec_w)
        bias_vec = _bias_vectors(rel_bias[layer]).reshape(-1, 2, BIAS_W)
        o_a = _attention(proj_a, bias_vec, attn_norm_g[layer].reshape(-1, 1, LANES))
        lbl = lb_logits.reshape(n_slots, rec_heads, REC_HEAD_DIM).transpose(1, 0, 2)
        o_b = _hgrn(proj_b, lbl, gnorm_g[layer].reshape(1, REC_HEAD_DIM))
        x, h2 = _outproj(o_a, o_b, w_o[layer].astype(BF16), x, mod3,
                         ln1_g[layer].reshape(1, D), ln1_b[layer].reshape(1, D), alpha)
        x = _ffn(h2, w_ffn_in[layer].astype(BF16), w_ffn_out[layer].astype(BF16), x, mod3,
                 ln2_g[layer].reshape(1, D), ln2_b[layer].reshape(1, D), alpha)
    return x
```

```python
import functools

import jax
import jax.numpy as jnp
from jax import lax
from jax.experimental import pallas as pl
from jax.experimental.pallas import tpu as pltpu

F32 = jnp.float32
BF16 = jnp.bfloat16

CHUNK = 64
N_PAST_CHUNKS = 8
BAND = (N_PAST_CHUNKS + 1) * CHUNK
ATTN_HEAD_DIM = 64
REC_HEAD_DIM = 128
N_MOD = 6
EPS = 1e-5
LANES = 128
SUBLANES = 8
QBLK = 2 * CHUNK
KBLK = BAND + CHUNK
BIAS_W = KBLK + QBLK
N_PAD_STEPS = N_PAST_CHUNKS * CHUNK // QBLK

MIB = 1024 * 1024


def _cparams(sem, vmem_mib):
    return pltpu.CompilerParams(dimension_semantics=sem, vmem_limit_bytes=vmem_mib * MIB)


def _sigmoid(x):
    return 1.0 / (1.0 + jnp.exp(-x))


def _silu(x):
    return x * _sigmoid(x)


def _ln_rows(x):
    mu = jnp.mean(x, axis=-1, keepdims=True)
    xc = x - mu
    var = jnp.mean(xc * xc, axis=-1, keepdims=True)
    return xc * lax.rsqrt(var + EPS)


def _mod_kernel(ct_ref, w_ref, b_ref, o_ref):
    ct = ct_ref[...]
    cat = _silu(ct)
    w = w_ref[...]
    rows = [jnp.sum(w * cat[:, b:b + 1], axis=0, keepdims=True) for b in range(ct.shape[1])]
    o_ref[...] = jnp.concatenate(rows, axis=0) + b_ref[...]


def _mod(c, w_ada, b_ada, tn=512):
    B, D = c.shape
    N = w_ada.shape[1]
    return pl.pallas_call(
        _mod_kernel,
        out_shape=jax.ShapeDtypeStruct((B, N), F32),
        grid=(N // tn,),
        in_specs=[pl.BlockSpec((D, B), lambda j: (0, 0)),
                  pl.BlockSpec((D, tn), lambda j: (0, j)),
                  pl.BlockSpec((1, tn), lambda j: (0, j))],
        out_specs=pl.BlockSpec((B, tn), lambda j: (0, j)),
        compiler_params=_cparams(("parallel",), 32),
        name="adaln_mod",
    )(c.T, w_ada, b_ada.reshape(1, N))


def _ln_mod_kernel(x_ref, mod_ref, o_ref, *, shift_row):
    y = _ln_rows(x_ref[0])
    shift = mod_ref[0, shift_row:shift_row + 1, :]
    scale = mod_ref[0, shift_row + 1:shift_row + 2, :]
    o_ref[0] = (y * (1.0 + scale) + shift).astype(o_ref.dtype)


def _ln_mod(x, mod3, shift_row, tm=512):
    B, T, D = x.shape
    return pl.pallas_call(
        functools.partial(_ln_mod_kernel, shift_row=shift_row),
        out_shape=jax.ShapeDtypeStruct((B, T, D), BF16),
        grid=(B, T // tm),
        in_specs=[pl.BlockSpec((1, tm, D), lambda b, i: (b, i, 0)),
                  pl.BlockSpec((1, N_MOD, D), lambda b, i: (b, 0, 0))],
        out_specs=pl.BlockSpec((1, tm, D), lambda b, i: (b, i, 0)),
        compiler_params=_cparams(("parallel", "parallel"), 32),
        name="ln_modulate",
    )(x, mod3)


def _matmul_kernel(a_ref, w_ref, o_ref):
    o_ref[...] = jnp.dot(a_ref[...], w_ref[...], preferred_element_type=F32).astype(o_ref.dtype)


def _matmul(a, w, col_block0, n_out, out_dtype, tm=1024, tn=1024):
    M, K = a.shape
    return pl.pallas_call(
        _matmul_kernel,
        out_shape=jax.ShapeDtypeStruct((M, n_out), out_dtype),
        grid=(M // tm, n_out // tn),
        in_specs=[pl.BlockSpec((tm, K), lambda i, j: (i, 0)),
                  pl.BlockSpec((K, tn), lambda i, j: (0, j + col_block0))],
        out_specs=pl.BlockSpec((tm, tn), lambda i, j: (i, j)),
        compiler_params=_cparams(("parallel", "arbitrary"), 40),
        name="in_proj",
    )(a, w)


def _attn_kernel(q_ref, k_ref, v_ref, bias_ref, gain_ref, o_ref, kpad, vpad, tab, s_a, s_b, *, n_chunks):
    pad = N_PAST_CHUNKS * CHUNK
    T = n_chunks * CHUNK
    n_steps = T // QBLK
    zeros = jnp.zeros((pad, LANES), BF16)
    kpad[0:pad, :] = zeros
    vpad[0:pad, :] = zeros
    kpad[pad:pad + T, :] = k_ref[0]
    vpad[pad:pad + T, :] = v_ref[0]

    col = lax.broadcasted_iota(jnp.int32, (QBLK, KBLK), 1)
    rowi = lax.broadcasted_iota(jnp.int32, (QBLK, KBLK), 0)
    in_band = ((rowi < CHUNK) & (col < BAND)) | ((rowi >= CHUNK) & (col >= CHUNK))
    for hh in range(2):
        g = jnp.broadcast_to(bias_ref[0, hh:hh + 1, :], (QBLK, BIAS_W))
        t = pltpu.roll(g, BIAS_W - QBLK, 1, stride=1, stride_axis=0)[:, :KBLK]
        t = jnp.where(in_band, t, -jnp.inf)
        for v in range(N_PAD_STEPS + 1):
            first_real = pad - v * QBLK
            tv = jnp.where(col >= first_real, t, -jnp.inf) if first_real > 0 else t
            tab[v, hh * QBLK:(hh + 1) * QBLK, :] = tv

    lane = lax.broadcasted_iota(jnp.int32, (QBLK, LANES), 1)
    head0 = lane < ATTN_HEAD_DIM
    gain = gain_ref[0]
    scale = ATTN_HEAD_DIM ** -0.5
    nt = (((1,), (1,)), ((), ()))

    def scores(m, dst):
        row0 = pl.multiple_of(m * QBLK, QBLK)
        q = q_ref[0, pl.ds(row0, QBLK), :] * scale
        zq = jnp.zeros_like(q)
        qs = jnp.concatenate([jnp.where(head0, q, zq), jnp.where(head0, zq, q)], axis=0)
        s = lax.dot_general(qs, kpad[pl.ds(row0, KBLK), :], nt, preferred_element_type=F32)
        dst[...] = s + tab[jnp.minimum(m, N_PAD_STEPS)]

    def finish(m, src):
        row0 = pl.multiple_of(m * QBLK, QBLK)
        s = src[...]
        p = jnp.exp(s - jnp.max(s, axis=-1, keepdims=True))
        l = jnp.sum(p, axis=-1, keepdims=True)
        pv = jnp.dot(p.astype(BF16), vpad[pl.ds(row0, KBLK), :], preferred_element_type=F32)
        pv = pv * (1.0 / l)
        o = jnp.where(head0, pv[0:QBLK], pv[QBLK:2 * QBLK])
        o2 = o * o
        ms0 = jnp.sum(jnp.where(head0, o2, 0.0), axis=-1, keepdims=True) / ATTN_HEAD_DIM
        ms1 = jnp.sum(jnp.where(head0, 0.0, o2), axis=-1, keepdims=True) / ATTN_HEAD_DIM
        y = o * lax.rsqrt(jnp.where(head0, ms0, ms1) + EPS) * gain
        o_ref[0, pl.ds(row0, QBLK), :] = y.astype(o_ref.dtype)

    scores(0, s_a)

    def body(i, carry):
        scores(2 * i + 1, s_b)
        finish(2 * i, s_a)
        scores(2 * i + 2, s_a)
        finish(2 * i + 1, s_b)
        return carry

    lax.fori_loop(0, n_steps // 2 - 1, body, 0)
    scores(n_steps - 1, s_b)
    finish(n_steps - 2, s_a)
    finish(n_steps - 1, s_b)


def _attention(proj_a, bias_vec, attn_gain):
    B, T, W3 = proj_a.shape
    W = W3 // 3
    n_pairs = W // LANES
    assert T % (2 * QBLK) == 0
    return pl.pallas_call(
        functools.partial(_attn_kernel, n_chunks=T // CHUNK),
        out_shape=jax.ShapeDtypeStruct((B, T, W), BF16),
        grid=(B, n_pairs),
        in_specs=[pl.BlockSpec((1, T, LANES), lambda b, h: (b, 0, h)),
                  pl.BlockSpec((1, T, LANES), lambda b, h: (b, 0, n_pairs + h)),
                  pl.BlockSpec((1, T, LANES), lambda b, h: (b, 0, 2 * n_pairs + h)),
                  pl.BlockSpec((1, 2, BIAS_W), lambda b, h: (h, 0, 0)),
                  pl.BlockSpec((1, 1, LANES), lambda b, h: (h, 0, 0))],
        out_specs=pl.BlockSpec((1, T, LANES), lambda b, h: (b, 0, h)),
        scratch_shapes=[pltpu.VMEM((T + N_PAST_CHUNKS * CHUNK, LANES), BF16),
                        pltpu.VMEM((T + N_PAST_CHUNKS * CHUNK, LANES), BF16),
                        pltpu.VMEM((N_PAD_STEPS + 1, 2 * QBLK, KBLK), F32),
                        pltpu.VMEM((2 * QBLK, KBLK), F32),
                        pltpu.VMEM((2 * QBLK, KBLK), F32)],
        compiler_params=_cparams(("parallel", "parallel"), 32),
        name="chunk_attention",
    )(proj_a, proj_a, proj_a, bias_vec, attn_gain)


HG_CHUNK = 256
HG_LEVELS = (128, 64, 32, 16, 8)
HG_DIAG = SUBLANES


def _split3(x):
    hi = x.astype(BF16)
    r1 = x - hi.astype(F32)
    mid = r1.astype(BF16)
    lo = (r1 - mid.astype(F32)).astype(BF16)
    return hi, mid, lo


def _hgrn_kernel(q_ref, f_ref, i_ref, g_ref, lbl_ref, gn_ref, o_ref, *, n_steps):
    C = HG_CHUNK
    H2 = C // 2
    Dk = REC_HEAD_DIM
    nt = (((1,), (1,)), ((), ()))
    lbl = lbl_ref[0]
    e = jnp.exp(lbl - jnp.max(lbl, axis=0, keepdims=True))
    lb = e[0:1, :] / jnp.sum(e, axis=0, keepdims=True)
    gn = gn_ref[...]

    r = lax.broadcasted_iota(jnp.int32, (C, C), 0)
    s = lax.broadcasted_iota(jnp.int32, (C, C), 1)
    tril = (s <= r).astype(BF16)
    rh = lax.broadcasted_iota(jnp.int32, (H2, H2), 0)
    sh = lax.broadcasted_iota(jnp.int32, (H2, H2), 1)
    lvl_mask = {m: ((rh // (2 * m)) == (sh // (2 * m))) & (((rh // m) % 2) == 1) & (((sh // m) % 2) == 0)
                for m in HG_LEVELS[1:]}
    sub = lax.broadcasted_iota(jnp.int32, (C // SUBLANES, SUBLANES, Dk), 1)

    def roll8(x, d):
        return pltpu.roll(x.reshape(C // SUBLANES, SUBLANES, Dk), d, 1)

    def step(n, st):
        row0 = pl.multiple_of(n * C, C)
        rows = pl.ds(row0, C)
        f = lb + (1.0 - lb) * _sigmoid(f_ref[0, rows, :])
        logf = jnp.log(f)
        kk = 1.0 - f
        qq = _silu(q_ref[0, rows, :])
        ii = i_ref[0, rows, :]
        ii_bf = ii.astype(BF16)

        hi, mid, lo = _split3(logf)
        bb = jnp.dot(tril, jnp.concatenate([hi, mid, lo], axis=1), preferred_element_type=F32)
        b = bb[:, 0:Dk] + bb[:, Dk:2 * Dk] + bb[:, 2 * Dk:3 * Dk]
        b_last = b[C - 1:C, :]

        def level_z(m):
            bm = jnp.concatenate(
                [jnp.broadcast_to(b[p + m - 1:p + m, :], (2 * m, Dk)) for p in range(0, C, 2 * m)], axis=0)
            src = jnp.concatenate(
                [kk[p:p + m] if (p // m) % 2 == 0 else qq[p:p + m] for p in range(0, C, m)], axis=0)
            return (src * jnp.exp(-jnp.abs(b - bm))).astype(BF16)

        z = level_z(HG_LEVELS[0])
        a_lo = lax.dot_general(z[H2:], z[:H2], nt, preferred_element_type=F32)
        a_d0 = jnp.zeros((H2, H2), F32)
        a_d1 = jnp.zeros((H2, H2), F32)
        for m in HG_LEVELS[1:]:
            z = level_z(m)
            g0 = lax.dot_general(z[:H2], z[:H2], nt, preferred_element_type=F32)
            g1 = lax.dot_general(z[H2:], z[H2:], nt, preferred_element_type=F32)
            a_d0 = jnp.where(lvl_mask[m], g0, a_d0)
            a_d1 = jnp.where(lvl_mask[m], g1, a_d1)
        o_top = jnp.dot(a_d0.astype(BF16), ii_bf[:H2], preferred_element_type=F32)
        o_bot = jnp.dot(jnp.concatenate([a_lo, a_d1], axis=1).astype(BF16), ii_bf, preferred_element_type=F32)
        o = jnp.concatenate([o_top, o_bot], axis=0)

        k3 = kk.reshape(C // SUBLANES, SUBLANES, Dk)
        b3 = b.reshape(C // SUBLANES, SUBLANES, Dk)
        q3 = qq.reshape(C // SUBLANES, SUBLANES, Dk)
        i3 = ii.reshape(C // SUBLANES, SUBLANES, Dk)
        o3 = jnp.sum(q3 * k3, axis=-1, keepdims=True) * i3
        for dlt in range(1, HG_DIAG):
            w = q3 * roll8(kk, dlt) * jnp.exp(b3 - roll8(b, dlt))
            w = jnp.where(sub >= dlt, w, 0.0)
            o3 = o3 + jnp.sum(w, axis=-1, keepdims=True) * roll8(ii, dlt)
        o = o + o3.reshape(C, Dk)

        qe = (qq * jnp.exp(b)).astype(BF16)
        o = o + lax.dot_general(qe, st.astype(BF16), nt, preferred_element_type=F32)
        ke = (kk * jnp.exp(b_last - b)).astype(BF16)
        st = st * jnp.exp(b_last) + lax.dot_general(
            ii_bf, ke, (((0,), (0,)), ((), ())), preferred_element_type=F32)

        ms = jnp.mean(o * o, axis=-1, keepdims=True)
        y = o * lax.rsqrt(ms + EPS) * gn
        y = y * _silu(g_ref[0, rows, :])
        o_ref[0, rows, :] = y.astype(o_ref.dtype)
        return st

    lax.fori_loop(0, n_steps, step, jnp.zeros((Dk, Dk), F32))


def _hgrn(proj_b, lb_logits_h, gnorm_g):
    B, T, W4 = proj_b.shape
    W = W4 // 4
    H = W // REC_HEAD_DIM
    n_slots = lb_logits_h.shape[1]
    blk = lambda off: pl.BlockSpec((1, T, REC_HEAD_DIM), lambda b, h, off=off: (b, 0, off * H + h))
    return pl.pallas_call(
        functools.partial(_hgrn_kernel, n_steps=T // HG_CHUNK),
        out_shape=jax.ShapeDtypeStruct((B, T, W), BF16),
        grid=(B, H),
        in_specs=[blk(0), blk(1), blk(2), blk(3),
                  pl.BlockSpec((1, n_slots, REC_HEAD_DIM), lambda b, h: (h, 0, 0)),
                  pl.BlockSpec((1, REC_HEAD_DIM), lambda b, h: (0, 0))],
        out_specs=pl.BlockSpec((1, T, REC_HEAD_DIM), lambda b, h: (b, 0, h)),
        compiler_params=_cparams(("parallel", "parallel"), 32),
        name="hgrn2",
    )(proj_b, proj_b, proj_b, proj_b, lb_logits_h, gnorm_g)


def _outproj_kernel(oa_ref, ob_ref, w_ref, x_ref, mod_ref, g_ref, b_ref, x1_ref, h2_ref, *, alpha, sub):
    wa = oa_ref.shape[-1]
    gate1 = mod_ref[0, 2:3, :]
    shift2 = mod_ref[0, 3:4, :]
    scale2 = mod_ref[0, 4:5, :]
    for r0 in range(0, x_ref.shape[1], sub):
        rs = slice(r0, r0 + sub)
        mix = jnp.dot(oa_ref[0, rs, :], w_ref[0:wa, :], preferred_element_type=F32)
        mix = mix + jnp.dot(ob_ref[0, rs, :], w_ref[wa:, :], preferred_element_type=F32)
        x1 = _ln_rows(alpha * x_ref[0, rs, :] + gate1 * mix) * g_ref[...] + b_ref[...]
        x1_ref[0, rs, :] = x1
        h2_ref[0, rs, :] = (_ln_rows(x1) * (1.0 + scale2) + shift2).astype(h2_ref.dtype)


def _outproj(o_a, o_b, w_o, x, mod3, ln_g, ln_b, alpha, tm=512, sub=128):
    B, T, D = x.shape
    Wa, Wb = o_a.shape[-1], o_b.shape[-1]
    return pl.pallas_call(
        functools.partial(_outproj_kernel, alpha=alpha, sub=sub),
        out_shape=(jax.ShapeDtypeStruct((B, T, D), F32), jax.ShapeDtypeStruct((B, T, D), BF16)),
        grid=(B, T // tm),
        in_specs=[pl.BlockSpec((1, tm, Wa), lambda b, i: (b, i, 0)),
                  pl.BlockSpec((1, tm, Wb), lambda b, i: (b, i, 0)),
                  pl.BlockSpec((Wa + Wb, D), lambda b, i: (0, 0)),
                  pl.BlockSpec((1, tm, D), lambda b, i: (b, i, 0)),
                  pl.BlockSpec((1, N_MOD, D), lambda b, i: (b, 0, 0)),
                  pl.BlockSpec((1, D), lambda b, i: (0, 0)),
                  pl.BlockSpec((1, D), lambda b, i: (0, 0))],
        out_specs=(pl.BlockSpec((1, tm, D), lambda b, i: (b, i, 0)),
                   pl.BlockSpec((1, tm, D), lambda b, i: (b, i, 0))),
        compiler_params=_cparams(("parallel", "parallel"), 48),
        name="out_proj_ln1",
    )(o_a, o_b, w_o, x, mod3, ln_g, ln_b)


def _ffn_kernel(h_ref, wg_ref, wu_ref, wo_ref, x_ref, mod_ref, g_ref, b_ref, o_ref, *, alpha):
    f = pl.program_id(2)

    @pl.when(f == 0)
    def _():
        o_ref[0] = jnp.zeros(o_ref.shape[1:], F32)

    h = h_ref[0]
    gate = jnp.dot(h, wg_ref[...], preferred_element_type=F32)
    up = jnp.dot(h, wu_ref[...], preferred_element_type=F32)
    act = (_silu(gate) * up).astype(BF16)
    o_ref[0] += jnp.dot(act, wo_ref[...], preferred_element_type=F32)

    @pl.when(f == pl.num_programs(2) - 1)
    def _():
        gate2 = mod_ref[0, 5:6, :]
        o_ref[0] = _ln_rows(alpha * x_ref[0] + gate2 * o_ref[0]) * g_ref[...] + b_ref[...]


def _ffn(h2, w_in, w_out, x1, mod3, ln_g, ln_b, alpha, tm=512, tf=512):
    B, T, D = x1.shape
    F = w_out.shape[0]
    nf = F // tf
    return pl.pallas_call(
        functools.partial(_ffn_kernel, alpha=alpha),
        out_shape=jax.ShapeDtypeStruct((B, T, D), F32),
        grid=(B, T // tm, nf),
        in_specs=[pl.BlockSpec((1, tm, D), lambda b, i, f: (b, i, 0)),
                  pl.BlockSpec((D, tf), lambda b, i, f: (0, f)),
                  pl.BlockSpec((D, tf), lambda b, i, f: (0, nf + f)),
                  pl.BlockSpec((tf, D), lambda b, i, f: (f, 0)),
                  pl.BlockSpec((1, tm, D), lambda b, i, f: (b, i, 0)),
                  pl.BlockSpec((1, N_MOD, D), lambda b, i, f: (b, 0, 0)),
                  pl.BlockSpec((1, D), lambda b, i, f: (0, 0)),
                  pl.BlockSpec((1, D), lambda b, i, f: (0, 0))],
        out_specs=pl.BlockSpec((1, tm, D), lambda b, i, f: (b, i, 0)),
        compiler_params=_cparams(("parallel", "parallel", "arbitrary"), 48),
        name="swiglu_ffn_ln2",
    )(h2, w_in, w_in, w_out, x1, mod3, ln_g, ln_b)


def _bias_vectors(rel_bias):
    H, n_rel = rel_bias.shape
    max_rel = (n_rel - 1) // 2
    u = jnp.arange(BIAS_W)
    idx = jnp.clip(KBLK - u, -max_rel, max_rel) + max_rel
    return rel_bias[:, idx]


def kernel(x, c, w_ada, b_ada, w_in, rel_bias, attn_norm_g, lb_logits, gnorm_g, w_o,
           ln1_g, ln1_b, w_ffn_in, w_ffn_out, ln2_g, ln2_b):
    B, T, D = x.shape
    depth = w_ada.shape[0]
    alpha = (2 * depth) ** 0.25
    attn_w = attn_norm_g.shape[1]
    rec_w = lb_logits.shape[1]
    n_slots = lb_logits.shape[0]
    rec_heads = rec_w // REC_HEAD_DIM
    assert depth == 1 and n_slots == depth + 1
    for layer in range(depth):
        mod3 = _mod(c, w_ada[layer], b_ada[layer]).reshape(B, N_MOD, D)
        h1 = _ln_mod(x, mod3, 0).reshape(B * T, D)
        w_in_bf = w_in[layer].astype(BF16)
        proj_a = _matmul(h1, w_in_bf, 0, 3 * attn_w, BF16).reshape(B, T, 3 * attn_w)
        proj_b = _matmul(h1, w_in_bf, 3 * attn_w // 1024, 4 * rec_w, F32).reshape(B, T, 4 * rec_w)
        bias_vec = _bias_vectors(rel_bias[layer]).reshape(-1, 2, BIAS_W)
        o_a = _attention(proj_a, bias_vec, attn_norm_g[layer].reshape(-1, 1, LANES))
        lbl = lb_logits.reshape(n_slots, rec_heads, REC_HEAD_DIM).transpose(1, 0, 2)
        o_b = _hgrn(proj_b, lbl, gnorm_g[layer].reshape(1, REC_HEAD_DIM))
        x, h2 = _outproj(o_a, o_b, w_o[layer].astype(BF16), x, mod3,
                         ln1_g[layer].reshape(1, D), ln1_b[layer].reshape(1, D), alpha)
        x = _ffn(h2, w_ffn_in[layer].astype(BF16), w_ffn_out[layer].astype(BF16), x, mod3,
                 ln2_g[layer].reshape(1, D), ln2_b[layer].reshape(1, D), alpha)
    return x
```

```python
import functools

import jax
import jax.numpy as jnp
from jax import lax
from jax.experimental import pallas as pl
from jax.experimental.pallas import tpu as pltpu

F32 = jnp.float32
BF16 = jnp.bfloat16

CHUNK = 64
N_PAST_CHUNKS = 8
BAND = (N_PAST_CHUNKS + 1) * CHUNK
ATTN_HEAD_DIM = 64
REC_HEAD_DIM = 128
N_MOD = 6
EPS = 1e-5
LANES = 128
SUBLANES = 8
QBLK = 2 * CHUNK
KBLK = BAND + CHUNK
BIAS_W = KBLK + QBLK
N_PAD_STEPS = N_PAST_CHUNKS * CHUNK // QBLK

MIB = 1024 * 1024


def _cparams(sem, vmem_mib):
    return pltpu.CompilerParams(dimension_semantics=sem, vmem_limit_bytes=vmem_mib * MIB)


def _sigmoid(x):
    return 1.0 / (1.0 + jnp.exp(-x))


def _silu(x):
    return x * _sigmoid(x)


def _ln_rows(x):
    mu = jnp.mean(x, axis=-1, keepdims=True)
    xc = x - mu
    var = jnp.mean(xc * xc, axis=-1, keepdims=True)
    return xc * lax.rsqrt(var + EPS)


def _mod_kernel(ct_ref, w_ref, b_ref, o_ref):
    ct = ct_ref[...]
    cat = _silu(ct)
    w = w_ref[...]
    rows = [jnp.sum(w * cat[:, b:b + 1], axis=0, keepdims=True) for b in range(ct.shape[1])]
    o_ref[...] = jnp.concatenate(rows, axis=0) + b_ref[...]


def _mod(c, w_ada, b_ada, tn=512):
    B, D = c.shape
    N = w_ada.shape[1]
    return pl.pallas_call(
        _mod_kernel,
        out_shape=jax.ShapeDtypeStruct((B, N), F32),
        grid=(N // tn,),
        in_specs=[pl.BlockSpec((D, B), lambda j: (0, 0)),
                  pl.BlockSpec((D, tn), lambda j: (0, j)),
                  pl.BlockSpec((1, tn), lambda j: (0, j))],
        out_specs=pl.BlockSpec((B, tn), lambda j: (0, j)),
        compiler_params=_cparams(("parallel",), 32),
        name="adaln_mod",
    )(c.T, w_ada, b_ada.reshape(1, N))


def _ln_mod_kernel(x_ref, mod_ref, o_ref, *, shift_row):
    y = _ln_rows(x_ref[0])
    shift = mod_ref[0, shift_row:shift_row + 1, :]
    scale = mod_ref[0, shift_row + 1:shift_row + 2, :]
    o_ref[0] = (y * (1.0 + scale) + shift).astype(o_ref.dtype)


def _ln_mod(x, mod3, shift_row, tm=512):
    B, T, D = x.shape
    return pl.pallas_call(
        functools.partial(_ln_mod_kernel, shift_row=shift_row),
        out_shape=jax.ShapeDtypeStruct((B, T, D), BF16),
        grid=(B, T // tm),
        in_specs=[pl.BlockSpec((1, tm, D), lambda b, i: (b, i, 0)),
                  pl.BlockSpec((1, N_MOD, D), lambda b, i: (b, 0, 0))],
        out_specs=pl.BlockSpec((1, tm, D), lambda b, i: (b, i, 0)),
        compiler_params=_cparams(("parallel", "parallel"), 32),
        name="ln_modulate",
    )(x, mod3)


def _matmul_kernel(a_ref, w_ref, o_ref):
    o_ref[...] = jnp.dot(a_ref[...], w_ref[...], preferred_element_type=F32).astype(o_ref.dtype)


def _matmul(a, w, col_block0, n_out, out_dtype, tm=1024, tn=1024):
    M, K = a.shape
    return pl.pallas_call(
        _matmul_kernel,
        out_shape=jax.ShapeDtypeStruct((M, n_out), out_dtype),
        grid=(M // tm, n_out // tn),
        in_specs=[pl.BlockSpec((tm, K), lambda i, j: (i, 0)),
                  pl.BlockSpec((K, tn), lambda i, j: (0, j + col_block0))],
        out_specs=pl.BlockSpec((tm, tn), lambda i, j: (i, j)),
        compiler_params=_cparams(("parallel", "arbitrary"), 40),
        name="in_proj",
    )(a, w)


NEG_BIG = -1e30
LOG2E = 1.4426950408889634


def _attn_kernel(q_ref, k_ref, v_ref, bias_ref, gain_ref, o_ref, kpad, vpa, vpb, tab_t, s_a, s_b, s_c, *,
                 n_chunks):
    pad = N_PAST_CHUNKS * CHUNK
    T = n_chunks * CHUNK
    n_steps = T // QBLK
    head0_t = lax.broadcasted_iota(jnp.int32, (T, LANES), 1) < ATTN_HEAD_DIM
    m0 = jnp.where(head0_t, 1.0, 0.0).astype(BF16)
    m1 = jnp.where(head0_t, 0.0, 1.0).astype(BF16)

    kpad[0:pad, :] = jnp.zeros((pad, LANES), BF16)
    kpad[pad:pad + T, :] = k_ref[0]
    v = v_ref[0]
    vpa[0:pad, :] = jnp.zeros((pad, 2 * LANES), BF16)
    vpb[0:pad, :] = jnp.zeros((pad, 2 * LANES), BF16)
    vpa[pad:pad + T, 0:LANES] = v * m0
    vpb[pad:pad + T, 0:LANES] = v * m1
    vpa[pad:pad + T, LANES:2 * LANES] = m0
    vpb[pad:pad + T, LANES:2 * LANES] = m1

    @pl.when(pl.program_id(1) == 0)
    def _():
        key = lax.broadcasted_iota(jnp.int32, (KBLK, QBLK), 0)
        qry = lax.broadcasted_iota(jnp.int32, (KBLK, QBLK), 1)
        in_band = ((qry < CHUNK) & (key < BAND)) | ((qry >= CHUNK) & (key >= CHUNK))
        for hh in range(2):
            g = jnp.broadcast_to(bias_ref[0, hh:hh + 1, :], (QBLK, BIAS_W))
            t = pltpu.roll(g, BIAS_W - QBLK, 1, stride=1, stride_axis=0)[:, :KBLK].T
            t = jnp.where(in_band, t, NEG_BIG)
            for var in range(N_PAD_STEPS + 1):
                first_real = pad - var * QBLK
                tv = jnp.where(key >= first_real, t, NEG_BIG) if first_real > 0 else t
                tab_t[var, hh] = tv.astype(BF16)

    lane = lax.broadcasted_iota(jnp.int32, (QBLK, LANES), 1)
    head0 = lane < ATTN_HEAD_DIM
    eye = (lax.broadcasted_iota(jnp.int32, (QBLK, QBLK), 0)
           == lax.broadcasted_iota(jnp.int32, (QBLK, QBLK), 1)).astype(BF16)
    gain = gain_ref[0]
    nt = (((1,), (1,)), ((), ()))

    def scores(m, dst):
        row0 = pl.multiple_of(m * QBLK, QBLK)
        q = q_ref[0, pl.ds(row0, QBLK), :]
        zq = jnp.zeros_like(q)
        kb = kpad[pl.ds(row0, KBLK), :]
        var = jnp.minimum(m, N_PAD_STEPS)
        for hh in range(2):
            lhs = jnp.concatenate([jnp.where(head0, q, zq) if hh == 0 else jnp.where(head0, zq, q), eye], axis=1)
            rhs = jnp.concatenate([kb, tab_t[var, hh]], axis=1)
            dst[hh * QBLK:(hh + 1) * QBLK, :] = lax.dot_general(lhs, rhs, nt, preferred_element_type=F32)

    def finish(m, src):
        row0 = pl.multiple_of(m * QBLK, QBLK)
        s = src[...]
        p = jnp.exp2(s - jnp.max(s, axis=-1, keepdims=True)).astype(BF16)
        pv = jnp.dot(jnp.concatenate([p[0:QBLK], p[QBLK:2 * QBLK]], axis=1),
                     jnp.concatenate([vpa[pl.ds(row0, KBLK), :], vpb[pl.ds(row0, KBLK), :]], axis=0),
                     preferred_element_type=F32)
        o = pv[:, 0:LANES] * (1.0 / pv[:, LANES:2 * LANES])
        o2 = o * o
        ms0 = jnp.sum(jnp.where(head0, o2, 0.0), axis=-1, keepdims=True) / ATTN_HEAD_DIM
        ms1 = jnp.sum(jnp.where(head0, 0.0, o2), axis=-1, keepdims=True) / ATTN_HEAD_DIM
        y = o * lax.rsqrt(jnp.where(head0, ms0, ms1) + EPS) * gain
        o_ref[0, pl.ds(row0, QBLK), :] = y.astype(o_ref.dtype)

    bufs = (s_a, s_b, s_c)
    depth = len(bufs)
    ahead = depth - 1
    for m in range(ahead):
        scores(m, bufs[m % depth])
    for m in range(n_steps):
        if m + ahead < n_steps:
            scores(m + ahead, bufs[(m + ahead) % depth])
        finish(m, bufs[m % depth])


def _attention(proj_a, bias_vec, attn_gain):
    B, T, W3 = proj_a.shape
    W = W3 // 3
    n_pairs = W // LANES
    assert T % (2 * QBLK) == 0
    t_pad = T + N_PAST_CHUNKS * CHUNK
    return pl.pallas_call(
        functools.partial(_attn_kernel, n_chunks=T // CHUNK),
        out_shape=jax.ShapeDtypeStruct((B, T, W), BF16),
        grid=(n_pairs, B),
        in_specs=[pl.BlockSpec((1, T, LANES), lambda h, b: (b, 0, h)),
                  pl.BlockSpec((1, T, LANES), lambda h, b: (b, 0, n_pairs + h)),
                  pl.BlockSpec((1, T, LANES), lambda h, b: (b, 0, 2 * n_pairs + h)),
                  pl.BlockSpec((1, 2, BIAS_W), lambda h, b: (h, 0, 0)),
                  pl.BlockSpec((1, 1, LANES), lambda h, b: (h, 0, 0))],
        out_specs=pl.BlockSpec((1, T, LANES), lambda h, b: (b, 0, h)),
        scratch_shapes=[pltpu.VMEM((t_pad, LANES), BF16),
                        pltpu.VMEM((t_pad, 2 * LANES), BF16),
                        pltpu.VMEM((t_pad, 2 * LANES), BF16),
                        pltpu.VMEM((N_PAD_STEPS + 1, 2, KBLK, QBLK), BF16),
                        pltpu.VMEM((2 * QBLK, KBLK), F32),
                        pltpu.VMEM((2 * QBLK, KBLK), F32),
                        pltpu.VMEM((2 * QBLK, KBLK), F32)],
        compiler_params=_cparams(("parallel", "arbitrary"), 32),
        name="chunk_attention",
    )(proj_a, proj_a, proj_a, bias_vec, attn_gain)


HG_CHUNK = 256
HG_LEVELS = (128, 64, 32, 16, 8)
HG_DIAG = SUBLANES


def _split3(x):
    hi = x.astype(BF16)
    r1 = x - hi.astype(F32)
    mid = r1.astype(BF16)
    lo = (r1 - mid.astype(F32)).astype(BF16)
    return hi, mid, lo


def _hgrn_kernel(q_ref, f_ref, i_ref, g_ref, lbl_ref, gn_ref, o_ref, *, n_steps):
    C = HG_CHUNK
    H2 = C // 2
    Dk = REC_HEAD_DIM
    nt = (((1,), (1,)), ((), ()))
    lbl = lbl_ref[0]
    e = jnp.exp(lbl - jnp.max(lbl, axis=0, keepdims=True))
    lb = e[0:1, :] / jnp.sum(e, axis=0, keepdims=True)
    gn = gn_ref[...]

    r = lax.broadcasted_iota(jnp.int32, (C, C), 0)
    s = lax.broadcasted_iota(jnp.int32, (C, C), 1)
    tril = (s <= r).astype(BF16)
    rh = lax.broadcasted_iota(jnp.int32, (H2, H2), 0)
    sh = lax.broadcasted_iota(jnp.int32, (H2, H2), 1)
    lvl_mask = {m: ((rh // (2 * m)) == (sh // (2 * m))) & (((rh // m) % 2) == 1) & (((sh // m) % 2) == 0)
                for m in HG_LEVELS[1:]}
    sub = lax.broadcasted_iota(jnp.int32, (C // SUBLANES, SUBLANES, Dk), 1)

    def roll8(x, d):
        return pltpu.roll(x.reshape(C // SUBLANES, SUBLANES, Dk), d, 1)

    def step(n, st):
        row0 = pl.multiple_of(n * C, C)
        rows = pl.ds(row0, C)
        f = lb + (1.0 - lb) * _sigmoid(f_ref[0, rows, :])
        logf = jnp.log(f)
        kk = 1.0 - f
        qq = _silu(q_ref[0, rows, :])
        ii = i_ref[0, rows, :]
        ii_bf = ii.astype(BF16)

        hi, mid, lo = _split3(logf)
        bb = jnp.dot(tril, jnp.concatenate([hi, mid, lo], axis=1), preferred_element_type=F32)
        b = bb[:, 0:Dk] + bb[:, Dk:2 * Dk] + bb[:, 2 * Dk:3 * Dk]
        b_last = b[C - 1:C, :]

        def level_z(m):
            bm = jnp.concatenate(
                [jnp.broadcast_to(b[p + m - 1:p + m, :], (2 * m, Dk)) for p in range(0, C, 2 * m)], axis=0)
            src = jnp.concatenate(
                [kk[p:p + m] if (p // m) % 2 == 0 else qq[p:p + m] for p in range(0, C, m)], axis=0)
            return (src * jnp.exp(-jnp.abs(b - bm))).astype(BF16)

        z = level_z(HG_LEVELS[0])
        a_lo = lax.dot_general(z[H2:], z[:H2], nt, preferred_element_type=F32)
        a_d0 = jnp.zeros((H2, H2), F32)
        a_d1 = jnp.zeros((H2, H2), F32)
        for m in HG_LEVELS[1:]:
            z = level_z(m)
            g0 = lax.dot_general(z[:H2], z[:H2], nt, preferred_element_type=F32)
            g1 = lax.dot_general(z[H2:], z[H2:], nt, preferred_element_type=F32)
            a_d0 = jnp.where(lvl_mask[m], g0, a_d0)
            a_d1 = jnp.where(lvl_mask[m], g1, a_d1)
        o_top = jnp.dot(a_d0.astype(BF16), ii_bf[:H2], preferred_element_type=F32)
        o_bot = jnp.dot(jnp.concatenate([a_lo, a_d1], axis=1).astype(BF16), ii_bf, preferred_element_type=F32)
        o = jnp.concatenate([o_top, o_bot], axis=0)

        k3 = kk.reshape(C // SUBLANES, SUBLANES, Dk)
        b3 = b.reshape(C // SUBLANES, SUBLANES, Dk)
        q3 = qq.reshape(C // SUBLANES, SUBLANES, Dk)
        i3 = ii.reshape(C // SUBLANES, SUBLANES, Dk)
        o3 = jnp.sum(q3 * k3, axis=-1, keepdims=True) * i3
        for dlt in range(1, HG_DIAG):
            w = q3 * roll8(kk, dlt) * jnp.exp(b3 - roll8(b, dlt))
            w = jnp.where(sub >= dlt, w, 0.0)
            o3 = o3 + jnp.sum(w, axis=-1, keepdims=True) * roll8(ii, dlt)
        o = o + o3.reshape(C, Dk)

        qe = (qq * jnp.exp(b)).astype(BF16)
        o = o + lax.dot_general(qe, st.astype(BF16), nt, preferred_element_type=F32)
        ke = (kk * jnp.exp(b_last - b)).astype(BF16)
        st = st * jnp.exp(b_last) + lax.dot_general(
            ii_bf, ke, (((0,), (0,)), ((), ())), preferred_element_type=F32)

        ms = jnp.mean(o * o, axis=-1, keepdims=True)
        y = o * lax.rsqrt(ms + EPS) * gn
        y = y * _silu(g_ref[0, rows, :])
        o_ref[0, rows, :] = y.astype(o_ref.dtype)
        return st

    lax.fori_loop(0, n_steps, step, jnp.zeros((Dk, Dk), F32))


def _hgrn(proj_b, lb_logits_h, gnorm_g):
    B, T, W4 = proj_b.shape
    W = W4 // 4
    H = W // REC_HEAD_DIM
    n_slots = lb_logits_h.shape[1]
    blk = lambda off: pl.BlockSpec((1, T, REC_HEAD_DIM), lambda b, h, off=off: (b, 0, off * H + h))
    return pl.pallas_call(
        functools.partial(_hgrn_kernel, n_steps=T // HG_CHUNK),
        out_shape=jax.ShapeDtypeStruct((B, T, W), BF16),
        grid=(B, H),
        in_specs=[blk(0), blk(1), blk(2), blk(3),
                  pl.BlockSpec((1, n_slots, REC_HEAD_DIM), lambda b, h: (h, 0, 0)),
                  pl.BlockSpec((1, REC_HEAD_DIM), lambda b, h: (0, 0))],
        out_specs=pl.BlockSpec((1, T, REC_HEAD_DIM), lambda b, h: (b, 0, h)),
        compiler_params=_cparams(("parallel", "parallel"), 32),
        name="hgrn2",
    )(proj_b, proj_b, proj_b, proj_b, lb_logits_h, gnorm_g)


def _outproj_kernel(oa_ref, ob_ref, w_ref, x_ref, mod_ref, g_ref, b_ref, x1_ref, h2_ref, *, alpha, sub):
    wa = oa_ref.shape[-1]
    gate1 = mod_ref[0, 2:3, :]
    shift2 = mod_ref[0, 3:4, :]
    scale2 = mod_ref[0, 4:5, :]
    for r0 in range(0, x_ref.shape[1], sub):
        rs = slice(r0, r0 + sub)
        mix = jnp.dot(oa_ref[0, rs, :], w_ref[0:wa, :], preferred_element_type=F32)
        mix = mix + jnp.dot(ob_ref[0, rs, :], w_ref[wa:, :], preferred_element_type=F32)
        x1 = _ln_rows(alpha * x_ref[0, rs, :] + gate1 * mix) * g_ref[...] + b_ref[...]
        x1_ref[0, rs, :] = x1
        h2_ref[0, rs, :] = (_ln_rows(x1) * (1.0 + scale2) + shift2).astype(h2_ref.dtype)


def _outproj(o_a, o_b, w_o, x, mod3, ln_g, ln_b, alpha, tm=512, sub=128):
    B, T, D = x.shape
    Wa, Wb = o_a.shape[-1], o_b.shape[-1]
    return pl.pallas_call(
        functools.partial(_outproj_kernel, alpha=alpha, sub=sub),
        out_shape=(jax.ShapeDtypeStruct((B, T, D), F32), jax.ShapeDtypeStruct((B, T, D), BF16)),
        grid=(B, T // tm),
        in_specs=[pl.BlockSpec((1, tm, Wa), lambda b, i: (b, i, 0)),
                  pl.BlockSpec((1, tm, Wb), lambda b, i: (b, i, 0)),
                  pl.BlockSpec((Wa + Wb, D), lambda b, i: (0, 0)),
                  pl.BlockSpec((1, tm, D), lambda b, i: (b, i, 0)),
                  pl.BlockSpec((1, N_MOD, D), lambda b, i: (b, 0, 0)),
                  pl.BlockSpec((1, D), lambda b, i: (0, 0)),
                  pl.BlockSpec((1, D), lambda b, i: (0, 0))],
        out_specs=(pl.BlockSpec((1, tm, D), lambda b, i: (b, i, 0)),
                   pl.BlockSpec((1, tm, D), lambda b, i: (b, i, 0))),
        compiler_params=_cparams(("parallel", "parallel"), 48),
        name="out_proj_ln1",
    )(o_a, o_b, w_o, x, mod3, ln_g, ln_b)


def _ffn_kernel(h_ref, wg_ref, wu_ref, wo_ref, x_ref, mod_ref, g_ref, b_ref, o_ref, *, alpha):
    f = pl.program_id(2)

    @pl.when(f == 0)
    def _():
        o_ref[0] = jnp.zeros(o_ref.shape[1:], F32)

    h = h_ref[0]
    gate = jnp.dot(h, wg_ref[...], preferred_element_type=F32)
    up = jnp.dot(h, wu_ref[...], preferred_element_type=F32)
    act = (_silu(gate) * up).astype(BF16)
    o_ref[0] += jnp.dot(act, wo_ref[...], preferred_element_type=F32)

    @pl.when(f == pl.num_programs(2) - 1)
    def _():
        gate2 = mod_ref[0, 5:6, :]
        o_ref[0] = _ln_rows(alpha * x_ref[0] + gate2 * o_ref[0]) * g_ref[...] + b_ref[...]


def _ffn(h2, w_in, w_out, x1, mod3, ln_g, ln_b, alpha, tm=1024, tf=256):
    B, T, D = x1.shape
    F = w_out.shape[0]
    nf = F // tf
    return pl.pallas_call(
        functools.partial(_ffn_kernel, alpha=alpha),
        out_shape=jax.ShapeDtypeStruct((B, T, D), F32),
        grid=(B, T // tm, nf),
        in_specs=[pl.BlockSpec((1, tm, D), lambda b, i, f: (b, i, 0)),
                  pl.BlockSpec((D, tf), lambda b, i, f: (0, f)),
                  pl.BlockSpec((D, tf), lambda b, i, f: (0, nf + f)),
                  pl.BlockSpec((tf, D), lambda b, i, f: (f, 0)),
                  pl.BlockSpec((1, tm, D), lambda b, i, f: (b, i, 0)),
                  pl.BlockSpec((1, N_MOD, D), lambda b, i, f: (b, 0, 0)),
                  pl.BlockSpec((1, D), lambda b, i, f: (0, 0)),
                  pl.BlockSpec((1, D), lambda b, i, f: (0, 0))],
        out_specs=pl.BlockSpec((1, tm, D), lambda b, i, f: (b, i, 0)),
        compiler_params=_cparams(("parallel", "parallel", "arbitrary"), 56),
        name="swiglu_ffn_ln2",
    )(h2, w_in, w_in, w_out, x1, mod3, ln_g, ln_b)


def _bias_vectors(rel_bias):
    H, n_rel = rel_bias.shape
    max_rel = (n_rel - 1) // 2
    u = jnp.arange(BIAS_W)
    idx = jnp.clip(KBLK - u, -max_rel, max_rel) + max_rel
    return rel_bias[:, idx]


def kernel(x, c, w_ada, b_ada, w_in, rel_bias, attn_norm_g, lb_logits, gnorm_g, w_o,
           ln1_g, ln1_b, w_ffn_in, w_ffn_out, ln2_g, ln2_b):
    B, T, D = x.shape
    depth = w_ada.shape[0]
    alpha = (2 * depth) ** 0.25
    attn_w = attn_norm_g.shape[1]
    rec_w = lb_logits.shape[1]
    n_slots = lb_logits.shape[0]
    rec_heads = rec_w // REC_HEAD_DIM
    assert depth == 1 and n_slots == depth + 1
    for layer in range(depth):
        mod3 = _mod(c, w_ada[layer], b_ada[layer]).reshape(B, N_MOD, D)
        h1 = _ln_mod(x, mod3, 0).reshape(B * T, D)
        q_scale = jnp.where(jnp.arange(w_in.shape[2]) < attn_w, ATTN_HEAD_DIM ** -0.5 * LOG2E, 1.0)
        w_in_bf = (w_in[layer] * q_scale).astype(BF16)
        proj_a = _matmul(h1, w_in_bf, 0, 3 * attn_w, BF16).reshape(B, T, 3 * attn_w)
        proj_b = _matmul(h1, w_in_bf, 3 * attn_w // 1024, 4 * rec_w, F32).reshape(B, T, 4 * rec_w)
        bias_vec = (_bias_vectors(rel_bias[layer]) * LOG2E).reshape(-1, 2, BIAS_W)
        o_a = _attention(proj_a, bias_vec, attn_norm_g[layer].reshape(-1, 1, LANES))
        lbl = lb_logits.reshape(n_slots, rec_heads, REC_HEAD_DIM).transpose(1, 0, 2)
        o_b = _hgrn(proj_b, lbl, gnorm_g[layer].reshape(1, REC_HEAD_DIM))
        x, h2 = _outproj(o_a, o_b, w_o[layer].astype(BF16), x, mod3,
                         ln1_g[layer].reshape(1, D), ln1_b[layer].reshape(1, D), alpha)
        x = _ffn(h2, w_ffn_in[layer].astype(BF16), w_ffn_out[layer].astype(BF16), x, mod3,
                 ln2_g[layer].reshape(1, D), ln2_b[layer].reshape(1, D), alpha)
    return x
```

```python
import functools

import jax
import jax.numpy as jnp
from jax import lax
from jax.experimental import pallas as pl
from jax.experimental.pallas import tpu as pltpu

F32 = jnp.float32
BF16 = jnp.bfloat16

CHUNK = 64
N_PAST_CHUNKS = 8
BAND = (N_PAST_CHUNKS + 1) * CHUNK
ATTN_HEAD_DIM = 64
REC_HEAD_DIM = 128
N_MOD = 6
EPS = 1e-5
LANES = 128
SUBLANES = 8
QBLK = 2 * CHUNK
KBLK = BAND + CHUNK
BIAS_W = KBLK + QBLK
N_PAD_STEPS = N_PAST_CHUNKS * CHUNK // QBLK

MIB = 1024 * 1024


def _cparams(sem, vmem_mib):
    return pltpu.CompilerParams(dimension_semantics=sem, vmem_limit_bytes=vmem_mib * MIB)


def _sigmoid(x):
    return 1.0 / (1.0 + jnp.exp(-x))


def _silu(x):
    return x * _sigmoid(x)


def _silu_tanh(x):
    h = 0.5 * x
    return h + h * jnp.tanh(h)


def _ln_rows(x):
    mu = jnp.mean(x, axis=-1, keepdims=True)
    xc = x - mu
    var = jnp.mean(xc * xc, axis=-1, keepdims=True)
    return xc * lax.rsqrt(var + EPS)


def _mod_kernel(ct_ref, w_ref, b_ref, o_ref):
    ct = ct_ref[...]
    cat = _silu(ct)
    w = w_ref[...]
    rows = [jnp.sum(w * cat[:, b:b + 1], axis=0, keepdims=True) for b in range(ct.shape[1])]
    o_ref[...] = jnp.concatenate(rows, axis=0) + b_ref[...]


def _mod(c, w_ada, b_ada, tn=512):
    B, D = c.shape
    N = w_ada.shape[1]
    return pl.pallas_call(
        _mod_kernel,
        out_shape=jax.ShapeDtypeStruct((B, N), F32),
        grid=(N // tn,),
        in_specs=[pl.BlockSpec((D, B), lambda j: (0, 0)),
                  pl.BlockSpec((D, tn), lambda j: (0, j)),
                  pl.BlockSpec((1, tn), lambda j: (0, j))],
        out_specs=pl.BlockSpec((B, tn), lambda j: (0, j)),
        compiler_params=_cparams(("parallel",), 32),
        name="adaln_mod",
    )(c.T, w_ada, b_ada.reshape(1, N))


def _ln_mod_kernel(x_ref, mod_ref, o_ref, *, shift_row):
    y = _ln_rows(x_ref[0])
    shift = mod_ref[0, shift_row:shift_row + 1, :]
    scale = mod_ref[0, shift_row + 1:shift_row + 2, :]
    o_ref[0] = (y * (1.0 + scale) + shift).astype(o_ref.dtype)


def _ln_mod(x, mod3, shift_row, tm=512):
    B, T, D = x.shape
    return pl.pallas_call(
        functools.partial(_ln_mod_kernel, shift_row=shift_row),
        out_shape=jax.ShapeDtypeStruct((B, T, D), BF16),
        grid=(B, T // tm),
        in_specs=[pl.BlockSpec((1, tm, D), lambda b, i: (b, i, 0)),
                  pl.BlockSpec((1, N_MOD, D), lambda b, i: (b, 0, 0))],
        out_specs=pl.BlockSpec((1, tm, D), lambda b, i: (b, i, 0)),
        compiler_params=_cparams(("parallel", "parallel"), 32),
        name="ln_modulate",
    )(x, mod3)


def _matmul_kernel(a_ref, w_ref, o_ref):
    o_ref[...] = jnp.dot(a_ref[...], w_ref[...], preferred_element_type=F32).astype(o_ref.dtype)


def _matmul(a, w, col_block0, n_out, out_dtype, tm=1024, tn=1024):
    M, K = a.shape
    return pl.pallas_call(
        _matmul_kernel,
        out_shape=jax.ShapeDtypeStruct((M, n_out), out_dtype),
        grid=(M // tm, n_out // tn),
        in_specs=[pl.BlockSpec((tm, K), lambda i, j: (i, 0)),
                  pl.BlockSpec((K, tn), lambda i, j: (0, j + col_block0))],
        out_specs=pl.BlockSpec((tm, tn), lambda i, j: (i, j)),
        compiler_params=_cparams(("parallel", "arbitrary"), 40),
        name="in_proj",
    )(a, w)


NEG_BIG = -1e30
LOG2E = 1.4426950408889634


def _attn_kernel(q_ref, k_ref, v_ref, bias_ref, gain_ref, o_ref, kpad, vpa, vpb, tab_t, s_a, s_b, s_c, *,
                 n_chunks):
    pad = N_PAST_CHUNKS * CHUNK
    T = n_chunks * CHUNK
    n_steps = T // QBLK
    head0_t = lax.broadcasted_iota(jnp.int32, (T, LANES), 1) < ATTN_HEAD_DIM
    m0 = jnp.where(head0_t, 1.0, 0.0).astype(BF16)
    m1 = jnp.where(head0_t, 0.0, 1.0).astype(BF16)

    kpad[0:pad, :] = jnp.zeros((pad, LANES), BF16)
    kpad[pad:pad + T, :] = k_ref[0]
    v = v_ref[0]
    vpa[0:pad, :] = jnp.zeros((pad, 2 * LANES), BF16)
    vpb[0:pad, :] = jnp.zeros((pad, 2 * LANES), BF16)
    vpa[pad:pad + T, 0:LANES] = v * m0
    vpb[pad:pad + T, 0:LANES] = v * m1
    vpa[pad:pad + T, LANES:2 * LANES] = m0
    vpb[pad:pad + T, LANES:2 * LANES] = m1

    @pl.when(pl.program_id(1) == 0)
    def _():
        key = lax.broadcasted_iota(jnp.int32, (KBLK, QBLK), 0)
        qry = lax.broadcasted_iota(jnp.int32, (KBLK, QBLK), 1)
        in_band = ((qry < CHUNK) & (key < BAND)) | ((qry >= CHUNK) & (key >= CHUNK))
        for hh in range(2):
            g = jnp.broadcast_to(bias_ref[0, hh:hh + 1, :], (QBLK, BIAS_W))
            t = pltpu.roll(g, BIAS_W - QBLK, 1, stride=1, stride_axis=0)[:, :KBLK].T
            t = jnp.where(in_band, t, NEG_BIG)
            for var in range(N_PAD_STEPS + 1):
                first_real = pad - var * QBLK
                tv = jnp.where(key >= first_real, t, NEG_BIG) if first_real > 0 else t
                tab_t[var, hh] = tv.astype(BF16)

    lane = lax.broadcasted_iota(jnp.int32, (QBLK, LANES), 1)
    head0 = lane < ATTN_HEAD_DIM
    eye = (lax.broadcasted_iota(jnp.int32, (QBLK, QBLK), 0)
           == lax.broadcasted_iota(jnp.int32, (QBLK, QBLK), 1)).astype(BF16)
    gain = gain_ref[0]
    nt = (((1,), (1,)), ((), ()))

    def scores(m, dst):
        row0 = pl.multiple_of(m * QBLK, QBLK)
        q = q_ref[0, pl.ds(row0, QBLK), :]
        zq = jnp.zeros_like(q)
        kb = kpad[pl.ds(row0, KBLK), :]
        var = jnp.minimum(m, N_PAD_STEPS)
        for hh in range(2):
            lhs = jnp.concatenate([jnp.where(head0, q, zq) if hh == 0 else jnp.where(head0, zq, q), eye], axis=1)
            rhs = jnp.concatenate([kb, tab_t[var, hh]], axis=1)
            dst[hh * QBLK:(hh + 1) * QBLK, :] = lax.dot_general(lhs, rhs, nt, preferred_element_type=F32)

    def finish(m, src):
        row0 = pl.multiple_of(m * QBLK, QBLK)
        s = src[...]
        p = jnp.exp2(s - jnp.max(s, axis=-1, keepdims=True)).astype(BF16)
        pv = jnp.dot(jnp.concatenate([p[0:QBLK], p[QBLK:2 * QBLK]], axis=1),
                     jnp.concatenate([vpa[pl.ds(row0, KBLK), :], vpb[pl.ds(row0, KBLK), :]], axis=0),
                     preferred_element_type=F32)
        o = pv[:, 0:LANES] * (1.0 / pv[:, LANES:2 * LANES])
        o2 = o * o
        ms0 = jnp.sum(jnp.where(head0, o2, 0.0), axis=-1, keepdims=True) / ATTN_HEAD_DIM
        ms1 = jnp.sum(jnp.where(head0, 0.0, o2), axis=-1, keepdims=True) / ATTN_HEAD_DIM
        y = o * lax.rsqrt(jnp.where(head0, ms0, ms1) + EPS) * gain
        o_ref[0, pl.ds(row0, QBLK), :] = y.astype(o_ref.dtype)

    bufs = (s_a, s_b, s_c)
    depth = len(bufs)
    ahead = depth - 1
    for m in range(ahead):
        scores(m, bufs[m % depth])
    for m in range(n_steps):
        if m + ahead < n_steps:
            scores(m + ahead, bufs[(m + ahead) % depth])
        finish(m, bufs[m % depth])


def _attention(proj_a, bias_vec, attn_gain):
    B, T, W3 = proj_a.shape
    W = W3 // 3
    n_pairs = W // LANES
    assert T % (2 * QBLK) == 0
    t_pad = T + N_PAST_CHUNKS * CHUNK
    return pl.pallas_call(
        functools.partial(_attn_kernel, n_chunks=T // CHUNK),
        out_shape=jax.ShapeDtypeStruct((B, T, W), BF16),
        grid=(n_pairs, B),
        in_specs=[pl.BlockSpec((1, T, LANES), lambda h, b: (b, 0, h)),
                  pl.BlockSpec((1, T, LANES), lambda h, b: (b, 0, n_pairs + h)),
                  pl.BlockSpec((1, T, LANES), lambda h, b: (b, 0, 2 * n_pairs + h)),
                  pl.BlockSpec((1, 2, BIAS_W), lambda h, b: (h, 0, 0)),
                  pl.BlockSpec((1, 1, LANES), lambda h, b: (h, 0, 0))],
        out_specs=pl.BlockSpec((1, T, LANES), lambda h, b: (b, 0, h)),
        scratch_shapes=[pltpu.VMEM((t_pad, LANES), BF16),
                        pltpu.VMEM((t_pad, 2 * LANES), BF16),
                        pltpu.VMEM((t_pad, 2 * LANES), BF16),
                        pltpu.VMEM((N_PAD_STEPS + 1, 2, KBLK, QBLK), BF16),
                        pltpu.VMEM((2 * QBLK, KBLK), F32),
                        pltpu.VMEM((2 * QBLK, KBLK), F32),
                        pltpu.VMEM((2 * QBLK, KBLK), F32)],
        compiler_params=_cparams(("parallel", "arbitrary"), 32),
        name="chunk_attention",
    )(proj_a, proj_a, proj_a, bias_vec, attn_gain)


HG_CHUNK = 256
HG_LEVELS = (128, 64, 32, 16, 8, 4, 2, 1)
assert HG_LEVELS[-1] == 1
HG_UNROLL = 4


def _split3(x):
    hi = x.astype(BF16)
    r1 = x - hi.astype(F32)
    mid = r1.astype(BF16)
    lo = (r1 - mid.astype(F32)).astype(BF16)
    return hi, mid, lo


def _hgrn_kernel(q_ref, f_ref, i_ref, g_ref, lbl_ref, gn_ref, o_ref, *, n_steps):
    C = HG_CHUNK
    H2 = C // 2
    Dk = REC_HEAD_DIM
    nt = (((1,), (1,)), ((), ()))
    lbl = lbl_ref[0]
    e = jnp.exp(lbl - jnp.max(lbl, axis=0, keepdims=True))
    lb = e[0:1, :] / jnp.sum(e, axis=0, keepdims=True)
    c1 = 0.5 * (1.0 - lb)
    gn = gn_ref[...]

    r = lax.broadcasted_iota(jnp.int32, (C, C), 0)
    s = lax.broadcasted_iota(jnp.int32, (C, C), 1)
    tril = (s <= r).astype(BF16)
    rh = lax.broadcasted_iota(jnp.int32, (H2, H2), 0)
    sh = lax.broadcasted_iota(jnp.int32, (H2, H2), 1)
    lvl_mask = {m: ((rh // (2 * m)) == (sh // (2 * m))) & (((rh // m) % 2) == 1) & (((sh // m) % 2) == 0)
                for m in HG_LEVELS[1:]}
    sub = lax.broadcasted_iota(jnp.int32, (C // SUBLANES, SUBLANES, Dk), 1)

    def roll8(x, d):
        return pltpu.roll(x.reshape(C // SUBLANES, SUBLANES, Dk), d, 1)

    def front(n):
        row0 = pl.multiple_of(n * C, C)
        rows = pl.ds(row0, C)
        c1t = c1 * jnp.tanh(0.5 * f_ref[0, rows, :])
        f = (1.0 - c1) + c1t
        kk = c1 - c1t
        qq = _silu_tanh(q_ref[0, rows, :])
        ii = i_ref[0, rows, :]
        hi, mid, lo = _split3(jnp.log2(f))
        bb = jnp.dot(tril, jnp.concatenate([hi, mid, lo], axis=1), preferred_element_type=F32)
        b = bb[:, 0:Dk] + bb[:, Dk:2 * Dk] + bb[:, 2 * Dk:3 * Dk]
        return dict(rows=rows, kk=kk, qq=qq, ii=ii, ii_bf=ii.astype(BF16), b=b)

    def level_z(v, m):
        b, kk, qq = v["b"], v["kk"], v["qq"]
        if m >= SUBLANES:
            parts, srcs = [], []
            for p in range(0, C, 2 * m):
                bm = b[p + m - 1:p + m, :]
                parts += [bm - b[p:p + m], b[p + m:p + 2 * m] - bm]
                srcs += [kk[p:p + m], qq[p + m:p + 2 * m]]
            arg = jnp.concatenate(parts, axis=0)
            src = jnp.concatenate(srcs, axis=0)
        else:
            b3 = b.reshape(C // SUBLANES, SUBLANES, Dk)
            if m == 1:
                bm = jnp.where(sub % 2 == 1, roll8(b, 1), b3)
            else:
                bm = jnp.broadcast_to(b3[:, m - 1:m, :], b3.shape)
                for p in range(2 * m, SUBLANES, 2 * m):
                    bm = jnp.where(sub >= p, jnp.broadcast_to(b3[:, p + m - 1:p + m, :], b3.shape), bm)
            upper = (sub // m) % 2 == 1
            arg = ((b3 - bm) * jnp.where(upper, 1.0, -1.0)).reshape(C, Dk)
            src = jnp.where(upper, qq.reshape(b3.shape), kk.reshape(b3.shape)).reshape(C, Dk)
        return (src * jnp.exp2(arg)).astype(BF16)

    def group(gi, st):
        vs = [front(gi * HG_UNROLL + j) for j in range(HG_UNROLL)]
        a_lo = [None] * HG_UNROLL
        a_d0 = [jnp.zeros((H2, H2), F32)] * HG_UNROLL
        a_d1 = [jnp.zeros((H2, H2), F32)] * HG_UNROLL
        for li, m in enumerate(HG_LEVELS):
            for j, v in enumerate(vs):
                z = level_z(v, m)
                if li == 0:
                    a_lo[j] = lax.dot_general(z[H2:], z[:H2], nt, preferred_element_type=F32)
                else:
                    g0 = lax.dot_general(z[:H2], z[:H2], nt, preferred_element_type=F32)
                    g1 = lax.dot_general(z[H2:], z[H2:], nt, preferred_element_type=F32)
                    a_d0[j] = jnp.where(lvl_mask[m], g0, a_d0[j])
                    a_d1[j] = jnp.where(lvl_mask[m], g1, a_d1[j])
        intra = []
        for j, v in enumerate(vs):
            o_top = jnp.dot(a_d0[j].astype(BF16), v["ii_bf"][:H2], preferred_element_type=F32)
            o_bot = jnp.dot(jnp.concatenate([a_lo[j], a_d1[j]], axis=1).astype(BF16), v["ii_bf"],
                            preferred_element_type=F32)
            o_diag = jnp.sum(v["qq"] * v["kk"], axis=-1, keepdims=True) * v["ii"]
            intra.append(jnp.concatenate([o_top, o_bot], axis=0) + o_diag)
        for j, v in enumerate(vs):
            b = v["b"]
            b_last = b[C - 1:C, :]
            qe = (v["qq"] * jnp.exp2(b)).astype(BF16)
            o = intra[j] + lax.dot_general(qe, st.astype(BF16), nt, preferred_element_type=F32)
            ke = (v["kk"] * jnp.exp2(b_last - b)).astype(BF16)
            st = st * jnp.exp2(b_last) + lax.dot_general(
                v["ii_bf"], ke, (((0,), (0,)), ((), ())), preferred_element_type=F32)
            ms = jnp.mean(o * o, axis=-1, keepdims=True)
            y = o * lax.rsqrt(ms + EPS) * gn
            y = y * _silu_tanh(g_ref[0, v["rows"], :])
            o_ref[0, v["rows"], :] = y.astype(o_ref.dtype)
        return st

    lax.fori_loop(0, n_steps // HG_UNROLL, group, jnp.zeros((Dk, Dk), F32))


def _hgrn(proj_b, lb_logits_h, gnorm_g):
    B, T, W4 = proj_b.shape
    W = W4 // 4
    H = W // REC_HEAD_DIM
    n_slots = lb_logits_h.shape[1]
    blk = lambda off: pl.BlockSpec((1, T, REC_HEAD_DIM), lambda b, h, off=off: (b, 0, off * H + h))
    return pl.pallas_call(
        functools.partial(_hgrn_kernel, n_steps=T // HG_CHUNK),
        out_shape=jax.ShapeDtypeStruct((B, T, W), BF16),
        grid=(B, H),
        in_specs=[blk(0), blk(1), blk(2), blk(3),
                  pl.BlockSpec((1, n_slots, REC_HEAD_DIM), lambda b, h: (h, 0, 0)),
                  pl.BlockSpec((1, REC_HEAD_DIM), lambda b, h: (0, 0))],
        out_specs=pl.BlockSpec((1, T, REC_HEAD_DIM), lambda b, h: (b, 0, h)),
        compiler_params=_cparams(("parallel", "parallel"), 32),
        name="hgrn2",
    )(proj_b, proj_b, proj_b, proj_b, lb_logits_h, gnorm_g)


def _outproj_kernel(oa_ref, ob_ref, w_ref, x_ref, mod_ref, g_ref, b_ref, x1_ref, h2_ref, *, alpha, sub):
    wa = oa_ref.shape[-1]
    gate1 = mod_ref[0, 2:3, :]
    shift2 = mod_ref[0, 3:4, :]
    scale2 = mod_ref[0, 4:5, :]
    for r0 in range(0, x_ref.shape[1], sub):
        rs = slice(r0, r0 + sub)
        mix = jnp.dot(oa_ref[0, rs, :], w_ref[0:wa, :], preferred_element_type=F32)
        mix = mix + jnp.dot(ob_ref[0, rs, :], w_ref[wa:, :], preferred_element_type=F32)
        x1 = _ln_rows(alpha * x_ref[0, rs, :] + gate1 * mix) * g_ref[...] + b_ref[...]
        x1_ref[0, rs, :] = x1
        h2_ref[0, rs, :] = (_ln_rows(x1) * (1.0 + scale2) + shift2).astype(h2_ref.dtype)


def _outproj(o_a, o_b, w_o, x, mod3, ln_g, ln_b, alpha, tm=512, sub=128):
    B, T, D = x.shape
    Wa, Wb = o_a.shape[-1], o_b.shape[-1]
    return pl.pallas_call(
        functools.partial(_outproj_kernel, alpha=alpha, sub=sub),
        out_shape=(jax.ShapeDtypeStruct((B, T, D), F32), jax.ShapeDtypeStruct((B, T, D), BF16)),
        grid=(B, T // tm),
        in_specs=[pl.BlockSpec((1, tm, Wa), lambda b, i: (b, i, 0)),
                  pl.BlockSpec((1, tm, Wb), lambda b, i: (b, i, 0)),
                  pl.BlockSpec((Wa + Wb, D), lambda b, i: (0, 0)),
                  pl.BlockSpec((1, tm, D), lambda b, i: (b, i, 0)),
                  pl.BlockSpec((1, N_MOD, D), lambda b, i: (b, 0, 0)),
                  pl.BlockSpec((1, D), lambda b, i: (0, 0)),
                  pl.BlockSpec((1, D), lambda b, i: (0, 0))],
        out_specs=(pl.BlockSpec((1, tm, D), lambda b, i: (b, i, 0)),
                   pl.BlockSpec((1, tm, D), lambda b, i: (b, i, 0))),
        compiler_params=_cparams(("parallel", "parallel"), 48),
        name="out_proj_ln1",
    )(o_a, o_b, w_o, x, mod3, ln_g, ln_b)


def _ffn_kernel(h_ref, wg_ref, wu_ref, wo_ref, x_ref, mod_ref, g_ref, b_ref, o_ref, *, alpha):
    f = pl.program_id(2)

    @pl.when(f == 0)
    def _():
        o_ref[0] = jnp.zeros(o_ref.shape[1:], F32)

    h = h_ref[0]
    gate = jnp.dot(h, wg_ref[...], preferred_element_type=F32)
    up = jnp.dot(h, wu_ref[...], preferred_element_type=F32)
    act = (_silu(gate) * up).astype(BF16)
    o_ref[0] += jnp.dot(act, wo_ref[...], preferred_element_type=F32)

    @pl.when(f == pl.num_programs(2) - 1)
    def _():
        gate2 = mod_ref[0, 5:6, :]
        o_ref[0] = _ln_rows(alpha * x_ref[0] + gate2 * o_ref[0]) * g_ref[...] + b_ref[...]


def _ffn(h2, w_in, w_out, x1, mod3, ln_g, ln_b, alpha, tm=1024, tf=256):
    B, T, D = x1.shape
    F = w_out.shape[0]
    nf = F // tf
    return pl.pallas_call(
        functools.partial(_ffn_kernel, alpha=alpha),
        out_shape=jax.ShapeDtypeStruct((B, T, D), F32),
        grid=(B, T // tm, nf),
        in_specs=[pl.BlockSpec((1, tm, D), lambda b, i, f: (b, i, 0)),
                  pl.BlockSpec((D, tf), lambda b, i, f: (0, f)),
                  pl.BlockSpec((D, tf), lambda b, i, f: (0, nf + f)),
                  pl.BlockSpec((tf, D), lambda b, i, f: (f, 0)),
                  pl.BlockSpec((1, tm, D), lambda b, i, f: (b, i, 0)),
                  pl.BlockSpec((1, N_MOD, D), lambda b, i, f: (b, 0, 0)),
                  pl.BlockSpec((1, D), lambda b, i, f: (0, 0)),
                  pl.BlockSpec((1, D), lambda b, i, f: (0, 0))],
        out_specs=pl.BlockSpec((1, tm, D), lambda b, i, f: (b, i, 0)),
        compiler_params=_cparams(("parallel", "parallel", "arbitrary"), 56),
        name="swiglu_ffn_ln2",
    )(h2, w_in, w_in, w_out, x1, mod3, ln_g, ln_b)


def _bias_vectors(rel_bias):
    H, n_rel = rel_bias.shape
    max_rel = (n_rel - 1) // 2
    u = jnp.arange(BIAS_W)
    idx = jnp.clip(KBLK - u, -max_rel, max_rel) + max_rel
    return rel_bias[:, idx]


def kernel(x, c, w_ada, b_ada, w_in, rel_bias, attn_norm_g, lb_logits, gnorm_g, w_o,
           ln1_g, ln1_b, w_ffn_in, w_ffn_out, ln2_g, ln2_b):
    B, T, D = x.shape
    depth = w_ada.shape[0]
    alpha = (2 * depth) ** 0.25
    attn_w = attn_norm_g.shape[1]
    rec_w = lb_logits.shape[1]
    n_slots = lb_logits.shape[0]
    rec_heads = rec_w // REC_HEAD_DIM
    assert depth == 1 and n_slots == depth + 1
    for layer in range(depth):
        mod3 = _mod(c, w_ada[layer], b_ada[layer]).reshape(B, N_MOD, D)
        h1 = _ln_mod(x, mod3, 0).reshape(B * T, D)
        q_scale = jnp.where(jnp.arange(w_in.shape[2]) < attn_w, ATTN_HEAD_DIM ** -0.5 * LOG2E, 1.0)
        w_in_bf = (w_in[layer] * q_scale).astype(BF16)
        proj_a = _matmul(h1, w_in_bf, 0, 3 * attn_w, BF16).reshape(B, T, 3 * attn_w)
        proj_b = _matmul(h1, w_in_bf, 3 * attn_w // 1024, 4 * rec_w, F32).reshape(B, T, 4 * rec_w)
        bias_vec = (_bias_vectors(rel_bias[layer]) * LOG2E).reshape(-1, 2, BIAS_W)
        o_a = _attention(proj_a, bias_vec, attn_norm_g[layer].reshape(-1, 1, LANES))
        lbl = lb_logits.reshape(n_slots, rec_heads, REC_HEAD_DIM).transpose(1, 0, 2)
        o_b = _hgrn(proj_b, lbl, gnorm_g[layer].reshape(1, REC_HEAD_DIM))
        x, h2 = _outproj(o_a, o_b, w_o[layer].astype(BF16), x, mod3,
                         ln1_g[layer].reshape(1, D), ln1_b[layer].reshape(1, D), alpha)
        x = _ffn(h2, w_ffn_in[layer].astype(BF16), w_ffn_out[layer].astype(BF16), x, mod3,
                 ln2_g[layer].reshape(1, D), ln2_b[layer].reshape(1, D), alpha)
    return x
```

```python
import functools

import jax
import jax.numpy as jnp
from jax import lax
from jax.experimental import pallas as pl
from jax.experimental.pallas import tpu as pltpu

F32 = jnp.float32
BF16 = jnp.bfloat16

CHUNK = 64
N_PAST_CHUNKS = 8
BAND = (N_PAST_CHUNKS + 1) * CHUNK
ATTN_HEAD_DIM = 64
REC_HEAD_DIM = 128
N_MOD = 6
EPS = 1e-5
LANES = 128
SUBLANES = 8
BF16_ROWS = 16
QBLK = 2 * CHUNK
KBLK = BAND + CHUNK
BIAS_W = KBLK + QBLK
N_PAD_STEPS = N_PAST_CHUNKS * CHUNK // QBLK

MIB = 1024 * 1024


def _cparams(sem, vmem_mib):
    return pltpu.CompilerParams(dimension_semantics=sem, vmem_limit_bytes=vmem_mib * MIB)


def _sigmoid(x):
    return 1.0 / (1.0 + jnp.exp(-x))


def _silu(x):
    return x * _sigmoid(x)


def _silu_tanh(x):
    h = 0.5 * x
    return h + h * jnp.tanh(h)


def _ln_rows(x):
    mu = jnp.mean(x, axis=-1, keepdims=True)
    xc = x - mu
    var = jnp.mean(xc * xc, axis=-1, keepdims=True)
    return xc * lax.rsqrt(var + EPS)


def _mod_kernel(ct_ref, w_ref, b_ref, o_ref):
    ct = ct_ref[...]
    cat = _silu(ct)
    w = w_ref[...]
    rows = [jnp.sum(w * cat[:, b:b + 1], axis=0, keepdims=True) for b in range(ct.shape[1])]
    o_ref[...] = jnp.concatenate(rows, axis=0) + b_ref[...]


def _mod(c, w_ada, b_ada, tn=512):
    B, D = c.shape
    N = w_ada.shape[1]
    return pl.pallas_call(
        _mod_kernel,
        out_shape=jax.ShapeDtypeStruct((B, N), F32),
        grid=(N // tn,),
        in_specs=[pl.BlockSpec((D, B), lambda j: (0, 0)),
                  pl.BlockSpec((D, tn), lambda j: (0, j)),
                  pl.BlockSpec((1, tn), lambda j: (0, j))],
        out_specs=pl.BlockSpec((B, tn), lambda j: (0, j)),
        compiler_params=_cparams(("parallel",), 32),
        name="adaln_mod",
    )(c.T, w_ada, b_ada.reshape(1, N))


def _ln_mod_kernel(x_ref, mod_ref, o_ref, *, shift_row):
    y = _ln_rows(x_ref[0])
    shift = mod_ref[0, shift_row:shift_row + 1, :]
    scale = mod_ref[0, shift_row + 1:shift_row + 2, :]
    o_ref[0] = (y * (1.0 + scale) + shift).astype(o_ref.dtype)


def _ln_mod(x, mod3, shift_row, tm=512):
    B, T, D = x.shape
    return pl.pallas_call(
        functools.partial(_ln_mod_kernel, shift_row=shift_row),
        out_shape=jax.ShapeDtypeStruct((B, T, D), BF16),
        grid=(B, T // tm),
        in_specs=[pl.BlockSpec((1, tm, D), lambda b, i: (b, i, 0)),
                  pl.BlockSpec((1, N_MOD, D), lambda b, i: (b, 0, 0))],
        out_specs=pl.BlockSpec((1, tm, D), lambda b, i: (b, i, 0)),
        compiler_params=_cparams(("parallel", "parallel"), 32),
        name="ln_modulate",
    )(x, mod3)


def _matmul_kernel(a_ref, w_ref, o_ref):
    o_ref[...] = jnp.dot(a_ref[...], w_ref[...], preferred_element_type=F32).astype(o_ref.dtype)


def _matmul(a, w, col_block0, n_out, out_dtype, tm=1024, tn=1024):
    M, K = a.shape
    return pl.pallas_call(
        _matmul_kernel,
        out_shape=jax.ShapeDtypeStruct((M, n_out), out_dtype),
        grid=(M // tm, n_out // tn),
        in_specs=[pl.BlockSpec((tm, K), lambda i, j: (i, 0)),
                  pl.BlockSpec((K, tn), lambda i, j: (0, j + col_block0))],
        out_specs=pl.BlockSpec((tm, tn), lambda i, j: (i, j)),
        compiler_params=_cparams(("parallel", "arbitrary"), 40),
        name="in_proj",
    )(a, w)


NEG_BIG = -1e30
LOG2E = 1.4426950408889634


def _attn_kernel(q_ref, k_ref, v_ref, bias_ref, gain_ref, *rest, n_chunks, n_cast):
    cast_in = rest[:n_cast]
    o_ref = rest[n_cast]
    cast_out = rest[n_cast + 1:2 * n_cast + 1]
    kpa, kpb, vpa, vpb, tab_t, s_a, s_b, s_c = rest[2 * n_cast + 1:]
    pad = N_PAST_CHUNKS * CHUNK
    T = n_chunks * CHUNK
    n_steps = T // QBLK
    head0 = lax.broadcasted_iota(jnp.int32, (QBLK, LANES), 1) < ATTN_HEAD_DIM
    m0 = jnp.where(head0, 1.0, 0.0).astype(BF16)
    m1 = jnp.where(head0, 0.0, 1.0).astype(BF16)

    for ref in (kpa, kpb):
        ref[0:pad, :] = jnp.zeros((pad, LANES), BF16)
    for ref in (vpa, vpb):
        ref[0:pad, :] = jnp.zeros((pad, 2 * LANES), BF16)

    def prep(blk):
        src = slice(blk * QBLK, (blk + 1) * QBLK)
        dst = slice(pad + blk * QBLK, pad + (blk + 1) * QBLK)
        k = k_ref[0, src, :]
        v = v_ref[0, src, :]
        kpa[dst, :] = k * m0
        kpb[dst, :] = k * m1
        vpa[dst, 0:LANES] = v * m0
        vpb[dst, 0:LANES] = v * m1
        vpa[dst, LANES:2 * LANES] = m0
        vpb[dst, LANES:2 * LANES] = m1

    @pl.when(pl.program_id(1) == 0)
    def _():
        key = lax.broadcasted_iota(jnp.int32, (KBLK, QBLK), 0)
        qry = lax.broadcasted_iota(jnp.int32, (KBLK, QBLK), 1)
        in_band = ((qry < CHUNK) & (key < BAND)) | ((qry >= CHUNK) & (key >= CHUNK))
        for hh in range(2):
            g = jnp.broadcast_to(bias_ref[0, hh:hh + 1, :], (QBLK, BIAS_W))
            t = pltpu.roll(g, BIAS_W - QBLK, 1, stride=1, stride_axis=0)[:, :KBLK].T
            t = jnp.where(in_band, t, NEG_BIG)
            for var in range(N_PAD_STEPS + 1):
                first_real = pad - var * QBLK
                tv = jnp.where(key >= first_real, t, NEG_BIG) if first_real > 0 else t
                tab_t[var, hh] = tv.astype(BF16)

    eye = (lax.broadcasted_iota(jnp.int32, (QBLK, QBLK), 0)
           == lax.broadcasted_iota(jnp.int32, (QBLK, QBLK), 1)).astype(BF16)
    gain = gain_ref[0]
    nt = (((1,), (1,)), ((), ()))

    def scores(m, dst):
        prep(m)
        band = slice(m * QBLK, m * QBLK + KBLK)
        var = min(m, N_PAD_STEPS)
        lhs = jnp.concatenate([q_ref[0, m * QBLK:(m + 1) * QBLK, :], eye], axis=1)
        rhs = jnp.concatenate([jnp.concatenate([kpa[band, :], tab_t[var, 0]], axis=1),
                               jnp.concatenate([kpb[band, :], tab_t[var, 1]], axis=1)], axis=0)
        dst[...] = lax.dot_general(lhs, rhs, nt, preferred_element_type=F32)

    def finish(m, src):
        band = slice(m * QBLK, m * QBLK + KBLK)
        s = src[...]
        p = jnp.concatenate(
            [jnp.exp2(sh - jnp.max(sh, axis=-1, keepdims=True)) for sh in (s[:, 0:KBLK], s[:, KBLK:2 * KBLK])],
            axis=1).astype(BF16)
        pv = jnp.dot(p, jnp.concatenate([vpa[band, :], vpb[band, :]], axis=0), preferred_element_type=F32)
        o = pv[:, 0:LANES] * (1.0 / pv[:, LANES:2 * LANES])
        o2 = o * o
        ms0 = jnp.sum(jnp.where(head0, o2, 0.0), axis=-1, keepdims=True) / ATTN_HEAD_DIM
        ms1 = jnp.sum(jnp.where(head0, 0.0, o2), axis=-1, keepdims=True) / ATTN_HEAD_DIM
        y = o * lax.rsqrt(jnp.where(head0, ms0, ms1) + EPS) * gain
        o_ref[0, m * QBLK:(m + 1) * QBLK, :] = y.astype(o_ref.dtype)

    pieces = [(src_ref, dst_ref, r0) for src_ref, dst_ref in zip(cast_in, cast_out)
              for r0 in range(0, src_ref.shape[0], BF16_ROWS)]
    bufs = (s_a, s_b, s_c)
    depth = len(bufs)
    ahead = depth - 1
    for m in range(ahead):
        scores(m, bufs[m % depth])
    for m in range(n_steps):
        if m + ahead < n_steps:
            scores(m + ahead, bufs[(m + ahead) % depth])
        finish(m, bufs[m % depth])
        for src_ref, dst_ref, r0 in pieces[m * len(pieces) // n_steps:(m + 1) * len(pieces) // n_steps]:
            dst_ref[r0:r0 + BF16_ROWS, :] = src_ref[r0:r0 + BF16_ROWS, :].astype(BF16)


def _attention(proj_a, bias_vec, attn_gain, cast_weights):
    B, T, W3 = proj_a.shape
    W = W3 // 3
    n_pairs = W // LANES
    n_grid = n_pairs * B
    t_pad = T + N_PAST_CHUNKS * CHUNK
    cast_specs = []
    for w in cast_weights:
        rows = w.shape[0] // n_grid
        assert w.shape[0] % n_grid == 0 and rows % BF16_ROWS == 0
        cast_specs.append(pl.BlockSpec((rows, w.shape[1]), lambda h, b: (h * B + b, 0)))
    outs = pl.pallas_call(
        functools.partial(_attn_kernel, n_chunks=T // CHUNK, n_cast=len(cast_weights)),
        out_shape=(jax.ShapeDtypeStruct((B, T, W), BF16),
                   *[jax.ShapeDtypeStruct(w.shape, BF16) for w in cast_weights]),
        grid=(n_pairs, B),
        in_specs=[pl.BlockSpec((1, T, LANES), lambda h, b: (b, 0, h)),
                  pl.BlockSpec((1, T, LANES), lambda h, b: (b, 0, n_pairs + h)),
                  pl.BlockSpec((1, T, LANES), lambda h, b: (b, 0, 2 * n_pairs + h)),
                  pl.BlockSpec((1, 2, BIAS_W), lambda h, b: (h, 0, 0)),
                  pl.BlockSpec((1, 1, LANES), lambda h, b: (h, 0, 0)),
                  *cast_specs],
        out_specs=(pl.BlockSpec((1, T, LANES), lambda h, b: (b, 0, h)), *cast_specs),
        scratch_shapes=[pltpu.VMEM((t_pad, LANES), BF16),
                        pltpu.VMEM((t_pad, LANES), BF16),
                        pltpu.VMEM((t_pad, 2 * LANES), BF16),
                        pltpu.VMEM((t_pad, 2 * LANES), BF16),
                        pltpu.VMEM((N_PAD_STEPS + 1, 2, KBLK, QBLK), BF16),
                        pltpu.VMEM((QBLK, 2 * KBLK), F32),
                        pltpu.VMEM((QBLK, 2 * KBLK), F32),
                        pltpu.VMEM((QBLK, 2 * KBLK), F32)],
        compiler_params=_cparams(("parallel", "arbitrary"), 48),
        name="chunk_attention",
    )(proj_a, proj_a, proj_a, bias_vec, attn_gain, *cast_weights)
    return outs[0], outs[1:]


HG_CHUNK = 256
HG_LEVELS = (128, 64, 32, 16, 8, 4, 2, 1)
assert HG_LEVELS[-1] == 1
HG_UNROLL = 4


def _split3(x):
    hi = x.astype(BF16)
    r1 = x - hi.astype(F32)
    mid = r1.astype(BF16)
    lo = (r1 - mid.astype(F32)).astype(BF16)
    return hi, mid, lo


def _hgrn_kernel(q_ref, f_ref, i_ref, g_ref, lbl_ref, gn_ref, o_ref, *, n_steps):
    C = HG_CHUNK
    H2 = C // 2
    Dk = REC_HEAD_DIM
    nt = (((1,), (1,)), ((), ()))
    lbl = lbl_ref[0]
    e = jnp.exp(lbl - jnp.max(lbl, axis=0, keepdims=True))
    lb = e[0:1, :] / jnp.sum(e, axis=0, keepdims=True)
    c1 = 0.5 * (1.0 - lb)
    gn = gn_ref[...]

    r = lax.broadcasted_iota(jnp.int32, (C, C), 0)
    s = lax.broadcasted_iota(jnp.int32, (C, C), 1)
    tril = (s <= r).astype(BF16)
    rh = lax.broadcasted_iota(jnp.int32, (H2, H2), 0)
    sh = lax.broadcasted_iota(jnp.int32, (H2, H2), 1)
    lvl_mask = {m: ((rh // (2 * m)) == (sh // (2 * m))) & (((rh // m) % 2) == 1) & (((sh // m) % 2) == 0)
                for m in HG_LEVELS[1:]}
    sub = lax.broadcasted_iota(jnp.int32, (C // SUBLANES, SUBLANES, Dk), 1)

    def roll8(x, d):
        return pltpu.roll(x.reshape(C // SUBLANES, SUBLANES, Dk), d, 1)

    def front(n):
        row0 = pl.multiple_of(n * C, C)
        rows = pl.ds(row0, C)
        c1t = c1 * jnp.tanh(0.5 * f_ref[0, rows, :])
        f = (1.0 - c1) + c1t
        kk = c1 - c1t
        qq = _silu_tanh(q_ref[0, rows, :])
        ii = i_ref[0, rows, :]
        hi, mid, lo = _split3(jnp.log2(f))
        bb = jnp.dot(tril, jnp.concatenate([hi, mid, lo], axis=1), preferred_element_type=F32)
        b = bb[:, 0:Dk] + bb[:, Dk:2 * Dk] + bb[:, 2 * Dk:3 * Dk]
        return dict(rows=rows, kk=kk, qq=qq, ii=ii, ii_bf=ii.astype(BF16), b=b)

    def level_z(v, m):
        b, kk, qq = v["b"], v["kk"], v["qq"]
        if m >= SUBLANES:
            parts, srcs = [], []
            for p in range(0, C, 2 * m):
                bm = b[p + m - 1:p + m, :]
                parts += [bm - b[p:p + m], b[p + m:p + 2 * m] - bm]
                srcs += [kk[p:p + m], qq[p + m:p + 2 * m]]
            arg = jnp.concatenate(parts, axis=0)
            src = jnp.concatenate(srcs, axis=0)
        else:
            b3 = b.reshape(C // SUBLANES, SUBLANES, Dk)
            if m == 1:
                bm = jnp.where(sub % 2 == 1, roll8(b, 1), b3)
            else:
                bm = jnp.broadcast_to(b3[:, m - 1:m, :], b3.shape)
                for p in range(2 * m, SUBLANES, 2 * m):
                    bm = jnp.where(sub >= p, jnp.broadcast_to(b3[:, p + m - 1:p + m, :], b3.shape), bm)
            upper = (sub // m) % 2 == 1
            arg = ((b3 - bm) * jnp.where(upper, 1.0, -1.0)).reshape(C, Dk)
            src = jnp.where(upper, qq.reshape(b3.shape), kk.reshape(b3.shape)).reshape(C, Dk)
        return (src * jnp.exp2(arg)).astype(BF16)

    def group(gi, st):
        vs = [front(gi * HG_UNROLL + j) for j in range(HG_UNROLL)]
        a_lo = [None] * HG_UNROLL
        a_d0 = [jnp.zeros((H2, H2), F32)] * HG_UNROLL
        a_d1 = [jnp.zeros((H2, H2), F32)] * HG_UNROLL
        for li, m in enumerate(HG_LEVELS):
            for j, v in enumerate(vs):
                z = level_z(v, m)
                if li == 0:
                    a_lo[j] = lax.dot_general(z[H2:], z[:H2], nt, preferred_element_type=F32)
                else:
                    g0 = lax.dot_general(z[:H2], z[:H2], nt, preferred_element_type=F32)
                    g1 = lax.dot_general(z[H2:], z[H2:], nt, preferred_element_type=F32)
                    a_d0[j] = jnp.where(lvl_mask[m], g0, a_d0[j])
                    a_d1[j] = jnp.where(lvl_mask[m], g1, a_d1[j])
        intra = []
        for j, v in enumerate(vs):
            o_top = jnp.dot(a_d0[j].astype(BF16), v["ii_bf"][:H2], preferred_element_type=F32)
            o_bot = jnp.dot(jnp.concatenate([a_lo[j], a_d1[j]], axis=1).astype(BF16), v["ii_bf"],
                            preferred_element_type=F32)
            o_diag = jnp.sum(v["qq"] * v["kk"], axis=-1, keepdims=True) * v["ii"]
            intra.append(jnp.concatenate([o_top, o_bot], axis=0) + o_diag)
        for j, v in enumerate(vs):
            b = v["b"]
            b_last = b[C - 1:C, :]
            qe = (v["qq"] * jnp.exp2(b)).astype(BF16)
            o = intra[j] + lax.dot_general(qe, st.astype(BF16), nt, preferred_element_type=F32)
            ke = (v["kk"] * jnp.exp2(b_last - b)).astype(BF16)
            st = st * jnp.exp2(b_last) + lax.dot_general(
                v["ii_bf"], ke, (((0,), (0,)), ((), ())), preferred_element_type=F32)
            ms = jnp.mean(o * o, axis=-1, keepdims=True)
            y = o * lax.rsqrt(ms + EPS) * gn
            y = y * _silu_tanh(g_ref[0, v["rows"], :])
            o_ref[0, v["rows"], :] = y.astype(o_ref.dtype)
        return st

    lax.fori_loop(0, n_steps // HG_UNROLL, group, jnp.zeros((Dk, Dk), F32))


def _hgrn(proj_b, lb_logits_h, gnorm_g):
    B, T, W4 = proj_b.shape
    W = W4 // 4
    H = W // REC_HEAD_DIM
    n_slots = lb_logits_h.shape[1]
    blk = lambda off: pl.BlockSpec((1, T, REC_HEAD_DIM), lambda b, h, off=off: (b, 0, off * H + h))
    return pl.pallas_call(
        functools.partial(_hgrn_kernel, n_steps=T // HG_CHUNK),
        out_shape=jax.ShapeDtypeStruct((B, T, W), BF16),
        grid=(B, H),
        in_specs=[blk(0), blk(1), blk(2), blk(3),
                  pl.BlockSpec((1, n_slots, REC_HEAD_DIM), lambda b, h: (h, 0, 0)),
                  pl.BlockSpec((1, REC_HEAD_DIM), lambda b, h: (0, 0))],
        out_specs=pl.BlockSpec((1, T, REC_HEAD_DIM), lambda b, h: (b, 0, h)),
        compiler_params=_cparams(("parallel", "parallel"), 32),
        name="hgrn2",
    )(proj_b, proj_b, proj_b, proj_b, lb_logits_h, gnorm_g)


def _outproj_kernel(oa_ref, ob_ref, w_ref, x_ref, mod_ref, g_ref, b_ref, x1_ref, h2_ref, *, alpha, sub):
    wa = oa_ref.shape[-1]
    gate1 = mod_ref[0, 2:3, :]
    shift2 = mod_ref[0, 3:4, :]
    scale2 = mod_ref[0, 4:5, :]
    for r0 in range(0, x_ref.shape[1], sub):
        rs = slice(r0, r0 + sub)
        mix = jnp.dot(oa_ref[0, rs, :], w_ref[0:wa, :], preferred_element_type=F32)
        mix = mix + jnp.dot(ob_ref[0, rs, :], w_ref[wa:, :], preferred_element_type=F32)
        x1 = _ln_rows(alpha * x_ref[0, rs, :] + gate1 * mix) * g_ref[...] + b_ref[...]
        x1_ref[0, rs, :] = x1
        h2_ref[0, rs, :] = (_ln_rows(x1) * (1.0 + scale2) + shift2).astype(h2_ref.dtype)


def _outproj(o_a, o_b, w_o, x, mod3, ln_g, ln_b, alpha, tm=512, sub=128):
    B, T, D = x.shape
    Wa, Wb = o_a.shape[-1], o_b.shape[-1]
    return pl.pallas_call(
        functools.partial(_outproj_kernel, alpha=alpha, sub=sub),
        out_shape=(jax.ShapeDtypeStruct((B, T, D), F32), jax.ShapeDtypeStruct((B, T, D), BF16)),
        grid=(B, T // tm),
        in_specs=[pl.BlockSpec((1, tm, Wa), lambda b, i: (b, i, 0)),
                  pl.BlockSpec((1, tm, Wb), lambda b, i: (b, i, 0)),
                  pl.BlockSpec((Wa + Wb, D), lambda b, i: (0, 0)),
                  pl.BlockSpec((1, tm, D), lambda b, i: (b, i, 0)),
                  pl.BlockSpec((1, N_MOD, D), lambda b, i: (b, 0, 0)),
                  pl.BlockSpec((1, D), lambda b, i: (0, 0)),
                  pl.BlockSpec((1, D), lambda b, i: (0, 0))],
        out_specs=(pl.BlockSpec((1, tm, D), lambda b, i: (b, i, 0)),
                   pl.BlockSpec((1, tm, D), lambda b, i: (b, i, 0))),
        compiler_params=_cparams(("parallel", "parallel"), 48),
        name="out_proj_ln1",
    )(o_a, o_b, w_o, x, mod3, ln_g, ln_b)


def _ffn_kernel(h_ref, wg_ref, wu_ref, wo_ref, x_ref, mod_ref, g_ref, b_ref, o_ref, *, alpha):
    f = pl.program_id(2)

    @pl.when(f == 0)
    def _():
        o_ref[0] = jnp.zeros(o_ref.shape[1:], F32)

    h = h_ref[0]
    gate = jnp.dot(h, wg_ref[...], preferred_element_type=F32)
    up = jnp.dot(h, wu_ref[...], preferred_element_type=F32)
    act = (_silu(gate) * up).astype(BF16)
    o_ref[0] += jnp.dot(act, wo_ref[...], preferred_element_type=F32)

    @pl.when(f == pl.num_programs(2) - 1)
    def _():
        gate2 = mod_ref[0, 5:6, :]
        o_ref[0] = _ln_rows(alpha * x_ref[0] + gate2 * o_ref[0]) * g_ref[...] + b_ref[...]


def _ffn(h2, w_in, w_out, x1, mod3, ln_g, ln_b, alpha, tm=1024, tf=256):
    B, T, D = x1.shape
    F = w_out.shape[0]
    nf = F // tf
    return pl.pallas_call(
        functools.partial(_ffn_kernel, alpha=alpha),
        out_shape=jax.ShapeDtypeStruct((B, T, D), F32),
        grid=(B, T // tm, nf),
        in_specs=[pl.BlockSpec((1, tm, D), lambda b, i, f: (b, i, 0)),
                  pl.BlockSpec((D, tf), lambda b, i, f: (0, f)),
                  pl.BlockSpec((D, tf), lambda b, i, f: (0, nf + f)),
                  pl.BlockSpec((tf, D), lambda b, i, f: (f, 0)),
                  pl.BlockSpec((1, tm, D), lambda b, i, f: (b, i, 0)),
                  pl.BlockSpec((1, N_MOD, D), lambda b, i, f: (b, 0, 0)),
                  pl.BlockSpec((1, D), lambda b, i, f: (0, 0)),
                  pl.BlockSpec((1, D), lambda b, i, f: (0, 0))],
        out_specs=pl.BlockSpec((1, tm, D), lambda b, i, f: (b, i, 0)),
        compiler_params=_cparams(("parallel", "parallel", "arbitrary"), 56),
        name="swiglu_ffn_ln2",
    )(h2, w_in, w_in, w_out, x1, mod3, ln_g, ln_b)


def _bias_vectors(rel_bias):
    H, n_rel = rel_bias.shape
    max_rel = (n_rel - 1) // 2
    u = jnp.arange(BIAS_W)
    idx = jnp.clip(KBLK - u, -max_rel, max_rel) + max_rel
    return rel_bias[:, idx]


def kernel(x, c, w_ada, b_ada, w_in, rel_bias, attn_norm_g, lb_logits, gnorm_g, w_o,
           ln1_g, ln1_b, w_ffn_in, w_ffn_out, ln2_g, ln2_b):
    B, T, D = x.shape
    depth = w_ada.shape[0]
    alpha = (2 * depth) ** 0.25
    attn_w = attn_norm_g.shape[1]
    rec_w = lb_logits.shape[1]
    n_slots = lb_logits.shape[0]
    rec_heads = rec_w // REC_HEAD_DIM
    assert depth == 1 and n_slots == depth + 1
    for layer in range(depth):
        mod3 = _mod(c, w_ada[layer], b_ada[layer]).reshape(B, N_MOD, D)
        h1 = _ln_mod(x, mod3, 0).reshape(B * T, D)
        q_scale = jnp.where(jnp.arange(w_in.shape[2]) < attn_w, ATTN_HEAD_DIM ** -0.5 * LOG2E, 1.0)
        w_in_bf = (w_in[layer] * q_scale).astype(BF16)
        proj_a = _matmul(h1, w_in_bf, 0, 3 * attn_w, BF16).reshape(B, T, 3 * attn_w)
        proj_b = _matmul(h1, w_in_bf, 3 * attn_w // 1024, 4 * rec_w, F32).reshape(B, T, 4 * rec_w)
        bias_vec = (_bias_vectors(rel_bias[layer]) * LOG2E).reshape(-1, 2, BIAS_W)
        o_a, (w_o_bf, w_ffn_in_bf, w_ffn_out_bf) = _attention(
            proj_a, bias_vec, attn_norm_g[layer].reshape(-1, 1, LANES),
            (w_o[layer], w_ffn_in[layer], w_ffn_out[layer]))
        lbl = lb_logits.reshape(n_slots, rec_heads, REC_HEAD_DIM).transpose(1, 0, 2)
        o_b = _hgrn(proj_b, lbl, gnorm_g[layer].reshape(1, REC_HEAD_DIM))
        x, h2 = _outproj(o_a, o_b, w_o_bf, x, mod3,
                         ln1_g[layer].reshape(1, D), ln1_b[layer].reshape(1, D), alpha)
        x = _ffn(h2, w_ffn_in_bf, w_ffn_out_bf, x, mod3,
                 ln2_g[layer].reshape(1, D), ln2_b[layer].reshape(1, D), alpha)
    return x
```

```python
import functools

import jax
import jax.numpy as jnp
from jax import lax
from jax.experimental import pallas as pl
from jax.experimental.pallas import tpu as pltpu

F32 = jnp.float32
BF16 = jnp.bfloat16

CHUNK = 64
N_PAST_CHUNKS = 8
BAND = (N_PAST_CHUNKS + 1) * CHUNK
ATTN_HEAD_DIM = 64
REC_HEAD_DIM = 128
N_MOD = 6
EPS = 1e-5
LANES = 128
SUBLANES = 8
BF16_ROWS = 16
QBLK = 2 * CHUNK
KBLK = BAND + CHUNK
BIAS_W = KBLK + QBLK
N_PAD_STEPS = N_PAST_CHUNKS * CHUNK // QBLK

MIB = 1024 * 1024


def _cparams(sem, vmem_mib):
    return pltpu.CompilerParams(dimension_semantics=sem, vmem_limit_bytes=vmem_mib * MIB)


def _sigmoid(x):
    return 1.0 / (1.0 + jnp.exp(-x))


def _silu(x):
    return x * _sigmoid(x)


def _silu_tanh(x):
    h = 0.5 * x
    return h + h * jnp.tanh(h)


def _ln_rows(x):
    mu = jnp.mean(x, axis=-1, keepdims=True)
    xc = x - mu
    var = jnp.mean(xc * xc, axis=-1, keepdims=True)
    return xc * lax.rsqrt(var + EPS)


def _mod_kernel(ct_ref, w_ref, b_ref, o_ref):
    ct = ct_ref[...]
    cat = _silu(ct)
    w = w_ref[...]
    rows = [jnp.sum(w * cat[:, b:b + 1], axis=0, keepdims=True) for b in range(ct.shape[1])]
    o_ref[...] = jnp.concatenate(rows, axis=0) + b_ref[...]


def _mod(c, w_ada, b_ada, tn=512):
    B, D = c.shape
    N = w_ada.shape[1]
    return pl.pallas_call(
        _mod_kernel,
        out_shape=jax.ShapeDtypeStruct((B, N), F32),
        grid=(N // tn,),
        in_specs=[pl.BlockSpec((D, B), lambda j: (0, 0)),
                  pl.BlockSpec((D, tn), lambda j: (0, j)),
                  pl.BlockSpec((1, tn), lambda j: (0, j))],
        out_specs=pl.BlockSpec((B, tn), lambda j: (0, j)),
        compiler_params=_cparams(("parallel",), 32),
        name="adaln_mod",
    )(c.T, w_ada, b_ada.reshape(1, N))


def _ln_mod_kernel(x_ref, mod_ref, o_ref, *, shift_row):
    y = _ln_rows(x_ref[0])
    shift = mod_ref[0, shift_row:shift_row + 1, :]
    scale = mod_ref[0, shift_row + 1:shift_row + 2, :]
    o_ref[0] = (y * (1.0 + scale) + shift).astype(o_ref.dtype)


def _ln_mod(x, mod3, shift_row, tm=512):
    B, T, D = x.shape
    return pl.pallas_call(
        functools.partial(_ln_mod_kernel, shift_row=shift_row),
        out_shape=jax.ShapeDtypeStruct((B, T, D), BF16),
        grid=(B, T // tm),
        in_specs=[pl.BlockSpec((1, tm, D), lambda b, i: (b, i, 0)),
                  pl.BlockSpec((1, N_MOD, D), lambda b, i: (b, 0, 0))],
        out_specs=pl.BlockSpec((1, tm, D), lambda b, i: (b, i, 0)),
        compiler_params=_cparams(("parallel", "parallel"), 32),
        name="ln_modulate",
    )(x, mod3)


def _matmul_kernel(a_ref, w_ref, o_ref):
    o_ref[...] = jnp.dot(a_ref[...], w_ref[...], preferred_element_type=F32).astype(o_ref.dtype)


def _matmul(a, w, col_block0, n_out, out_dtype, tm=1024, tn=1024):
    M, K = a.shape
    return pl.pallas_call(
        _matmul_kernel,
        out_shape=jax.ShapeDtypeStruct((M, n_out), out_dtype),
        grid=(M // tm, n_out // tn),
        in_specs=[pl.BlockSpec((tm, K), lambda i, j: (i, 0)),
                  pl.BlockSpec((K, tn), lambda i, j: (0, j + col_block0))],
        out_specs=pl.BlockSpec((tm, tn), lambda i, j: (i, j)),
        compiler_params=_cparams(("parallel", "arbitrary"), 40),
        name="in_proj",
    )(a, w)


NEG_BIG = -1e30
LOG2E = 1.4426950408889634


def _attn_kernel(q_ref, k_ref, v_ref, bias_ref, gain_ref, *rest, n_chunks, n_cast):
    cast_in = rest[:n_cast]
    o_ref = rest[n_cast]
    cast_out = rest[n_cast + 1:2 * n_cast + 1]
    kta, ktb, vpa, vpb, tab, s_a, s_b, s_c = rest[2 * n_cast + 1:]
    pad = N_PAST_CHUNKS * CHUNK
    T = n_chunks * CHUNK
    n_steps = T // QBLK
    head0 = lax.broadcasted_iota(jnp.int32, (QBLK, LANES), 1) < ATTN_HEAD_DIM
    m0 = jnp.where(head0, 1.0, 0.0).astype(BF16)
    m1 = jnp.where(head0, 0.0, 1.0).astype(BF16)

    head0_t = lax.broadcasted_iota(jnp.int32, (LANES, QBLK), 0) < ATTN_HEAD_DIM
    mt0 = jnp.where(head0_t, 1.0, 0.0).astype(BF16)
    mt1 = jnp.where(head0_t, 0.0, 1.0).astype(BF16)

    for ref in (kta, ktb):
        ref[:, 0:pad] = jnp.zeros((LANES, pad), BF16)
    for ref in (vpa, vpb):
        ref[0:pad, :] = jnp.zeros((pad, 2 * LANES), BF16)

    def prep(blk):
        src = slice(blk * QBLK, (blk + 1) * QBLK)
        dst = slice(pad + blk * QBLK, pad + (blk + 1) * QBLK)
        kt = k_ref[0, src, :].T
        v = v_ref[0, src, :]
        kta[:, dst] = kt * mt0
        ktb[:, dst] = kt * mt1
        vpa[dst, 0:LANES] = v * m0
        vpb[dst, 0:LANES] = v * m1
        vpa[dst, LANES:2 * LANES] = m0
        vpb[dst, LANES:2 * LANES] = m1

    @pl.when(pl.program_id(1) == 0)
    def _():
        qry = lax.broadcasted_iota(jnp.int32, (QBLK, KBLK), 0)
        key = lax.broadcasted_iota(jnp.int32, (QBLK, KBLK), 1)
        in_band = ((qry < CHUNK) & (key < BAND)) | ((qry >= CHUNK) & (key >= CHUNK))
        for hh in range(2):
            g = jnp.broadcast_to(bias_ref[0, hh:hh + 1, :], (QBLK, BIAS_W))
            t = pltpu.roll(g, BIAS_W - QBLK, 1, stride=1, stride_axis=0)[:, :KBLK]
            t = jnp.where(in_band, t, NEG_BIG)
            for var in range(N_PAD_STEPS + 1):
                first_real = pad - var * QBLK
                tv = jnp.where(key >= first_real, t, NEG_BIG) if first_real > 0 else t
                tab[var, hh] = tv.astype(BF16)

    eye = (lax.broadcasted_iota(jnp.int32, (QBLK, QBLK), 0)
           == lax.broadcasted_iota(jnp.int32, (QBLK, QBLK), 1)).astype(BF16)
    gain = gain_ref[0]

    def scores(m, dst):
        prep(m)
        band = slice(m * QBLK, m * QBLK + KBLK)
        var = min(m, N_PAD_STEPS)
        lhs = jnp.concatenate([q_ref[0, m * QBLK:(m + 1) * QBLK, :], eye], axis=1)
        rhs = jnp.concatenate([jnp.concatenate([kta[:, band], ktb[:, band]], axis=1),
                               jnp.concatenate([tab[var, 0], tab[var, 1]], axis=1)], axis=0)
        dst[...] = jnp.dot(lhs, rhs, preferred_element_type=F32)

    def finish(m, src):
        band = slice(m * QBLK, m * QBLK + KBLK)
        s = src[...]
        p = jnp.concatenate(
            [jnp.exp2(sh - jnp.max(sh, axis=-1, keepdims=True)) for sh in (s[:, 0:KBLK], s[:, KBLK:2 * KBLK])],
            axis=1).astype(BF16)
        pv = jnp.dot(p, jnp.concatenate([vpa[band, :], vpb[band, :]], axis=0), preferred_element_type=F32)
        o = pv[:, 0:LANES] * (1.0 / pv[:, LANES:2 * LANES])
        o2 = o * o
        ms0 = jnp.sum(jnp.where(head0, o2, 0.0), axis=-1, keepdims=True) / ATTN_HEAD_DIM
        ms1 = jnp.sum(jnp.where(head0, 0.0, o2), axis=-1, keepdims=True) / ATTN_HEAD_DIM
        y = o * lax.rsqrt(jnp.where(head0, ms0, ms1) + EPS) * gain
        o_ref[0, m * QBLK:(m + 1) * QBLK, :] = y.astype(o_ref.dtype)

    pieces = [(src_ref, dst_ref, r0) for src_ref, dst_ref in zip(cast_in, cast_out)
              for r0 in range(0, src_ref.shape[0], BF16_ROWS)]
    bufs = (s_a, s_b, s_c)
    depth = len(bufs)
    ahead = depth - 1
    for m in range(ahead):
        scores(m, bufs[m % depth])
    for m in range(n_steps):
        if m + ahead < n_steps:
            scores(m + ahead, bufs[(m + ahead) % depth])
        finish(m, bufs[m % depth])
        for src_ref, dst_ref, r0 in pieces[m * len(pieces) // n_steps:(m + 1) * len(pieces) // n_steps]:
            dst_ref[r0:r0 + BF16_ROWS, :] = src_ref[r0:r0 + BF16_ROWS, :].astype(BF16)


def _attention(proj_a, bias_vec, attn_gain, cast_weights):
    B, T, W3 = proj_a.shape
    W = W3 // 3
    n_pairs = W // LANES
    n_grid = n_pairs * B
    t_pad = T + N_PAST_CHUNKS * CHUNK
    cast_specs = []
    for w in cast_weights:
        rows = w.shape[0] // n_grid
        assert w.shape[0] % n_grid == 0 and rows % BF16_ROWS == 0
        cast_specs.append(pl.BlockSpec((rows, w.shape[1]), lambda h, b: (h * B + b, 0)))
    outs = pl.pallas_call(
        functools.partial(_attn_kernel, n_chunks=T // CHUNK, n_cast=len(cast_weights)),
        out_shape=(jax.ShapeDtypeStruct((B, T, W), BF16),
                   *[jax.ShapeDtypeStruct(w.shape, BF16) for w in cast_weights]),
        grid=(n_pairs, B),
        in_specs=[pl.BlockSpec((1, T, LANES), lambda h, b: (b, 0, h)),
                  pl.BlockSpec((1, T, LANES), lambda h, b: (b, 0, n_pairs + h)),
                  pl.BlockSpec((1, T, LANES), lambda h, b: (b, 0, 2 * n_pairs + h)),
                  pl.BlockSpec((1, 2, BIAS_W), lambda h, b: (h, 0, 0)),
                  pl.BlockSpec((1, 1, LANES), lambda h, b: (h, 0, 0)),
                  *cast_specs],
        out_specs=(pl.BlockSpec((1, T, LANES), lambda h, b: (b, 0, h)), *cast_specs),
        scratch_shapes=[pltpu.VMEM((LANES, t_pad), BF16),
                        pltpu.VMEM((LANES, t_pad), BF16),
                        pltpu.VMEM((t_pad, 2 * LANES), BF16),
                        pltpu.VMEM((t_pad, 2 * LANES), BF16),
                        pltpu.VMEM((N_PAD_STEPS + 1, 2, QBLK, KBLK), BF16),
                        pltpu.VMEM((QBLK, 2 * KBLK), F32),
                        pltpu.VMEM((QBLK, 2 * KBLK), F32),
                        pltpu.VMEM((QBLK, 2 * KBLK), F32)],
        compiler_params=_cparams(("parallel", "arbitrary"), 48),
        name="chunk_attention",
    )(proj_a, proj_a, proj_a, bias_vec, attn_gain, *cast_weights)
    return outs[0], outs[1:]


HG_CHUNK = 256
HG_LEVELS = (128, 64, 32, 16, 8, 4, 2, 1)
assert HG_LEVELS[-1] == 1
HG_UNROLL = 8


def _split3(x):
    hi = x.astype(BF16)
    r1 = x - hi.astype(F32)
    mid = r1.astype(BF16)
    lo = (r1 - mid.astype(F32)).astype(BF16)
    return hi, mid, lo


def _hgrn_kernel(q_ref, f_ref, i_ref, g_ref, lbl_ref, gn_ref, o_ref, *, n_steps):
    C = HG_CHUNK
    H2 = C // 2
    Dk = REC_HEAD_DIM
    nt = (((1,), (1,)), ((), ()))
    lbl = lbl_ref[0]
    e = jnp.exp(lbl - jnp.max(lbl, axis=0, keepdims=True))
    lb = e[0:1, :] / jnp.sum(e, axis=0, keepdims=True)
    c1 = 0.5 * (1.0 - lb)
    gn = gn_ref[...]

    r = lax.broadcasted_iota(jnp.int32, (C, C), 0)
    s = lax.broadcasted_iota(jnp.int32, (C, C), 1)
    tril = (s <= r).astype(BF16)
    rh = lax.broadcasted_iota(jnp.int32, (H2, H2), 0)
    sh = lax.broadcasted_iota(jnp.int32, (H2, H2), 1)
    lvl_mask = {m: ((rh // (2 * m)) == (sh // (2 * m))) & (((rh // m) % 2) == 1) & (((sh // m) % 2) == 0)
                for m in HG_LEVELS[1:]}
    sub = lax.broadcasted_iota(jnp.int32, (C // SUBLANES, SUBLANES, Dk), 1)

    def roll8(x, d):
        return pltpu.roll(x.reshape(C // SUBLANES, SUBLANES, Dk), d, 1)

    def front(n):
        row0 = pl.multiple_of(n * C, C)
        rows = pl.ds(row0, C)
        c1t = c1 * jnp.tanh(0.5 * f_ref[0, rows, :])
        f = (1.0 - c1) + c1t
        kk = c1 - c1t
        qq = _silu_tanh(q_ref[0, rows, :])
        ii = i_ref[0, rows, :]
        hi, mid, lo = _split3(jnp.log2(f))
        bb = jnp.dot(tril, jnp.concatenate([hi, mid, lo], axis=1), preferred_element_type=F32)
        b = bb[:, 0:Dk] + bb[:, Dk:2 * Dk] + bb[:, 2 * Dk:3 * Dk]
        return dict(rows=rows, kk=kk, qq=qq, ii=ii, ii_bf=ii.astype(BF16), b=b)

    def level_z(v, m):
        b, kk, qq = v["b"], v["kk"], v["qq"]
        if m >= SUBLANES:
            parts, srcs = [], []
            for p in range(0, C, 2 * m):
                bm = b[p + m - 1:p + m, :]
                parts += [bm - b[p:p + m], b[p + m:p + 2 * m] - bm]
                srcs += [kk[p:p + m], qq[p + m:p + 2 * m]]
            arg = jnp.concatenate(parts, axis=0)
            src = jnp.concatenate(srcs, axis=0)
        else:
            b3 = b.reshape(C // SUBLANES, SUBLANES, Dk)
            if m == 1:
                bm = jnp.where(sub % 2 == 1, roll8(b, 1), b3)
            else:
                bm = jnp.broadcast_to(b3[:, m - 1:m, :], b3.shape)
                for p in range(2 * m, SUBLANES, 2 * m):
                    bm = jnp.where(sub >= p, jnp.broadcast_to(b3[:, p + m - 1:p + m, :], b3.shape), bm)
            upper = (sub // m) % 2 == 1
            arg = ((b3 - bm) * jnp.where(upper, 1.0, -1.0)).reshape(C, Dk)
            src = jnp.where(upper, qq.reshape(b3.shape), kk.reshape(b3.shape)).reshape(C, Dk)
        return (src * jnp.exp2(arg)).astype(BF16)

    def group(gi, st):
        vs = [front(gi * HG_UNROLL + j) for j in range(HG_UNROLL)]
        a_lo = [None] * HG_UNROLL
        a_d0 = [jnp.zeros((H2, H2), F32)] * HG_UNROLL
        a_d1 = [jnp.zeros((H2, H2), F32)] * HG_UNROLL
        for li, m in enumerate(HG_LEVELS):
            for j, v in enumerate(vs):
                z = level_z(v, m)
                if li == 0:
                    a_lo[j] = lax.dot_general(z[H2:], z[:H2], nt, preferred_element_type=F32)
                else:
                    g0 = lax.dot_general(z[:H2], z[:H2], nt, preferred_element_type=F32)
                    g1 = lax.dot_general(z[H2:], z[H2:], nt, preferred_element_type=F32)
                    a_d0[j] = jnp.where(lvl_mask[m], g0, a_d0[j])
                    a_d1[j] = jnp.where(lvl_mask[m], g1, a_d1[j])
        intra = []
        for j, v in enumerate(vs):
            o_top = jnp.dot(a_d0[j].astype(BF16), v["ii_bf"][:H2], preferred_element_type=F32)
            o_bot = jnp.dot(jnp.concatenate([a_lo[j], a_d1[j]], axis=1).astype(BF16), v["ii_bf"],
                            preferred_element_type=F32)
            o_diag = jnp.sum(v["qq"] * v["kk"], axis=-1, keepdims=True) * v["ii"]
            intra.append(jnp.concatenate([o_top, o_bot], axis=0) + o_diag)
        for j, v in enumerate(vs):
            b = v["b"]
            b_last = b[C - 1:C, :]
            qe = (v["qq"] * jnp.exp2(b)).astype(BF16)
            o = intra[j] + lax.dot_general(qe, st.astype(BF16), nt, preferred_element_type=F32)
            ke = (v["kk"] * jnp.exp2(b_last - b)).astype(BF16)
            st = st * jnp.exp2(b_last) + lax.dot_general(
                v["ii_bf"], ke, (((0,), (0,)), ((), ())), preferred_element_type=F32)
            ms = jnp.mean(o * o, axis=-1, keepdims=True)
            y = o * lax.rsqrt(ms + EPS) * gn
            y = y * _silu_tanh(g_ref[0, v["rows"], :])
            o_ref[0, v["rows"], :] = y.astype(o_ref.dtype)
        return st

    lax.fori_loop(0, n_steps // HG_UNROLL, group, jnp.zeros((Dk, Dk), F32))


def _hgrn(proj_b, lb_logits_h, gnorm_g):
    B, T, W4 = proj_b.shape
    W = W4 // 4
    H = W // REC_HEAD_DIM
    n_slots = lb_logits_h.shape[1]
    blk = lambda off: pl.BlockSpec((1, T, REC_HEAD_DIM), lambda b, h, off=off: (b, 0, off * H + h))
    return pl.pallas_call(
        functools.partial(_hgrn_kernel, n_steps=T // HG_CHUNK),
        out_shape=jax.ShapeDtypeStruct((B, T, W), BF16),
        grid=(B, H),
        in_specs=[blk(0), blk(1), blk(2), blk(3),
                  pl.BlockSpec((1, n_slots, REC_HEAD_DIM), lambda b, h: (h, 0, 0)),
                  pl.BlockSpec((1, REC_HEAD_DIM), lambda b, h: (0, 0))],
        out_specs=pl.BlockSpec((1, T, REC_HEAD_DIM), lambda b, h: (b, 0, h)),
        compiler_params=_cparams(("parallel", "parallel"), 32),
        name="hgrn2",
    )(proj_b, proj_b, proj_b, proj_b, lb_logits_h, gnorm_g)


def _outproj_kernel(oa_ref, ob_ref, w_ref, x_ref, mod_ref, g_ref, b_ref, x1_ref, h2_ref, *, alpha, sub):
    wa = oa_ref.shape[-1]
    gate1 = mod_ref[0, 2:3, :]
    shift2 = mod_ref[0, 3:4, :]
    scale2 = mod_ref[0, 4:5, :]
    for r0 in range(0, x_ref.shape[1], sub):
        rs = slice(r0, r0 + sub)
        mix = jnp.dot(oa_ref[0, rs, :], w_ref[0:wa, :], preferred_element_type=F32)
        mix = mix + jnp.dot(ob_ref[0, rs, :], w_ref[wa:, :], preferred_element_type=F32)
        x1 = _ln_rows(alpha * x_ref[0, rs, :] + gate1 * mix) * g_ref[...] + b_ref[...]
        x1_ref[0, rs, :] = x1
        h2_ref[0, rs, :] = (_ln_rows(x1) * (1.0 + scale2) + shift2).astype(h2_ref.dtype)


def _outproj(o_a, o_b, w_o, x, mod3, ln_g, ln_b, alpha, tm=512, sub=128):
    B, T, D = x.shape
    Wa, Wb = o_a.shape[-1], o_b.shape[-1]
    return pl.pallas_call(
        functools.partial(_outproj_kernel, alpha=alpha, sub=sub),
        out_shape=(jax.ShapeDtypeStruct((B, T, D), F32), jax.ShapeDtypeStruct((B, T, D), BF16)),
        grid=(B, T // tm),
        in_specs=[pl.BlockSpec((1, tm, Wa), lambda b, i: (b, i, 0)),
                  pl.BlockSpec((1, tm, Wb), lambda b, i: (b, i, 0)),
                  pl.BlockSpec((Wa + Wb, D), lambda b, i: (0, 0)),
                  pl.BlockSpec((1, tm, D), lambda b, i: (b, i, 0)),
                  pl.BlockSpec((1, N_MOD, D), lambda b, i: (b, 0, 0)),
                  pl.BlockSpec((1, D), lambda b, i: (0, 0)),
                  pl.BlockSpec((1, D), lambda b, i: (0, 0))],
        out_specs=(pl.BlockSpec((1, tm, D), lambda b, i: (b, i, 0)),
                   pl.BlockSpec((1, tm, D), lambda b, i: (b, i, 0))),
        compiler_params=_cparams(("parallel", "parallel"), 48),
        name="out_proj_ln1",
    )(o_a, o_b, w_o, x, mod3, ln_g, ln_b)


def _ffn_kernel(h_ref, wg_ref, wu_ref, wo_ref, x_ref, mod_ref, g_ref, b_ref, o_ref, *, alpha):
    f = pl.program_id(2)

    @pl.when(f == 0)
    def _():
        o_ref[0] = jnp.zeros(o_ref.shape[1:], F32)

    h = h_ref[0]
    gate = jnp.dot(h, wg_ref[...], preferred_element_type=F32)
    up = jnp.dot(h, wu_ref[...], preferred_element_type=F32)
    act = (_silu(gate) * up).astype(BF16)
    o_ref[0] += jnp.dot(act, wo_ref[...], preferred_element_type=F32)

    @pl.when(f == pl.num_programs(2) - 1)
    def _():
        gate2 = mod_ref[0, 5:6, :]
        o_ref[0] = _ln_rows(alpha * x_ref[0] + gate2 * o_ref[0]) * g_ref[...] + b_ref[...]


def _ffn(h2, w_in, w_out, x1, mod3, ln_g, ln_b, alpha, tm=1024, tf=256):
    B, T, D = x1.shape
    F = w_out.shape[0]
    nf = F // tf
    return pl.pallas_call(
        functools.partial(_ffn_kernel, alpha=alpha),
        out_shape=jax.ShapeDtypeStruct((B, T, D), F32),
        grid=(B, T // tm, nf),
        in_specs=[pl.BlockSpec((1, tm, D), lambda b, i, f: (b, i, 0)),
                  pl.BlockSpec((D, tf), lambda b, i, f: (0, f)),
                  pl.BlockSpec((D, tf), lambda b, i, f: (0, nf + f)),
                  pl.BlockSpec((tf, D), lambda b, i, f: (f, 0)),
                  pl.BlockSpec((1, tm, D), lambda b, i, f: (b, i, 0)),
                  pl.BlockSpec((1, N_MOD, D), lambda b, i, f: (b, 0, 0)),
                  pl.BlockSpec((1, D), lambda b, i, f: (0, 0)),
                  pl.BlockSpec((1, D), lambda b, i, f: (0, 0))],
        out_specs=pl.BlockSpec((1, tm, D), lambda b, i, f: (b, i, 0)),
        compiler_params=_cparams(("parallel", "parallel", "arbitrary"), 56),
        name="swiglu_ffn_ln2",
    )(h2, w_in, w_in, w_out, x1, mod3, ln_g, ln_b)


def _bias_vectors(rel_bias):
    H, n_rel = rel_bias.shape
    max_rel = (n_rel - 1) // 2
    u = jnp.arange(BIAS_W)
    idx = jnp.clip(KBLK - u, -max_rel, max_rel) + max_rel
    return rel_bias[:, idx]


def kernel(x, c, w_ada, b_ada, w_in, rel_bias, attn_norm_g, lb_logits, gnorm_g, w_o,
           ln1_g, ln1_b, w_ffn_in, w_ffn_out, ln2_g, ln2_b):
    B, T, D = x.shape
    depth = w_ada.shape[0]
    alpha = (2 * depth) ** 0.25
    attn_w = attn_norm_g.shape[1]
    rec_w = lb_logits.shape[1]
    n_slots = lb_logits.shape[0]
    rec_heads = rec_w // REC_HEAD_DIM
    assert depth == 1 and n_slots == depth + 1
    for layer in range(depth):
        mod3 = _mod(c, w_ada[layer], b_ada[layer]).reshape(B, N_MOD, D)
        h1 = _ln_mod(x, mod3, 0).reshape(B * T, D)
        q_scale = jnp.where(jnp.arange(w_in.shape[2]) < attn_w, ATTN_HEAD_DIM ** -0.5 * LOG2E, 1.0)
        w_in_bf = (w_in[layer] * q_scale).astype(BF16)
        proj_a = _matmul(h1, w_in_bf, 0, 3 * attn_w, BF16).reshape(B, T, 3 * attn_w)
        proj_b = _matmul(h1, w_in_bf, 3 * attn_w // 1024, 4 * rec_w, F32).reshape(B, T, 4 * rec_w)
        bias_vec = (_bias_vectors(rel_bias[layer]) * LOG2E).reshape(-1, 2, BIAS_W)
        o_a, (w_o_bf, w_ffn_in_bf, w_ffn_out_bf) = _attention(
            proj_a, bias_vec, attn_norm_g[layer].reshape(-1, 1, LANES),
            (w_o[layer], w_ffn_in[layer], w_ffn_out[layer]))
        lbl = lb_logits.reshape(n_slots, rec_heads, REC_HEAD_DIM).transpose(1, 0, 2)
        o_b = _hgrn(proj_b, lbl, gnorm_g[layer].reshape(1, REC_HEAD_DIM))
        x, h2 = _outproj(o_a, o_b, w_o_bf, x, mod3,
                         ln1_g[layer].reshape(1, D), ln1_b[layer].reshape(1, D), alpha)
        x = _ffn(h2, w_ffn_in_bf, w_ffn_out_bf, x, mod3,
                 ln2_g[layer].reshape(1, D), ln2_b[layer].reshape(1, D), alpha)
    return x
```

```python
import functools

import jax
import jax.numpy as jnp
from jax import lax
from jax.experimental import pallas as pl
from jax.experimental.pallas import tpu as pltpu

F32 = jnp.float32
BF16 = jnp.bfloat16

CHUNK = 64
N_PAST_CHUNKS = 8
BAND = (N_PAST_CHUNKS + 1) * CHUNK
ATTN_HEAD_DIM = 64
REC_HEAD_DIM = 128
N_MOD = 6
EPS = 1e-5
LANES = 128
SUBLANES = 8
BF16_ROWS = 16
QBLK = 2 * CHUNK
KBLK = BAND + CHUNK
BIAS_W = KBLK + QBLK
N_PAD_STEPS = N_PAST_CHUNKS * CHUNK // QBLK

MIB = 1024 * 1024


def _cparams(sem, vmem_mib):
    return pltpu.CompilerParams(dimension_semantics=sem, vmem_limit_bytes=vmem_mib * MIB)


def _sigmoid(x):
    return 1.0 / (1.0 + jnp.exp(-x))


def _silu(x):
    return x * _sigmoid(x)


def _silu_tanh(x):
    h = 0.5 * x
    return h + h * jnp.tanh(h)


def _ln_rows(x):
    mu = jnp.mean(x, axis=-1, keepdims=True)
    xc = x - mu
    var = jnp.mean(xc * xc, axis=-1, keepdims=True)
    return xc * lax.rsqrt(var + EPS)


def _mod_kernel(ct_ref, w_ref, b_ref, o_ref):
    ct = ct_ref[...]
    cat = _silu(ct)
    w = w_ref[...]
    rows = [jnp.sum(w * cat[:, b:b + 1], axis=0, keepdims=True) for b in range(ct.shape[1])]
    o_ref[...] = jnp.concatenate(rows, axis=0) + b_ref[...]


def _mod(c, w_ada, b_ada, tn=512):
    B, D = c.shape
    N = w_ada.shape[1]
    return pl.pallas_call(
        _mod_kernel,
        out_shape=jax.ShapeDtypeStruct((B, N), F32),
        grid=(N // tn,),
        in_specs=[pl.BlockSpec((D, B), lambda j: (0, 0)),
                  pl.BlockSpec((D, tn), lambda j: (0, j)),
                  pl.BlockSpec((1, tn), lambda j: (0, j))],
        out_specs=pl.BlockSpec((B, tn), lambda j: (0, j)),
        compiler_params=_cparams(("parallel",), 32),
        name="adaln_mod",
    )(c.T, w_ada, b_ada.reshape(1, N))


def _ln_mod_kernel(x_ref, mod_ref, o_ref, *, shift_row):
    y = _ln_rows(x_ref[0])
    shift = mod_ref[0, shift_row:shift_row + 1, :]
    scale = mod_ref[0, shift_row + 1:shift_row + 2, :]
    o_ref[0] = (y * (1.0 + scale) + shift).astype(o_ref.dtype)


def _ln_mod(x, mod3, shift_row, tm=512):
    B, T, D = x.shape
    return pl.pallas_call(
        functools.partial(_ln_mod_kernel, shift_row=shift_row),
        out_shape=jax.ShapeDtypeStruct((B, T, D), BF16),
        grid=(B, T // tm),
        in_specs=[pl.BlockSpec((1, tm, D), lambda b, i: (b, i, 0)),
                  pl.BlockSpec((1, N_MOD, D), lambda b, i: (b, 0, 0))],
        out_specs=pl.BlockSpec((1, tm, D), lambda b, i: (b, i, 0)),
        compiler_params=_cparams(("parallel", "parallel"), 32),
        name="ln_modulate",
    )(x, mod3)


def _matmul_kernel(a_ref, w_ref, s_ref, o_ref, w_bf):
    @pl.when(pl.program_id(1) == 0)
    def _():
        w_bf[...] = (w_ref[...] * s_ref[...]).astype(BF16)

    o_ref[...] = jnp.dot(a_ref[...], w_bf[...], preferred_element_type=F32).astype(o_ref.dtype)


def _matmul(a, w, col_scale, col_block0, n_out, out_dtype, tm=1024, tn=1024):
    M, K = a.shape
    return pl.pallas_call(
        _matmul_kernel,
        out_shape=jax.ShapeDtypeStruct((M, n_out), out_dtype),
        grid=(n_out // tn, M // tm),
        in_specs=[pl.BlockSpec((tm, K), lambda j, i: (i, 0)),
                  pl.BlockSpec((K, tn), lambda j, i: (0, j + col_block0)),
                  pl.BlockSpec((1, tn), lambda j, i: (0, j + col_block0))],
        out_specs=pl.BlockSpec((tm, tn), lambda j, i: (i, j)),
        scratch_shapes=[pltpu.VMEM((K, tn), BF16)],
        compiler_params=_cparams(("parallel", "arbitrary"), 48),
        name="in_proj",
    )(a, w, col_scale)


NEG_BIG = -1e30
LOG2E = 1.4426950408889634


def _attn_kernel(q_ref, k_ref, v_ref, bias_ref, gain_ref, *rest, n_chunks, n_cast):
    cast_in = rest[:n_cast]
    o_ref = rest[n_cast]
    cast_out = rest[n_cast + 1:2 * n_cast + 1]
    kta, ktb, vpa, vpb, tab, s_a, s_b, s_c = rest[2 * n_cast + 1:]
    pad = N_PAST_CHUNKS * CHUNK
    T = n_chunks * CHUNK
    n_steps = T // QBLK
    head0 = lax.broadcasted_iota(jnp.int32, (QBLK, LANES), 1) < ATTN_HEAD_DIM
    m0 = jnp.where(head0, 1.0, 0.0).astype(BF16)
    m1 = jnp.where(head0, 0.0, 1.0).astype(BF16)

    head0_t = lax.broadcasted_iota(jnp.int32, (LANES, QBLK), 0) < ATTN_HEAD_DIM
    mt0 = jnp.where(head0_t, 1.0, 0.0).astype(BF16)
    mt1 = jnp.where(head0_t, 0.0, 1.0).astype(BF16)

    for ref in (kta, ktb):
        ref[:, 0:pad] = jnp.zeros((LANES, pad), BF16)
    for ref in (vpa, vpb):
        ref[0:pad, :] = jnp.zeros((pad, 2 * LANES), BF16)

    def prep(blk):
        src = slice(blk * QBLK, (blk + 1) * QBLK)
        dst = slice(pad + blk * QBLK, pad + (blk + 1) * QBLK)
        kt = k_ref[0, src, :].T
        v = v_ref[0, src, :]
        kta[:, dst] = kt * mt0
        ktb[:, dst] = kt * mt1
        vpa[dst, 0:LANES] = v * m0
        vpb[dst, 0:LANES] = v * m1
        vpa[dst, LANES:2 * LANES] = m0
        vpb[dst, LANES:2 * LANES] = m1

    @pl.when(pl.program_id(1) == 0)
    def _():
        qry = lax.broadcasted_iota(jnp.int32, (QBLK, KBLK), 0)
        key = lax.broadcasted_iota(jnp.int32, (QBLK, KBLK), 1)
        in_band = ((qry < CHUNK) & (key < BAND)) | ((qry >= CHUNK) & (key >= CHUNK))
        for hh in range(2):
            g = jnp.broadcast_to(bias_ref[0, hh:hh + 1, :], (QBLK, BIAS_W))
            t = pltpu.roll(g, BIAS_W - QBLK, 1, stride=1, stride_axis=0)[:, :KBLK]
            t = jnp.where(in_band, t, NEG_BIG)
            for var in range(N_PAD_STEPS + 1):
                first_real = pad - var * QBLK
                tv = jnp.where(key >= first_real, t, NEG_BIG) if first_real > 0 else t
                tab[var, hh] = tv.astype(BF16)

    eye = (lax.broadcasted_iota(jnp.int32, (QBLK, QBLK), 0)
           == lax.broadcasted_iota(jnp.int32, (QBLK, QBLK), 1)).astype(BF16)
    gain = gain_ref[0]

    def scores(m, dst):
        prep(m)
        band = slice(m * QBLK, m * QBLK + KBLK)
        var = min(m, N_PAD_STEPS)
        lhs = jnp.concatenate([q_ref[0, m * QBLK:(m + 1) * QBLK, :], eye], axis=1)
        rhs = jnp.concatenate([jnp.concatenate([kta[:, band], ktb[:, band]], axis=1),
                               jnp.concatenate([tab[var, 0], tab[var, 1]], axis=1)], axis=0)
        dst[...] = jnp.dot(lhs, rhs, preferred_element_type=F32)

    def finish(m, src):
        band = slice(m * QBLK, m * QBLK + KBLK)
        s = src[...]
        p = jnp.concatenate(
            [jnp.exp2(sh - jnp.max(sh, axis=-1, keepdims=True)) for sh in (s[:, 0:KBLK], s[:, KBLK:2 * KBLK])],
            axis=1).astype(BF16)
        pv = jnp.dot(p, jnp.concatenate([vpa[band, :], vpb[band, :]], axis=0), preferred_element_type=F32)
        o = pv[:, 0:LANES] * (1.0 / pv[:, LANES:2 * LANES])
        o2 = o * o
        ms0 = jnp.sum(jnp.where(head0, o2, 0.0), axis=-1, keepdims=True) / ATTN_HEAD_DIM
        ms1 = jnp.sum(jnp.where(head0, 0.0, o2), axis=-1, keepdims=True) / ATTN_HEAD_DIM
        y = o * lax.rsqrt(jnp.where(head0, ms0, ms1) + EPS) * gain
        o_ref[0, m * QBLK:(m + 1) * QBLK, :] = y.astype(o_ref.dtype)

    pieces = [(src_ref, dst_ref, r0) for src_ref, dst_ref in zip(cast_in, cast_out)
              for r0 in range(0, src_ref.shape[0], BF16_ROWS)]
    bufs = (s_a, s_b, s_c)
    depth = len(bufs)
    ahead = depth - 1
    for m in range(ahead):
        scores(m, bufs[m % depth])
    for m in range(n_steps):
        if m + ahead < n_steps:
            scores(m + ahead, bufs[(m + ahead) % depth])
        finish(m, bufs[m % depth])
        for src_ref, dst_ref, r0 in pieces[m * len(pieces) // n_steps:(m + 1) * len(pieces) // n_steps]:
            dst_ref[r0:r0 + BF16_ROWS, :] = src_ref[r0:r0 + BF16_ROWS, :].astype(BF16)


def _attention(proj_a, bias_vec, attn_gain, cast_weights):
    B, T, W3 = proj_a.shape
    W = W3 // 3
    n_pairs = W // LANES
    n_grid = n_pairs * B
    t_pad = T + N_PAST_CHUNKS * CHUNK
    cast_specs = []
    for w in cast_weights:
        rows = w.shape[0] // n_grid
        assert w.shape[0] % n_grid == 0 and rows % BF16_ROWS == 0
        cast_specs.append(pl.BlockSpec((rows, w.shape[1]), lambda h, b: (h * B + b, 0)))
    outs = pl.pallas_call(
        functools.partial(_attn_kernel, n_chunks=T // CHUNK, n_cast=len(cast_weights)),
        out_shape=(jax.ShapeDtypeStruct((B, T, W), BF16),
                   *[jax.ShapeDtypeStruct(w.shape, BF16) for w in cast_weights]),
        grid=(n_pairs, B),
        in_specs=[pl.BlockSpec((1, T, LANES), lambda h, b: (b, 0, h)),
                  pl.BlockSpec((1, T, LANES), lambda h, b: (b, 0, n_pairs + h)),
                  pl.BlockSpec((1, T, LANES), lambda h, b: (b, 0, 2 * n_pairs + h)),
                  pl.BlockSpec((1, 2, BIAS_W), lambda h, b: (h, 0, 0)),
                  pl.BlockSpec((1, 1, LANES), lambda h, b: (h, 0, 0)),
                  *cast_specs],
        out_specs=(pl.BlockSpec((1, T, LANES), lambda h, b: (b, 0, h)), *cast_specs),
        scratch_shapes=[pltpu.VMEM((LANES, t_pad), BF16),
                        pltpu.VMEM((LANES, t_pad), BF16),
                        pltpu.VMEM((t_pad, 2 * LANES), BF16),
                        pltpu.VMEM((t_pad, 2 * LANES), BF16),
                        pltpu.VMEM((N_PAD_STEPS + 1, 2, QBLK, KBLK), BF16),
                        pltpu.VMEM((QBLK, 2 * KBLK), F32),
                        pltpu.VMEM((QBLK, 2 * KBLK), F32),
                        pltpu.VMEM((QBLK, 2 * KBLK), F32)],
        compiler_params=_cparams(("parallel", "arbitrary"), 48),
        name="chunk_attention",
    )(proj_a, proj_a, proj_a, bias_vec, attn_gain, *cast_weights)
    return outs[0], outs[1:]


HG_CHUNK = 256
HG_LEVELS = (128, 64, 32, 16, 8, 4, 2, 1)
assert HG_LEVELS[-1] == 1
HG_UNROLL = 8


def _split3(x):
    hi = x.astype(BF16)
    r1 = x - hi.astype(F32)
    mid = r1.astype(BF16)
    lo = (r1 - mid.astype(F32)).astype(BF16)
    return hi, mid, lo


def _hgrn_kernel(q_ref, f_ref, i_ref, g_ref, lbl_ref, gn_ref, o_ref, *, n_steps):
    C = HG_CHUNK
    H2 = C // 2
    Dk = REC_HEAD_DIM
    nt = (((1,), (1,)), ((), ()))
    lbl = lbl_ref[0]
    e = jnp.exp(lbl - jnp.max(lbl, axis=0, keepdims=True))
    lb = e[0:1, :] / jnp.sum(e, axis=0, keepdims=True)
    c1 = 0.5 * (1.0 - lb)
    gn = gn_ref[...]

    r = lax.broadcasted_iota(jnp.int32, (C, C), 0)
    s = lax.broadcasted_iota(jnp.int32, (C, C), 1)
    tril = (s <= r).astype(BF16)
    rh = lax.broadcasted_iota(jnp.int32, (H2, H2), 0)
    sh = lax.broadcasted_iota(jnp.int32, (H2, H2), 1)
    lvl_mask = {m: ((rh // (2 * m)) == (sh // (2 * m))) & (((rh // m) % 2) == 1) & (((sh // m) % 2) == 0)
                for m in HG_LEVELS[1:]}
    sub = lax.broadcasted_iota(jnp.int32, (C // SUBLANES, SUBLANES, Dk), 1)

    def roll8(x, d):
        return pltpu.roll(x.reshape(C // SUBLANES, SUBLANES, Dk), d, 1)

    def front(n):
        row0 = pl.multiple_of(n * C, C)
        rows = pl.ds(row0, C)
        c1t = c1 * jnp.tanh(0.5 * f_ref[0, rows, :])
        f = (1.0 - c1) + c1t
        kk = c1 - c1t
        qq = _silu_tanh(q_ref[0, rows, :])
        ii = i_ref[0, rows, :]
        hi, mid, lo = _split3(jnp.log2(f))
        bb = jnp.dot(tril, jnp.concatenate([hi, mid, lo], axis=1), preferred_element_type=F32)
        b = bb[:, 0:Dk] + bb[:, Dk:2 * Dk] + bb[:, 2 * Dk:3 * Dk]
        return dict(rows=rows, kk=kk, qq=qq, ii=ii, ii_bf=ii.astype(BF16), b=b)

    def level_z(v, m):
        b, kk, qq = v["b"], v["kk"], v["qq"]
        if m >= SUBLANES:
            parts, srcs = [], []
            for p in range(0, C, 2 * m):
                bm = b[p + m - 1:p + m, :]
                parts += [bm - b[p:p + m], b[p + m:p + 2 * m] - bm]
                srcs += [kk[p:p + m], qq[p + m:p + 2 * m]]
            arg = jnp.concatenate(parts, axis=0)
            src = jnp.concatenate(srcs, axis=0)
        else:
            b3 = b.reshape(C // SUBLANES, SUBLANES, Dk)
            if m == 1:
                bm = jnp.where(sub % 2 == 1, roll8(b, 1), b3)
            else:
                bm = jnp.broadcast_to(b3[:, m - 1:m, :], b3.shape)
                for p in range(2 * m, SUBLANES, 2 * m):
                    bm = jnp.where(sub >= p, jnp.broadcast_to(b3[:, p + m - 1:p + m, :], b3.shape), bm)
            upper = (sub // m) % 2 == 1
            arg = ((b3 - bm) * jnp.where(upper, 1.0, -1.0)).reshape(C, Dk)
            src = jnp.where(upper, qq.reshape(b3.shape), kk.reshape(b3.shape)).reshape(C, Dk)
        return (src * jnp.exp2(arg)).astype(BF16)

    def group(gi, st):
        vs = [front(gi * HG_UNROLL + j) for j in range(HG_UNROLL)]
        a_lo = [None] * HG_UNROLL
        a_d0 = [jnp.zeros((H2, H2), F32)] * HG_UNROLL
        a_d1 = [jnp.zeros((H2, H2), F32)] * HG_UNROLL
        for li, m in enumerate(HG_LEVELS):
            for j, v in enumerate(vs):
                z = level_z(v, m)
                if li == 0:
                    a_lo[j] = lax.dot_general(z[H2:], z[:H2], nt, preferred_element_type=F32)
                else:
                    g0 = lax.dot_general(z[:H2], z[:H2], nt, preferred_element_type=F32)
                    g1 = lax.dot_general(z[H2:], z[H2:], nt, preferred_element_type=F32)
                    a_d0[j] = jnp.where(lvl_mask[m], g0, a_d0[j])
                    a_d1[j] = jnp.where(lvl_mask[m], g1, a_d1[j])
        intra = []
        for j, v in enumerate(vs):
            o_top = jnp.dot(a_d0[j].astype(BF16), v["ii_bf"][:H2], preferred_element_type=F32)
            o_bot = jnp.dot(jnp.concatenate([a_lo[j], a_d1[j]], axis=1).astype(BF16), v["ii_bf"],
                            preferred_element_type=F32)
            o_diag = jnp.sum(v["qq"] * v["kk"], axis=-1, keepdims=True) * v["ii"]
            intra.append(jnp.concatenate([o_top, o_bot], axis=0) + o_diag)
        for j, v in enumerate(vs):
            b = v["b"]
            b_last = b[C - 1:C, :]
            qe = (v["qq"] * jnp.exp2(b)).astype(BF16)
            o = intra[j] + lax.dot_general(qe, st.astype(BF16), nt, preferred_element_type=F32)
            ke = (v["kk"] * jnp.exp2(b_last - b)).astype(BF16)
            st = st * jnp.exp2(b_last) + lax.dot_general(
                v["ii_bf"], ke, (((0,), (0,)), ((), ())), preferred_element_type=F32)
            ms = jnp.mean(o * o, axis=-1, keepdims=True)
            y = o * lax.rsqrt(ms + EPS) * gn
            y = y * _silu_tanh(g_ref[0, v["rows"], :])
            o_ref[0, v["rows"], :] = y.astype(o_ref.dtype)
        return st

    lax.fori_loop(0, n_steps // HG_UNROLL, group, jnp.zeros((Dk, Dk), F32))


def _hgrn(proj_b, lb_logits_h, gnorm_g):
    B, T, W4 = proj_b.shape
    W = W4 // 4
    H = W // REC_HEAD_DIM
    n_slots = lb_logits_h.shape[1]
    blk = lambda off: pl.BlockSpec((1, T, REC_HEAD_DIM), lambda b, h, off=off: (b, 0, off * H + h))
    return pl.pallas_call(
        functools.partial(_hgrn_kernel, n_steps=T // HG_CHUNK),
        out_shape=jax.ShapeDtypeStruct((B, T, W), BF16),
        grid=(B, H),
        in_specs=[blk(0), blk(1), blk(2), blk(3),
                  pl.BlockSpec((1, n_slots, REC_HEAD_DIM), lambda b, h: (h, 0, 0)),
                  pl.BlockSpec((1, REC_HEAD_DIM), lambda b, h: (0, 0))],
        out_specs=pl.BlockSpec((1, T, REC_HEAD_DIM), lambda b, h: (b, 0, h)),
        compiler_params=_cparams(("parallel", "parallel"), 32),
        name="hgrn2",
    )(proj_b, proj_b, proj_b, proj_b, lb_logits_h, gnorm_g)


def _outproj_kernel(oa_ref, ob_ref, w_ref, x_ref, mod_ref, g_ref, b_ref, x1_ref, h2_ref, *, alpha, sub):
    wa = oa_ref.shape[-1]
    gate1 = mod_ref[0, 2:3, :]
    shift2 = mod_ref[0, 3:4, :]
    scale2 = mod_ref[0, 4:5, :]
    for r0 in range(0, x_ref.shape[1], sub):
        rs = slice(r0, r0 + sub)
        mix = jnp.dot(oa_ref[0, rs, :], w_ref[0:wa, :], preferred_element_type=F32)
        mix = mix + jnp.dot(ob_ref[0, rs, :], w_ref[wa:, :], preferred_element_type=F32)
        x1 = _ln_rows(alpha * x_ref[0, rs, :] + gate1 * mix) * g_ref[...] + b_ref[...]
        x1_ref[0, rs, :] = x1
        h2_ref[0, rs, :] = (_ln_rows(x1) * (1.0 + scale2) + shift2).astype(h2_ref.dtype)


def _outproj(o_a, o_b, w_o, x, mod3, ln_g, ln_b, alpha, tm=512, sub=128):
    B, T, D = x.shape
    Wa, Wb = o_a.shape[-1], o_b.shape[-1]
    return pl.pallas_call(
        functools.partial(_outproj_kernel, alpha=alpha, sub=sub),
        out_shape=(jax.ShapeDtypeStruct((B, T, D), F32), jax.ShapeDtypeStruct((B, T, D), BF16)),
        grid=(B, T // tm),
        in_specs=[pl.BlockSpec((1, tm, Wa), lambda b, i: (b, i, 0)),
                  pl.BlockSpec((1, tm, Wb), lambda b, i: (b, i, 0)),
                  pl.BlockSpec((Wa + Wb, D), lambda b, i: (0, 0)),
                  pl.BlockSpec((1, tm, D), lambda b, i: (b, i, 0)),
                  pl.BlockSpec((1, N_MOD, D), lambda b, i: (b, 0, 0)),
                  pl.BlockSpec((1, D), lambda b, i: (0, 0)),
                  pl.BlockSpec((1, D), lambda b, i: (0, 0))],
        out_specs=(pl.BlockSpec((1, tm, D), lambda b, i: (b, i, 0)),
                   pl.BlockSpec((1, tm, D), lambda b, i: (b, i, 0))),
        compiler_params=_cparams(("parallel", "parallel"), 48),
        name="out_proj_ln1",
    )(o_a, o_b, w_o, x, mod3, ln_g, ln_b)


def _ffn_kernel(h_ref, wg_ref, wu_ref, wo_ref, x_ref, mod_ref, g_ref, b_ref, o_ref, *, alpha, sub):
    f = pl.program_id(2)
    last = pl.num_programs(2) - 1

    def partial_out(rs):
        h = h_ref[0, rs, :]
        gate = jnp.dot(h, wg_ref[...], preferred_element_type=F32)
        up = jnp.dot(h, wu_ref[...], preferred_element_type=F32)
        act = (_silu_tanh(gate) * up).astype(BF16)
        return jnp.dot(act, wo_ref[...], preferred_element_type=F32)

    everything = slice(0, o_ref.shape[1])

    @pl.when(f == 0)
    def _():
        o_ref[0] = partial_out(everything)

    @pl.when((f > 0) & (f < last))
    def _():
        o_ref[0] += partial_out(everything)

    @pl.when(f == last)
    def _():
        gate2 = mod_ref[0, 5:6, :]
        for r0 in range(0, o_ref.shape[1], sub):
            rs = slice(r0, r0 + sub)
            y = o_ref[0, rs, :] + partial_out(rs)
            o_ref[0, rs, :] = _ln_rows(alpha * x_ref[0, rs, :] + gate2 * y) * g_ref[...] + b_ref[...]


def _ffn(h2, w_in, w_out, x1, mod3, ln_g, ln_b, alpha, tm=1024, tf=256, sub=256):
    B, T, D = x1.shape
    F = w_out.shape[0]
    nf = F // tf
    return pl.pallas_call(
        functools.partial(_ffn_kernel, alpha=alpha, sub=sub),
        out_shape=jax.ShapeDtypeStruct((B, T, D), F32),
        grid=(B, T // tm, nf),
        in_specs=[pl.BlockSpec((1, tm, D), lambda b, i, f: (b, i, 0)),
                  pl.BlockSpec((D, tf), lambda b, i, f: (0, f)),
                  pl.BlockSpec((D, tf), lambda b, i, f: (0, nf + f)),
                  pl.BlockSpec((tf, D), lambda b, i, f: (f, 0)),
                  pl.BlockSpec((1, tm, D), lambda b, i, f: (b, i, 0)),
                  pl.BlockSpec((1, N_MOD, D), lambda b, i, f: (b, 0, 0)),
                  pl.BlockSpec((1, D), lambda b, i, f: (0, 0)),
                  pl.BlockSpec((1, D), lambda b, i, f: (0, 0))],
        out_specs=pl.BlockSpec((1, tm, D), lambda b, i, f: (b, i, 0)),
        compiler_params=_cparams(("parallel", "parallel", "arbitrary"), 56),
        name="swiglu_ffn_ln2",
    )(h2, w_in, w_in, w_out, x1, mod3, ln_g, ln_b)


def _bias_vectors(rel_bias):
    H, n_rel = rel_bias.shape
    max_rel = (n_rel - 1) // 2
    u = jnp.arange(BIAS_W)
    idx = jnp.clip(KBLK - u, -max_rel, max_rel) + max_rel
    return rel_bias[:, idx]


def kernel(x, c, w_ada, b_ada, w_in, rel_bias, attn_norm_g, lb_logits, gnorm_g, w_o,
           ln1_g, ln1_b, w_ffn_in, w_ffn_out, ln2_g, ln2_b):
    B, T, D = x.shape
    depth = w_ada.shape[0]
    alpha = (2 * depth) ** 0.25
    attn_w = attn_norm_g.shape[1]
    rec_w = lb_logits.shape[1]
    n_slots = lb_logits.shape[0]
    rec_heads = rec_w // REC_HEAD_DIM
    assert depth == 1 and n_slots == depth + 1
    for layer in range(depth):
        mod3 = _mod(c, w_ada[layer], b_ada[layer]).reshape(B, N_MOD, D)
        h1 = _ln_mod(x, mod3, 0).reshape(B * T, D)
        q_scale = jnp.where(jnp.arange(w_in.shape[2]) < attn_w, ATTN_HEAD_DIM ** -0.5 * LOG2E, 1.0)
        q_scale = q_scale.astype(F32).reshape(1, -1)
        proj_a = _matmul(h1, w_in[layer], q_scale, 0, 3 * attn_w, BF16).reshape(B, T, 3 * attn_w)
        proj_b = _matmul(h1, w_in[layer], q_scale, 3 * attn_w // 1024, 4 * rec_w, F32).reshape(B, T, 4 * rec_w)
        bias_vec = (_bias_vectors(rel_bias[layer]) * LOG2E).reshape(-1, 2, BIAS_W)
        o_a, (w_o_bf, w_ffn_in_bf, w_ffn_out_bf) = _attention(
            proj_a, bias_vec, attn_norm_g[layer].reshape(-1, 1, LANES),
            (w_o[layer], w_ffn_in[layer], w_ffn_out[layer]))
        lbl = lb_logits.reshape(n_slots, rec_heads, REC_HEAD_DIM).transpose(1, 0, 2)
        o_b = _hgrn(proj_b, lbl, gnorm_g[layer].reshape(1, REC_HEAD_DIM))
        x, h2 = _outproj(o_a, o_b, w_o_bf, x, mod3,
                         ln1_g[layer].reshape(1, D), ln1_b[layer].reshape(1, D), alpha)
        x = _ffn(h2, w_ffn_in_bf, w_ffn_out_bf, x, mod3,
                 ln2_g[layer].reshape(1, D), ln2_b[layer].reshape(1, D), alpha)
    return x
```

```python
import functools

import jax
import jax.numpy as jnp
from jax import lax
from jax.experimental import pallas as pl
from jax.experimental.pallas import tpu as pltpu

F32 = jnp.float32
BF16 = jnp.bfloat16

CHUNK = 64
N_PAST_CHUNKS = 8
BAND = (N_PAST_CHUNKS + 1) * CHUNK
ATTN_HEAD_DIM = 64
REC_HEAD_DIM = 128
N_MOD = 6
EPS = 1e-5
LANES = 128
SUBLANES = 8
BF16_ROWS = 16
QBLK = 2 * CHUNK
KBLK = BAND + CHUNK
BIAS_W = KBLK + QBLK
N_PAD_STEPS = N_PAST_CHUNKS * CHUNK // QBLK

MIB = 1024 * 1024


def _cparams(sem, vmem_mib):
    return pltpu.CompilerParams(dimension_semantics=sem, vmem_limit_bytes=vmem_mib * MIB)


def _sigmoid(x):
    return 1.0 / (1.0 + jnp.exp(-x))


def _silu(x):
    return x * _sigmoid(x)


def _silu_tanh(x):
    h = 0.5 * x
    return h + h * jnp.tanh(h)


def _ln_rows(x):
    mu = jnp.mean(x, axis=-1, keepdims=True)
    xc = x - mu
    var = jnp.mean(xc * xc, axis=-1, keepdims=True)
    return xc * lax.rsqrt(var + EPS)


def _mod_kernel(ct_ref, w_ref, b_ref, o_ref):
    ct = ct_ref[...]
    cat = _silu(ct)
    w = w_ref[...]
    rows = [jnp.sum(w * cat[:, b:b + 1], axis=0, keepdims=True) for b in range(ct.shape[1])]
    o_ref[...] = jnp.concatenate(rows, axis=0) + b_ref[...]


def _mod(c, w_ada, b_ada, tn=512):
    B, D = c.shape
    N = w_ada.shape[1]
    return pl.pallas_call(
        _mod_kernel,
        out_shape=jax.ShapeDtypeStruct((B, N), F32),
        grid=(N // tn,),
        in_specs=[pl.BlockSpec((D, B), lambda j: (0, 0)),
                  pl.BlockSpec((D, tn), lambda j: (0, j)),
                  pl.BlockSpec((1, tn), lambda j: (0, j))],
        out_specs=pl.BlockSpec((B, tn), lambda j: (0, j)),
        compiler_params=_cparams(("parallel",), 32),
        name="adaln_mod",
    )(c.T, w_ada, b_ada.reshape(1, N))


def _ln_mod_kernel(x_ref, mod_ref, o_ref, *, shift_row):
    y = _ln_rows(x_ref[0])
    shift = mod_ref[0, shift_row:shift_row + 1, :]
    scale = mod_ref[0, shift_row + 1:shift_row + 2, :]
    o_ref[0] = (y * (1.0 + scale) + shift).astype(o_ref.dtype)


def _ln_mod(x, mod3, shift_row, tm=1024):
    B, T, D = x.shape
    return pl.pallas_call(
        functools.partial(_ln_mod_kernel, shift_row=shift_row),
        out_shape=jax.ShapeDtypeStruct((B, T, D), BF16),
        grid=(B, T // tm),
        in_specs=[pl.BlockSpec((1, tm, D), lambda b, i: (b, i, 0)),
                  pl.BlockSpec((1, N_MOD, D), lambda b, i: (b, 0, 0))],
        out_specs=pl.BlockSpec((1, tm, D), lambda b, i: (b, i, 0)),
        compiler_params=_cparams(("parallel", "parallel"), 32),
        name="ln_modulate",
    )(x, mod3)


def _matmul_kernel(a_ref, w_ref, s_ref, o_ref, w_bf):
    @pl.when(pl.program_id(1) == 0)
    def _():
        w_bf[...] = (w_ref[...] * s_ref[...]).astype(BF16)

    o_ref[...] = jnp.dot(a_ref[...], w_bf[...], preferred_element_type=F32).astype(o_ref.dtype)


def _matmul(a, w, col_scale, col_block0, n_out, out_dtype, tm=1024, tn=1024):
    M, K = a.shape
    return pl.pallas_call(
        _matmul_kernel,
        out_shape=jax.ShapeDtypeStruct((M, n_out), out_dtype),
        grid=(n_out // tn, M // tm),
        in_specs=[pl.BlockSpec((tm, K), lambda j, i: (i, 0)),
                  pl.BlockSpec((K, tn), lambda j, i: (0, j + col_block0)),
                  pl.BlockSpec((1, tn), lambda j, i: (0, j + col_block0))],
        out_specs=pl.BlockSpec((tm, tn), lambda j, i: (i, j)),
        scratch_shapes=[pltpu.VMEM((K, tn), BF16)],
        compiler_params=_cparams(("parallel", "arbitrary"), 48),
        name="in_proj",
    )(a, w, col_scale)


NEG_BIG = -1e30
LOG2E = 1.4426950408889634


def _attn_kernel(q_ref, k_ref, v_ref, bias_ref, gain_ref, *rest, n_chunks, n_cast):
    cast_in = rest[:n_cast]
    o_ref = rest[n_cast]
    cast_out = rest[n_cast + 1:2 * n_cast + 1]
    kta, ktb, vpa, vpb, tab, s_a, s_b, s_c = rest[2 * n_cast + 1:]
    pad = N_PAST_CHUNKS * CHUNK
    T = n_chunks * CHUNK
    n_steps = T // QBLK
    head0 = lax.broadcasted_iota(jnp.int32, (QBLK, LANES), 1) < ATTN_HEAD_DIM
    m0 = jnp.where(head0, 1.0, 0.0).astype(BF16)
    m1 = jnp.where(head0, 0.0, 1.0).astype(BF16)

    head0_t = lax.broadcasted_iota(jnp.int32, (LANES, QBLK), 0) < ATTN_HEAD_DIM
    mt0 = jnp.where(head0_t, 1.0, 0.0).astype(BF16)
    mt1 = jnp.where(head0_t, 0.0, 1.0).astype(BF16)

    for ref in (kta, ktb):
        ref[:, 0:pad] = jnp.zeros((LANES, pad), BF16)
    for ref in (vpa, vpb):
        ref[0:pad, :] = jnp.zeros((pad, 2 * LANES), BF16)

    def prep(blk):
        src = slice(blk * QBLK, (blk + 1) * QBLK)
        dst = slice(pad + blk * QBLK, pad + (blk + 1) * QBLK)
        kt = k_ref[0, src, :].T
        v = v_ref[0, src, :]
        kta[:, dst] = kt * mt0
        ktb[:, dst] = kt * mt1
        vpa[dst, 0:LANES] = v * m0
        vpb[dst, 0:LANES] = v * m1
        vpa[dst, LANES:2 * LANES] = m0
        vpb[dst, LANES:2 * LANES] = m1

    @pl.when(pl.program_id(1) == 0)
    def _():
        qry = lax.broadcasted_iota(jnp.int32, (QBLK, KBLK), 0)
        key = lax.broadcasted_iota(jnp.int32, (QBLK, KBLK), 1)
        in_band = ((qry < CHUNK) & (key < BAND)) | ((qry >= CHUNK) & (key >= CHUNK))
        for hh in range(2):
            g = jnp.broadcast_to(bias_ref[0, hh:hh + 1, :], (QBLK, BIAS_W))
            t = pltpu.roll(g, BIAS_W - QBLK, 1, stride=1, stride_axis=0)[:, :KBLK]
            t = jnp.where(in_band, t, NEG_BIG)
            for var in range(N_PAD_STEPS + 1):
                first_real = pad - var * QBLK
                tv = jnp.where(key >= first_real, t, NEG_BIG) if first_real > 0 else t
                tab[var, hh] = tv.astype(BF16)

    eye = (lax.broadcasted_iota(jnp.int32, (QBLK, QBLK), 0)
           == lax.broadcasted_iota(jnp.int32, (QBLK, QBLK), 1)).astype(BF16)
    gain = gain_ref[0]

    def scores(m, dst):
        prep(m)
        band = slice(m * QBLK, m * QBLK + KBLK)
        var = min(m, N_PAD_STEPS)
        lhs = jnp.concatenate([q_ref[0, m * QBLK:(m + 1) * QBLK, :], eye], axis=1)
        rhs = jnp.concatenate([jnp.concatenate([kta[:, band], ktb[:, band]], axis=1),
                               jnp.concatenate([tab[var, 0], tab[var, 1]], axis=1)], axis=0)
        dst[...] = jnp.dot(lhs, rhs, preferred_element_type=F32)

    def finish(m, src):
        band = slice(m * QBLK, m * QBLK + KBLK)
        s = src[...]
        p = jnp.concatenate(
            [jnp.exp2(sh - jnp.max(sh, axis=-1, keepdims=True)) for sh in (s[:, 0:KBLK], s[:, KBLK:2 * KBLK])],
            axis=1).astype(BF16)
        pv = jnp.dot(p, jnp.concatenate([vpa[band, :], vpb[band, :]], axis=0), preferred_element_type=F32)
        o = pv[:, 0:LANES] * (1.0 / pv[:, LANES:2 * LANES])
        o2 = o * o
        ms0 = jnp.sum(jnp.where(head0, o2, 0.0), axis=-1, keepdims=True) / ATTN_HEAD_DIM
        ms1 = jnp.sum(jnp.where(head0, 0.0, o2), axis=-1, keepdims=True) / ATTN_HEAD_DIM
        y = o * lax.rsqrt(jnp.where(head0, ms0, ms1) + EPS) * gain
        o_ref[0, m * QBLK:(m + 1) * QBLK, :] = y.astype(o_ref.dtype)

    pieces = [(src_ref, dst_ref, r0) for src_ref, dst_ref in zip(cast_in, cast_out)
              for r0 in range(0, src_ref.shape[0], BF16_ROWS)]
    bufs = (s_a, s_b, s_c)
    depth = len(bufs)
    ahead = depth - 1
    for m in range(ahead):
        scores(m, bufs[m % depth])
    for m in range(n_steps):
        if m + ahead < n_steps:
            scores(m + ahead, bufs[(m + ahead) % depth])
        finish(m, bufs[m % depth])
        for src_ref, dst_ref, r0 in pieces[m * len(pieces) // n_steps:(m + 1) * len(pieces) // n_steps]:
            dst_ref[r0:r0 + BF16_ROWS, :] = src_ref[r0:r0 + BF16_ROWS, :].astype(BF16)


def _attention(proj_a, bias_vec, attn_gain, cast_weights):
    B, T, W3 = proj_a.shape
    W = W3 // 3
    n_pairs = W // LANES
    n_grid = n_pairs * B
    t_pad = T + N_PAST_CHUNKS * CHUNK
    cast_specs = []
    for w in cast_weights:
        rows = w.shape[0] // n_grid
        assert w.shape[0] % n_grid == 0 and rows % BF16_ROWS == 0
        cast_specs.append(pl.BlockSpec((rows, w.shape[1]), lambda h, b: (h * B + b, 0)))
    outs = pl.pallas_call(
        functools.partial(_attn_kernel, n_chunks=T // CHUNK, n_cast=len(cast_weights)),
        out_shape=(jax.ShapeDtypeStruct((B, T, W), BF16),
                   *[jax.ShapeDtypeStruct(w.shape, BF16) for w in cast_weights]),
        grid=(n_pairs, B),
        in_specs=[pl.BlockSpec((1, T, LANES), lambda h, b: (b, 0, h)),
                  pl.BlockSpec((1, T, LANES), lambda h, b: (b, 0, n_pairs + h)),
                  pl.BlockSpec((1, T, LANES), lambda h, b: (b, 0, 2 * n_pairs + h)),
                  pl.BlockSpec((1, 2, BIAS_W), lambda h, b: (h, 0, 0)),
                  pl.BlockSpec((1, 1, LANES), lambda h, b: (h, 0, 0)),
                  *cast_specs],
        out_specs=(pl.BlockSpec((1, T, LANES), lambda h, b: (b, 0, h)), *cast_specs),
        scratch_shapes=[pltpu.VMEM((LANES, t_pad), BF16),
                        pltpu.VMEM((LANES, t_pad), BF16),
                        pltpu.VMEM((t_pad, 2 * LANES), BF16),
                        pltpu.VMEM((t_pad, 2 * LANES), BF16),
                        pltpu.VMEM((N_PAD_STEPS + 1, 2, QBLK, KBLK), BF16),
                        pltpu.VMEM((QBLK, 2 * KBLK), F32),
                        pltpu.VMEM((QBLK, 2 * KBLK), F32),
                        pltpu.VMEM((QBLK, 2 * KBLK), F32)],
        compiler_params=_cparams(("parallel", "arbitrary"), 48),
        name="chunk_attention",
    )(proj_a, proj_a, proj_a, bias_vec, attn_gain, *cast_weights)
    return outs[0], outs[1:]


HG_CHUNK = 256
HG_LEVELS = (128, 64, 32, 16, 8, 4, 2, 1)
assert HG_LEVELS[-1] == 1
HG_UNROLL = 8


def _split3(x):
    hi = x.astype(BF16)
    r1 = x - hi.astype(F32)
    mid = r1.astype(BF16)
    lo = (r1 - mid.astype(F32)).astype(BF16)
    return hi, mid, lo


def _hgrn_kernel(q_ref, f_ref, i_ref, g_ref, lbl_ref, gn_ref, o_ref, *, n_steps):
    C = HG_CHUNK
    H2 = C // 2
    Dk = REC_HEAD_DIM
    nt = (((1,), (1,)), ((), ()))
    lbl = lbl_ref[0]
    e = jnp.exp(lbl - jnp.max(lbl, axis=0, keepdims=True))
    lb = e[0:1, :] / jnp.sum(e, axis=0, keepdims=True)
    c1 = 0.5 * (1.0 - lb)
    gn = gn_ref[...]

    r = lax.broadcasted_iota(jnp.int32, (C, C), 0)
    s = lax.broadcasted_iota(jnp.int32, (C, C), 1)
    tril = (s <= r).astype(BF16)
    rh = lax.broadcasted_iota(jnp.int32, (H2, H2), 0)
    sh = lax.broadcasted_iota(jnp.int32, (H2, H2), 1)
    lvl_mask = {m: ((rh // (2 * m)) == (sh // (2 * m))) & (((rh // m) % 2) == 1) & (((sh // m) % 2) == 0)
                for m in HG_LEVELS[1:]}
    sub = lax.broadcasted_iota(jnp.int32, (C // SUBLANES, SUBLANES, Dk), 1)

    def roll8(x, d):
        return pltpu.roll(x.reshape(C // SUBLANES, SUBLANES, Dk), d, 1)

    def front(n):
        row0 = pl.multiple_of(n * C, C)
        rows = pl.ds(row0, C)
        c1t = c1 * jnp.tanh(0.5 * f_ref[0, rows, :])
        f = (1.0 - c1) + c1t
        kk = c1 - c1t
        qq = _silu_tanh(q_ref[0, rows, :])
        ii = i_ref[0, rows, :]
        hi, mid, lo = _split3(jnp.log2(f))
        bb = jnp.dot(tril, jnp.concatenate([hi, mid, lo], axis=1), preferred_element_type=F32)
        b = bb[:, 0:Dk] + bb[:, Dk:2 * Dk] + bb[:, 2 * Dk:3 * Dk]
        return dict(rows=rows, kk=kk, qq=qq, ii=ii, ii_bf=ii.astype(BF16), b=b)

    def level_z(v, m):
        b, kk, qq = v["b"], v["kk"], v["qq"]
        if m >= SUBLANES:
            parts, srcs = [], []
            for p in range(0, C, 2 * m):
                bm = b[p + m - 1:p + m, :]
                parts += [bm - b[p:p + m], b[p + m:p + 2 * m] - bm]
                srcs += [kk[p:p + m], qq[p + m:p + 2 * m]]
            arg = jnp.concatenate(parts, axis=0)
            src = jnp.concatenate(srcs, axis=0)
        else:
            b3 = b.reshape(C // SUBLANES, SUBLANES, Dk)
            if m == 1:
                bm = jnp.where(sub % 2 == 1, roll8(b, 1), b3)
            else:
                bm = jnp.broadcast_to(b3[:, m - 1:m, :], b3.shape)
                for p in range(2 * m, SUBLANES, 2 * m):
                    bm = jnp.where(sub >= p, jnp.broadcast_to(b3[:, p + m - 1:p + m, :], b3.shape), bm)
            upper = (sub // m) % 2 == 1
            arg = ((b3 - bm) * jnp.where(upper, 1.0, -1.0)).reshape(C, Dk)
            src = jnp.where(upper, qq.reshape(b3.shape), kk.reshape(b3.shape)).reshape(C, Dk)
        return (src * jnp.exp2(arg)).astype(BF16)

    def group(gi, st):
        vs = [front(gi * HG_UNROLL + j) for j in range(HG_UNROLL)]
        a_lo = [None] * HG_UNROLL
        a_d0 = [jnp.zeros((H2, H2), F32)] * HG_UNROLL
        a_d1 = [jnp.zeros((H2, H2), F32)] * HG_UNROLL
        for li, m in enumerate(HG_LEVELS):
            for j, v in enumerate(vs):
                z = level_z(v, m)
                if li == 0:
                    a_lo[j] = lax.dot_general(z[H2:], z[:H2], nt, preferred_element_type=F32)
                else:
                    g0 = lax.dot_general(z[:H2], z[:H2], nt, preferred_element_type=F32)
                    g1 = lax.dot_general(z[H2:], z[H2:], nt, preferred_element_type=F32)
                    a_d0[j] = jnp.where(lvl_mask[m], g0, a_d0[j])
                    a_d1[j] = jnp.where(lvl_mask[m], g1, a_d1[j])
        intra = []
        for j, v in enumerate(vs):
            o_top = jnp.dot(a_d0[j].astype(BF16), v["ii_bf"][:H2], preferred_element_type=F32)
            o_bot = jnp.dot(jnp.concatenate([a_lo[j], a_d1[j]], axis=1).astype(BF16), v["ii_bf"],
                            preferred_element_type=F32)
            o_diag = jnp.sum(v["qq"] * v["kk"], axis=-1, keepdims=True) * v["ii"]
            intra.append(jnp.concatenate([o_top, o_bot], axis=0) + o_diag)
        for j, v in enumerate(vs):
            b = v["b"]
            b_last = b[C - 1:C, :]
            qe = (v["qq"] * jnp.exp2(b)).astype(BF16)
            o = intra[j] + lax.dot_general(qe, st.astype(BF16), nt, preferred_element_type=F32)
            ke = (v["kk"] * jnp.exp2(b_last - b)).astype(BF16)
            st = st * jnp.exp2(b_last) + lax.dot_general(
                v["ii_bf"], ke, (((0,), (0,)), ((), ())), preferred_element_type=F32)
            ms = jnp.mean(o * o, axis=-1, keepdims=True)
            y = o * lax.rsqrt(ms + EPS) * gn
            y = y * _silu_tanh(g_ref[0, v["rows"], :])
            o_ref[0, v["rows"], :] = y.astype(o_ref.dtype)
        return st

    lax.fori_loop(0, n_steps // HG_UNROLL, group, jnp.zeros((Dk, Dk), F32))


def _hgrn(proj_b, lb_logits_h, gnorm_g):
    B, T, W4 = proj_b.shape
    W = W4 // 4
    H = W // REC_HEAD_DIM
    n_slots = lb_logits_h.shape[1]
    blk = lambda off: pl.BlockSpec((1, T, REC_HEAD_DIM), lambda b, h, off=off: (b, 0, off * H + h))
    return pl.pallas_call(
        functools.partial(_hgrn_kernel, n_steps=T // HG_CHUNK),
        out_shape=jax.ShapeDtypeStruct((B, T, W), BF16),
        grid=(B, H),
        in_specs=[blk(0), blk(1), blk(2), blk(3),
                  pl.BlockSpec((1, n_slots, REC_HEAD_DIM), lambda b, h: (h, 0, 0)),
                  pl.BlockSpec((1, REC_HEAD_DIM), lambda b, h: (0, 0))],
        out_specs=pl.BlockSpec((1, T, REC_HEAD_DIM), lambda b, h: (b, 0, h)),
        compiler_params=_cparams(("parallel", "parallel"), 32),
        name="hgrn2",
    )(proj_b, proj_b, proj_b, proj_b, lb_logits_h, gnorm_g)


def _outproj_kernel(oa_ref, ob_ref, w_ref, x_ref, mod_ref, g_ref, b_ref, x1_ref, *, alpha, sub):
    wa = oa_ref.shape[-1]
    gate1 = mod_ref[0, 2:3, :]
    for r0 in range(0, x_ref.shape[1], sub):
        rs = slice(r0, r0 + sub)
        mix = jnp.dot(oa_ref[0, rs, :], w_ref[0:wa, :], preferred_element_type=F32)
        mix = mix + jnp.dot(ob_ref[0, rs, :], w_ref[wa:, :], preferred_element_type=F32)
        x1_ref[0, rs, :] = _ln_rows(alpha * x_ref[0, rs, :] + gate1 * mix) * g_ref[...] + b_ref[...]


def _outproj(o_a, o_b, w_o, x, mod3, ln_g, ln_b, alpha, tm=512, sub=128):
    B, T, D = x.shape
    Wa, Wb = o_a.shape[-1], o_b.shape[-1]
    return pl.pallas_call(
        functools.partial(_outproj_kernel, alpha=alpha, sub=sub),
        out_shape=jax.ShapeDtypeStruct((B, T, D), F32),
        grid=(B, T // tm),
        in_specs=[pl.BlockSpec((1, tm, Wa), lambda b, i: (b, i, 0)),
                  pl.BlockSpec((1, tm, Wb), lambda b, i: (b, i, 0)),
                  pl.BlockSpec((Wa + Wb, D), lambda b, i: (0, 0)),
                  pl.BlockSpec((1, tm, D), lambda b, i: (b, i, 0)),
                  pl.BlockSpec((1, N_MOD, D), lambda b, i: (b, 0, 0)),
                  pl.BlockSpec((1, D), lambda b, i: (0, 0)),
                  pl.BlockSpec((1, D), lambda b, i: (0, 0))],
        out_specs=pl.BlockSpec((1, tm, D), lambda b, i: (b, i, 0)),
        compiler_params=_cparams(("parallel", "parallel"), 48),
        name="out_proj_ln1",
    )(o_a, o_b, w_o, x, mod3, ln_g, ln_b)


def _ffn_kernel(wg_ref, wu_ref, wo_ref, x_ref, mod_ref, g_ref, b_ref, o_ref, h_sc, *, alpha, sub):
    f = pl.program_id(2)
    last = pl.num_programs(2) - 1

    def partial_out(rs):
        h = h_sc[rs, :]
        gate = jnp.dot(h, wg_ref[...], preferred_element_type=F32)
        up = jnp.dot(h, wu_ref[...], preferred_element_type=F32)
        act = (_silu_tanh(gate) * up).astype(BF16)
        return jnp.dot(act, wo_ref[...], preferred_element_type=F32)

    everything = slice(0, o_ref.shape[1])

    @pl.when(f == 0)
    def _():
        shift2 = mod_ref[0, 3:4, :]
        scale2 = mod_ref[0, 4:5, :]
        for r0 in range(0, o_ref.shape[1], sub):
            rs = slice(r0, r0 + sub)
            h_sc[rs, :] = (_ln_rows(x_ref[0, rs, :]) * (1.0 + scale2) + shift2).astype(BF16)
        o_ref[0] = partial_out(everything)

    @pl.when((f > 0) & (f < last))
    def _():
        o_ref[0] += partial_out(everything)

    @pl.when(f == last)
    def _():
        gate2 = mod_ref[0, 5:6, :]
        for r0 in range(0, o_ref.shape[1], sub):
            rs = slice(r0, r0 + sub)
            y = o_ref[0, rs, :] + partial_out(rs)
            o_ref[0, rs, :] = _ln_rows(alpha * x_ref[0, rs, :] + gate2 * y) * g_ref[...] + b_ref[...]


def _ffn(w_in, w_out, x1, mod3, ln_g, ln_b, alpha, tm=1024, tf=256, sub=256):
    B, T, D = x1.shape
    F = w_out.shape[0]
    nf = F // tf
    return pl.pallas_call(
        functools.partial(_ffn_kernel, alpha=alpha, sub=sub),
        out_shape=jax.ShapeDtypeStruct((B, T, D), F32),
        grid=(B, T // tm, nf),
        in_specs=[pl.BlockSpec((D, tf), lambda b, i, f: (0, f)),
                  pl.BlockSpec((D, tf), lambda b, i, f: (0, nf + f)),
                  pl.BlockSpec((tf, D), lambda b, i, f: (f, 0)),
                  pl.BlockSpec((1, tm, D), lambda b, i, f: (b, i, 0)),
                  pl.BlockSpec((1, N_MOD, D), lambda b, i, f: (b, 0, 0)),
                  pl.BlockSpec((1, D), lambda b, i, f: (0, 0)),
                  pl.BlockSpec((1, D), lambda b, i, f: (0, 0))],
        out_specs=pl.BlockSpec((1, tm, D), lambda b, i, f: (b, i, 0)),
        scratch_shapes=[pltpu.VMEM((tm, D), BF16)],
        compiler_params=_cparams(("parallel", "parallel", "arbitrary"), 56),
        name="swiglu_ffn_ln2",
    )(w_in, w_in, w_out, x1, mod3, ln_g, ln_b)


def _bias_vectors(rel_bias):
    H, n_rel = rel_bias.shape
    max_rel = (n_rel - 1) // 2
    u = jnp.arange(BIAS_W)
    idx = jnp.clip(KBLK - u, -max_rel, max_rel) + max_rel
    return rel_bias[:, idx]


def kernel(x, c, w_ada, b_ada, w_in, rel_bias, attn_norm_g, lb_logits, gnorm_g, w_o,
           ln1_g, ln1_b, w_ffn_in, w_ffn_out, ln2_g, ln2_b):
    B, T, D = x.shape
    depth = w_ada.shape[0]
    alpha = (2 * depth) ** 0.25
    attn_w = attn_norm_g.shape[1]
    rec_w = lb_logits.shape[1]
    n_slots = lb_logits.shape[0]
    rec_heads = rec_w // REC_HEAD_DIM
    assert depth == 1 and n_slots == depth + 1
    for layer in range(depth):
        mod3 = _mod(c, w_ada[layer], b_ada[layer]).reshape(B, N_MOD, D)
        h1 = _ln_mod(x, mod3, 0).reshape(B * T, D)
        q_scale = jnp.where(jnp.arange(w_in.shape[2]) < attn_w, ATTN_HEAD_DIM ** -0.5 * LOG2E, 1.0)
        q_scale = q_scale.astype(F32).reshape(1, -1)
        proj_a = _matmul(h1, w_in[layer], q_scale, 0, 3 * attn_w, BF16).reshape(B, T, 3 * attn_w)
        proj_b = _matmul(h1, w_in[layer], q_scale, 3 * attn_w // 1024, 4 * rec_w, F32).reshape(B, T, 4 * rec_w)
        bias_vec = (_bias_vectors(rel_bias[layer]) * LOG2E).reshape(-1, 2, BIAS_W)
        o_a, (w_o_bf, w_ffn_in_bf, w_ffn_out_bf) = _attention(
            proj_a, bias_vec, attn_norm_g[layer].reshape(-1, 1, LANES),
            (w_o[layer], w_ffn_in[layer], w_ffn_out[layer]))
        lbl = lb_logits.reshape(n_slots, rec_heads, REC_HEAD_DIM).transpose(1, 0, 2)
        o_b = _hgrn(proj_b, lbl, gnorm_g[layer].reshape(1, REC_HEAD_DIM))
        x = _outproj(o_a, o_b, w_o_bf, x, mod3,
                     ln1_g[layer].reshape(1, D), ln1_b[layer].reshape(1, D), alpha)
        x = _ffn(w_ffn_in_bf, w_ffn_out_bf, x, mod3,
                 ln2_g[layer].reshape(1, D), ln2_b[layer].reshape(1, D), alpha)
    return x
```

```python
import functools

import jax
import jax.numpy as jnp
from jax import lax
from jax.experimental import pallas as pl
from jax.experimental.pallas import tpu as pltpu

F32 = jnp.float32
BF16 = jnp.bfloat16

CHUNK = 64
N_PAST_CHUNKS = 8
BAND = (N_PAST_CHUNKS + 1) * CHUNK
ATTN_HEAD_DIM = 64
REC_HEAD_DIM = 128
N_MOD = 6
EPS = 1e-5
LANES = 128
SUBLANES = 8
BF16_ROWS = 16
QBLK = 2 * CHUNK
KBLK = BAND + CHUNK
BIAS_W = KBLK + QBLK
N_PAD_STEPS = N_PAST_CHUNKS * CHUNK // QBLK

MIB = 1024 * 1024


def _cparams(sem, vmem_mib):
    return pltpu.CompilerParams(dimension_semantics=sem, vmem_limit_bytes=vmem_mib * MIB)


def _sigmoid(x):
    return 1.0 / (1.0 + jnp.exp(-x))


def _silu(x):
    return x * _sigmoid(x)


def _silu_tanh(x):
    h = 0.5 * x
    return h + h * jnp.tanh(h)


def _ln_rows(x):
    mu = jnp.mean(x, axis=-1, keepdims=True)
    xc = x - mu
    var = jnp.mean(xc * xc, axis=-1, keepdims=True)
    return xc * lax.rsqrt(var + EPS)


def _mod_kernel(ct_ref, w_ref, b_ref, o_ref):
    ct = ct_ref[...]
    cat = _silu(ct)
    w = w_ref[...]
    rows = [jnp.sum(w * cat[:, b:b + 1], axis=0, keepdims=True) for b in range(ct.shape[1])]
    o_ref[...] = jnp.concatenate(rows, axis=0) + b_ref[...]


def _mod(c, w_ada, b_ada, tn=512):
    B, D = c.shape
    N = w_ada.shape[1]
    return pl.pallas_call(
        _mod_kernel,
        out_shape=jax.ShapeDtypeStruct((B, N), F32),
        grid=(N // tn,),
        in_specs=[pl.BlockSpec((D, B), lambda j: (0, 0)),
                  pl.BlockSpec((D, tn), lambda j: (0, j)),
                  pl.BlockSpec((1, tn), lambda j: (0, j))],
        out_specs=pl.BlockSpec((B, tn), lambda j: (0, j)),
        compiler_params=_cparams(("parallel",), 32),
        name="adaln_mod",
    )(c.T, w_ada, b_ada.reshape(1, N))


def _ln_mod_kernel(x_ref, mod_ref, o_ref, *, shift_row):
    y = _ln_rows(x_ref[0])
    shift = mod_ref[0, shift_row:shift_row + 1, :]
    scale = mod_ref[0, shift_row + 1:shift_row + 2, :]
    o_ref[0] = (y * (1.0 + scale) + shift).astype(o_ref.dtype)


def _ln_mod(x, mod3, shift_row, tm=1024):
    B, T, D = x.shape
    return pl.pallas_call(
        functools.partial(_ln_mod_kernel, shift_row=shift_row),
        out_shape=jax.ShapeDtypeStruct((B, T, D), BF16),
        grid=(B, T // tm),
        in_specs=[pl.BlockSpec((1, tm, D), lambda b, i: (b, i, 0)),
                  pl.BlockSpec((1, N_MOD, D), lambda b, i: (b, 0, 0))],
        out_specs=pl.BlockSpec((1, tm, D), lambda b, i: (b, i, 0)),
        compiler_params=_cparams(("parallel", "parallel"), 32),
        name="ln_modulate",
    )(x, mod3)


def _matmul_kernel(a_ref, w_ref, s_ref, o_ref, w_bf):
    @pl.when(pl.program_id(1) == 0)
    def _():
        w_bf[...] = (w_ref[...] * s_ref[...]).astype(BF16)

    o_ref[...] = jnp.dot(a_ref[...], w_bf[...], preferred_element_type=F32).astype(o_ref.dtype)


def _matmul(a, w, col_scale, col_block0, n_out, out_dtype, tm=1024, tn=1024):
    M, K = a.shape
    return pl.pallas_call(
        _matmul_kernel,
        out_shape=jax.ShapeDtypeStruct((M, n_out), out_dtype),
        grid=(n_out // tn, M // tm),
        in_specs=[pl.BlockSpec((tm, K), lambda j, i: (i, 0)),
                  pl.BlockSpec((K, tn), lambda j, i: (0, j + col_block0)),
                  pl.BlockSpec((1, tn), lambda j, i: (0, j + col_block0))],
        out_specs=pl.BlockSpec((tm, tn), lambda j, i: (i, j)),
        scratch_shapes=[pltpu.VMEM((K, tn), BF16)],
        compiler_params=_cparams(("parallel", "arbitrary"), 48),
        name="in_proj",
    )(a, w, col_scale)


NEG_BIG = -1e30
LOG2E = 1.4426950408889634


def _attn_kernel(q_ref, k_ref, v_ref, bias_ref, gain_ref, *rest, n_chunks, n_cast):
    cast_in = rest[:n_cast]
    o_ref = rest[n_cast]
    cast_out = rest[n_cast + 1:2 * n_cast + 1]
    kta, ktb, vpa, vpb, tab, s_a, s_b, s_c = rest[2 * n_cast + 1:]
    pad = N_PAST_CHUNKS * CHUNK
    T = n_chunks * CHUNK
    n_steps = T // QBLK
    head0 = lax.broadcasted_iota(jnp.int32, (QBLK, LANES), 1) < ATTN_HEAD_DIM
    m0 = jnp.where(head0, 1.0, 0.0).astype(BF16)
    m1 = jnp.where(head0, 0.0, 1.0).astype(BF16)

    head0_t = lax.broadcasted_iota(jnp.int32, (LANES, QBLK), 0) < ATTN_HEAD_DIM
    mt0 = jnp.where(head0_t, 1.0, 0.0).astype(BF16)
    mt1 = jnp.where(head0_t, 0.0, 1.0).astype(BF16)

    for ref in (kta, ktb):
        ref[:, 0:pad] = jnp.zeros((LANES, pad), BF16)
    for ref in (vpa, vpb):
        ref[0:pad, :] = jnp.zeros((pad, 2 * LANES), BF16)

    def prep(blk):
        src = slice(blk * QBLK, (blk + 1) * QBLK)
        dst = slice(pad + blk * QBLK, pad + (blk + 1) * QBLK)
        kt = k_ref[0, src, :].T
        v = v_ref[0, src, :]
        kta[:, dst] = kt * mt0
        ktb[:, dst] = kt * mt1
        vpa[dst, 0:LANES] = v * m0
        vpb[dst, 0:LANES] = v * m1
        vpa[dst, LANES:2 * LANES] = m0
        vpb[dst, LANES:2 * LANES] = m1

    @pl.when(pl.program_id(1) == 0)
    def _():
        qry = lax.broadcasted_iota(jnp.int32, (QBLK, KBLK), 0)
        key = lax.broadcasted_iota(jnp.int32, (QBLK, KBLK), 1)
        in_band = ((qry < CHUNK) & (key < BAND)) | ((qry >= CHUNK) & (key >= CHUNK))
        for hh in range(2):
            g = jnp.broadcast_to(bias_ref[0, hh:hh + 1, :], (QBLK, BIAS_W))
            t = pltpu.roll(g, BIAS_W - QBLK, 1, stride=1, stride_axis=0)[:, :KBLK]
            t = jnp.where(in_band, t, NEG_BIG)
            for var in range(N_PAD_STEPS + 1):
                first_real = pad - var * QBLK
                tv = jnp.where(key >= first_real, t, NEG_BIG) if first_real > 0 else t
                tab[var, hh] = tv.astype(BF16)

    eye = (lax.broadcasted_iota(jnp.int32, (QBLK, QBLK), 0)
           == lax.broadcasted_iota(jnp.int32, (QBLK, QBLK), 1)).astype(BF16)
    gain = gain_ref[0]

    def scores(m, dst):
        prep(m)
        band = slice(m * QBLK, m * QBLK + KBLK)
        var = min(m, N_PAD_STEPS)
        lhs = jnp.concatenate([q_ref[0, m * QBLK:(m + 1) * QBLK, :], eye], axis=1)
        rhs = jnp.concatenate([jnp.concatenate([kta[:, band], ktb[:, band]], axis=1),
                               jnp.concatenate([tab[var, 0], tab[var, 1]], axis=1)], axis=0)
        dst[...] = jnp.dot(lhs, rhs, preferred_element_type=F32)

    def finish(m, src):
        band = slice(m * QBLK, m * QBLK + KBLK)
        s = src[...]
        p = jnp.concatenate(
            [jnp.exp2(sh - jnp.max(sh, axis=-1, keepdims=True)) for sh in (s[:, 0:KBLK], s[:, KBLK:2 * KBLK])],
            axis=1).astype(BF16)
        pv = jnp.dot(p, jnp.concatenate([vpa[band, :], vpb[band, :]], axis=0), preferred_element_type=F32)
        o = pv[:, 0:LANES] * (1.0 / pv[:, LANES:2 * LANES])
        o2 = o * o
        ms0 = jnp.sum(jnp.where(head0, o2, 0.0), axis=-1, keepdims=True) / ATTN_HEAD_DIM
        ms1 = jnp.sum(jnp.where(head0, 0.0, o2), axis=-1, keepdims=True) / ATTN_HEAD_DIM
        y = o * lax.rsqrt(jnp.where(head0, ms0, ms1) + EPS) * gain
        o_ref[0, m * QBLK:(m + 1) * QBLK, :] = y.astype(o_ref.dtype)

    pieces = [(src_ref, dst_ref, r0) for src_ref, dst_ref in zip(cast_in, cast_out)
              for r0 in range(0, src_ref.shape[0], BF16_ROWS)]
    bufs = (s_a, s_b, s_c)
    depth = len(bufs)
    ahead = depth - 1
    for m in range(ahead):
        scores(m, bufs[m % depth])
    for m in range(n_steps):
        if m + ahead < n_steps:
            scores(m + ahead, bufs[(m + ahead) % depth])
        finish(m, bufs[m % depth])
        for src_ref, dst_ref, r0 in pieces[m * len(pieces) // n_steps:(m + 1) * len(pieces) // n_steps]:
            dst_ref[r0:r0 + BF16_ROWS, :] = src_ref[r0:r0 + BF16_ROWS, :].astype(BF16)


def _attention(proj_a, bias_vec, attn_gain, cast_weights):
    B, T, W3 = proj_a.shape
    W = W3 // 3
    n_pairs = W // LANES
    n_grid = n_pairs * B
    t_pad = T + N_PAST_CHUNKS * CHUNK
    cast_specs = []
    for w in cast_weights:
        rows = w.shape[0] // n_grid
        assert w.shape[0] % n_grid == 0 and rows % BF16_ROWS == 0
        cast_specs.append(pl.BlockSpec((rows, w.shape[1]), lambda h, b: (h * B + b, 0)))
    outs = pl.pallas_call(
        functools.partial(_attn_kernel, n_chunks=T // CHUNK, n_cast=len(cast_weights)),
        out_shape=(jax.ShapeDtypeStruct((B, T, W), BF16),
                   *[jax.ShapeDtypeStruct(w.shape, BF16) for w in cast_weights]),
        grid=(n_pairs, B),
        in_specs=[pl.BlockSpec((1, T, LANES), lambda h, b: (b, 0, h)),
                  pl.BlockSpec((1, T, LANES), lambda h, b: (b, 0, n_pairs + h)),
                  pl.BlockSpec((1, T, LANES), lambda h, b: (b, 0, 2 * n_pairs + h)),
                  pl.BlockSpec((1, 2, BIAS_W), lambda h, b: (h, 0, 0)),
                  pl.BlockSpec((1, 1, LANES), lambda h, b: (h, 0, 0)),
                  *cast_specs],
        out_specs=(pl.BlockSpec((1, T, LANES), lambda h, b: (b, 0, h)), *cast_specs),
        scratch_shapes=[pltpu.VMEM((LANES, t_pad), BF16),
                        pltpu.VMEM((LANES, t_pad), BF16),
                        pltpu.VMEM((t_pad, 2 * LANES), BF16),
                        pltpu.VMEM((t_pad, 2 * LANES), BF16),
                        pltpu.VMEM((N_PAD_STEPS + 1, 2, QBLK, KBLK), BF16),
                        pltpu.VMEM((QBLK, 2 * KBLK), F32),
                        pltpu.VMEM((QBLK, 2 * KBLK), F32),
                        pltpu.VMEM((QBLK, 2 * KBLK), F32)],
        compiler_params=_cparams(("parallel", "arbitrary"), 48),
        name="chunk_attention",
    )(proj_a, proj_a, proj_a, bias_vec, attn_gain, *cast_weights)
    return outs[0], outs[1:]


HG_CHUNK = 256
HG_LEVELS = (128, 64, 32, 16, 8, 4, 2, 1)
assert HG_LEVELS[-1] == 1
HG_UNROLL = 8


def _split3(x):
    hi = x.astype(BF16)
    r1 = x - hi.astype(F32)
    mid = r1.astype(BF16)
    lo = (r1 - mid.astype(F32)).astype(BF16)
    return hi, mid, lo


def _hgrn_kernel(q_ref, f_ref, i_ref, g_ref, lbl_ref, gn_ref, o_ref, *, n_steps):
    C = HG_CHUNK
    H2 = C // 2
    Dk = REC_HEAD_DIM
    nt = (((1,), (1,)), ((), ()))
    lbl = lbl_ref[0]
    e = jnp.exp(lbl - jnp.max(lbl, axis=0, keepdims=True))
    lb = e[0:1, :] / jnp.sum(e, axis=0, keepdims=True)
    c1 = 0.5 * (1.0 - lb)
    gn = gn_ref[...]

    r = lax.broadcasted_iota(jnp.int32, (C, C), 0)
    s = lax.broadcasted_iota(jnp.int32, (C, C), 1)
    tril = (s <= r).astype(BF16)
    rh = lax.broadcasted_iota(jnp.int32, (H2, H2), 0)
    sh = lax.broadcasted_iota(jnp.int32, (H2, H2), 1)
    lvl_mask = {m: ((rh // (2 * m)) == (sh // (2 * m))) & (((rh // m) % 2) == 1) & (((sh // m) % 2) == 0)
                for m in HG_LEVELS[1:]}
    sub = lax.broadcasted_iota(jnp.int32, (C // SUBLANES, SUBLANES, Dk), 1)

    def roll8(x, d):
        return pltpu.roll(x.reshape(C // SUBLANES, SUBLANES, Dk), d, 1)

    def front(n):
        row0 = pl.multiple_of(n * C, C)
        rows = pl.ds(row0, C)
        c1t = c1 * jnp.tanh(0.5 * f_ref[0, rows, :])
        f = (1.0 - c1) + c1t
        kk = c1 - c1t
        qq = _silu_tanh(q_ref[0, rows, :])
        ii = i_ref[0, rows, :]
        hi, mid, lo = _split3(jnp.log2(f))
        bb = jnp.dot(tril, jnp.concatenate([hi, mid, lo], axis=1), preferred_element_type=F32)
        b = bb[:, 0:Dk] + bb[:, Dk:2 * Dk] + bb[:, 2 * Dk:3 * Dk]
        return dict(rows=rows, kk=kk, qq=qq, ii=ii, ii_bf=ii.astype(BF16), b=b)

    def level_z(v, m):
        b, kk, qq = v["b"], v["kk"], v["qq"]
        if m >= SUBLANES:
            parts, srcs = [], []
            for p in range(0, C, 2 * m):
                bm = b[p + m - 1:p + m, :]
                parts += [bm - b[p:p + m], b[p + m:p + 2 * m] - bm]
                srcs += [kk[p:p + m], qq[p + m:p + 2 * m]]
            arg = jnp.concatenate(parts, axis=0)
            src = jnp.concatenate(srcs, axis=0)
        else:
            b3 = b.reshape(C // SUBLANES, SUBLANES, Dk)
            if m == 1:
                bm = jnp.where(sub % 2 == 1, roll8(b, 1), b3)
            else:
                bm = jnp.broadcast_to(b3[:, m - 1:m, :], b3.shape)
                for p in range(2 * m, SUBLANES, 2 * m):
                    bm = jnp.where(sub >= p, jnp.broadcast_to(b3[:, p + m - 1:p + m, :], b3.shape), bm)
            upper = (sub // m) % 2 == 1
            arg = ((b3 - bm) * jnp.where(upper, 1.0, -1.0)).reshape(C, Dk)
            src = jnp.where(upper, qq.reshape(b3.shape), kk.reshape(b3.shape)).reshape(C, Dk)
        return (src * jnp.exp2(arg)).astype(BF16)

    def group(gi, st):
        vs = [front(gi * HG_UNROLL + j) for j in range(HG_UNROLL)]
        a_lo = [None] * HG_UNROLL
        a_d0 = [jnp.zeros((H2, H2), F32)] * HG_UNROLL
        a_d1 = [jnp.zeros((H2, H2), F32)] * HG_UNROLL
        for li, m in enumerate(HG_LEVELS):
            for j, v in enumerate(vs):
                z = level_z(v, m)
                if li == 0:
                    a_lo[j] = lax.dot_general(z[H2:], z[:H2], nt, preferred_element_type=F32)
                else:
                    g0 = lax.dot_general(z[:H2], z[:H2], nt, preferred_element_type=F32)
                    g1 = lax.dot_general(z[H2:], z[H2:], nt, preferred_element_type=F32)
                    a_d0[j] = jnp.where(lvl_mask[m], g0, a_d0[j])
                    a_d1[j] = jnp.where(lvl_mask[m], g1, a_d1[j])
        intra = []
        for j, v in enumerate(vs):
            o_top = jnp.dot(a_d0[j].astype(BF16), v["ii_bf"][:H2], preferred_element_type=F32)
            o_bot = jnp.dot(jnp.concatenate([a_lo[j], a_d1[j]], axis=1).astype(BF16), v["ii_bf"],
                            preferred_element_type=F32)
            o_diag = jnp.sum(v["qq"] * v["kk"], axis=-1, keepdims=True) * v["ii"]
            intra.append(jnp.concatenate([o_top, o_bot], axis=0) + o_diag)
        for j, v in enumerate(vs):
            b = v["b"]
            b_last = b[C - 1:C, :]
            qe = (v["qq"] * jnp.exp2(b)).astype(BF16)
            o = intra[j] + lax.dot_general(qe, st.astype(BF16), nt, preferred_element_type=F32)
            ke = (v["kk"] * jnp.exp2(b_last - b)).astype(BF16)
            st = st * jnp.exp2(b_last) + lax.dot_general(
                v["ii_bf"], ke, (((0,), (0,)), ((), ())), preferred_element_type=F32)
            ms = jnp.mean(o * o, axis=-1, keepdims=True)
            y = o * lax.rsqrt(ms + EPS) * gn
            y = y * _silu_tanh(g_ref[0, v["rows"], :])
            o_ref[0, v["rows"], :] = y.astype(o_ref.dtype)
        return st

    lax.fori_loop(0, n_steps // HG_UNROLL, group, jnp.zeros((Dk, Dk), F32))


def _hgrn(proj_b, lb_logits_h, gnorm_g):
    B, T, W4 = proj_b.shape
    W = W4 // 4
    H = W // REC_HEAD_DIM
    n_slots = lb_logits_h.shape[1]
    blk = lambda off: pl.BlockSpec((1, T, REC_HEAD_DIM), lambda b, h, off=off: (b, 0, off * H + h))
    return pl.pallas_call(
        functools.partial(_hgrn_kernel, n_steps=T // HG_CHUNK),
        out_shape=jax.ShapeDtypeStruct((B, T, W), BF16),
        grid=(B, H),
        in_specs=[blk(0), blk(1), blk(2), blk(3),
                  pl.BlockSpec((1, n_slots, REC_HEAD_DIM), lambda b, h: (h, 0, 0)),
                  pl.BlockSpec((1, REC_HEAD_DIM), lambda b, h: (0, 0))],
        out_specs=pl.BlockSpec((1, T, REC_HEAD_DIM), lambda b, h: (b, 0, h)),
        compiler_params=_cparams(("parallel", "parallel"), 32),
        name="hgrn2",
    )(proj_b, proj_b, proj_b, proj_b, lb_logits_h, gnorm_g)


def _outproj_kernel(oa_ref, ob_ref, w_ref, x_ref, mod_ref, g_ref, b_ref, x1_ref, *, alpha, sub):
    wa = oa_ref.shape[-1]
    gate1 = mod_ref[0, 2:3, :]
    for r0 in range(0, x_ref.shape[1], sub):
        rs = slice(r0, r0 + sub)
        mix = jnp.dot(oa_ref[0, rs, :], w_ref[0:wa, :], preferred_element_type=F32)
        mix = mix + jnp.dot(ob_ref[0, rs, :], w_ref[wa:, :], preferred_element_type=F32)
        x1_ref[0, rs, :] = _ln_rows(alpha * x_ref[0, rs, :] + gate1 * mix) * g_ref[...] + b_ref[...]


def _outproj(o_a, o_b, w_o, x, mod3, ln_g, ln_b, alpha, tm=512, sub=128):
    B, T, D = x.shape
    Wa, Wb = o_a.shape[-1], o_b.shape[-1]
    return pl.pallas_call(
        functools.partial(_outproj_kernel, alpha=alpha, sub=sub),
        out_shape=jax.ShapeDtypeStruct((B, T, D), F32),
        grid=(B, T // tm),
        in_specs=[pl.BlockSpec((1, tm, Wa), lambda b, i: (b, i, 0)),
                  pl.BlockSpec((1, tm, Wb), lambda b, i: (b, i, 0)),
                  pl.BlockSpec((Wa + Wb, D), lambda b, i: (0, 0)),
                  pl.BlockSpec((1, tm, D), lambda b, i: (b, i, 0)),
                  pl.BlockSpec((1, N_MOD, D), lambda b, i: (b, 0, 0)),
                  pl.BlockSpec((1, D), lambda b, i: (0, 0)),
                  pl.BlockSpec((1, D), lambda b, i: (0, 0))],
        out_specs=pl.BlockSpec((1, tm, D), lambda b, i: (b, i, 0)),
        compiler_params=_cparams(("parallel", "parallel"), 48),
        name="out_proj_ln1",
    )(o_a, o_b, w_o, x, mod3, ln_g, ln_b)


def _ffn_kernel(wg_ref, wu_ref, wo_ref, x_ref, mod_ref, g_ref, b_ref, o_ref, h_sc, *, alpha, sub):
    f = pl.program_id(2)
    last = pl.num_programs(2) - 1

    def partial_out(rs):
        h = h_sc[rs, :]
        gate = jnp.dot(h, wg_ref[...], preferred_element_type=F32)
        up = jnp.dot(h, wu_ref[...], preferred_element_type=F32)
        act = (_silu_tanh(gate) * up).astype(BF16)
        return jnp.dot(act, wo_ref[...], preferred_element_type=F32)

    everything = slice(0, o_ref.shape[1])

    @pl.when(f == 0)
    def _():
        shift2 = mod_ref[0, 3:4, :]
        scale2 = mod_ref[0, 4:5, :]
        for r0 in range(0, o_ref.shape[1], sub):
            rs = slice(r0, r0 + sub)
            h_sc[rs, :] = (_ln_rows(x_ref[0, rs, :]) * (1.0 + scale2) + shift2).astype(BF16)
        o_ref[0] = partial_out(everything)

    @pl.when((f > 0) & (f < last))
    def _():
        o_ref[0] += partial_out(everything)

    @pl.when(f == last)
    def _():
        gate2 = mod_ref[0, 5:6, :]
        for r0 in range(0, o_ref.shape[1], sub):
            rs = slice(r0, r0 + sub)
            y = o_ref[0, rs, :] + partial_out(rs)
            o_ref[0, rs, :] = _ln_rows(alpha * x_ref[0, rs, :] + gate2 * y) * g_ref[...] + b_ref[...]


def _ffn(w_in, w_out, x1, mod3, ln_g, ln_b, alpha, tm=1024, tf=512, sub=256):
    B, T, D = x1.shape
    F = w_out.shape[0]
    nf = F // tf
    return pl.pallas_call(
        functools.partial(_ffn_kernel, alpha=alpha, sub=sub),
        out_shape=jax.ShapeDtypeStruct((B, T, D), F32),
        grid=(B, T // tm, nf),
        in_specs=[pl.BlockSpec((D, tf), lambda b, i, f: (0, f)),
                  pl.BlockSpec((D, tf), lambda b, i, f: (0, nf + f)),
                  pl.BlockSpec((tf, D), lambda b, i, f: (f, 0)),
                  pl.BlockSpec((1, tm, D), lambda b, i, f: (b, i, 0)),
                  pl.BlockSpec((1, N_MOD, D), lambda b, i, f: (b, 0, 0)),
                  pl.BlockSpec((1, D), lambda b, i, f: (0, 0)),
                  pl.BlockSpec((1, D), lambda b, i, f: (0, 0))],
        out_specs=pl.BlockSpec((1, tm, D), lambda b, i, f: (b, i, 0)),
        scratch_shapes=[pltpu.VMEM((tm, D), BF16)],
        compiler_params=_cparams(("parallel", "parallel", "arbitrary"), 60),
        name="swiglu_ffn_ln2",
    )(w_in, w_in, w_out, x1, mod3, ln_g, ln_b)


def _bias_vectors(rel_bias):
    H, n_rel = rel_bias.shape
    max_rel = (n_rel - 1) // 2
    u = jnp.arange(BIAS_W)
    idx = jnp.clip(KBLK - u, -max_rel, max_rel) + max_rel
    return rel_bias[:, idx]


def kernel(x, c, w_ada, b_ada, w_in, rel_bias, attn_norm_g, lb_logits, gnorm_g, w_o,
           ln1_g, ln1_b, w_ffn_in, w_ffn_out, ln2_g, ln2_b):
    B, T, D = x.shape
    depth = w_ada.shape[0]
    alpha = (2 * depth) ** 0.25
    attn_w = attn_norm_g.shape[1]
    rec_w = lb_logits.shape[1]
    n_slots = lb_logits.shape[0]
    rec_heads = rec_w // REC_HEAD_DIM
    assert depth == 1 and n_slots == depth + 1
    for layer in range(depth):
        mod3 = _mod(c, w_ada[layer], b_ada[layer]).reshape(B, N_MOD, D)
        h1 = _ln_mod(x, mod3, 0).reshape(B * T, D)
        q_scale = jnp.where(jnp.arange(w_in.shape[2]) < attn_w, ATTN_HEAD_DIM ** -0.5 * LOG2E, 1.0)
        q_scale = q_scale.astype(F32).reshape(1, -1)
        proj_a = _matmul(h1, w_in[layer], q_scale, 0, 3 * attn_w, BF16).reshape(B, T, 3 * attn_w)
        proj_b = _matmul(h1, w_in[layer], q_scale, 3 * attn_w // 1024, 4 * rec_w, F32).reshape(B, T, 4 * rec_w)
        bias_vec = (_bias_vectors(rel_bias[layer]) * LOG2E).reshape(-1, 2, BIAS_W)
        o_a, (w_o_bf, w_ffn_in_bf, w_ffn_out_bf) = _attention(
            proj_a, bias_vec, attn_norm_g[layer].reshape(-1, 1, LANES),
            (w_o[layer], w_ffn_in[layer], w_ffn_out[layer]))
        lbl = lb_logits.reshape(n_slots, rec_heads, REC_HEAD_DIM).transpose(1, 0, 2)
        o_b = _hgrn(proj_b, lbl, gnorm_g[layer].reshape(1, REC_HEAD_DIM))
        x = _outproj(o_a, o_b, w_o_bf, x, mod3,
                     ln1_g[layer].reshape(1, D), ln1_b[layer].reshape(1, D), alpha)
        x = _ffn(w_ffn_in_bf, w_ffn_out_bf, x, mod3,
                 ln2_g[layer].reshape(1, D), ln2_b[layer].reshape(1, D), alpha)
    return x
```

```python
import functools

import jax
import jax.numpy as jnp
from jax import lax
from jax.experimental import pallas as pl
from jax.experimental.pallas import tpu as pltpu

F32 = jnp.float32
BF16 = jnp.bfloat16

CHUNK = 64
N_PAST_CHUNKS = 8
BAND = (N_PAST_CHUNKS + 1) * CHUNK
ATTN_HEAD_DIM = 64
REC_HEAD_DIM = 128
N_MOD = 6
EPS = 1e-5
LANES = 128
SUBLANES = 8
BF16_ROWS = 16
QBLK = 2 * CHUNK
KBLK = BAND + CHUNK
BIAS_W = KBLK + QBLK
N_PAD_STEPS = N_PAST_CHUNKS * CHUNK // QBLK

MIB = 1024 * 1024


def _cparams(sem, vmem_mib):
    return pltpu.CompilerParams(dimension_semantics=sem, vmem_limit_bytes=vmem_mib * MIB)


def _sigmoid(x):
    return 1.0 / (1.0 + jnp.exp(-x))


def _silu(x):
    return x * _sigmoid(x)


def _silu_tanh(x):
    h = 0.5 * x
    return h + h * jnp.tanh(h)


def _mod_row(mod_ref, b, r):
    d = mod_ref.shape[1] // N_MOD
    return mod_ref[pl.ds(b, 1), r * d:(r + 1) * d]


def _ln_rows(x):
    mu = jnp.mean(x, axis=-1, keepdims=True)
    xc = x - mu
    var = jnp.mean(xc * xc, axis=-1, keepdims=True)
    return xc * lax.rsqrt(var + EPS)


def _mod_kernel(ct_ref, w_ref, b_ref, o_ref):
    ct = ct_ref[...]
    cat = _silu(ct)
    w = w_ref[...]
    rows = [jnp.sum(w * cat[:, b:b + 1], axis=0, keepdims=True) for b in range(ct.shape[1])]
    o_ref[...] = jnp.concatenate(rows, axis=0) + b_ref[...]


def _mod(c, w_ada, b_ada, tn=512):
    B, D = c.shape
    N = w_ada.shape[1]
    return pl.pallas_call(
        _mod_kernel,
        out_shape=jax.ShapeDtypeStruct((B, N), F32),
        grid=(N // tn,),
        in_specs=[pl.BlockSpec((D, B), lambda j: (0, 0)),
                  pl.BlockSpec((D, tn), lambda j: (0, j)),
                  pl.BlockSpec((1, tn), lambda j: (0, j))],
        out_specs=pl.BlockSpec((B, tn), lambda j: (0, j)),
        compiler_params=_cparams(("parallel",), 32),
        name="adaln_mod",
    )(c.T, w_ada, b_ada.reshape(1, N))


def _ln_mod_kernel(x_ref, mod_ref, o_ref, *, shift_row):
    y = _ln_rows(x_ref[0])
    b = pl.program_id(0)
    shift = _mod_row(mod_ref, b, shift_row)
    scale = _mod_row(mod_ref, b, shift_row + 1)
    o_ref[0] = (y * (1.0 + scale) + shift).astype(o_ref.dtype)


def _ln_mod(x, mod, shift_row, tm=1024):
    B, T, D = x.shape
    return pl.pallas_call(
        functools.partial(_ln_mod_kernel, shift_row=shift_row),
        out_shape=jax.ShapeDtypeStruct((B, T, D), BF16),
        grid=(B, T // tm),
        in_specs=[pl.BlockSpec((1, tm, D), lambda b, i: (b, i, 0)),
                  pl.BlockSpec((B, N_MOD * D), lambda b, i: (0, 0))],
        out_specs=pl.BlockSpec((1, tm, D), lambda b, i: (b, i, 0)),
        compiler_params=_cparams(("parallel", "parallel"), 32),
        name="ln_modulate",
    )(x, mod)


def _matmul_kernel(a_ref, w_ref, s_ref, o_ref, w_bf):
    @pl.when(pl.program_id(1) == 0)
    def _():
        w_bf[...] = (w_ref[...] * s_ref[...]).astype(BF16)

    o_ref[...] = jnp.dot(a_ref[...], w_bf[...], preferred_element_type=F32).astype(o_ref.dtype)


def _matmul(a, w, col_scale, col_block0, n_out, out_dtype, tm=1024, tn=1024):
    M, K = a.shape
    return pl.pallas_call(
        _matmul_kernel,
        out_shape=jax.ShapeDtypeStruct((M, n_out), out_dtype),
        grid=(n_out // tn, M // tm),
        in_specs=[pl.BlockSpec((tm, K), lambda j, i: (i, 0)),
                  pl.BlockSpec((K, tn), lambda j, i: (0, j + col_block0)),
                  pl.BlockSpec((1, tn), lambda j, i: (0, j + col_block0))],
        out_specs=pl.BlockSpec((tm, tn), lambda j, i: (i, j)),
        scratch_shapes=[pltpu.VMEM((K, tn), BF16)],
        compiler_params=_cparams(("parallel", "arbitrary"), 48),
        name="in_proj",
    )(a, w, col_scale)


NEG_BIG = -1e30
LOG2E = 1.4426950408889634


def _attn_kernel(q_ref, k_ref, v_ref, bias_ref, gain_ref, *rest, n_chunks, n_cast):
    cast_in = rest[:n_cast]
    o_ref = rest[n_cast]
    cast_out = rest[n_cast + 1:2 * n_cast + 1]
    kta, ktb, vpa, vpb, tab, s_a, s_b, s_c = rest[2 * n_cast + 1:]
    pad = N_PAST_CHUNKS * CHUNK
    T = n_chunks * CHUNK
    n_steps = T // QBLK
    head0 = lax.broadcasted_iota(jnp.int32, (QBLK, LANES), 1) < ATTN_HEAD_DIM
    m0 = jnp.where(head0, 1.0, 0.0).astype(BF16)
    m1 = jnp.where(head0, 0.0, 1.0).astype(BF16)

    head0_t = lax.broadcasted_iota(jnp.int32, (LANES, QBLK), 0) < ATTN_HEAD_DIM
    mt0 = jnp.where(head0_t, 1.0, 0.0).astype(BF16)
    mt1 = jnp.where(head0_t, 0.0, 1.0).astype(BF16)

    for ref in (kta, ktb):
        ref[:, 0:pad] = jnp.zeros((LANES, pad), BF16)
    for ref in (vpa, vpb):
        ref[0:pad, :] = jnp.zeros((pad, 2 * LANES), BF16)

    def prep(blk):
        src = slice(blk * QBLK, (blk + 1) * QBLK)
        dst = slice(pad + blk * QBLK, pad + (blk + 1) * QBLK)
        kt = k_ref[0, src, :].T
        v = v_ref[0, src, :]
        kta[:, dst] = kt * mt0
        ktb[:, dst] = kt * mt1
        vpa[dst, 0:LANES] = v * m0
        vpb[dst, 0:LANES] = v * m1
        vpa[dst, LANES:2 * LANES] = m0
        vpb[dst, LANES:2 * LANES] = m1

    @pl.when(pl.program_id(1) == 0)
    def _():
        qry = lax.broadcasted_iota(jnp.int32, (QBLK, KBLK), 0)
        key = lax.broadcasted_iota(jnp.int32, (QBLK, KBLK), 1)
        in_band = ((qry < CHUNK) & (key < BAND)) | ((qry >= CHUNK) & (key >= CHUNK))
        for hh in range(2):
            g = jnp.broadcast_to(bias_ref[0, hh:hh + 1, :], (QBLK, BIAS_W))
            t = pltpu.roll(g, BIAS_W - QBLK, 1, stride=1, stride_axis=0)[:, :KBLK]
            t = jnp.where(in_band, t, NEG_BIG)
            for var in range(N_PAD_STEPS + 1):
                first_real = pad - var * QBLK
                tv = jnp.where(key >= first_real, t, NEG_BIG) if first_real > 0 else t
                tab[var, hh] = tv.astype(BF16)

    eye = (lax.broadcasted_iota(jnp.int32, (QBLK, QBLK), 0)
           == lax.broadcasted_iota(jnp.int32, (QBLK, QBLK), 1)).astype(BF16)
    gain = gain_ref[0]

    def scores(m, dst):
        prep(m)
        band = slice(m * QBLK, m * QBLK + KBLK)
        var = min(m, N_PAD_STEPS)
        lhs = jnp.concatenate([q_ref[0, m * QBLK:(m + 1) * QBLK, :], eye], axis=1)
        rhs = jnp.concatenate([jnp.concatenate([kta[:, band], ktb[:, band]], axis=1),
                               jnp.concatenate([tab[var, 0], tab[var, 1]], axis=1)], axis=0)
        dst[...] = jnp.dot(lhs, rhs, preferred_element_type=F32)

    def finish(m, src):
        band = slice(m * QBLK, m * QBLK + KBLK)
        s = src[...]
        p = jnp.concatenate(
            [jnp.exp2(sh - jnp.max(sh, axis=-1, keepdims=True)) for sh in (s[:, 0:KBLK], s[:, KBLK:2 * KBLK])],
            axis=1).astype(BF16)
        pv = jnp.dot(p, jnp.concatenate([vpa[band, :], vpb[band, :]], axis=0), preferred_element_type=F32)
        o = pv[:, 0:LANES] * (1.0 / pv[:, LANES:2 * LANES])
        o2 = o * o
        ms0 = jnp.sum(jnp.where(head0, o2, 0.0), axis=-1, keepdims=True) / ATTN_HEAD_DIM
        ms1 = jnp.sum(jnp.where(head0, 0.0, o2), axis=-1, keepdims=True) / ATTN_HEAD_DIM
        y = o * lax.rsqrt(jnp.where(head0, ms0, ms1) + EPS) * gain
        o_ref[0, m * QBLK:(m + 1) * QBLK, :] = y.astype(o_ref.dtype)

    pieces = [(src_ref, dst_ref, r0) for src_ref, dst_ref in zip(cast_in, cast_out)
              for r0 in range(0, src_ref.shape[0], BF16_ROWS)]
    bufs = (s_a, s_b, s_c)
    depth = len(bufs)
    ahead = depth - 1
    for m in range(ahead):
        scores(m, bufs[m % depth])
    for m in range(n_steps):
        if m + ahead < n_steps:
            scores(m + ahead, bufs[(m + ahead) % depth])
        finish(m, bufs[m % depth])
        for src_ref, dst_ref, r0 in pieces[m * len(pieces) // n_steps:(m + 1) * len(pieces) // n_steps]:
            dst_ref[r0:r0 + BF16_ROWS, :] = src_ref[r0:r0 + BF16_ROWS, :].astype(BF16)


def _attention(proj_a, bias_vec, attn_gain, cast_weights):
    B, T, W3 = proj_a.shape
    W = W3 // 3
    n_pairs = W // LANES
    n_grid = n_pairs * B
    t_pad = T + N_PAST_CHUNKS * CHUNK
    cast_specs = []
    for w in cast_weights:
        rows = w.shape[0] // n_grid
        assert w.shape[0] % n_grid == 0 and rows % BF16_ROWS == 0
        cast_specs.append(pl.BlockSpec((rows, w.shape[1]), lambda h, b: (h * B + b, 0)))
    outs = pl.pallas_call(
        functools.partial(_attn_kernel, n_chunks=T // CHUNK, n_cast=len(cast_weights)),
        out_shape=(jax.ShapeDtypeStruct((B, T, W), BF16),
                   *[jax.ShapeDtypeStruct(w.shape, BF16) for w in cast_weights]),
        grid=(n_pairs, B),
        in_specs=[pl.BlockSpec((1, T, LANES), lambda h, b: (b, 0, h)),
                  pl.BlockSpec((1, T, LANES), lambda h, b: (b, 0, n_pairs + h)),
                  pl.BlockSpec((1, T, LANES), lambda h, b: (b, 0, 2 * n_pairs + h)),
                  pl.BlockSpec((1, 2, BIAS_W), lambda h, b: (h, 0, 0)),
                  pl.BlockSpec((1, 1, LANES), lambda h, b: (h, 0, 0)),
                  *cast_specs],
        out_specs=(pl.BlockSpec((1, T, LANES), lambda h, b: (b, 0, h)), *cast_specs),
        scratch_shapes=[pltpu.VMEM((LANES, t_pad), BF16),
                        pltpu.VMEM((LANES, t_pad), BF16),
                        pltpu.VMEM((t_pad, 2 * LANES), BF16),
                        pltpu.VMEM((t_pad, 2 * LANES), BF16),
                        pltpu.VMEM((N_PAD_STEPS + 1, 2, QBLK, KBLK), BF16),
                        pltpu.VMEM((QBLK, 2 * KBLK), F32),
                        pltpu.VMEM((QBLK, 2 * KBLK), F32),
                        pltpu.VMEM((QBLK, 2 * KBLK), F32)],
        compiler_params=_cparams(("parallel", "arbitrary"), 48),
        name="chunk_attention",
    )(proj_a, proj_a, proj_a, bias_vec, attn_gain, *cast_weights)
    return outs[0], outs[1:]


HG_CHUNK = 256
HG_LEVELS = (128, 64, 32, 16, 8, 4, 2, 1)
assert HG_LEVELS[-1] == 1


def _split3(x):
    hi = x.astype(BF16)
    r1 = x - hi.astype(F32)
    mid = r1.astype(BF16)
    lo = (r1 - mid.astype(F32)).astype(BF16)
    return hi, mid, lo


def _hgrn_kernel(q_ref, f_ref, i_ref, g_ref, lbl_ref, gn_ref, o_ref, *, n_steps):
    C = HG_CHUNK
    H2 = C // 2
    Dk = REC_HEAD_DIM
    nt = (((1,), (1,)), ((), ()))
    lbl = lbl_ref[0]
    e = jnp.exp(lbl - jnp.max(lbl, axis=0, keepdims=True))
    lb = e[0:1, :] / jnp.sum(e, axis=0, keepdims=True)
    c1 = 0.5 * (1.0 - lb)
    gn = gn_ref[...]

    r = lax.broadcasted_iota(jnp.int32, (C, C), 0)
    s = lax.broadcasted_iota(jnp.int32, (C, C), 1)
    tril = (s <= r).astype(BF16)
    rh = lax.broadcasted_iota(jnp.int32, (H2, H2), 0)
    sh = lax.broadcasted_iota(jnp.int32, (H2, H2), 1)
    lvl_mask = {m: ((rh // (2 * m)) == (sh // (2 * m))) & (((rh // m) % 2) == 1) & (((sh // m) % 2) == 0)
                for m in HG_LEVELS[1:]}
    sub = lax.broadcasted_iota(jnp.int32, (C // SUBLANES, SUBLANES, Dk), 1)

    def roll8(x, d):
        return pltpu.roll(x.reshape(C // SUBLANES, SUBLANES, Dk), d, 1)

    def front(n):
        rows = slice(n * C, (n + 1) * C)
        c1t = c1 * jnp.tanh(0.5 * f_ref[0, rows, :])
        f = (1.0 - c1) + c1t
        kk = c1 - c1t
        qq = _silu_tanh(q_ref[0, rows, :])
        ii = i_ref[0, rows, :]
        hi, mid, lo = _split3(jnp.log2(f))
        bb = jnp.dot(tril, jnp.concatenate([hi, mid, lo], axis=1), preferred_element_type=F32)
        b = bb[:, 0:Dk] + bb[:, Dk:2 * Dk] + bb[:, 2 * Dk:3 * Dk]
        return dict(rows=rows, kk=kk, qq=qq, ii=ii, ii_bf=ii.astype(BF16), b=b)

    def level_z(v, m):
        b, kk, qq = v["b"], v["kk"], v["qq"]
        if m >= SUBLANES:
            parts, srcs = [], []
            for p in range(0, C, 2 * m):
                bm = b[p + m - 1:p + m, :]
                parts += [bm - b[p:p + m], b[p + m:p + 2 * m] - bm]
                srcs += [kk[p:p + m], qq[p + m:p + 2 * m]]
            arg = jnp.concatenate(parts, axis=0)
            src = jnp.concatenate(srcs, axis=0)
        else:
            b3 = b.reshape(C // SUBLANES, SUBLANES, Dk)
            if m == 1:
                bm = jnp.where(sub % 2 == 1, roll8(b, 1), b3)
            else:
                bm = jnp.broadcast_to(b3[:, m - 1:m, :], b3.shape)
                for p in range(2 * m, SUBLANES, 2 * m):
                    bm = jnp.where(sub >= p, jnp.broadcast_to(b3[:, p + m - 1:p + m, :], b3.shape), bm)
            upper = (sub // m) % 2 == 1
            arg = ((b3 - bm) * jnp.where(upper, 1.0, -1.0)).reshape(C, Dk)
            src = jnp.where(upper, qq.reshape(b3.shape), kk.reshape(b3.shape)).reshape(C, Dk)
        return (src * jnp.exp2(arg)).astype(BF16)

    vs = [front(j) for j in range(n_steps)]
    a_lo = [None] * n_steps
    a_d0 = [jnp.zeros((H2, H2), F32)] * n_steps
    a_d1 = [jnp.zeros((H2, H2), F32)] * n_steps
    for li, m in enumerate(HG_LEVELS):
        for j, v in enumerate(vs):
            z = level_z(v, m)
            if li == 0:
                a_lo[j] = lax.dot_general(z[H2:], z[:H2], nt, preferred_element_type=F32)
            else:
                g0 = lax.dot_general(z[:H2], z[:H2], nt, preferred_element_type=F32)
                g1 = lax.dot_general(z[H2:], z[H2:], nt, preferred_element_type=F32)
                a_d0[j] = jnp.where(lvl_mask[m], g0, a_d0[j])
                a_d1[j] = jnp.where(lvl_mask[m], g1, a_d1[j])
    intra = []
    for j, v in enumerate(vs):
        o_top = jnp.dot(a_d0[j].astype(BF16), v["ii_bf"][:H2], preferred_element_type=F32)
        o_bot = jnp.dot(jnp.concatenate([a_lo[j], a_d1[j]], axis=1).astype(BF16), v["ii_bf"],
                        preferred_element_type=F32)
        o_diag = jnp.sum(v["qq"] * v["kk"], axis=-1, keepdims=True) * v["ii"]
        intra.append(jnp.concatenate([o_top, o_bot], axis=0) + o_diag)
    st = jnp.zeros((Dk, Dk), F32)
    for j, v in enumerate(vs):
        b = v["b"]
        b_last = b[C - 1:C, :]
        qe = (v["qq"] * jnp.exp2(b)).astype(BF16)
        o = intra[j] + lax.dot_general(qe, st.astype(BF16), nt, preferred_element_type=F32)
        ke = (v["kk"] * jnp.exp2(b_last - b)).astype(BF16)
        st = st * jnp.exp2(b_last) + lax.dot_general(
            v["ii_bf"], ke, (((0,), (0,)), ((), ())), preferred_element_type=F32)
        ms = jnp.mean(o * o, axis=-1, keepdims=True)
        y = o * lax.rsqrt(ms + EPS) * gn
        y = y * _silu_tanh(g_ref[0, v["rows"], :])
        o_ref[0, v["rows"], :] = y.astype(o_ref.dtype)


def _hgrn(proj_b, lb_logits_h, gnorm_g):
    B, T, W4 = proj_b.shape
    W = W4 // 4
    H = W // REC_HEAD_DIM
    n_slots = lb_logits_h.shape[1]
    blk = lambda off: pl.BlockSpec((1, T, REC_HEAD_DIM), lambda b, h, off=off: (b, 0, off * H + h))
    return pl.pallas_call(
        functools.partial(_hgrn_kernel, n_steps=T // HG_CHUNK),
        out_shape=jax.ShapeDtypeStruct((B, T, W), BF16),
        grid=(B, H),
        in_specs=[blk(0), blk(1), blk(2), blk(3),
                  pl.BlockSpec((1, n_slots, REC_HEAD_DIM), lambda b, h: (h, 0, 0)),
                  pl.BlockSpec((1, REC_HEAD_DIM), lambda b, h: (0, 0))],
        out_specs=pl.BlockSpec((1, T, REC_HEAD_DIM), lambda b, h: (b, 0, h)),
        compiler_params=_cparams(("parallel", "parallel"), 32),
        name="hgrn2",
    )(proj_b, proj_b, proj_b, proj_b, lb_logits_h, gnorm_g)


def _outproj_kernel(oa_ref, ob_ref, w_ref, x_ref, mod_ref, g_ref, b_ref, x1_ref, *, alpha, sub):
    wa = oa_ref.shape[-1]
    gate1 = _mod_row(mod_ref, pl.program_id(0), 2)
    for r0 in range(0, x_ref.shape[1], sub):
        rs = slice(r0, r0 + sub)
        mix = jnp.dot(oa_ref[0, rs, :], w_ref[0:wa, :], preferred_element_type=F32)
        mix = mix + jnp.dot(ob_ref[0, rs, :], w_ref[wa:, :], preferred_element_type=F32)
        x1_ref[0, rs, :] = _ln_rows(alpha * x_ref[0, rs, :] + gate1 * mix) * g_ref[...] + b_ref[...]


def _outproj(o_a, o_b, w_o, x, mod, ln_g, ln_b, alpha, tm=512, sub=128):
    B, T, D = x.shape
    Wa, Wb = o_a.shape[-1], o_b.shape[-1]
    return pl.pallas_call(
        functools.partial(_outproj_kernel, alpha=alpha, sub=sub),
        out_shape=jax.ShapeDtypeStruct((B, T, D), F32),
        grid=(B, T // tm),
        in_specs=[pl.BlockSpec((1, tm, Wa), lambda b, i: (b, i, 0)),
                  pl.BlockSpec((1, tm, Wb), lambda b, i: (b, i, 0)),
                  pl.BlockSpec((Wa + Wb, D), lambda b, i: (0, 0)),
                  pl.BlockSpec((1, tm, D), lambda b, i: (b, i, 0)),
                  pl.BlockSpec((B, N_MOD * D), lambda b, i: (0, 0)),
                  pl.BlockSpec((1, D), lambda b, i: (0, 0)),
                  pl.BlockSpec((1, D), lambda b, i: (0, 0))],
        out_specs=pl.BlockSpec((1, tm, D), lambda b, i: (b, i, 0)),
        compiler_params=_cparams(("parallel", "parallel"), 48),
        name="out_proj_ln1",
    )(o_a, o_b, w_o, x, mod, ln_g, ln_b)


def _ffn_kernel(wg_ref, wu_ref, wo_ref, x_ref, mod_ref, g_ref, b_ref, o_ref, h_sc, *, alpha, sub):
    f = pl.program_id(2)
    last = pl.num_programs(2) - 1

    def partial_out(rs):
        h = h_sc[rs, :]
        gate = jnp.dot(h, wg_ref[...], preferred_element_type=F32)
        up = jnp.dot(h, wu_ref[...], preferred_element_type=F32)
        act = (_silu_tanh(gate) * up).astype(BF16)
        return jnp.dot(act, wo_ref[...], preferred_element_type=F32)

    everything = slice(0, o_ref.shape[1])

    @pl.when(f == 0)
    def _():
        shift2 = _mod_row(mod_ref, pl.program_id(0), 3)
        scale2 = _mod_row(mod_ref, pl.program_id(0), 4)
        for r0 in range(0, o_ref.shape[1], sub):
            rs = slice(r0, r0 + sub)
            h_sc[rs, :] = (_ln_rows(x_ref[0, rs, :]) * (1.0 + scale2) + shift2).astype(BF16)
        o_ref[0] = partial_out(everything)

    @pl.when((f > 0) & (f < last))
    def _():
        o_ref[0] += partial_out(everything)

    @pl.when(f == last)
    def _():
        gate2 = _mod_row(mod_ref, pl.program_id(0), 5)
        for r0 in range(0, o_ref.shape[1], sub):
            rs = slice(r0, r0 + sub)
            y = o_ref[0, rs, :] + partial_out(rs)
            o_ref[0, rs, :] = _ln_rows(alpha * x_ref[0, rs, :] + gate2 * y) * g_ref[...] + b_ref[...]


def _ffn(w_in, w_out, x1, mod, ln_g, ln_b, alpha, tm=1024, tf=512, sub=256):
    B, T, D = x1.shape
    F = w_out.shape[0]
    nf = F // tf
    return pl.pallas_call(
        functools.partial(_ffn_kernel, alpha=alpha, sub=sub),
        out_shape=jax.ShapeDtypeStruct((B, T, D), F32),
        grid=(B, T // tm, nf),
        in_specs=[pl.BlockSpec((D, tf), lambda b, i, f: (0, f)),
                  pl.BlockSpec((D, tf), lambda b, i, f: (0, nf + f)),
                  pl.BlockSpec((tf, D), lambda b, i, f: (f, 0)),
                  pl.BlockSpec((1, tm, D), lambda b, i, f: (b, i, 0)),
                  pl.BlockSpec((B, N_MOD * D), lambda b, i, f: (0, 0)),
                  pl.BlockSpec((1, D), lambda b, i, f: (0, 0)),
                  pl.BlockSpec((1, D), lambda b, i, f: (0, 0))],
        out_specs=pl.BlockSpec((1, tm, D), lambda b, i, f: (b, i, 0)),
        scratch_shapes=[pltpu.VMEM((tm, D), BF16)],
        compiler_params=_cparams(("parallel", "parallel", "arbitrary"), 60),
        name="swiglu_ffn_ln2",
    )(w_in, w_in, w_out, x1, mod, ln_g, ln_b)


def _bias_vectors(rel_bias):
    H, n_rel = rel_bias.shape
    max_rel = (n_rel - 1) // 2
    u = jnp.arange(BIAS_W)
    idx = jnp.clip(KBLK - u, -max_rel, max_rel) + max_rel
    return rel_bias[:, idx]


def kernel(x, c, w_ada, b_ada, w_in, rel_bias, attn_norm_g, lb_logits, gnorm_g, w_o,
           ln1_g, ln1_b, w_ffn_in, w_ffn_out, ln2_g, ln2_b):
    B, T, D = x.shape
    depth = w_ada.shape[0]
    alpha = (2 * depth) ** 0.25
    attn_w = attn_norm_g.shape[1]
    rec_w = lb_logits.shape[1]
    n_slots = lb_logits.shape[0]
    rec_heads = rec_w // REC_HEAD_DIM
    assert depth == 1 and n_slots == depth + 1
    for layer in range(depth):
        mod = _mod(c, w_ada[layer], b_ada[layer])
        h1 = _ln_mod(x, mod, 0).reshape(B * T, D)
        q_scale = jnp.where(jnp.arange(w_in.shape[2]) < attn_w, ATTN_HEAD_DIM ** -0.5 * LOG2E, 1.0)
        q_scale = q_scale.astype(F32).reshape(1, -1)
        proj_a = _matmul(h1, w_in[layer], q_scale, 0, 3 * attn_w, BF16).reshape(B, T, 3 * attn_w)
        proj_b = _matmul(h1, w_in[layer], q_scale, 3 * attn_w // 1024, 4 * rec_w, F32).reshape(B, T, 4 * rec_w)
        bias_vec = (_bias_vectors(rel_bias[layer]) * LOG2E).reshape(-1, 2, BIAS_W)
        o_a, (w_o_bf, w_ffn_in_bf, w_ffn_out_bf) = _attention(
            proj_a, bias_vec, attn_norm_g[layer].reshape(-1, 1, LANES),
            (w_o[layer], w_ffn_in[layer], w_ffn_out[layer]))
        lbl = lb_logits.reshape(n_slots, rec_heads, REC_HEAD_DIM).transpose(1, 0, 2)
        o_b = _hgrn(proj_b, lbl, gnorm_g[layer].reshape(1, REC_HEAD_DIM))
        x = _outproj(o_a, o_b, w_o_bf, x, mod,
                     ln1_g[layer].reshape(1, D), ln1_b[layer].reshape(1, D), alpha)
        x = _ffn(w_ffn_in_bf, w_ffn_out_bf, x, mod,
                 ln2_g[layer].reshape(1, D), ln2_b[layer].reshape(1, D), alpha)
    return x
```

```python
import functools

import jax
import jax.numpy as jnp
from jax import lax
from jax.experimental import pallas as pl
from jax.experimental.pallas import tpu as pltpu

F32 = jnp.float32
BF16 = jnp.bfloat16

CHUNK = 64
N_PAST_CHUNKS = 8
BAND = (N_PAST_CHUNKS + 1) * CHUNK
ATTN_HEAD_DIM = 64
REC_HEAD_DIM = 128
N_MOD = 6
EPS = 1e-5
LANES = 128
SUBLANES = 8
BF16_ROWS = 16
QBLK = 2 * CHUNK
KBLK = BAND + CHUNK
BIAS_W = KBLK + QBLK
N_PAD_STEPS = N_PAST_CHUNKS * CHUNK // QBLK

MIB = 1024 * 1024


def _cparams(sem, vmem_mib):
    return pltpu.CompilerParams(dimension_semantics=sem, vmem_limit_bytes=vmem_mib * MIB)


def _silu_tanh(x):
    h = 0.5 * x
    return h + h * jnp.tanh(h)


def _mod_row(mod_ref, b, r):
    d = mod_ref.shape[1] // N_MOD
    return mod_ref[pl.ds(b, 1), r * d:(r + 1) * d]


def _ln_rows(x):
    mu = jnp.mean(x, axis=-1, keepdims=True)
    xc = x - mu
    var = jnp.mean(xc * xc, axis=-1, keepdims=True)
    return xc * lax.rsqrt(var + EPS)


def _split3(x):
    hi = x.astype(BF16)
    r1 = x - hi.astype(F32)
    mid = r1.astype(BF16)
    lo = (r1 - mid.astype(F32)).astype(BF16)
    return hi, mid, lo


def _mod_kernel(c_ref, w_ref, b_ref, o_ref):
    act = jnp.concatenate(_split3(_silu_tanh(c_ref[...])), axis=0)
    w = w_ref[...]
    w_hi = w.astype(BF16)
    w_mid = (w - w_hi.astype(F32)).astype(BF16)
    acc = jnp.dot(act, w_hi, preferred_element_type=F32) + jnp.dot(act, w_mid, preferred_element_type=F32)
    out = acc[0:SUBLANES] + acc[SUBLANES:2 * SUBLANES] + acc[2 * SUBLANES:3 * SUBLANES]
    o_ref[...] = out[0:o_ref.shape[0]] + b_ref[...]


def _mod(c, w_ada, b_ada, tn=1024):
    B, D = c.shape
    N = w_ada.shape[1]
    assert B <= SUBLANES
    return pl.pallas_call(
        _mod_kernel,
        out_shape=jax.ShapeDtypeStruct((B, N), F32),
        grid=(N // tn,),
        in_specs=[pl.BlockSpec((SUBLANES, D), lambda j: (0, 0)),
                  pl.BlockSpec((D, tn), lambda j: (0, j)),
                  pl.BlockSpec((1, tn), lambda j: (0, j))],
        out_specs=pl.BlockSpec((B, tn), lambda j: (0, j)),
        compiler_params=_cparams(("parallel",), 40),
        name="adaln_mod",
    )(jnp.pad(c, ((0, SUBLANES - B), (0, 0))), w_ada, b_ada.reshape(1, N))


def _ln_mod_kernel(x_ref, mod_ref, o_ref, *, shift_row):
    y = _ln_rows(x_ref[0])
    b = pl.program_id(0)
    shift = _mod_row(mod_ref, b, shift_row)
    scale = _mod_row(mod_ref, b, shift_row + 1)
    o_ref[0] = (y * (1.0 + scale) + shift).astype(o_ref.dtype)


def _ln_mod(x, mod, shift_row, tm=1024):
    B, T, D = x.shape
    return pl.pallas_call(
        functools.partial(_ln_mod_kernel, shift_row=shift_row),
        out_shape=jax.ShapeDtypeStruct((B, T, D), BF16),
        grid=(B, T // tm),
        in_specs=[pl.BlockSpec((1, tm, D), lambda b, i: (b, i, 0)),
                  pl.BlockSpec((B, N_MOD * D), lambda b, i: (0, 0))],
        out_specs=pl.BlockSpec((1, tm, D), lambda b, i: (b, i, 0)),
        compiler_params=_cparams(("parallel", "parallel"), 32),
        name="ln_modulate",
    )(x, mod)


def _matmul_kernel(a_ref, w_ref, s_ref, o_ref, w_bf):
    @pl.when(pl.program_id(1) == 0)
    def _():
        w_bf[...] = (w_ref[...] * s_ref[...]).astype(BF16)

    o_ref[...] = jnp.dot(a_ref[...], w_bf[...], preferred_element_type=F32).astype(o_ref.dtype)


def _matmul(a, w, col_scale, col0, n_out, out_dtype, tm=1024, tn=1024):
    M, K = a.shape
    assert col0 % tn == 0 and n_out % tn == 0 and M % tm == 0
    col_block0 = col0 // tn
    return pl.pallas_call(
        _matmul_kernel,
        out_shape=jax.ShapeDtypeStruct((M, n_out), out_dtype),
        grid=(n_out // tn, M // tm),
        in_specs=[pl.BlockSpec((tm, K), lambda j, i: (i, 0)),
                  pl.BlockSpec((K, tn), lambda j, i: (0, j + col_block0)),
                  pl.BlockSpec((1, tn), lambda j, i: (0, j + col_block0))],
        out_specs=pl.BlockSpec((tm, tn), lambda j, i: (i, j)),
        scratch_shapes=[pltpu.VMEM((K, tn), BF16)],
        compiler_params=_cparams(("parallel", "arbitrary"), 48),
        name="in_proj",
    )(a, w, col_scale)


NEG_BIG = -1e30
LOG2E = 1.4426950408889634


def _attn_kernel(q_ref, k_ref, v_ref, bias_ref, gain_ref, *rest, n_chunks, n_cast):
    cast_in = rest[:n_cast]
    o_ref = rest[n_cast]
    cast_out = rest[n_cast + 1:2 * n_cast + 1]
    kta, ktb, vpa, vpb, tab, s_a, s_b, s_c = rest[2 * n_cast + 1:]
    pad = N_PAST_CHUNKS * CHUNK
    T = n_chunks * CHUNK
    n_steps = T // QBLK
    head0 = lax.broadcasted_iota(jnp.int32, (QBLK, LANES), 1) < ATTN_HEAD_DIM
    m0 = jnp.where(head0, 1.0, 0.0).astype(BF16)
    m1 = jnp.where(head0, 0.0, 1.0).astype(BF16)

    head0_t = lax.broadcasted_iota(jnp.int32, (LANES, QBLK), 0) < ATTN_HEAD_DIM
    mt0 = jnp.where(head0_t, 1.0, 0.0).astype(BF16)
    mt1 = jnp.where(head0_t, 0.0, 1.0).astype(BF16)

    for ref in (kta, ktb):
        ref[:, 0:pad] = jnp.zeros((LANES, pad), BF16)
    for ref in (vpa, vpb):
        ref[0:pad, :] = jnp.zeros((pad, 2 * LANES), BF16)

    def prep(blk):
        src = slice(blk * QBLK, (blk + 1) * QBLK)
        dst = slice(pad + blk * QBLK, pad + (blk + 1) * QBLK)
        kt = k_ref[0, src, :].T
        v = v_ref[0, src, :]
        kta[:, dst] = kt * mt0
        ktb[:, dst] = kt * mt1
        vpa[dst, 0:LANES] = v * m0
        vpb[dst, 0:LANES] = v * m1
        vpa[dst, LANES:2 * LANES] = m0
        vpb[dst, LANES:2 * LANES] = m1

    @pl.when(pl.program_id(1) == 0)
    def _():
        qry = lax.broadcasted_iota(jnp.int32, (QBLK, KBLK), 0)
        key = lax.broadcasted_iota(jnp.int32, (QBLK, KBLK), 1)
        in_band = ((qry < CHUNK) & (key < BAND)) | ((qry >= CHUNK) & (key >= CHUNK))
        for hh in range(2):
            g = jnp.broadcast_to(bias_ref[0, hh:hh + 1, :], (QBLK, BIAS_W))
            t = pltpu.roll(g, BIAS_W - QBLK, 1, stride=1, stride_axis=0)[:, :KBLK]
            t = jnp.where(in_band, t, NEG_BIG)
            for var in range(N_PAD_STEPS + 1):
                first_real = pad - var * QBLK
                tv = jnp.where(key >= first_real, t, NEG_BIG) if first_real > 0 else t
                tab[var, hh] = tv.astype(BF16)

    eye = (lax.broadcasted_iota(jnp.int32, (QBLK, QBLK), 0)
           == lax.broadcasted_iota(jnp.int32, (QBLK, QBLK), 1)).astype(BF16)
    gain = gain_ref[0]

    def scores(m, dst):
        prep(m)
        band = slice(m * QBLK, m * QBLK + KBLK)
        var = min(m, N_PAD_STEPS)
        lhs = jnp.concatenate([q_ref[0, m * QBLK:(m + 1) * QBLK, :], eye], axis=1)
        rhs = jnp.concatenate([jnp.concatenate([kta[:, band], ktb[:, band]], axis=1),
                               jnp.concatenate([tab[var, 0], tab[var, 1]], axis=1)], axis=0)
        dst[...] = jnp.dot(lhs, rhs, preferred_element_type=F32)

    def finish(m, src):
        band = slice(m * QBLK, m * QBLK + KBLK)
        s = src[...]
        p = jnp.concatenate(
            [jnp.exp2(sh - jnp.max(sh, axis=-1, keepdims=True)) for sh in (s[:, 0:KBLK], s[:, KBLK:2 * KBLK])],
            axis=1).astype(BF16)
        pv = jnp.dot(p, jnp.concatenate([vpa[band, :], vpb[band, :]], axis=0), preferred_element_type=F32)
        o = pv[:, 0:LANES] * (1.0 / pv[:, LANES:2 * LANES])
        o2 = o * o
        ms0 = jnp.sum(jnp.where(head0, o2, 0.0), axis=-1, keepdims=True) / ATTN_HEAD_DIM
        ms1 = jnp.sum(jnp.where(head0, 0.0, o2), axis=-1, keepdims=True) / ATTN_HEAD_DIM
        y = o * lax.rsqrt(jnp.where(head0, ms0, ms1) + EPS) * gain
        o_ref[0, m * QBLK:(m + 1) * QBLK, :] = y.astype(o_ref.dtype)

    pieces = [(src_ref, dst_ref, r0) for src_ref, dst_ref in zip(cast_in, cast_out)
              for r0 in range(0, src_ref.shape[0], BF16_ROWS)]
    bufs = (s_a, s_b, s_c)
    depth = len(bufs)
    ahead = depth - 1
    for m in range(ahead):
        scores(m, bufs[m % depth])
    for m in range(n_steps):
        if m + ahead < n_steps:
            scores(m + ahead, bufs[(m + ahead) % depth])
        finish(m, bufs[m % depth])
        for src_ref, dst_ref, r0 in pieces[m * len(pieces) // n_steps:(m + 1) * len(pieces) // n_steps]:
            dst_ref[r0:r0 + BF16_ROWS, :] = src_ref[r0:r0 + BF16_ROWS, :].astype(BF16)


def _attention(proj_a, bias_vec, attn_gain, cast_weights):
    B, T, W3 = proj_a.shape
    W = W3 // 3
    n_pairs = W // LANES
    n_grid = n_pairs * B
    t_pad = T + N_PAST_CHUNKS * CHUNK
    cast_specs = []
    for w in cast_weights:
        rows = w.shape[0] // n_grid
        assert w.shape[0] % n_grid == 0 and rows % BF16_ROWS == 0
        cast_specs.append(pl.BlockSpec((rows, w.shape[1]), lambda h, b: (h * B + b, 0)))
    outs = pl.pallas_call(
        functools.partial(_attn_kernel, n_chunks=T // CHUNK, n_cast=len(cast_weights)),
        out_shape=(jax.ShapeDtypeStruct((B, T, W), BF16),
                   *[jax.ShapeDtypeStruct(w.shape, BF16) for w in cast_weights]),
        grid=(n_pairs, B),
        in_specs=[pl.BlockSpec((1, T, LANES), lambda h, b: (b, 0, h)),
                  pl.BlockSpec((1, T, LANES), lambda h, b: (b, 0, n_pairs + h)),
                  pl.BlockSpec((1, T, LANES), lambda h, b: (b, 0, 2 * n_pairs + h)),
                  pl.BlockSpec((1, 2, BIAS_W), lambda h, b: (h, 0, 0)),
                  pl.BlockSpec((1, 1, LANES), lambda h, b: (h, 0, 0)),
                  *cast_specs],
        out_specs=(pl.BlockSpec((1, T, LANES), lambda h, b: (b, 0, h)), *cast_specs),
        scratch_shapes=[pltpu.VMEM((LANES, t_pad), BF16),
                        pltpu.VMEM((LANES, t_pad), BF16),
                        pltpu.VMEM((t_pad, 2 * LANES), BF16),
                        pltpu.VMEM((t_pad, 2 * LANES), BF16),
                        pltpu.VMEM((N_PAD_STEPS + 1, 2, QBLK, KBLK), BF16),
                        pltpu.VMEM((QBLK, 2 * KBLK), F32),
                        pltpu.VMEM((QBLK, 2 * KBLK), F32),
                        pltpu.VMEM((QBLK, 2 * KBLK), F32)],
        compiler_params=_cparams(("parallel", "arbitrary"), 48),
        name="chunk_attention",
    )(proj_a, proj_a, proj_a, bias_vec, attn_gain, *cast_weights)
    return outs[0], outs[1:]


HG_CHUNK = 256
HG_LEVELS = (128, 64, 32, 16, 8, 4, 2, 1)
assert HG_LEVELS[-1] == 1


def _hgrn_kernel(q_ref, f_ref, i_ref, g_ref, lbl_ref, gn_ref, o_ref, *, n_steps):
    C = HG_CHUNK
    H2 = C // 2
    Dk = REC_HEAD_DIM
    nt = (((1,), (1,)), ((), ()))
    lbl = lbl_ref[0]
    e = jnp.exp(lbl - jnp.max(lbl, axis=0, keepdims=True))
    lb = e[0:1, :] / jnp.sum(e, axis=0, keepdims=True)
    c1 = 0.5 * (1.0 - lb)
    gn = gn_ref[...]

    r = lax.broadcasted_iota(jnp.int32, (C, C), 0)
    s = lax.broadcasted_iota(jnp.int32, (C, C), 1)
    tril = (s <= r).astype(BF16)
    rh = lax.broadcasted_iota(jnp.int32, (H2, H2), 0)
    sh = lax.broadcasted_iota(jnp.int32, (H2, H2), 1)
    lvl_mask = {m: ((rh // (2 * m)) == (sh // (2 * m))) & (((rh // m) % 2) == 1) & (((sh // m) % 2) == 0)
                for m in HG_LEVELS[1:]}
    sub = lax.broadcasted_iota(jnp.int32, (C // SUBLANES, SUBLANES, Dk), 1)

    def roll8(x, d):
        return pltpu.roll(x.reshape(C // SUBLANES, SUBLANES, Dk), d, 1)

    def front(n):
        rows = slice(n * C, (n + 1) * C)
        c1t = c1 * jnp.tanh(0.5 * f_ref[0, rows, :])
        f = (1.0 - c1) + c1t
        kk = c1 - c1t
        qq = _silu_tanh(q_ref[0, rows, :])
        ii = i_ref[0, rows, :]
        hi, mid, lo = _split3(jnp.log2(f))
        bb = jnp.dot(tril, jnp.concatenate([hi, mid, lo], axis=1), preferred_element_type=F32)
        b = bb[:, 0:Dk] + bb[:, Dk:2 * Dk] + bb[:, 2 * Dk:3 * Dk]
        return dict(rows=rows, kk=kk, qq=qq, ii=ii, ii_bf=ii.astype(BF16), b=b)

    def level_z(v, m):
        b, kk, qq = v["b"], v["kk"], v["qq"]
        if m >= SUBLANES:
            parts, srcs = [], []
            for p in range(0, C, 2 * m):
                bm = b[p + m - 1:p + m, :]
                parts += [bm - b[p:p + m], b[p + m:p + 2 * m] - bm]
                srcs += [kk[p:p + m], qq[p + m:p + 2 * m]]
            arg = jnp.concatenate(parts, axis=0)
            src = jnp.concatenate(srcs, axis=0)
        else:
            b3 = b.reshape(C // SUBLANES, SUBLANES, Dk)
            if m == 1:
                bm = jnp.where(sub % 2 == 1, roll8(b, 1), b3)
            else:
                bm = jnp.broadcast_to(b3[:, m - 1:m, :], b3.shape)
                for p in range(2 * m, SUBLANES, 2 * m):
                    bm = jnp.where(sub >= p, jnp.broadcast_to(b3[:, p + m - 1:p + m, :], b3.shape), bm)
            upper = (sub // m) % 2 == 1
            arg = ((b3 - bm) * jnp.where(upper, 1.0, -1.0)).reshape(C, Dk)
            src = jnp.where(upper, qq.reshape(b3.shape), kk.reshape(b3.shape)).reshape(C, Dk)
        return (src * jnp.exp2(arg)).astype(BF16)

    vs = [front(j) for j in range(n_steps)]
    a_lo = [None] * n_steps
    a_d0 = [jnp.zeros((H2, H2), F32)] * n_steps
    a_d1 = [jnp.zeros((H2, H2), F32)] * n_steps
    for li, m in enumerate(HG_LEVELS):
        for j, v in enumerate(vs):
            z = level_z(v, m)
            if li == 0:
                a_lo[j] = lax.dot_general(z[H2:], z[:H2], nt, preferred_element_type=F32)
            else:
                g0 = lax.dot_general(z[:H2], z[:H2], nt, preferred_element_type=F32)
                g1 = lax.dot_general(z[H2:], z[H2:], nt, preferred_element_type=F32)
                a_d0[j] = jnp.where(lvl_mask[m], g0, a_d0[j])
                a_d1[j] = jnp.where(lvl_mask[m], g1, a_d1[j])
    intra = []
    for j, v in enumerate(vs):
        o_top = jnp.dot(a_d0[j].astype(BF16), v["ii_bf"][:H2], preferred_element_type=F32)
        o_bot = jnp.dot(jnp.concatenate([a_lo[j], a_d1[j]], axis=1).astype(BF16), v["ii_bf"],
                        preferred_element_type=F32)
        o_diag = jnp.sum(v["qq"] * v["kk"], axis=-1, keepdims=True) * v["ii"]
        intra.append(jnp.concatenate([o_top, o_bot], axis=0) + o_diag)
    st = jnp.zeros((Dk, Dk), F32)
    for j, v in enumerate(vs):
        b = v["b"]
        b_last = b[C - 1:C, :]
        qe = (v["qq"] * jnp.exp2(b)).astype(BF16)
        o = intra[j] + lax.dot_general(qe, st.astype(BF16), nt, preferred_element_type=F32)
        ke = (v["kk"] * jnp.exp2(b_last - b)).astype(BF16)
        st = st * jnp.exp2(b_last) + lax.dot_general(
            v["ii_bf"], ke, (((0,), (0,)), ((), ())), preferred_element_type=F32)
        ms = jnp.mean(o * o, axis=-1, keepdims=True)
        y = o * lax.rsqrt(ms + EPS) * gn
        y = y * _silu_tanh(g_ref[0, v["rows"], :])
        o_ref[0, v["rows"], :] = y.astype(o_ref.dtype)


def _hgrn(proj_b, lb_logits_h, gnorm_g):
    B, T, W4 = proj_b.shape
    W = W4 // 4
    H = W // REC_HEAD_DIM
    n_slots = lb_logits_h.shape[1]
    blk = lambda off: pl.BlockSpec((1, T, REC_HEAD_DIM), lambda b, h, off=off: (b, 0, off * H + h))
    return pl.pallas_call(
        functools.partial(_hgrn_kernel, n_steps=T // HG_CHUNK),
        out_shape=jax.ShapeDtypeStruct((B, T, W), BF16),
        grid=(B, H),
        in_specs=[blk(0), blk(1), blk(2), blk(3),
                  pl.BlockSpec((1, n_slots, REC_HEAD_DIM), lambda b, h: (h, 0, 0)),
                  pl.BlockSpec((1, REC_HEAD_DIM), lambda b, h: (0, 0))],
        out_specs=pl.BlockSpec((1, T, REC_HEAD_DIM), lambda b, h: (b, 0, h)),
        compiler_params=_cparams(("parallel", "parallel"), 32),
        name="hgrn2",
    )(proj_b, proj_b, proj_b, proj_b, lb_logits_h, gnorm_g)


def _outproj_kernel(oa_ref, ob_ref, w_ref, x_ref, mod_ref, g_ref, b_ref, x1_ref, *, alpha, sub):
    wa = oa_ref.shape[-1]
    gate1 = _mod_row(mod_ref, pl.program_id(0), 2)
    for r0 in range(0, x_ref.shape[1], sub):
        rs = slice(r0, r0 + sub)
        mix = jnp.dot(oa_ref[0, rs, :], w_ref[0:wa, :], preferred_element_type=F32)
        mix = mix + jnp.dot(ob_ref[0, rs, :], w_ref[wa:, :], preferred_element_type=F32)
        x1_ref[0, rs, :] = _ln_rows(alpha * x_ref[0, rs, :] + gate1 * mix) * g_ref[...] + b_ref[...]


def _outproj(o_a, o_b, w_o, x, mod, ln_g, ln_b, alpha, tm=512, sub=128):
    B, T, D = x.shape
    Wa, Wb = o_a.shape[-1], o_b.shape[-1]
    return pl.pallas_call(
        functools.partial(_outproj_kernel, alpha=alpha, sub=sub),
        out_shape=jax.ShapeDtypeStruct((B, T, D), F32),
        grid=(B, T // tm),
        in_specs=[pl.BlockSpec((1, tm, Wa), lambda b, i: (b, i, 0)),
                  pl.BlockSpec((1, tm, Wb), lambda b, i: (b, i, 0)),
                  pl.BlockSpec((Wa + Wb, D), lambda b, i: (0, 0)),
                  pl.BlockSpec((1, tm, D), lambda b, i: (b, i, 0)),
                  pl.BlockSpec((B, N_MOD * D), lambda b, i: (0, 0)),
                  pl.BlockSpec((1, D), lambda b, i: (0, 0)),
                  pl.BlockSpec((1, D), lambda b, i: (0, 0))],
        out_specs=pl.BlockSpec((1, tm, D), lambda b, i: (b, i, 0)),
        compiler_params=_cparams(("parallel", "parallel"), 48),
        name="out_proj_ln1",
    )(o_a, o_b, w_o, x, mod, ln_g, ln_b)


def _ffn_kernel(wg_ref, wu_ref, wo_ref, x_ref, mod_ref, g_ref, b_ref, o_ref, h_sc, *, alpha, sub):
    f = pl.program_id(2)
    last = pl.num_programs(2) - 1

    def partial_out(rs):
        h = h_sc[rs, :]
        gate = jnp.dot(h, wg_ref[...], preferred_element_type=F32)
        up = jnp.dot(h, wu_ref[...], preferred_element_type=F32)
        act = (_silu_tanh(gate) * up).astype(BF16)
        return jnp.dot(act, wo_ref[...], preferred_element_type=F32)

    everything = slice(0, o_ref.shape[1])

    @pl.when(f == 0)
    def _():
        shift2 = _mod_row(mod_ref, pl.program_id(0), 3)
        scale2 = _mod_row(mod_ref, pl.program_id(0), 4)
        for r0 in range(0, o_ref.shape[1], sub):
            rs = slice(r0, r0 + sub)
            h_sc[rs, :] = (_ln_rows(x_ref[0, rs, :]) * (1.0 + scale2) + shift2).astype(BF16)
        o_ref[0] = partial_out(everything)

    @pl.when((f > 0) & (f < last))
    def _():
        o_ref[0] += partial_out(everything)

    @pl.when(f == last)
    def _():
        gate2 = _mod_row(mod_ref, pl.program_id(0), 5)
        for r0 in range(0, o_ref.shape[1], sub):
            rs = slice(r0, r0 + sub)
            y = o_ref[0, rs, :] + partial_out(rs)
            o_ref[0, rs, :] = _ln_rows(alpha * x_ref[0, rs, :] + gate2 * y) * g_ref[...] + b_ref[...]


def _ffn(w_in, w_out, x1, mod, ln_g, ln_b, alpha, tm=1024, tf=512, sub=256):
    B, T, D = x1.shape
    F = w_out.shape[0]
    nf = F // tf
    return pl.pallas_call(
        functools.partial(_ffn_kernel, alpha=alpha, sub=sub),
        out_shape=jax.ShapeDtypeStruct((B, T, D), F32),
        grid=(B, T // tm, nf),
        in_specs=[pl.BlockSpec((D, tf), lambda b, i, f: (0, f)),
                  pl.BlockSpec((D, tf), lambda b, i, f: (0, nf + f)),
                  pl.BlockSpec((tf, D), lambda b, i, f: (f, 0)),
                  pl.BlockSpec((1, tm, D), lambda b, i, f: (b, i, 0)),
                  pl.BlockSpec((B, N_MOD * D), lambda b, i, f: (0, 0)),
                  pl.BlockSpec((1, D), lambda b, i, f: (0, 0)),
                  pl.BlockSpec((1, D), lambda b, i, f: (0, 0))],
        out_specs=pl.BlockSpec((1, tm, D), lambda b, i, f: (b, i, 0)),
        scratch_shapes=[pltpu.VMEM((tm, D), BF16)],
        compiler_params=_cparams(("parallel", "parallel", "arbitrary"), 60),
        name="swiglu_ffn_ln2",
    )(w_in, w_in, w_out, x1, mod, ln_g, ln_b)


def _bias_vectors(rel_bias):
    H, n_rel = rel_bias.shape
    max_rel = (n_rel - 1) // 2
    u = jnp.arange(BIAS_W)
    idx = jnp.clip(KBLK - u, -max_rel, max_rel) + max_rel
    return rel_bias[:, idx]


def kernel(x, c, w_ada, b_ada, w_in, rel_bias, attn_norm_g, lb_logits, gnorm_g, w_o,
           ln1_g, ln1_b, w_ffn_in, w_ffn_out, ln2_g, ln2_b):
    B, T, D = x.shape
    depth = w_ada.shape[0]
    alpha = (2 * depth) ** 0.25
    attn_w = attn_norm_g.shape[1]
    rec_w = lb_logits.shape[1]
    n_slots = lb_logits.shape[0]
    rec_heads = rec_w // REC_HEAD_DIM
    assert depth == 1 and n_slots == depth + 1
    for layer in range(depth):
        mod = _mod(c, w_ada[layer], b_ada[layer])
        h1 = _ln_mod(x, mod, 0).reshape(B * T, D)
        q_scale = jnp.where(jnp.arange(w_in.shape[2]) < attn_w, ATTN_HEAD_DIM ** -0.5 * LOG2E, 1.0)
        q_scale = q_scale.astype(F32).reshape(1, -1)
        proj_a = _matmul(h1, w_in[layer], q_scale, 0, 3 * attn_w, BF16).reshape(B, T, 3 * attn_w)
        proj_b = _matmul(h1, w_in[layer], q_scale, 3 * attn_w, 4 * rec_w, F32).reshape(B, T, 4 * rec_w)
        bias_vec = (_bias_vectors(rel_bias[layer]) * LOG2E).reshape(-1, 2, BIAS_W)
        o_a, (w_o_bf, w_ffn_in_bf, w_ffn_out_bf) = _attention(
            proj_a, bias_vec, attn_norm_g[layer].reshape(-1, 1, LANES),
            (w_o[layer], w_ffn_in[layer], w_ffn_out[layer]))
        lbl = lb_logits.reshape(n_slots, rec_heads, REC_HEAD_DIM).transpose(1, 0, 2)
        o_b = _hgrn(proj_b, lbl, gnorm_g[layer].reshape(1, REC_HEAD_DIM))
        x = _outproj(o_a, o_b, w_o_bf, x, mod,
                     ln1_g[layer].reshape(1, D), ln1_b[layer].reshape(1, D), alpha)
        x = _ffn(w_ffn_in_bf, w_ffn_out_bf, x, mod,
                 ln2_g[layer].reshape(1, D), ln2_b[layer].reshape(1, D), alpha)
    return x
```

```python
import functools

import jax
import jax.numpy as jnp
from jax import lax
from jax.experimental import pallas as pl
from jax.experimental.pallas import tpu as pltpu

F32 = jnp.float32
BF16 = jnp.bfloat16

CHUNK = 64
N_PAST_CHUNKS = 8
BAND = (N_PAST_CHUNKS + 1) * CHUNK
ATTN_HEAD_DIM = 64
REC_HEAD_DIM = 128
N_MOD = 6
EPS = 1e-5
LANES = 128
SUBLANES = 8
BF16_ROWS = 16
QBLK = 2 * CHUNK
KBLK = BAND + CHUNK
BIAS_W = KBLK + QBLK
N_PAD_STEPS = N_PAST_CHUNKS * CHUNK // QBLK

MIB = 1024 * 1024


def _cparams(sem, vmem_mib):
    return pltpu.CompilerParams(dimension_semantics=sem, vmem_limit_bytes=vmem_mib * MIB)


def _silu_tanh(x):
    h = 0.5 * x
    return h + h * jnp.tanh(h)


def _mod_row(mod_ref, b, r):
    d = mod_ref.shape[1] // N_MOD
    return mod_ref[pl.ds(b, 1), r * d:(r + 1) * d]


def _ln_rows(x):
    mu = jnp.mean(x, axis=-1, keepdims=True)
    xc = x - mu
    var = jnp.mean(xc * xc, axis=-1, keepdims=True)
    return xc * lax.rsqrt(var + EPS)


def _split3(x):
    hi = x.astype(BF16)
    r1 = x - hi.astype(F32)
    mid = r1.astype(BF16)
    lo = (r1 - mid.astype(F32)).astype(BF16)
    return hi, mid, lo


def _mod_kernel(c_ref, w_ref, b_ref, o_ref):
    act = jnp.concatenate(_split3(_silu_tanh(c_ref[...])), axis=0)
    w = w_ref[...]
    w_hi = w.astype(BF16)
    w_mid = (w - w_hi.astype(F32)).astype(BF16)
    acc = jnp.dot(act, w_hi, preferred_element_type=F32) + jnp.dot(act, w_mid, preferred_element_type=F32)
    out = acc[0:SUBLANES] + acc[SUBLANES:2 * SUBLANES] + acc[2 * SUBLANES:3 * SUBLANES]
    o_ref[...] = out[0:o_ref.shape[0]] + b_ref[...]


def _mod(c, w_ada, b_ada, tn=1024):
    B, D = c.shape
    N = w_ada.shape[1]
    assert B <= SUBLANES
    return pl.pallas_call(
        _mod_kernel,
        out_shape=jax.ShapeDtypeStruct((B, N), F32),
        grid=(N // tn,),
        in_specs=[pl.BlockSpec((SUBLANES, D), lambda j: (0, 0)),
                  pl.BlockSpec((D, tn), lambda j: (0, j)),
                  pl.BlockSpec((1, tn), lambda j: (0, j))],
        out_specs=pl.BlockSpec((B, tn), lambda j: (0, j)),
        compiler_params=_cparams(("parallel",), 40),
        name="adaln_mod",
    )(jnp.pad(c, ((0, SUBLANES - B), (0, 0))), w_ada, b_ada.reshape(1, N))


def _ln_mod_kernel(x_ref, mod_ref, o_ref, *, shift_row):
    y = _ln_rows(x_ref[0])
    b = pl.program_id(0)
    shift = _mod_row(mod_ref, b, shift_row)
    scale = _mod_row(mod_ref, b, shift_row + 1)
    o_ref[0] = (y * (1.0 + scale) + shift).astype(o_ref.dtype)


def _ln_mod(x, mod, shift_row, tm=1024):
    B, T, D = x.shape
    return pl.pallas_call(
        functools.partial(_ln_mod_kernel, shift_row=shift_row),
        out_shape=jax.ShapeDtypeStruct((B, T, D), BF16),
        grid=(B, T // tm),
        in_specs=[pl.BlockSpec((1, tm, D), lambda b, i: (b, i, 0)),
                  pl.BlockSpec((B, N_MOD * D), lambda b, i: (0, 0))],
        out_specs=pl.BlockSpec((1, tm, D), lambda b, i: (b, i, 0)),
        compiler_params=_cparams(("parallel", "parallel"), 32),
        name="ln_modulate",
    )(x, mod)


def _matmul_kernel(a_ref, w_ref, s_ref, o_ref, w_bf):
    @pl.when(pl.program_id(1) == 0)
    def _():
        w_bf[...] = (w_ref[...] * s_ref[...]).astype(BF16)

    o_ref[...] = jnp.dot(a_ref[...], w_bf[...], preferred_element_type=F32).astype(o_ref.dtype)


def _matmul(a, w, col_scale, col0, n_out, out_dtype, tm=1024, tn=1024):
    M, K = a.shape
    assert col0 % tn == 0 and n_out % tn == 0 and M % tm == 0
    col_block0 = col0 // tn
    return pl.pallas_call(
        _matmul_kernel,
        out_shape=jax.ShapeDtypeStruct((M, n_out), out_dtype),
        grid=(n_out // tn, M // tm),
        in_specs=[pl.BlockSpec((tm, K), lambda j, i: (i, 0)),
                  pl.BlockSpec((K, tn), lambda j, i: (0, j + col_block0)),
                  pl.BlockSpec((1, tn), lambda j, i: (0, j + col_block0))],
        out_specs=pl.BlockSpec((tm, tn), lambda j, i: (i, j)),
        scratch_shapes=[pltpu.VMEM((K, tn), BF16)],
        compiler_params=_cparams(("parallel", "arbitrary"), 48),
        name="in_proj",
    )(a, w, col_scale)


NEG_BIG = -1e30
LOG2E = 1.4426950408889634


def _attn_kernel(q_ref, k_ref, v_ref, bias_ref, gain_ref, *rest, n_chunks, n_cast):
    cast_in = rest[:n_cast]
    o_ref = rest[n_cast]
    cast_out = rest[n_cast + 1:2 * n_cast + 1]
    kta, ktb, vpa, vpb, tab, s_a, s_b, s_c = rest[2 * n_cast + 1:]
    pad = N_PAST_CHUNKS * CHUNK
    T = n_chunks * CHUNK
    n_steps = T // QBLK
    head0 = lax.broadcasted_iota(jnp.int32, (QBLK, LANES), 1) < ATTN_HEAD_DIM
    m0 = jnp.where(head0, 1.0, 0.0).astype(BF16)
    m1 = jnp.where(head0, 0.0, 1.0).astype(BF16)

    head0_t = lax.broadcasted_iota(jnp.int32, (LANES, QBLK), 0) < ATTN_HEAD_DIM
    mt0 = jnp.where(head0_t, 1.0, 0.0).astype(BF16)
    mt1 = jnp.where(head0_t, 0.0, 1.0).astype(BF16)

    for ref in (kta, ktb):
        ref[:, 0:pad] = jnp.zeros((LANES, pad), BF16)
    for ref in (vpa, vpb):
        ref[0:pad, :] = jnp.zeros((pad, 2 * LANES), BF16)

    def prep(blk):
        src = slice(blk * QBLK, (blk + 1) * QBLK)
        dst = slice(pad + blk * QBLK, pad + (blk + 1) * QBLK)
        kt = k_ref[0, src, :].T
        v = v_ref[0, src, :]
        kta[:, dst] = kt * mt0
        ktb[:, dst] = kt * mt1
        vpa[dst, 0:LANES] = v * m0
        vpb[dst, 0:LANES] = v * m1
        vpa[dst, LANES:2 * LANES] = m0
        vpb[dst, LANES:2 * LANES] = m1

    @pl.when(pl.program_id(1) == 0)
    def _():
        qry = lax.broadcasted_iota(jnp.int32, (QBLK, KBLK), 0)
        key = lax.broadcasted_iota(jnp.int32, (QBLK, KBLK), 1)
        in_band = ((qry < CHUNK) & (key < BAND)) | ((qry >= CHUNK) & (key >= CHUNK))
        for hh in range(2):
            g = jnp.broadcast_to(bias_ref[0, hh:hh + 1, :], (QBLK, BIAS_W))
            t = pltpu.roll(g, BIAS_W - QBLK, 1, stride=1, stride_axis=0)[:, :KBLK]
            t = jnp.where(in_band, t, NEG_BIG)
            for var in range(N_PAD_STEPS + 1):
                first_real = pad - var * QBLK
                tv = jnp.where(key >= first_real, t, NEG_BIG) if first_real > 0 else t
                tab[var, hh] = tv.astype(BF16)

    eye = (lax.broadcasted_iota(jnp.int32, (QBLK, QBLK), 0)
           == lax.broadcasted_iota(jnp.int32, (QBLK, QBLK), 1)).astype(BF16)
    gain = gain_ref[0]

    def scores(m, dst):
        prep(m)
        band = slice(m * QBLK, m * QBLK + KBLK)
        var = min(m, N_PAD_STEPS)
        lhs = jnp.concatenate([q_ref[0, m * QBLK:(m + 1) * QBLK, :], eye], axis=1)
        rhs = jnp.concatenate([jnp.concatenate([kta[:, band], ktb[:, band]], axis=1),
                               jnp.concatenate([tab[var, 0], tab[var, 1]], axis=1)], axis=0)
        dst[...] = jnp.dot(lhs, rhs, preferred_element_type=F32)

    def finish(m, src):
        band = slice(m * QBLK, m * QBLK + KBLK)
        s = src[...]
        p = jnp.concatenate(
            [jnp.exp2(sh - jnp.max(sh, axis=-1, keepdims=True)) for sh in (s[:, 0:KBLK], s[:, KBLK:2 * KBLK])],
            axis=1).astype(BF16)
        pv = jnp.dot(p, jnp.concatenate([vpa[band, :], vpb[band, :]], axis=0), preferred_element_type=F32)
        o = pv[:, 0:LANES] * (1.0 / pv[:, LANES:2 * LANES])
        o2 = o * o
        ms0 = jnp.sum(jnp.where(head0, o2, 0.0), axis=-1, keepdims=True) / ATTN_HEAD_DIM
        ms1 = jnp.sum(jnp.where(head0, 0.0, o2), axis=-1, keepdims=True) / ATTN_HEAD_DIM
        y = o * lax.rsqrt(jnp.where(head0, ms0, ms1) + EPS) * gain
        o_ref[0, m * QBLK:(m + 1) * QBLK, :] = y.astype(o_ref.dtype)

    pieces = [(src_ref, dst_ref, r0) for src_ref, dst_ref in zip(cast_in, cast_out)
              for r0 in range(0, src_ref.shape[0], BF16_ROWS)]
    bufs = (s_a, s_b, s_c)
    depth = len(bufs)
    ahead = depth - 1
    for m in range(ahead):
        scores(m, bufs[m % depth])
    for m in range(n_steps):
        if m + ahead < n_steps:
            scores(m + ahead, bufs[(m + ahead) % depth])
        finish(m, bufs[m % depth])
        for src_ref, dst_ref, r0 in pieces[m * len(pieces) // n_steps:(m + 1) * len(pieces) // n_steps]:
            dst_ref[r0:r0 + BF16_ROWS, :] = src_ref[r0:r0 + BF16_ROWS, :].astype(BF16)


def _attention(proj_a, bias_vec, attn_gain, cast_weights):
    B, T, W3 = proj_a.shape
    W = W3 // 3
    n_pairs = W // LANES
    n_grid = n_pairs * B
    t_pad = T + N_PAST_CHUNKS * CHUNK
    cast_specs = []
    for w in cast_weights:
        rows = w.shape[0] // n_grid
        assert w.shape[0] % n_grid == 0 and rows % BF16_ROWS == 0
        cast_specs.append(pl.BlockSpec((rows, w.shape[1]), lambda h, b: (h * B + b, 0)))
    outs = pl.pallas_call(
        functools.partial(_attn_kernel, n_chunks=T // CHUNK, n_cast=len(cast_weights)),
        out_shape=(jax.ShapeDtypeStruct((B, T, W), BF16),
                   *[jax.ShapeDtypeStruct(w.shape, BF16) for w in cast_weights]),
        grid=(n_pairs, B),
        in_specs=[pl.BlockSpec((1, T, LANES), lambda h, b: (b, 0, h)),
                  pl.BlockSpec((1, T, LANES), lambda h, b: (b, 0, n_pairs + h)),
                  pl.BlockSpec((1, T, LANES), lambda h, b: (b, 0, 2 * n_pairs + h)),
                  pl.BlockSpec((1, 2, BIAS_W), lambda h, b: (h, 0, 0)),
                  pl.BlockSpec((1, 1, LANES), lambda h, b: (h, 0, 0)),
                  *cast_specs],
        out_specs=(pl.BlockSpec((1, T, LANES), lambda h, b: (b, 0, h)), *cast_specs),
        scratch_shapes=[pltpu.VMEM((LANES, t_pad), BF16),
                        pltpu.VMEM((LANES, t_pad), BF16),
                        pltpu.VMEM((t_pad, 2 * LANES), BF16),
                        pltpu.VMEM((t_pad, 2 * LANES), BF16),
                        pltpu.VMEM((N_PAD_STEPS + 1, 2, QBLK, KBLK), BF16),
                        pltpu.VMEM((QBLK, 2 * KBLK), F32),
                        pltpu.VMEM((QBLK, 2 * KBLK), F32),
                        pltpu.VMEM((QBLK, 2 * KBLK), F32)],
        compiler_params=_cparams(("parallel", "arbitrary"), 48),
        name="chunk_attention",
    )(proj_a, proj_a, proj_a, bias_vec, attn_gain, *cast_weights)
    return outs[0], outs[1:]


HG_CHUNK = 256
HG_LEVELS = (128, 64, 32, 16, 8, 4, 2, 1)
assert HG_LEVELS[-1] == 1


def _hgrn_kernel(q_ref, f_ref, i_ref, g_ref, lbl_ref, gn_ref, o_ref, *, n_steps):
    C = HG_CHUNK
    H2 = C // 2
    Dk = REC_HEAD_DIM
    nt = (((1,), (1,)), ((), ()))
    lbl = lbl_ref[0]
    e = jnp.exp(lbl - jnp.max(lbl, axis=0, keepdims=True))
    lb = e[0:1, :] / jnp.sum(e, axis=0, keepdims=True)
    c1 = 0.5 * (1.0 - lb)
    gn = gn_ref[...]

    r = lax.broadcasted_iota(jnp.int32, (C, C), 0)
    s = lax.broadcasted_iota(jnp.int32, (C, C), 1)
    tril = (s <= r).astype(BF16)
    rh = lax.broadcasted_iota(jnp.int32, (H2, H2), 0)
    sh = lax.broadcasted_iota(jnp.int32, (H2, H2), 1)
    lvl_mask = {m: ((rh // (2 * m)) == (sh // (2 * m))) & (((rh // m) % 2) == 1) & (((sh // m) % 2) == 0)
                for m in HG_LEVELS[1:]}
    sub = lax.broadcasted_iota(jnp.int32, (C // SUBLANES, SUBLANES, Dk), 1)

    def roll8(x, d):
        return pltpu.roll(x.reshape(C // SUBLANES, SUBLANES, Dk), d, 1)

    def front(n):
        rows = slice(n * C, (n + 1) * C)
        c1t = c1 * jnp.tanh(0.5 * f_ref[0, rows, :])
        f = (1.0 - c1) + c1t
        kk = c1 - c1t
        qq = _silu_tanh(q_ref[0, rows, :])
        ii = i_ref[0, rows, :]
        logf = jnp.log2(f)
        hi = logf.astype(BF16)
        mid = (logf - hi.astype(F32)).astype(BF16)
        bb = jnp.dot(tril, jnp.concatenate([hi, mid], axis=1), preferred_element_type=F32)
        b = bb[:, 0:Dk] + bb[:, Dk:2 * Dk]
        return dict(rows=rows, kk=kk, qq=qq, ii=ii, ii_bf=ii.astype(BF16), b=b)

    def level_z(v, m):
        b, kk, qq = v["b"], v["kk"], v["qq"]
        if m >= SUBLANES:
            parts, srcs = [], []
            for p in range(0, C, 2 * m):
                bm = b[p + m - 1:p + m, :]
                parts += [bm - b[p:p + m], b[p + m:p + 2 * m] - bm]
                srcs += [kk[p:p + m], qq[p + m:p + 2 * m]]
            arg = jnp.concatenate(parts, axis=0)
            src = jnp.concatenate(srcs, axis=0)
        else:
            b3 = b.reshape(C // SUBLANES, SUBLANES, Dk)
            if m == 1:
                bm = jnp.where(sub % 2 == 1, roll8(b, 1), b3)
            else:
                bm = jnp.broadcast_to(b3[:, m - 1:m, :], b3.shape)
                for p in range(2 * m, SUBLANES, 2 * m):
                    bm = jnp.where(sub >= p, jnp.broadcast_to(b3[:, p + m - 1:p + m, :], b3.shape), bm)
            upper = (sub // m) % 2 == 1
            arg = ((b3 - bm) * jnp.where(upper, 1.0, -1.0)).reshape(C, Dk)
            src = jnp.where(upper, qq.reshape(b3.shape), kk.reshape(b3.shape)).reshape(C, Dk)
        return (src * jnp.exp2(arg)).astype(BF16)

    vs = [front(j) for j in range(n_steps)]
    a_lo = [None] * n_steps
    a_d0 = [jnp.zeros((H2, H2), F32)] * n_steps
    a_d1 = [jnp.zeros((H2, H2), F32)] * n_steps
    for li, m in enumerate(HG_LEVELS):
        for j, v in enumerate(vs):
            z = level_z(v, m)
            if li == 0:
                a_lo[j] = lax.dot_general(z[H2:], z[:H2], nt, preferred_element_type=F32)
            else:
                g0 = lax.dot_general(z[:H2], z[:H2], nt, preferred_element_type=F32)
                g1 = lax.dot_general(z[H2:], z[H2:], nt, preferred_element_type=F32)
                a_d0[j] = jnp.where(lvl_mask[m], g0, a_d0[j])
                a_d1[j] = jnp.where(lvl_mask[m], g1, a_d1[j])
    intra = []
    for j, v in enumerate(vs):
        o_top = jnp.dot(a_d0[j].astype(BF16), v["ii_bf"][:H2], preferred_element_type=F32)
        o_bot = jnp.dot(jnp.concatenate([a_lo[j], a_d1[j]], axis=1).astype(BF16), v["ii_bf"],
                        preferred_element_type=F32)
        o_diag = jnp.sum(v["qq"] * v["kk"], axis=-1, keepdims=True) * v["ii"]
        intra.append(jnp.concatenate([o_top, o_bot], axis=0) + o_diag)
    st = jnp.zeros((Dk, Dk), F32)
    for j, v in enumerate(vs):
        b = v["b"]
        b_last = b[C - 1:C, :]
        qe = (v["qq"] * jnp.exp2(b)).astype(BF16)
        o = intra[j] + lax.dot_general(qe, st.astype(BF16), nt, preferred_element_type=F32)
        ke = (v["kk"] * jnp.exp2(b_last - b)).astype(BF16)
        st = st * jnp.exp2(b_last) + lax.dot_general(
            v["ii_bf"], ke, (((0,), (0,)), ((), ())), preferred_element_type=F32)
        ms = jnp.mean(o * o, axis=-1, keepdims=True)
        y = o * lax.rsqrt(ms + EPS) * gn
        y = y * _silu_tanh(g_ref[0, v["rows"], :])
        o_ref[0, v["rows"], :] = y.astype(o_ref.dtype)


def _hgrn(proj_b, lb_logits_h, gnorm_g):
    B, T, W4 = proj_b.shape
    W = W4 // 4
    H = W // REC_HEAD_DIM
    n_slots = lb_logits_h.shape[1]
    blk = lambda off: pl.BlockSpec((1, T, REC_HEAD_DIM), lambda b, h, off=off: (b, 0, off * H + h))
    return pl.pallas_call(
        functools.partial(_hgrn_kernel, n_steps=T // HG_CHUNK),
        out_shape=jax.ShapeDtypeStruct((B, T, W), BF16),
        grid=(B, H),
        in_specs=[blk(0), blk(1), blk(2), blk(3),
                  pl.BlockSpec((1, n_slots, REC_HEAD_DIM), lambda b, h: (h, 0, 0)),
                  pl.BlockSpec((1, REC_HEAD_DIM), lambda b, h: (0, 0))],
        out_specs=pl.BlockSpec((1, T, REC_HEAD_DIM), lambda b, h: (b, 0, h)),
        compiler_params=_cparams(("parallel", "parallel"), 32),
        name="hgrn2",
    )(proj_b, proj_b, proj_b, proj_b, lb_logits_h, gnorm_g)


def _outproj_kernel(oa_ref, ob_ref, w_ref, x_ref, mod_ref, g_ref, b_ref, x1_ref, *, alpha, sub):
    wa = oa_ref.shape[-1]
    gate1 = _mod_row(mod_ref, pl.program_id(0), 2)
    for r0 in range(0, x_ref.shape[1], sub):
        rs = slice(r0, r0 + sub)
        mix = jnp.dot(oa_ref[0, rs, :], w_ref[0:wa, :], preferred_element_type=F32)
        mix = mix + jnp.dot(ob_ref[0, rs, :], w_ref[wa:, :], preferred_element_type=F32)
        x1_ref[0, rs, :] = _ln_rows(alpha * x_ref[0, rs, :] + gate1 * mix) * g_ref[...] + b_ref[...]


def _outproj(o_a, o_b, w_o, x, mod, ln_g, ln_b, alpha, tm=512, sub=128):
    B, T, D = x.shape
    Wa, Wb = o_a.shape[-1], o_b.shape[-1]
    return pl.pallas_call(
        functools.partial(_outproj_kernel, alpha=alpha, sub=sub),
        out_shape=jax.ShapeDtypeStruct((B, T, D), F32),
        grid=(B, T // tm),
        in_specs=[pl.BlockSpec((1, tm, Wa), lambda b, i: (b, i, 0)),
                  pl.BlockSpec((1, tm, Wb), lambda b, i: (b, i, 0)),
                  pl.BlockSpec((Wa + Wb, D), lambda b, i: (0, 0)),
                  pl.BlockSpec((1, tm, D), lambda b, i: (b, i, 0)),
                  pl.BlockSpec((B, N_MOD * D), lambda b, i: (0, 0)),
                  pl.BlockSpec((1, D), lambda b, i: (0, 0)),
                  pl.BlockSpec((1, D), lambda b, i: (0, 0))],
        out_specs=pl.BlockSpec((1, tm, D), lambda b, i: (b, i, 0)),
        compiler_params=_cparams(("parallel", "parallel"), 48),
        name="out_proj_ln1",
    )(o_a, o_b, w_o, x, mod, ln_g, ln_b)


def _ffn_kernel(wg_ref, wu_ref, wo_ref, x_ref, mod_ref, g_ref, b_ref, o_ref, h_sc, *, alpha, sub):
    f = pl.program_id(2)
    last = pl.num_programs(2) - 1

    def partial_out(rs):
        h = h_sc[rs, :]
        gate = jnp.dot(h, wg_ref[...], preferred_element_type=F32)
        up = jnp.dot(h, wu_ref[...], preferred_element_type=F32)
        act = (_silu_tanh(gate) * up).astype(BF16)
        return jnp.dot(act, wo_ref[...], preferred_element_type=F32)

    everything = slice(0, o_ref.shape[1])

    @pl.when(f == 0)
    def _():
        shift2 = _mod_row(mod_ref, pl.program_id(0), 3)
        scale2 = _mod_row(mod_ref, pl.program_id(0), 4)
        for r0 in range(0, o_ref.shape[1], sub):
            rs = slice(r0, r0 + sub)
            h_sc[rs, :] = (_ln_rows(x_ref[0, rs, :]) * (1.0 + scale2) + shift2).astype(BF16)
        o_ref[0] = partial_out(everything)

    @pl.when((f > 0) & (f < last))
    def _():
        o_ref[0] += partial_out(everything)

    @pl.when(f == last)
    def _():
        gate2 = _mod_row(mod_ref, pl.program_id(0), 5)
        for r0 in range(0, o_ref.shape[1], sub):
            rs = slice(r0, r0 + sub)
            y = o_ref[0, rs, :] + partial_out(rs)
            o_ref[0, rs, :] = _ln_rows(alpha * x_ref[0, rs, :] + gate2 * y) * g_ref[...] + b_ref[...]


def _ffn(w_in, w_out, x1, mod, ln_g, ln_b, alpha, tm=1024, tf=512, sub=256):
    B, T, D = x1.shape
    F = w_out.shape[0]
    nf = F // tf
    return pl.pallas_call(
        functools.partial(_ffn_kernel, alpha=alpha, sub=sub),
        out_shape=jax.ShapeDtypeStruct((B, T, D), F32),
        grid=(B, T // tm, nf),
        in_specs=[pl.BlockSpec((D, tf), lambda b, i, f: (0, f)),
                  pl.BlockSpec((D, tf), lambda b, i, f: (0, nf + f)),
                  pl.BlockSpec((tf, D), lambda b, i, f: (f, 0)),
                  pl.BlockSpec((1, tm, D), lambda b, i, f: (b, i, 0)),
                  pl.BlockSpec((B, N_MOD * D), lambda b, i, f: (0, 0)),
                  pl.BlockSpec((1, D), lambda b, i, f: (0, 0)),
                  pl.BlockSpec((1, D), lambda b, i, f: (0, 0))],
        out_specs=pl.BlockSpec((1, tm, D), lambda b, i, f: (b, i, 0)),
        scratch_shapes=[pltpu.VMEM((tm, D), BF16)],
        compiler_params=_cparams(("parallel", "parallel", "arbitrary"), 60),
        name="swiglu_ffn_ln2",
    )(w_in, w_in, w_out, x1, mod, ln_g, ln_b)


def _bias_vectors(rel_bias):
    H, n_rel = rel_bias.shape
    max_rel = (n_rel - 1) // 2
    u = jnp.arange(BIAS_W)
    idx = jnp.clip(KBLK - u, -max_rel, max_rel) + max_rel
    return rel_bias[:, idx]


def kernel(x, c, w_ada, b_ada, w_in, rel_bias, attn_norm_g, lb_logits, gnorm_g, w_o,
           ln1_g, ln1_b, w_ffn_in, w_ffn_out, ln2_g, ln2_b):
    B, T, D = x.shape
    depth = w_ada.shape[0]
    alpha = (2 * depth) ** 0.25
    attn_w = attn_norm_g.shape[1]
    rec_w = lb_logits.shape[1]
    n_slots = lb_logits.shape[0]
    rec_heads = rec_w // REC_HEAD_DIM
    assert depth == 1 and n_slots == depth + 1
    for layer in range(depth):
        mod = _mod(c, w_ada[layer], b_ada[layer])
        h1 = _ln_mod(x, mod, 0).reshape(B * T, D)
        q_scale = jnp.where(jnp.arange(w_in.shape[2]) < attn_w, ATTN_HEAD_DIM ** -0.5 * LOG2E, 1.0)
        q_scale = q_scale.astype(F32).reshape(1, -1)
        proj_a = _matmul(h1, w_in[layer], q_scale, 0, 3 * attn_w, BF16).reshape(B, T, 3 * attn_w)
        proj_b = _matmul(h1, w_in[layer], q_scale, 3 * attn_w, 4 * rec_w, F32).reshape(B, T, 4 * rec_w)
        bias_vec = (_bias_vectors(rel_bias[layer]) * LOG2E).reshape(-1, 2, BIAS_W)
        o_a, (w_o_bf, w_ffn_in_bf, w_ffn_out_bf) = _attention(
            proj_a, bias_vec, attn_norm_g[layer].reshape(-1, 1, LANES),
            (w_o[layer], w_ffn_in[layer], w_ffn_out[layer]))
        lbl = lb_logits.reshape(n_slots, rec_heads, REC_HEAD_DIM).transpose(1, 0, 2)
        o_b = _hgrn(proj_b, lbl, gnorm_g[layer].reshape(1, REC_HEAD_DIM))
        x = _outproj(o_a, o_b, w_o_bf, x, mod,
                     ln1_g[layer].reshape(1, D), ln1_b[layer].reshape(1, D), alpha)
        x = _ffn(w_ffn_in_bf, w_ffn_out_bf, x, mod,
                 ln2_g[layer].reshape(1, D), ln2_b[layer].reshape(1, D), alpha)
    return x
```

```python
import functools

import jax
import jax.numpy as jnp
from jax import lax
from jax.experimental import pallas as pl
from jax.experimental.pallas import tpu as pltpu

F32 = jnp.float32
BF16 = jnp.bfloat16

CHUNK = 64
N_PAST_CHUNKS = 8
BAND = (N_PAST_CHUNKS + 1) * CHUNK
ATTN_HEAD_DIM = 64
REC_HEAD_DIM = 128
N_MOD = 6
EPS = 1e-5
LANES = 128
SUBLANES = 8
BF16_ROWS = 16
QBLK = 2 * CHUNK
KBLK = BAND + CHUNK
BIAS_W = KBLK + QBLK

MIB = 1024 * 1024


def _cparams(sem, vmem_mib):
    return pltpu.CompilerParams(dimension_semantics=sem, vmem_limit_bytes=vmem_mib * MIB)


def _silu_tanh(x):
    h = 0.5 * x
    return h + h * jnp.tanh(h)


def _mod_row(mod_ref, b, r):
    d = mod_ref.shape[1] // N_MOD
    return mod_ref[pl.ds(b, 1), r * d:(r + 1) * d]


def _ln_rows(x):
    mu = jnp.mean(x, axis=-1, keepdims=True)
    xc = x - mu
    var = jnp.mean(xc * xc, axis=-1, keepdims=True)
    return xc * lax.rsqrt(var + EPS)


def _split3(x):
    hi = x.astype(BF16)
    r1 = x - hi.astype(F32)
    mid = r1.astype(BF16)
    lo = (r1 - mid.astype(F32)).astype(BF16)
    return hi, mid, lo


def _mod_kernel(c_ref, w_ref, b_ref, o_ref):
    act = jnp.concatenate(_split3(_silu_tanh(c_ref[...])), axis=0)
    w = w_ref[...]
    w_hi = w.astype(BF16)
    w_mid = (w - w_hi.astype(F32)).astype(BF16)
    acc = jnp.dot(act, w_hi, preferred_element_type=F32) + jnp.dot(act, w_mid, preferred_element_type=F32)
    out = acc[0:SUBLANES] + acc[SUBLANES:2 * SUBLANES] + acc[2 * SUBLANES:3 * SUBLANES]
    o_ref[...] = out[0:o_ref.shape[0]] + b_ref[...]


def _mod(c, w_ada, b_ada, tn=1024):
    B, D = c.shape
    N = w_ada.shape[1]
    assert B <= SUBLANES
    return pl.pallas_call(
        _mod_kernel,
        out_shape=jax.ShapeDtypeStruct((B, N), F32),
        grid=(N // tn,),
        in_specs=[pl.BlockSpec((SUBLANES, D), lambda j: (0, 0)),
                  pl.BlockSpec((D, tn), lambda j: (0, j)),
                  pl.BlockSpec((1, tn), lambda j: (0, j))],
        out_specs=pl.BlockSpec((B, tn), lambda j: (0, j)),
        compiler_params=_cparams(("parallel",), 40),
        name="adaln_mod",
    )(jnp.pad(c, ((0, SUBLANES - B), (0, 0))), w_ada, b_ada.reshape(1, N))


def _ln_mod_kernel(x_ref, mod_ref, o_ref, *, shift_row):
    y = _ln_rows(x_ref[0])
    b = pl.program_id(0)
    shift = _mod_row(mod_ref, b, shift_row)
    scale = _mod_row(mod_ref, b, shift_row + 1)
    o_ref[0] = (y * (1.0 + scale) + shift).astype(o_ref.dtype)


def _ln_mod(x, mod, shift_row, tm=1024):
    B, T, D = x.shape
    return pl.pallas_call(
        functools.partial(_ln_mod_kernel, shift_row=shift_row),
        out_shape=jax.ShapeDtypeStruct((B, T, D), BF16),
        grid=(B, T // tm),
        in_specs=[pl.BlockSpec((1, tm, D), lambda b, i: (b, i, 0)),
                  pl.BlockSpec((B, N_MOD * D), lambda b, i: (0, 0))],
        out_specs=pl.BlockSpec((1, tm, D), lambda b, i: (b, i, 0)),
        compiler_params=_cparams(("parallel", "parallel"), 32),
        name="ln_modulate",
    )(x, mod)


def _matmul_kernel(a_ref, w_ref, s_ref, o_ref, w_bf):
    @pl.when(pl.program_id(1) == 0)
    def _():
        w_bf[...] = (w_ref[...] * s_ref[...]).astype(BF16)

    o_ref[...] = jnp.dot(a_ref[...], w_bf[...], preferred_element_type=F32).astype(o_ref.dtype)


def _matmul(a, w, col_scale, col0, n_out, out_dtype, tm=1024, tn=1024):
    M, K = a.shape
    assert col0 % tn == 0 and n_out % tn == 0 and M % tm == 0
    col_block0 = col0 // tn
    return pl.pallas_call(
        _matmul_kernel,
        out_shape=jax.ShapeDtypeStruct((M, n_out), out_dtype),
        grid=(n_out // tn, M // tm),
        in_specs=[pl.BlockSpec((tm, K), lambda j, i: (i, 0)),
                  pl.BlockSpec((K, tn), lambda j, i: (0, j + col_block0)),
                  pl.BlockSpec((1, tn), lambda j, i: (0, j + col_block0))],
        out_specs=pl.BlockSpec((tm, tn), lambda j, i: (i, j)),
        scratch_shapes=[pltpu.VMEM((K, tn), BF16)],
        compiler_params=_cparams(("parallel", "arbitrary"), 48),
        name="in_proj",
    )(a, w, col_scale)


NEG_BIG = -1e30
LOG2E = 1.4426950408889634


def _attn_kernel(q_ref, k_ref, v_ref, bias_ref, gain_ref, *rest, n_chunks, n_cast):
    cast_in = rest[:n_cast]
    o_ref = rest[n_cast]
    cast_out = rest[n_cast + 1:2 * n_cast + 1]
    kta, ktb, vpa, vpb, tab, s_a, s_b, s_c = rest[2 * n_cast + 1:]
    T = n_chunks * CHUNK
    n_steps = T // QBLK
    head0 = lax.broadcasted_iota(jnp.int32, (QBLK, LANES), 1) < ATTN_HEAD_DIM
    m0 = jnp.where(head0, 1.0, 0.0).astype(BF16)
    m1 = jnp.where(head0, 0.0, 1.0).astype(BF16)

    head0_t = lax.broadcasted_iota(jnp.int32, (LANES, QBLK), 0) < ATTN_HEAD_DIM
    mt0 = jnp.where(head0_t, 1.0, 0.0).astype(BF16)
    mt1 = jnp.where(head0_t, 0.0, 1.0).astype(BF16)

    def prep(blk):
        rows = slice(blk * QBLK, (blk + 1) * QBLK)
        kt = k_ref[0, rows, :].T
        v = v_ref[0, rows, :]
        kta[:, rows] = kt * mt0
        ktb[:, rows] = kt * mt1
        vpa[rows, 0:LANES] = v * m0
        vpb[rows, 0:LANES] = v * m1
        vpa[rows, LANES:2 * LANES] = m0
        vpb[rows, LANES:2 * LANES] = m1

    @pl.when(pl.program_id(1) == 0)
    def _():
        qry = lax.broadcasted_iota(jnp.int32, (QBLK, KBLK), 0)
        key = lax.broadcasted_iota(jnp.int32, (QBLK, KBLK), 1)
        in_band = ((qry < CHUNK) & (key < BAND)) | ((qry >= CHUNK) & (key >= CHUNK))
        for hh in range(2):
            g = jnp.broadcast_to(bias_ref[0, hh:hh + 1, :], (QBLK, BIAS_W))
            t = pltpu.roll(g, BIAS_W - QBLK, 1, stride=1, stride_axis=0)[:, :KBLK]
            tab[hh] = jnp.where(in_band, t, NEG_BIG).astype(BF16)

    eye = (lax.broadcasted_iota(jnp.int32, (QBLK, QBLK), 0)
           == lax.broadcasted_iota(jnp.int32, (QBLK, QBLK), 1)).astype(BF16)
    gain = gain_ref[0]

    def band_of(m):
        hi = (m + 1) * QBLK
        lo = max(0, hi - KBLK)
        return slice(lo, hi), KBLK - (hi - lo)

    def scores(m, dst):
        prep(m)
        band, col0 = band_of(m)
        lhs = jnp.concatenate([q_ref[0, m * QBLK:(m + 1) * QBLK, :], eye], axis=1)
        rhs = jnp.concatenate([jnp.concatenate([kta[:, band], ktb[:, band]], axis=1),
                               jnp.concatenate([tab[0, :, col0:], tab[1, :, col0:]], axis=1)], axis=0)
        dst[:, 0:rhs.shape[1]] = jnp.dot(lhs, rhs, preferred_element_type=F32)

    def finish(m, src):
        band, col0 = band_of(m)
        w = KBLK - col0
        s = src[:, 0:2 * w]
        p = jnp.concatenate(
            [jnp.exp2(sh - jnp.max(sh, axis=-1, keepdims=True)) for sh in (s[:, 0:w], s[:, w:2 * w])],
            axis=1).astype(BF16)
        pv = jnp.dot(p, jnp.concatenate([vpa[band, :], vpb[band, :]], axis=0), preferred_element_type=F32)
        o = pv[:, 0:LANES] * (1.0 / pv[:, LANES:2 * LANES])
        o2 = o * o
        ms0 = jnp.sum(jnp.where(head0, o2, 0.0), axis=-1, keepdims=True) / ATTN_HEAD_DIM
        ms1 = jnp.sum(jnp.where(head0, 0.0, o2), axis=-1, keepdims=True) / ATTN_HEAD_DIM
        y = o * lax.rsqrt(jnp.where(head0, ms0, ms1) + EPS) * gain
        o_ref[0, m * QBLK:(m + 1) * QBLK, :] = y.astype(o_ref.dtype)

    pieces = [(src_ref, dst_ref, r0) for src_ref, dst_ref in zip(cast_in, cast_out)
              for r0 in range(0, src_ref.shape[0], BF16_ROWS)]
    bufs = (s_a, s_b, s_c)
    depth = len(bufs)
    ahead = depth - 1
    for m in range(ahead):
        scores(m, bufs[m % depth])
    for m in range(n_steps):
        if m + ahead < n_steps:
            scores(m + ahead, bufs[(m + ahead) % depth])
        finish(m, bufs[m % depth])
        for src_ref, dst_ref, r0 in pieces[m * len(pieces) // n_steps:(m + 1) * len(pieces) // n_steps]:
            dst_ref[r0:r0 + BF16_ROWS, :] = src_ref[r0:r0 + BF16_ROWS, :].astype(BF16)


def _attention(proj_a, bias_vec, attn_gain, cast_weights):
    B, T, W3 = proj_a.shape
    W = W3 // 3
    n_pairs = W // LANES
    n_grid = n_pairs * B
    cast_specs = []
    for w in cast_weights:
        rows = w.shape[0] // n_grid
        assert w.shape[0] % n_grid == 0 and rows % BF16_ROWS == 0
        cast_specs.append(pl.BlockSpec((rows, w.shape[1]), lambda h, b: (h * B + b, 0)))
    outs = pl.pallas_call(
        functools.partial(_attn_kernel, n_chunks=T // CHUNK, n_cast=len(cast_weights)),
        out_shape=(jax.ShapeDtypeStruct((B, T, W), BF16),
                   *[jax.ShapeDtypeStruct(w.shape, BF16) for w in cast_weights]),
        grid=(n_pairs, B),
        in_specs=[pl.BlockSpec((1, T, LANES), lambda h, b: (b, 0, h)),
                  pl.BlockSpec((1, T, LANES), lambda h, b: (b, 0, n_pairs + h)),
                  pl.BlockSpec((1, T, LANES), lambda h, b: (b, 0, 2 * n_pairs + h)),
                  pl.BlockSpec((1, 2, BIAS_W), lambda h, b: (h, 0, 0)),
                  pl.BlockSpec((1, 1, LANES), lambda h, b: (h, 0, 0)),
                  *cast_specs],
        out_specs=(pl.BlockSpec((1, T, LANES), lambda h, b: (b, 0, h)), *cast_specs),
        scratch_shapes=[pltpu.VMEM((LANES, T), BF16),
                        pltpu.VMEM((LANES, T), BF16),
                        pltpu.VMEM((T, 2 * LANES), BF16),
                        pltpu.VMEM((T, 2 * LANES), BF16),
                        pltpu.VMEM((2, QBLK, KBLK), BF16),
                        pltpu.VMEM((QBLK, 2 * KBLK), F32),
                        pltpu.VMEM((QBLK, 2 * KBLK), F32),
                        pltpu.VMEM((QBLK, 2 * KBLK), F32)],
        compiler_params=_cparams(("parallel", "arbitrary"), 48),
        name="chunk_attention",
    )(proj_a, proj_a, proj_a, bias_vec, attn_gain, *cast_weights)
    return outs[0], outs[1:]


HG_CHUNK = 256
HG_LEVELS = (128, 64, 32, 16, 8, 4, 2, 1)
assert HG_LEVELS[-1] == 1


def _hgrn_kernel(q_ref, f_ref, i_ref, g_ref, lbl_ref, gn_ref, o_ref, *, n_steps):
    C = HG_CHUNK
    H2 = C // 2
    Dk = REC_HEAD_DIM
    nt = (((1,), (1,)), ((), ()))
    lbl = lbl_ref[0]
    e = jnp.exp(lbl - jnp.max(lbl, axis=0, keepdims=True))
    lb = e[0:1, :] / jnp.sum(e, axis=0, keepdims=True)
    c1 = 0.5 * (1.0 - lb)
    gn = gn_ref[...]

    r = lax.broadcasted_iota(jnp.int32, (C, C), 0)
    s = lax.broadcasted_iota(jnp.int32, (C, C), 1)
    tril = (s <= r).astype(BF16)
    rh = lax.broadcasted_iota(jnp.int32, (H2, H2), 0)
    sh = lax.broadcasted_iota(jnp.int32, (H2, H2), 1)
    lvl_mask = {m: ((rh // (2 * m)) == (sh // (2 * m))) & (((rh // m) % 2) == 1) & (((sh // m) % 2) == 0)
                for m in HG_LEVELS[1:]}
    sub = lax.broadcasted_iota(jnp.int32, (C // SUBLANES, SUBLANES, Dk), 1)

    def roll8(x, d):
        return pltpu.roll(x.reshape(C // SUBLANES, SUBLANES, Dk), d, 1)

    def front(n):
        rows = slice(n * C, (n + 1) * C)
        c1t = c1 * jnp.tanh(0.5 * f_ref[0, rows, :])
        f = (1.0 - c1) + c1t
        kk = c1 - c1t
        qq = _silu_tanh(q_ref[0, rows, :])
        ii = i_ref[0, rows, :]
        hi, mid, lo = _split3(jnp.log2(f))
        bb = jnp.dot(tril, jnp.concatenate([hi, mid, lo], axis=1), preferred_element_type=F32)
        b = bb[:, 0:Dk] + bb[:, Dk:2 * Dk] + bb[:, 2 * Dk:3 * Dk]
        return dict(rows=rows, kk=kk, qq=qq, ii=ii, ii_bf=ii.astype(BF16), b=b)

    def level_z(v, m):
        b, kk, qq = v["b"], v["kk"], v["qq"]
        if m >= SUBLANES:
            parts, srcs = [], []
            for p in range(0, C, 2 * m):
                bm = b[p + m - 1:p + m, :]
                parts += [bm - b[p:p + m], b[p + m:p + 2 * m] - bm]
                srcs += [kk[p:p + m], qq[p + m:p + 2 * m]]
            arg = jnp.concatenate(parts, axis=0)
            src = jnp.concatenate(srcs, axis=0)
        else:
            b3 = b.reshape(C // SUBLANES, SUBLANES, Dk)
            if m == 1:
                bm = jnp.where(sub % 2 == 1, roll8(b, 1), b3)
            else:
                bm = jnp.broadcast_to(b3[:, m - 1:m, :], b3.shape)
                for p in range(2 * m, SUBLANES, 2 * m):
                    bm = jnp.where(sub >= p, jnp.broadcast_to(b3[:, p + m - 1:p + m, :], b3.shape), bm)
            upper = (sub // m) % 2 == 1
            arg = ((b3 - bm) * jnp.where(upper, 1.0, -1.0)).reshape(C, Dk)
            src = jnp.where(upper, qq.reshape(b3.shape), kk.reshape(b3.shape)).reshape(C, Dk)
        return (src * jnp.exp2(arg)).astype(BF16)

    vs = [front(j) for j in range(n_steps)]
    a_lo = [None] * n_steps
    a_d0 = [jnp.zeros((H2, H2), F32)] * n_steps
    a_d1 = [jnp.zeros((H2, H2), F32)] * n_steps
    for li, m in enumerate(HG_LEVELS):
        for j, v in enumerate(vs):
            z = level_z(v, m)
            if li == 0:
                a_lo[j] = lax.dot_general(z[H2:], z[:H2], nt, preferred_element_type=F32)
            else:
                g0 = lax.dot_general(z[:H2], z[:H2], nt, preferred_element_type=F32)
                g1 = lax.dot_general(z[H2:], z[H2:], nt, preferred_element_type=F32)
                a_d0[j] = jnp.where(lvl_mask[m], g0, a_d0[j])
                a_d1[j] = jnp.where(lvl_mask[m], g1, a_d1[j])
    intra = []
    for j, v in enumerate(vs):
        o_top = jnp.dot(a_d0[j].astype(BF16), v["ii_bf"][:H2], preferred_element_type=F32)
        o_bot = jnp.dot(jnp.concatenate([a_lo[j], a_d1[j]], axis=1).astype(BF16), v["ii_bf"],
                        preferred_element_type=F32)
        o_diag = jnp.sum(v["qq"] * v["kk"], axis=-1, keepdims=True) * v["ii"]
        intra.append(jnp.concatenate([o_top, o_bot], axis=0) + o_diag)
    st = jnp.zeros((Dk, Dk), F32)
    for j, v in enumerate(vs):
        b = v["b"]
        b_last = b[C - 1:C, :]
        qe = (v["qq"] * jnp.exp2(b)).astype(BF16)
        o = intra[j] + lax.dot_general(qe, st.astype(BF16), nt, preferred_element_type=F32)
        ke = (v["kk"] * jnp.exp2(b_last - b)).astype(BF16)
        st = st * jnp.exp2(b_last) + lax.dot_general(
            v["ii_bf"], ke, (((0,), (0,)), ((), ())), preferred_element_type=F32)
        ms = jnp.mean(o * o, axis=-1, keepdims=True)
        y = o * lax.rsqrt(ms + EPS) * gn
        y = y * _silu_tanh(g_ref[0, v["rows"], :])
        o_ref[0, v["rows"], :] = y.astype(o_ref.dtype)


def _hgrn(proj_b, lb_logits_h, gnorm_g):
    B, T, W4 = proj_b.shape
    W = W4 // 4
    H = W // REC_HEAD_DIM
    n_slots = lb_logits_h.shape[1]
    blk = lambda off: pl.BlockSpec((1, T, REC_HEAD_DIM), lambda b, h, off=off: (b, 0, off * H + h))
    return pl.pallas_call(
        functools.partial(_hgrn_kernel, n_steps=T // HG_CHUNK),
        out_shape=jax.ShapeDtypeStruct((B, T, W), BF16),
        grid=(B, H),
        in_specs=[blk(0), blk(1), blk(2), blk(3),
                  pl.BlockSpec((1, n_slots, REC_HEAD_DIM), lambda b, h: (h, 0, 0)),
                  pl.BlockSpec((1, REC_HEAD_DIM), lambda b, h: (0, 0))],
        out_specs=pl.BlockSpec((1, T, REC_HEAD_DIM), lambda b, h: (b, 0, h)),
        compiler_params=_cparams(("parallel", "parallel"), 32),
        name="hgrn2",
    )(proj_b, proj_b, proj_b, proj_b, lb_logits_h, gnorm_g)


def _outproj_kernel(oa_ref, ob_ref, w_ref, x_ref, mod_ref, g_ref, b_ref, x1_ref, *, alpha, sub):
    wa = oa_ref.shape[-1]
    gate1 = _mod_row(mod_ref, pl.program_id(0), 2)
    for r0 in range(0, x_ref.shape[1], sub):
        rs = slice(r0, r0 + sub)
        mix = jnp.dot(oa_ref[0, rs, :], w_ref[0:wa, :], preferred_element_type=F32)
        mix = mix + jnp.dot(ob_ref[0, rs, :], w_ref[wa:, :], preferred_element_type=F32)
        x1_ref[0, rs, :] = _ln_rows(alpha * x_ref[0, rs, :] + gate1 * mix) * g_ref[...] + b_ref[...]


def _outproj(o_a, o_b, w_o, x, mod, ln_g, ln_b, alpha, tm=512, sub=128):
    B, T, D = x.shape
    Wa, Wb = o_a.shape[-1], o_b.shape[-1]
    return pl.pallas_call(
        functools.partial(_outproj_kernel, alpha=alpha, sub=sub),
        out_shape=jax.ShapeDtypeStruct((B, T, D), F32),
        grid=(B, T // tm),
        in_specs=[pl.BlockSpec((1, tm, Wa), lambda b, i: (b, i, 0)),
                  pl.BlockSpec((1, tm, Wb), lambda b, i: (b, i, 0)),
                  pl.BlockSpec((Wa + Wb, D), lambda b, i: (0, 0)),
                  pl.BlockSpec((1, tm, D), lambda b, i: (b, i, 0)),
                  pl.BlockSpec((B, N_MOD * D), lambda b, i: (0, 0)),
                  pl.BlockSpec((1, D), lambda b, i: (0, 0)),
                  pl.BlockSpec((1, D), lambda b, i: (0, 0))],
        out_specs=pl.BlockSpec((1, tm, D), lambda b, i: (b, i, 0)),
        compiler_params=_cparams(("parallel", "parallel"), 48),
        name="out_proj_ln1",
    )(o_a, o_b, w_o, x, mod, ln_g, ln_b)


def _ffn_kernel(wg_ref, wu_ref, wo_ref, x_ref, mod_ref, g_ref, b_ref, o_ref, h_sc, *, alpha, sub):
    f = pl.program_id(2)
    last = pl.num_programs(2) - 1

    def partial_out(rs):
        h = h_sc[rs, :]
        gate = jnp.dot(h, wg_ref[...], preferred_element_type=F32)
        up = jnp.dot(h, wu_ref[...], preferred_element_type=F32)
        act = (_silu_tanh(gate) * up).astype(BF16)
        return jnp.dot(act, wo_ref[...], preferred_element_type=F32)

    everything = slice(0, o_ref.shape[1])

    @pl.when(f == 0)
    def _():
        shift2 = _mod_row(mod_ref, pl.program_id(0), 3)
        scale2 = _mod_row(mod_ref, pl.program_id(0), 4)
        for r0 in range(0, o_ref.shape[1], sub):
            rs = slice(r0, r0 + sub)
            h_sc[rs, :] = (_ln_rows(x_ref[0, rs, :]) * (1.0 + scale2) + shift2).astype(BF16)
        o_ref[0] = partial_out(everything)

    @pl.when((f > 0) & (f < last))
    def _():
        o_ref[0] += partial_out(everything)

    @pl.when(f == last)
    def _():
        gate2 = _mod_row(mod_ref, pl.program_id(0), 5)
        for r0 in range(0, o_ref.shape[1], sub):
            rs = slice(r0, r0 + sub)
            y = o_ref[0, rs, :] + partial_out(rs)
            o_ref[0, rs, :] = _ln_rows(alpha * x_ref[0, rs, :] + gate2 * y) * g_ref[...] + b_ref[...]


def _ffn(w_in, w_out, x1, mod, ln_g, ln_b, alpha, tm=1024, tf=512, sub=256):
    B, T, D = x1.shape
    F = w_out.shape[0]
    nf = F // tf
    return pl.pallas_call(
        functools.partial(_ffn_kernel, alpha=alpha, sub=sub),
        out_shape=jax.ShapeDtypeStruct((B, T, D), F32),
        grid=(B, T // tm, nf),
        in_specs=[pl.BlockSpec((D, tf), lambda b, i, f: (0, f)),
                  pl.BlockSpec((D, tf), lambda b, i, f: (0, nf + f)),
                  pl.BlockSpec((tf, D), lambda b, i, f: (f, 0)),
                  pl.BlockSpec((1, tm, D), lambda b, i, f: (b, i, 0)),
                  pl.BlockSpec((B, N_MOD * D), lambda b, i, f: (0, 0)),
                  pl.BlockSpec((1, D), lambda b, i, f: (0, 0)),
                  pl.BlockSpec((1, D), lambda b, i, f: (0, 0))],
        out_specs=pl.BlockSpec((1, tm, D), lambda b, i, f: (b, i, 0)),
        scratch_shapes=[pltpu.VMEM((tm, D), BF16)],
        compiler_params=_cparams(("parallel", "parallel", "arbitrary"), 60),
        name="swiglu_ffn_ln2",
    )(w_in, w_in, w_out, x1, mod, ln_g, ln_b)


def _bias_vectors(rel_bias):
    H, n_rel = rel_bias.shape
    max_rel = (n_rel - 1) // 2
    u = jnp.arange(BIAS_W)
    idx = jnp.clip(KBLK - u, -max_rel, max_rel) + max_rel
    return rel_bias[:, idx]


def kernel(x, c, w_ada, b_ada, w_in, rel_bias, attn_norm_g, lb_logits, gnorm_g, w_o,
           ln1_g, ln1_b, w_ffn_in, w_ffn_out, ln2_g, ln2_b):
    B, T, D = x.shape
    depth = w_ada.shape[0]
    alpha = (2 * depth) ** 0.25
    attn_w = attn_norm_g.shape[1]
    rec_w = lb_logits.shape[1]
    n_slots = lb_logits.shape[0]
    rec_heads = rec_w // REC_HEAD_DIM
    assert depth == 1 and n_slots == depth + 1
    for layer in range(depth):
        mod = _mod(c, w_ada[layer], b_ada[layer])
        h1 = _ln_mod(x, mod, 0).reshape(B * T, D)
        q_scale = jnp.where(jnp.arange(w_in.shape[2]) < attn_w, ATTN_HEAD_DIM ** -0.5 * LOG2E, 1.0)
        q_scale = q_scale.astype(F32).reshape(1, -1)
        proj_a = _matmul(h1, w_in[layer], q_scale, 0, 3 * attn_w, BF16).reshape(B, T, 3 * attn_w)
        proj_b = _matmul(h1, w_in[layer], q_scale, 3 * attn_w, 4 * rec_w, F32).reshape(B, T, 4 * rec_w)
        bias_vec = (_bias_vectors(rel_bias[layer]) * LOG2E).reshape(-1, 2, BIAS_W)
        o_a, (w_o_bf, w_ffn_in_bf, w_ffn_out_bf) = _attention(
            proj_a, bias_vec, attn_norm_g[layer].reshape(-1, 1, LANES),
            (w_o[layer], w_ffn_in[layer], w_ffn_out[layer]))
        lbl = lb_logits.reshape(n_slots, rec_heads, REC_HEAD_DIM).transpose(1, 0, 2)
        o_b = _hgrn(proj_b, lbl, gnorm_g[layer].reshape(1, REC_HEAD_DIM))
        x = _outproj(o_a, o_b, w_o_bf, x, mod,
                     ln1_g[layer].reshape(1, D), ln1_b[layer].reshape(1, D), alpha)
        x = _ffn(w_ffn_in_bf, w_ffn_out_bf, x, mod,
                 ln2_g[layer].reshape(1, D), ln2_b[layer].reshape(1, D), alpha)
    return x
```

```python
import functools

import jax
import jax.numpy as jnp
from jax import lax
from jax.experimental import pallas as pl
from jax.experimental.pallas import tpu as pltpu

F32 = jnp.float32
BF16 = jnp.bfloat16

CHUNK = 64
N_PAST_CHUNKS = 8
BAND = (N_PAST_CHUNKS + 1) * CHUNK
ATTN_HEAD_DIM = 64
REC_HEAD_DIM = 128
N_MOD = 6
EPS = 1e-5
LANES = 128
SUBLANES = 8
BF16_ROWS = 16
QBLK = 2 * CHUNK
KBLK = BAND + CHUNK
BIAS_W = KBLK + QBLK

MIB = 1024 * 1024


def _cparams(sem, vmem_mib):
    return pltpu.CompilerParams(dimension_semantics=sem, vmem_limit_bytes=vmem_mib * MIB)


def _silu_tanh(x):
    h = 0.5 * x
    return h + h * jnp.tanh(h)


def _mod_row(mod_ref, b, r):
    d = mod_ref.shape[1] // N_MOD
    return mod_ref[pl.ds(b, 1), r * d:(r + 1) * d]


def _ln_rows(x):
    mu = jnp.mean(x, axis=-1, keepdims=True)
    xc = x - mu
    var = jnp.mean(xc * xc, axis=-1, keepdims=True)
    return xc * lax.rsqrt(var + EPS)


def _split3(x):
    hi = x.astype(BF16)
    r1 = x - hi.astype(F32)
    mid = r1.astype(BF16)
    lo = (r1 - mid.astype(F32)).astype(BF16)
    return hi, mid, lo


def _mod_kernel(c_ref, w_ref, b_ref, o_ref):
    act = jnp.concatenate(_split3(_silu_tanh(c_ref[...])), axis=0)
    w = w_ref[...]
    w_hi = w.astype(BF16)
    w_mid = (w - w_hi.astype(F32)).astype(BF16)
    acc = jnp.dot(act, w_hi, preferred_element_type=F32) + jnp.dot(act, w_mid, preferred_element_type=F32)
    out = acc[0:SUBLANES] + acc[SUBLANES:2 * SUBLANES] + acc[2 * SUBLANES:3 * SUBLANES]
    o_ref[...] = out[0:o_ref.shape[0]] + b_ref[...]


def _mod(c, w_ada, b_ada, tn=1024):
    B, D = c.shape
    N = w_ada.shape[1]
    assert B <= SUBLANES
    return pl.pallas_call(
        _mod_kernel,
        out_shape=jax.ShapeDtypeStruct((B, N), F32),
        grid=(N // tn,),
        in_specs=[pl.BlockSpec((SUBLANES, D), lambda j: (0, 0)),
                  pl.BlockSpec((D, tn), lambda j: (0, j)),
                  pl.BlockSpec((1, tn), lambda j: (0, j))],
        out_specs=pl.BlockSpec((B, tn), lambda j: (0, j)),
        compiler_params=_cparams(("parallel",), 40),
        name="adaln_mod",
    )(jnp.pad(c, ((0, SUBLANES - B), (0, 0))), w_ada, b_ada.reshape(1, N))


def _ln_mod_kernel(x_ref, mod_ref, o_ref, *, shift_row):
    y = _ln_rows(x_ref[0])
    b = pl.program_id(0)
    shift = _mod_row(mod_ref, b, shift_row)
    scale = _mod_row(mod_ref, b, shift_row + 1)
    o_ref[0] = (y * (1.0 + scale) + shift).astype(o_ref.dtype)


def _ln_mod(x, mod, shift_row, tm=1024):
    B, T, D = x.shape
    return pl.pallas_call(
        functools.partial(_ln_mod_kernel, shift_row=shift_row),
        out_shape=jax.ShapeDtypeStruct((B, T, D), BF16),
        grid=(B, T // tm),
        in_specs=[pl.BlockSpec((1, tm, D), lambda b, i: (b, i, 0)),
                  pl.BlockSpec((B, N_MOD * D), lambda b, i: (0, 0))],
        out_specs=pl.BlockSpec((1, tm, D), lambda b, i: (b, i, 0)),
        compiler_params=_cparams(("parallel", "parallel"), 32),
        name="ln_modulate",
    )(x, mod)


def _matmul_kernel(a_ref, w_ref, s_ref, o_ref, w_bf):
    @pl.when(pl.program_id(1) == 0)
    def _():
        w_bf[...] = (w_ref[...] * s_ref[...]).astype(BF16)

    o_ref[...] = jnp.dot(a_ref[...], w_bf[...], preferred_element_type=F32).astype(o_ref.dtype)


def _matmul(a, w, col_scale, col0, n_out, out_dtype, tm=1024, tn=1024):
    M, K = a.shape
    assert col0 % tn == 0 and n_out % tn == 0 and M % tm == 0
    col_block0 = col0 // tn
    return pl.pallas_call(
        _matmul_kernel,
        out_shape=jax.ShapeDtypeStruct((M, n_out), out_dtype),
        grid=(n_out // tn, M // tm),
        in_specs=[pl.BlockSpec((tm, K), lambda j, i: (i, 0)),
                  pl.BlockSpec((K, tn), lambda j, i: (0, j + col_block0)),
                  pl.BlockSpec((1, tn), lambda j, i: (0, j + col_block0))],
        out_specs=pl.BlockSpec((tm, tn), lambda j, i: (i, j)),
        scratch_shapes=[pltpu.VMEM((K, tn), BF16)],
        compiler_params=_cparams(("parallel", "arbitrary"), 48),
        name="in_proj",
    )(a, w, col_scale)


def _cast_specs(weights, n_grid, step_of):
    specs = []
    for w in weights:
        rows = w.shape[0] // n_grid
        assert w.shape[0] % n_grid == 0 and rows % BF16_ROWS == 0
        specs.append(pl.BlockSpec((rows, w.shape[1]), lambda *g: (step_of(*g), 0)))
    return specs


def _cast_pieces(cast_in, cast_out):
    return [(src_ref, dst_ref, r0) for src_ref, dst_ref in zip(cast_in, cast_out)
            for r0 in range(0, src_ref.shape[0], BF16_ROWS)]


def _cast_some(pieces, m, n):
    for src_ref, dst_ref, r0 in pieces[m * len(pieces) // n:(m + 1) * len(pieces) // n]:
        dst_ref[r0:r0 + BF16_ROWS, :] = src_ref[r0:r0 + BF16_ROWS, :].astype(BF16)


NEG_BIG = -1e30
LOG2E = 1.4426950408889634


def _attn_kernel(q_ref, k_ref, v_ref, bias_ref, gain_ref, *rest, n_chunks, n_cast):
    cast_in = rest[:n_cast]
    o_ref = rest[n_cast]
    cast_out = rest[n_cast + 1:2 * n_cast + 1]
    kta, ktb, vpa, vpb, tab, s_a, s_b, s_c = rest[2 * n_cast + 1:]
    T = n_chunks * CHUNK
    n_steps = T // QBLK
    head0 = lax.broadcasted_iota(jnp.int32, (QBLK, LANES), 1) < ATTN_HEAD_DIM
    m0 = jnp.where(head0, 1.0, 0.0).astype(BF16)
    m1 = jnp.where(head0, 0.0, 1.0).astype(BF16)

    head0_t = lax.broadcasted_iota(jnp.int32, (LANES, QBLK), 0) < ATTN_HEAD_DIM
    mt0 = jnp.where(head0_t, 1.0, 0.0).astype(BF16)
    mt1 = jnp.where(head0_t, 0.0, 1.0).astype(BF16)

    def prep(blk):
        rows = slice(blk * QBLK, (blk + 1) * QBLK)
        kt = k_ref[0, rows, :].T
        v = v_ref[0, rows, :]
        kta[:, rows] = kt * mt0
        ktb[:, rows] = kt * mt1
        vpa[rows, 0:LANES] = v * m0
        vpb[rows, 0:LANES] = v * m1
        vpa[rows, LANES:2 * LANES] = m0
        vpb[rows, LANES:2 * LANES] = m1

    @pl.when(pl.program_id(1) == 0)
    def _():
        qry = lax.broadcasted_iota(jnp.int32, (QBLK, KBLK), 0)
        key = lax.broadcasted_iota(jnp.int32, (QBLK, KBLK), 1)
        in_band = ((qry < CHUNK) & (key < BAND)) | ((qry >= CHUNK) & (key >= CHUNK))
        for hh in range(2):
            g = jnp.broadcast_to(bias_ref[0, hh:hh + 1, :], (QBLK, BIAS_W))
            t = pltpu.roll(g, BIAS_W - QBLK, 1, stride=1, stride_axis=0)[:, :KBLK]
            tab[hh] = jnp.where(in_band, t, NEG_BIG).astype(BF16)

    eye = (lax.broadcasted_iota(jnp.int32, (QBLK, QBLK), 0)
           == lax.broadcasted_iota(jnp.int32, (QBLK, QBLK), 1)).astype(BF16)
    gain = gain_ref[0]

    def band_of(m):
        hi = (m + 1) * QBLK
        lo = max(0, hi - KBLK)
        return slice(lo, hi), KBLK - (hi - lo)

    def scores(m, dst):
        prep(m)
        band, col0 = band_of(m)
        lhs = jnp.concatenate([q_ref[0, m * QBLK:(m + 1) * QBLK, :], eye], axis=1)
        rhs = jnp.concatenate([jnp.concatenate([kta[:, band], ktb[:, band]], axis=1),
                               jnp.concatenate([tab[0, :, col0:], tab[1, :, col0:]], axis=1)], axis=0)
        dst[:, 0:rhs.shape[1]] = jnp.dot(lhs, rhs, preferred_element_type=F32)

    def finish(m, src):
        band, col0 = band_of(m)
        w = KBLK - col0
        s = src[:, 0:2 * w]
        p = jnp.concatenate(
            [jnp.exp2(sh - jnp.max(sh, axis=-1, keepdims=True)) for sh in (s[:, 0:w], s[:, w:2 * w])],
            axis=1).astype(BF16)
        pv = jnp.dot(p, jnp.concatenate([vpa[band, :], vpb[band, :]], axis=0), preferred_element_type=F32)
        o = pv[:, 0:LANES] * (1.0 / pv[:, LANES:2 * LANES])
        o2 = o * o
        ms0 = jnp.sum(jnp.where(head0, o2, 0.0), axis=-1, keepdims=True) / ATTN_HEAD_DIM
        ms1 = jnp.sum(jnp.where(head0, 0.0, o2), axis=-1, keepdims=True) / ATTN_HEAD_DIM
        y = o * lax.rsqrt(jnp.where(head0, ms0, ms1) + EPS) * gain
        o_ref[0, m * QBLK:(m + 1) * QBLK, :] = y.astype(o_ref.dtype)

    pieces = _cast_pieces(cast_in, cast_out)
    bufs = (s_a, s_b, s_c)
    depth = len(bufs)
    ahead = depth - 1
    for m in range(ahead):
        scores(m, bufs[m % depth])
    for m in range(n_steps):
        if m + ahead < n_steps:
            scores(m + ahead, bufs[(m + ahead) % depth])
        finish(m, bufs[m % depth])
        _cast_some(pieces, m, n_steps)


def _attention(proj_a, bias_vec, attn_gain, cast_weights):
    B, T, W3 = proj_a.shape
    W = W3 // 3
    n_pairs = W // LANES
    n_grid = n_pairs * B
    cast_specs = _cast_specs(cast_weights, n_grid, lambda h, b: h * B + b)
    outs = pl.pallas_call(
        functools.partial(_attn_kernel, n_chunks=T // CHUNK, n_cast=len(cast_weights)),
        out_shape=(jax.ShapeDtypeStruct((B, T, W), BF16),
                   *[jax.ShapeDtypeStruct(w.shape, BF16) for w in cast_weights]),
        grid=(n_pairs, B),
        in_specs=[pl.BlockSpec((1, T, LANES), lambda h, b: (b, 0, h)),
                  pl.BlockSpec((1, T, LANES), lambda h, b: (b, 0, n_pairs + h)),
                  pl.BlockSpec((1, T, LANES), lambda h, b: (b, 0, 2 * n_pairs + h)),
                  pl.BlockSpec((1, 2, BIAS_W), lambda h, b: (h, 0, 0)),
                  pl.BlockSpec((1, 1, LANES), lambda h, b: (h, 0, 0)),
                  *cast_specs],
        out_specs=(pl.BlockSpec((1, T, LANES), lambda h, b: (b, 0, h)), *cast_specs),
        scratch_shapes=[pltpu.VMEM((LANES, T), BF16),
                        pltpu.VMEM((LANES, T), BF16),
                        pltpu.VMEM((T, 2 * LANES), BF16),
                        pltpu.VMEM((T, 2 * LANES), BF16),
                        pltpu.VMEM((2, QBLK, KBLK), BF16),
                        pltpu.VMEM((QBLK, 2 * KBLK), F32),
                        pltpu.VMEM((QBLK, 2 * KBLK), F32),
                        pltpu.VMEM((QBLK, 2 * KBLK), F32)],
        compiler_params=_cparams(("parallel", "arbitrary"), 48),
        name="chunk_attention",
    )(proj_a, proj_a, proj_a, bias_vec, attn_gain, *cast_weights)
    return outs[0], outs[1:]


HG_CHUNK = 256
HG_LEVELS = (128, 64, 32, 16, 8, 4, 2, 1)
assert HG_LEVELS[-1] == 1


def _hgrn_kernel(q_ref, f_ref, i_ref, g_ref, lbl_ref, gn_ref, *rest, n_steps, n_cast):
    cast_in, o_ref, cast_out = rest[:n_cast], rest[n_cast], rest[n_cast + 1:]
    pieces = _cast_pieces(cast_in, cast_out)
    C = HG_CHUNK
    H2 = C // 2
    Dk = REC_HEAD_DIM
    nt = (((1,), (1,)), ((), ()))
    lbl = lbl_ref[0]
    e = jnp.exp(lbl - jnp.max(lbl, axis=0, keepdims=True))
    lb = e[0:1, :] / jnp.sum(e, axis=0, keepdims=True)
    c1 = 0.5 * (1.0 - lb)
    gn = gn_ref[...]

    r = lax.broadcasted_iota(jnp.int32, (C, C), 0)
    s = lax.broadcasted_iota(jnp.int32, (C, C), 1)
    tril = (s <= r).astype(BF16)
    rh = lax.broadcasted_iota(jnp.int32, (H2, H2), 0)
    sh = lax.broadcasted_iota(jnp.int32, (H2, H2), 1)
    lvl_mask = {m: ((rh // (2 * m)) == (sh // (2 * m))) & (((rh // m) % 2) == 1) & (((sh // m) % 2) == 0)
                for m in HG_LEVELS[1:]}
    sub = lax.broadcasted_iota(jnp.int32, (C // SUBLANES, SUBLANES, Dk), 1)

    def roll8(x, d):
        return pltpu.roll(x.reshape(C // SUBLANES, SUBLANES, Dk), d, 1)

    def front(n):
        rows = slice(n * C, (n + 1) * C)
        c1t = c1 * jnp.tanh(0.5 * f_ref[0, rows, :])
        f = (1.0 - c1) + c1t
        kk = c1 - c1t
        qq = _silu_tanh(q_ref[0, rows, :])
        ii = i_ref[0, rows, :]
        hi, mid, lo = _split3(jnp.log2(f))
        bb = jnp.dot(tril, jnp.concatenate([hi, mid, lo], axis=1), preferred_element_type=F32)
        b = bb[:, 0:Dk] + bb[:, Dk:2 * Dk] + bb[:, 2 * Dk:3 * Dk]
        return dict(rows=rows, kk=kk, qq=qq, ii=ii, ii_bf=ii.astype(BF16), b=b)

    def level_z(v, m):
        b, kk, qq = v["b"], v["kk"], v["qq"]
        if m >= SUBLANES:
            parts, srcs = [], []
            for p in range(0, C, 2 * m):
                bm = b[p + m - 1:p + m, :]
                parts += [bm - b[p:p + m], b[p + m:p + 2 * m] - bm]
                srcs += [kk[p:p + m], qq[p + m:p + 2 * m]]
            arg = jnp.concatenate(parts, axis=0)
            src = jnp.concatenate(srcs, axis=0)
        else:
            b3 = b.reshape(C // SUBLANES, SUBLANES, Dk)
            if m == 1:
                bm = jnp.where(sub % 2 == 1, roll8(b, 1), b3)
            else:
                bm = jnp.broadcast_to(b3[:, m - 1:m, :], b3.shape)
                for p in range(2 * m, SUBLANES, 2 * m):
                    bm = jnp.where(sub >= p, jnp.broadcast_to(b3[:, p + m - 1:p + m, :], b3.shape), bm)
            upper = (sub // m) % 2 == 1
            arg = ((b3 - bm) * jnp.where(upper, 1.0, -1.0)).reshape(C, Dk)
            src = jnp.where(upper, qq.reshape(b3.shape), kk.reshape(b3.shape)).reshape(C, Dk)
        return (src * jnp.exp2(arg)).astype(BF16)

    vs = [front(j) for j in range(n_steps)]
    a_lo = [None] * n_steps
    a_d0 = [jnp.zeros((H2, H2), F32)] * n_steps
    a_d1 = [jnp.zeros((H2, H2), F32)] * n_steps
    for li, m in enumerate(HG_LEVELS):
        for j, v in enumerate(vs):
            z = level_z(v, m)
            if li == 0:
                a_lo[j] = lax.dot_general(z[H2:], z[:H2], nt, preferred_element_type=F32)
            else:
                g0 = lax.dot_general(z[:H2], z[:H2], nt, preferred_element_type=F32)
                g1 = lax.dot_general(z[H2:], z[H2:], nt, preferred_element_type=F32)
                a_d0[j] = jnp.where(lvl_mask[m], g0, a_d0[j])
                a_d1[j] = jnp.where(lvl_mask[m], g1, a_d1[j])
    intra = []
    for j, v in enumerate(vs):
        o_top = jnp.dot(a_d0[j].astype(BF16), v["ii_bf"][:H2], preferred_element_type=F32)
        o_bot = jnp.dot(jnp.concatenate([a_lo[j], a_d1[j]], axis=1).astype(BF16), v["ii_bf"],
                        preferred_element_type=F32)
        o_diag = jnp.sum(v["qq"] * v["kk"], axis=-1, keepdims=True) * v["ii"]
        intra.append(jnp.concatenate([o_top, o_bot], axis=0) + o_diag)
    st = jnp.zeros((Dk, Dk), F32)
    for j, v in enumerate(vs):
        b = v["b"]
        b_last = b[C - 1:C, :]
        qe = (v["qq"] * jnp.exp2(b)).astype(BF16)
        o = intra[j] + lax.dot_general(qe, st.astype(BF16), nt, preferred_element_type=F32)
        ke = (v["kk"] * jnp.exp2(b_last - b)).astype(BF16)
        st = st * jnp.exp2(b_last) + lax.dot_general(
            v["ii_bf"], ke, (((0,), (0,)), ((), ())), preferred_element_type=F32)
        ms = jnp.mean(o * o, axis=-1, keepdims=True)
        y = o * lax.rsqrt(ms + EPS) * gn
        y = y * _silu_tanh(g_ref[0, v["rows"], :])
        o_ref[0, v["rows"], :] = y.astype(o_ref.dtype)
        _cast_some(pieces, j, n_steps)


def _hgrn(proj_b, lb_logits_h, gnorm_g, cast_weights):
    B, T, W4 = proj_b.shape
    W = W4 // 4
    H = W // REC_HEAD_DIM
    n_slots = lb_logits_h.shape[1]
    blk = lambda off: pl.BlockSpec((1, T, REC_HEAD_DIM), lambda b, h, off=off: (b, 0, off * H + h))
    cast_specs = _cast_specs(cast_weights, B * H, lambda b, h: b * H + h)
    outs = pl.pallas_call(
        functools.partial(_hgrn_kernel, n_steps=T // HG_CHUNK, n_cast=len(cast_weights)),
        out_shape=(jax.ShapeDtypeStruct((B, T, W), BF16),
                   *[jax.ShapeDtypeStruct(w.shape, BF16) for w in cast_weights]),
        grid=(B, H),
        in_specs=[blk(0), blk(1), blk(2), blk(3),
                  pl.BlockSpec((1, n_slots, REC_HEAD_DIM), lambda b, h: (h, 0, 0)),
                  pl.BlockSpec((1, REC_HEAD_DIM), lambda b, h: (0, 0)),
                  *cast_specs],
        out_specs=(pl.BlockSpec((1, T, REC_HEAD_DIM), lambda b, h: (b, 0, h)), *cast_specs),
        compiler_params=_cparams(("parallel", "parallel"), 40),
        name="hgrn2",
    )(proj_b, proj_b, proj_b, proj_b, lb_logits_h, gnorm_g, *cast_weights)
    return outs[0], outs[1:]


def _outproj_kernel(oa_ref, ob_ref, w_ref, x_ref, mod_ref, g_ref, b_ref, x1_ref, *, alpha, sub):
    wa = oa_ref.shape[-1]
    gate1 = _mod_row(mod_ref, pl.program_id(0), 2)
    for r0 in range(0, x_ref.shape[1], sub):
        rs = slice(r0, r0 + sub)
        mix = jnp.dot(oa_ref[0, rs, :], w_ref[0:wa, :], preferred_element_type=F32)
        mix = mix + jnp.dot(ob_ref[0, rs, :], w_ref[wa:, :], preferred_element_type=F32)
        x1_ref[0, rs, :] = _ln_rows(alpha * x_ref[0, rs, :] + gate1 * mix) * g_ref[...] + b_ref[...]


def _outproj(o_a, o_b, w_o, x, mod, ln_g, ln_b, alpha, tm=512, sub=128):
    B, T, D = x.shape
    Wa, Wb = o_a.shape[-1], o_b.shape[-1]
    return pl.pallas_call(
        functools.partial(_outproj_kernel, alpha=alpha, sub=sub),
        out_shape=jax.ShapeDtypeStruct((B, T, D), F32),
        grid=(B, T // tm),
        in_specs=[pl.BlockSpec((1, tm, Wa), lambda b, i: (b, i, 0)),
                  pl.BlockSpec((1, tm, Wb), lambda b, i: (b, i, 0)),
                  pl.BlockSpec((Wa + Wb, D), lambda b, i: (0, 0)),
                  pl.BlockSpec((1, tm, D), lambda b, i: (b, i, 0)),
                  pl.BlockSpec((B, N_MOD * D), lambda b, i: (0, 0)),
                  pl.BlockSpec((1, D), lambda b, i: (0, 0)),
                  pl.BlockSpec((1, D), lambda b, i: (0, 0))],
        out_specs=pl.BlockSpec((1, tm, D), lambda b, i: (b, i, 0)),
        compiler_params=_cparams(("parallel", "parallel"), 48),
        name="out_proj_ln1",
    )(o_a, o_b, w_o, x, mod, ln_g, ln_b)


def _ffn_kernel(wg_ref, wu_ref, wo_ref, x_ref, mod_ref, g_ref, b_ref, o_ref, h_sc, *, alpha, sub):
    f = pl.program_id(2)
    last = pl.num_programs(2) - 1

    def partial_out(rs):
        h = h_sc[rs, :]
        gate = jnp.dot(h, wg_ref[...], preferred_element_type=F32)
        up = jnp.dot(h, wu_ref[...], preferred_element_type=F32)
        act = (_silu_tanh(gate) * up).astype(BF16)
        return jnp.dot(act, wo_ref[...], preferred_element_type=F32)

    everything = slice(0, o_ref.shape[1])

    @pl.when(f == 0)
    def _():
        shift2 = _mod_row(mod_ref, pl.program_id(0), 3)
        scale2 = _mod_row(mod_ref, pl.program_id(0), 4)
        for r0 in range(0, o_ref.shape[1], sub):
            rs = slice(r0, r0 + sub)
            h_sc[rs, :] = (_ln_rows(x_ref[0, rs, :]) * (1.0 + scale2) + shift2).astype(BF16)
        o_ref[0] = partial_out(everything)

    @pl.when((f > 0) & (f < last))
    def _():
        o_ref[0] += partial_out(everything)

    @pl.when(f == last)
    def _():
        gate2 = _mod_row(mod_ref, pl.program_id(0), 5)
        for r0 in range(0, o_ref.shape[1], sub):
            rs = slice(r0, r0 + sub)
            y = o_ref[0, rs, :] + partial_out(rs)
            o_ref[0, rs, :] = _ln_rows(alpha * x_ref[0, rs, :] + gate2 * y) * g_ref[...] + b_ref[...]


def _ffn(w_in, w_out, x1, mod, ln_g, ln_b, alpha, tm=1024, tf=512, sub=256):
    B, T, D = x1.shape
    F = w_out.shape[0]
    nf = F // tf
    return pl.pallas_call(
        functools.partial(_ffn_kernel, alpha=alpha, sub=sub),
        out_shape=jax.ShapeDtypeStruct((B, T, D), F32),
        grid=(B, T // tm, nf),
        in_specs=[pl.BlockSpec((D, tf), lambda b, i, f: (0, f)),
                  pl.BlockSpec((D, tf), lambda b, i, f: (0, nf + f)),
                  pl.BlockSpec((tf, D), lambda b, i, f: (f, 0)),
                  pl.BlockSpec((1, tm, D), lambda b, i, f: (b, i, 0)),
                  pl.BlockSpec((B, N_MOD * D), lambda b, i, f: (0, 0)),
                  pl.BlockSpec((1, D), lambda b, i, f: (0, 0)),
                  pl.BlockSpec((1, D), lambda b, i, f: (0, 0))],
        out_specs=pl.BlockSpec((1, tm, D), lambda b, i, f: (b, i, 0)),
        scratch_shapes=[pltpu.VMEM((tm, D), BF16)],
        compiler_params=_cparams(("parallel", "parallel", "arbitrary"), 60),
        name="swiglu_ffn_ln2",
    )(w_in, w_in, w_out, x1, mod, ln_g, ln_b)


def _bias_vectors(rel_bias):
    H, n_rel = rel_bias.shape
    max_rel = (n_rel - 1) // 2
    u = jnp.arange(BIAS_W)
    idx = jnp.clip(KBLK - u, -max_rel, max_rel) + max_rel
    return rel_bias[:, idx]


def kernel(x, c, w_ada, b_ada, w_in, rel_bias, attn_norm_g, lb_logits, gnorm_g, w_o,
           ln1_g, ln1_b, w_ffn_in, w_ffn_out, ln2_g, ln2_b):
    B, T, D = x.shape
    depth = w_ada.shape[0]
    alpha = (2 * depth) ** 0.25
    attn_w = attn_norm_g.shape[1]
    rec_w = lb_logits.shape[1]
    n_slots = lb_logits.shape[0]
    rec_heads = rec_w // REC_HEAD_DIM
    assert depth == 1 and n_slots == depth + 1
    for layer in range(depth):
        mod = _mod(c, w_ada[layer], b_ada[layer])
        h1 = _ln_mod(x, mod, 0).reshape(B * T, D)
        q_scale = jnp.where(jnp.arange(w_in.shape[2]) < attn_w, ATTN_HEAD_DIM ** -0.5 * LOG2E, 1.0)
        q_scale = q_scale.astype(F32).reshape(1, -1)
        proj_a = _matmul(h1, w_in[layer], q_scale, 0, 3 * attn_w, BF16).reshape(B, T, 3 * attn_w)
        proj_b = _matmul(h1, w_in[layer], q_scale, 3 * attn_w, 4 * rec_w, F32).reshape(B, T, 4 * rec_w)
        bias_vec = (_bias_vectors(rel_bias[layer]) * LOG2E).reshape(-1, 2, BIAS_W)
        o_a, (w_o_bf, w_ffn_out_bf) = _attention(
            proj_a, bias_vec, attn_norm_g[layer].reshape(-1, 1, LANES), (w_o[layer], w_ffn_out[layer]))
        lbl = lb_logits.reshape(n_slots, rec_heads, REC_HEAD_DIM).transpose(1, 0, 2)
        o_b, (w_ffn_in_bf,) = _hgrn(proj_b, lbl, gnorm_g[layer].reshape(1, REC_HEAD_DIM), (w_ffn_in[layer],))
        x = _outproj(o_a, o_b, w_o_bf, x, mod,
                     ln1_g[layer].reshape(1, D), ln1_b[layer].reshape(1, D), alpha)
        x = _ffn(w_ffn_in_bf, w_ffn_out_bf, x, mod,
                 ln2_g[layer].reshape(1, D), ln2_b[layer].reshape(1, D), alpha)
    return x
```

```python
import functools

import jax
import jax.numpy as jnp
from jax import lax
from jax.experimental import pallas as pl
from jax.experimental.pallas import tpu as pltpu

F32 = jnp.float32
BF16 = jnp.bfloat16

CHUNK = 64
N_PAST_CHUNKS = 8
BAND = (N_PAST_CHUNKS + 1) * CHUNK
ATTN_HEAD_DIM = 64
REC_HEAD_DIM = 128
N_MOD = 6
EPS = 1e-5
LANES = 128
SUBLANES = 8
BF16_ROWS = 16
QBLK = 2 * CHUNK
KBLK = BAND + CHUNK
BIAS_W = KBLK + QBLK

MIB = 1024 * 1024


def _cparams(sem, vmem_mib):
    return pltpu.CompilerParams(dimension_semantics=sem, vmem_limit_bytes=vmem_mib * MIB)


def _silu_tanh(x):
    h = 0.5 * x
    return h + h * jnp.tanh(h)


def _mod_row(mod_ref, b, r):
    d = mod_ref.shape[1] // N_MOD
    return mod_ref[pl.ds(b, 1), r * d:(r + 1) * d]


def _ln_rows(x):
    mu = jnp.mean(x, axis=-1, keepdims=True)
    xc = x - mu
    var = jnp.mean(xc * xc, axis=-1, keepdims=True)
    return xc * lax.rsqrt(var + EPS)


def _split3(x):
    hi = x.astype(BF16)
    r1 = x - hi.astype(F32)
    mid = r1.astype(BF16)
    lo = (r1 - mid.astype(F32)).astype(BF16)
    return hi, mid, lo


def _mod_kernel(c_ref, w_ref, b_ref, o_ref):
    act = jnp.concatenate(_split3(_silu_tanh(c_ref[...])), axis=0)
    w = w_ref[...]
    w_hi = w.astype(BF16)
    w_mid = (w - w_hi.astype(F32)).astype(BF16)
    acc = jnp.dot(act, w_hi, preferred_element_type=F32) + jnp.dot(act, w_mid, preferred_element_type=F32)
    out = acc[0:SUBLANES] + acc[SUBLANES:2 * SUBLANES] + acc[2 * SUBLANES:3 * SUBLANES]
    o_ref[...] = out[0:o_ref.shape[0]] + b_ref[...]


def _mod(c, w_ada, b_ada, tn=1024):
    B, D = c.shape
    N = w_ada.shape[1]
    assert B <= SUBLANES
    return pl.pallas_call(
        _mod_kernel,
        out_shape=jax.ShapeDtypeStruct((B, N), F32),
        grid=(N // tn,),
        in_specs=[pl.BlockSpec((SUBLANES, D), lambda j: (0, 0)),
                  pl.BlockSpec((D, tn), lambda j: (0, j)),
                  pl.BlockSpec((1, tn), lambda j: (0, j))],
        out_specs=pl.BlockSpec((B, tn), lambda j: (0, j)),
        compiler_params=_cparams(("parallel",), 40),
        name="adaln_mod",
    )(jnp.pad(c, ((0, SUBLANES - B), (0, 0))), w_ada, b_ada.reshape(1, N))


def _ln_mod_kernel(x_ref, mod_ref, o_ref, *, shift_row):
    y = _ln_rows(x_ref[0])
    b = pl.program_id(0)
    shift = _mod_row(mod_ref, b, shift_row)
    scale = _mod_row(mod_ref, b, shift_row + 1)
    o_ref[0] = (y * (1.0 + scale) + shift).astype(o_ref.dtype)


def _ln_mod(x, mod, shift_row, tm=1024):
    B, T, D = x.shape
    return pl.pallas_call(
        functools.partial(_ln_mod_kernel, shift_row=shift_row),
        out_shape=jax.ShapeDtypeStruct((B, T, D), BF16),
        grid=(B, T // tm),
        in_specs=[pl.BlockSpec((1, tm, D), lambda b, i: (b, i, 0)),
                  pl.BlockSpec((B, N_MOD * D), lambda b, i: (0, 0))],
        out_specs=pl.BlockSpec((1, tm, D), lambda b, i: (b, i, 0)),
        compiler_params=_cparams(("parallel", "parallel"), 32),
        name="ln_modulate",
    )(x, mod)


def _matmul_kernel(a_ref, w_ref, s_ref, *rest, n_cast):
    cast_in, o_ref, cast_out, w_bf = rest[:n_cast], rest[n_cast], rest[n_cast + 1:-1], rest[-1]

    @pl.when(pl.program_id(1) == 0)
    def _():
        w_bf[...] = (w_ref[...] * s_ref[...]).astype(BF16)

    o_ref[...] = jnp.dot(a_ref[...], w_bf[...], preferred_element_type=F32).astype(o_ref.dtype)
    _cast_some(_cast_pieces(cast_in, cast_out), 0, 1)


def _matmul(a, w, col_scale, col0, n_out, out_dtype, cast_weights=(), tm=1024, tn=1024):
    M, K = a.shape
    assert col0 % tn == 0 and n_out % tn == 0 and M % tm == 0
    col_block0 = col0 // tn
    n_i = M // tm
    cast_specs = _cast_specs(cast_weights, (n_out // tn) * n_i, lambda j, i: j * n_i + i)
    outs = pl.pallas_call(
        functools.partial(_matmul_kernel, n_cast=len(cast_weights)),
        out_shape=(jax.ShapeDtypeStruct((M, n_out), out_dtype),
                   *[jax.ShapeDtypeStruct(cw.shape, BF16) for cw in cast_weights]),
        grid=(n_out // tn, n_i),
        in_specs=[pl.BlockSpec((tm, K), lambda j, i: (i, 0)),
                  pl.BlockSpec((K, tn), lambda j, i: (0, j + col_block0)),
                  pl.BlockSpec((1, tn), lambda j, i: (0, j + col_block0)),
                  *cast_specs],
        out_specs=(pl.BlockSpec((tm, tn), lambda j, i: (i, j)), *cast_specs),
        scratch_shapes=[pltpu.VMEM((K, tn), BF16)],
        compiler_params=_cparams(("parallel", "arbitrary"), 48),
        name="in_proj",
    )(a, w, col_scale, *cast_weights)
    return outs[0], outs[1:]


def _cast_specs(weights, n_grid, step_of):
    specs = []
    for w in weights:
        rows = w.shape[0] // n_grid
        assert w.shape[0] % n_grid == 0 and rows % BF16_ROWS == 0
        specs.append(pl.BlockSpec((rows, w.shape[1]), lambda *g: (step_of(*g), 0)))
    return specs


def _cast_pieces(cast_in, cast_out):
    return [(src_ref, dst_ref, r0) for src_ref, dst_ref in zip(cast_in, cast_out)
            for r0 in range(0, src_ref.shape[0], BF16_ROWS)]


def _cast_some(pieces, m, n):
    for src_ref, dst_ref, r0 in pieces[m * len(pieces) // n:(m + 1) * len(pieces) // n]:
        dst_ref[r0:r0 + BF16_ROWS, :] = src_ref[r0:r0 + BF16_ROWS, :].astype(BF16)


NEG_BIG = -1e30
LOG2E = 1.4426950408889634


def _attn_kernel(q_ref, k_ref, v_ref, bias_ref, gain_ref, *rest, n_chunks, n_cast):
    cast_in = rest[:n_cast]
    o_ref = rest[n_cast]
    cast_out = rest[n_cast + 1:2 * n_cast + 1]
    kta, ktb, vpa, vpb, tab, s_a, s_b, s_c = rest[2 * n_cast + 1:]
    T = n_chunks * CHUNK
    n_steps = T // QBLK
    head0 = lax.broadcasted_iota(jnp.int32, (QBLK, LANES), 1) < ATTN_HEAD_DIM
    m0 = jnp.where(head0, 1.0, 0.0).astype(BF16)
    m1 = jnp.where(head0, 0.0, 1.0).astype(BF16)

    head0_t = lax.broadcasted_iota(jnp.int32, (LANES, QBLK), 0) < ATTN_HEAD_DIM
    mt0 = jnp.where(head0_t, 1.0, 0.0).astype(BF16)
    mt1 = jnp.where(head0_t, 0.0, 1.0).astype(BF16)

    def prep(blk):
        rows = slice(blk * QBLK, (blk + 1) * QBLK)
        kt = k_ref[0, rows, :].T
        v = v_ref[0, rows, :]
        kta[:, rows] = kt * mt0
        ktb[:, rows] = kt * mt1
        vpa[rows, 0:LANES] = v * m0
        vpb[rows, 0:LANES] = v * m1
        vpa[rows, LANES:2 * LANES] = m0
        vpb[rows, LANES:2 * LANES] = m1

    @pl.when(pl.program_id(1) == 0)
    def _():
        qry = lax.broadcasted_iota(jnp.int32, (QBLK, KBLK), 0)
        key = lax.broadcasted_iota(jnp.int32, (QBLK, KBLK), 1)
        in_band = ((qry < CHUNK) & (key < BAND)) | ((qry >= CHUNK) & (key >= CHUNK))
        for hh in range(2):
            g = jnp.broadcast_to(bias_ref[0, hh:hh + 1, :], (QBLK, BIAS_W))
            t = pltpu.roll(g, BIAS_W - QBLK, 1, stride=1, stride_axis=0)[:, :KBLK]
            tab[hh] = jnp.where(in_band, t, NEG_BIG).astype(BF16)

    eye = (lax.broadcasted_iota(jnp.int32, (QBLK, QBLK), 0)
           == lax.broadcasted_iota(jnp.int32, (QBLK, QBLK), 1)).astype(BF16)
    gain = gain_ref[0]

    def band_of(m):
        hi = (m + 1) * QBLK
        lo = max(0, hi - KBLK)
        return slice(lo, hi), KBLK - (hi - lo)

    def scores(m, dst):
        prep(m)
        band, col0 = band_of(m)
        lhs = jnp.concatenate([q_ref[0, m * QBLK:(m + 1) * QBLK, :], eye], axis=1)
        rhs = jnp.concatenate([jnp.concatenate([kta[:, band], ktb[:, band]], axis=1),
                               jnp.concatenate([tab[0, :, col0:], tab[1, :, col0:]], axis=1)], axis=0)
        dst[:, 0:rhs.shape[1]] = jnp.dot(lhs, rhs, preferred_element_type=F32)

    def finish(m, src):
        band, col0 = band_of(m)
        w = KBLK - col0
        s = src[:, 0:2 * w]
        p = jnp.concatenate(
            [jnp.exp2(sh - jnp.max(sh, axis=-1, keepdims=True)) for sh in (s[:, 0:w], s[:, w:2 * w])],
            axis=1).astype(BF16)
        pv = jnp.dot(p, jnp.concatenate([vpa[band, :], vpb[band, :]], axis=0), preferred_element_type=F32)
        o = pv[:, 0:LANES] * (1.0 / pv[:, LANES:2 * LANES])
        o2 = o * o
        ms0 = jnp.sum(jnp.where(head0, o2, 0.0), axis=-1, keepdims=True) / ATTN_HEAD_DIM
        ms1 = jnp.sum(jnp.where(head0, 0.0, o2), axis=-1, keepdims=True) / ATTN_HEAD_DIM
        y = o * lax.rsqrt(jnp.where(head0, ms0, ms1) + EPS) * gain
        o_ref[0, m * QBLK:(m + 1) * QBLK, :] = y.astype(o_ref.dtype)

    pieces = _cast_pieces(cast_in, cast_out)
    bufs = (s_a, s_b, s_c)
    depth = len(bufs)
    ahead = depth - 1
    for m in range(ahead):
        scores(m, bufs[m % depth])
    for m in range(n_steps):
        if m + ahead < n_steps:
            scores(m + ahead, bufs[(m + ahead) % depth])
        finish(m, bufs[m % depth])
        _cast_some(pieces, m, n_steps)


def _attention(proj_a, bias_vec, attn_gain, cast_weights):
    B, T, W3 = proj_a.shape
    W = W3 // 3
    n_pairs = W // LANES
    n_grid = n_pairs * B
    cast_specs = _cast_specs(cast_weights, n_grid, lambda h, b: h * B + b)
    outs = pl.pallas_call(
        functools.partial(_attn_kernel, n_chunks=T // CHUNK, n_cast=len(cast_weights)),
        out_shape=(jax.ShapeDtypeStruct((B, T, W), BF16),
                   *[jax.ShapeDtypeStruct(w.shape, BF16) for w in cast_weights]),
        grid=(n_pairs, B),
        in_specs=[pl.BlockSpec((1, T, LANES), lambda h, b: (b, 0, h)),
                  pl.BlockSpec((1, T, LANES), lambda h, b: (b, 0, n_pairs + h)),
                  pl.BlockSpec((1, T, LANES), lambda h, b: (b, 0, 2 * n_pairs + h)),
                  pl.BlockSpec((1, 2, BIAS_W), lambda h, b: (h, 0, 0)),
                  pl.BlockSpec((1, 1, LANES), lambda h, b: (h, 0, 0)),
                  *cast_specs],
        out_specs=(pl.BlockSpec((1, T, LANES), lambda h, b: (b, 0, h)), *cast_specs),
        scratch_shapes=[pltpu.VMEM((LANES, T), BF16),
                        pltpu.VMEM((LANES, T), BF16),
                        pltpu.VMEM((T, 2 * LANES), BF16),
                        pltpu.VMEM((T, 2 * LANES), BF16),
                        pltpu.VMEM((2, QBLK, KBLK), BF16),
                        pltpu.VMEM((QBLK, 2 * KBLK), F32),
                        pltpu.VMEM((QBLK, 2 * KBLK), F32),
                        pltpu.VMEM((QBLK, 2 * KBLK), F32)],
        compiler_params=_cparams(("parallel", "arbitrary"), 48),
        name="chunk_attention",
    )(proj_a, proj_a, proj_a, bias_vec, attn_gain, *cast_weights)
    return outs[0], outs[1:]


HG_CHUNK = 256
HG_LEVELS = (128, 64, 32, 16, 8, 4, 2, 1)
assert HG_LEVELS[-1] == 1


def _hgrn_kernel(q_ref, f_ref, i_ref, g_ref, lbl_ref, gn_ref, *rest, n_steps, n_cast):
    cast_in, o_ref, cast_out = rest[:n_cast], rest[n_cast], rest[n_cast + 1:]
    pieces = _cast_pieces(cast_in, cast_out)
    C = HG_CHUNK
    H2 = C // 2
    Dk = REC_HEAD_DIM
    nt = (((1,), (1,)), ((), ()))
    lbl = lbl_ref[0]
    e = jnp.exp(lbl - jnp.max(lbl, axis=0, keepdims=True))
    lb = e[0:1, :] / jnp.sum(e, axis=0, keepdims=True)
    c1 = 0.5 * (1.0 - lb)
    gn = gn_ref[...]

    r = lax.broadcasted_iota(jnp.int32, (C, C), 0)
    s = lax.broadcasted_iota(jnp.int32, (C, C), 1)
    tril = (s <= r).astype(BF16)
    rh = lax.broadcasted_iota(jnp.int32, (H2, H2), 0)
    sh = lax.broadcasted_iota(jnp.int32, (H2, H2), 1)
    lvl_mask = {m: ((rh // (2 * m)) == (sh // (2 * m))) & (((rh // m) % 2) == 1) & (((sh // m) % 2) == 0)
                for m in HG_LEVELS[1:]}
    sub = lax.broadcasted_iota(jnp.int32, (C // SUBLANES, SUBLANES, Dk), 1)

    def roll8(x, d):
        return pltpu.roll(x.reshape(C // SUBLANES, SUBLANES, Dk), d, 1)

    def front(n):
        rows = slice(n * C, (n + 1) * C)
        c1t = c1 * jnp.tanh(0.5 * f_ref[0, rows, :])
        f = (1.0 - c1) + c1t
        kk = c1 - c1t
        qq = _silu_tanh(q_ref[0, rows, :])
        ii = i_ref[0, rows, :]
        hi, mid, lo = _split3(jnp.log2(f))
        bb = jnp.dot(tril, jnp.concatenate([hi, mid, lo], axis=1), preferred_element_type=F32)
        b = bb[:, 0:Dk] + bb[:, Dk:2 * Dk] + bb[:, 2 * Dk:3 * Dk]
        return dict(rows=rows, kk=kk, qq=qq, ii=ii, ii_bf=ii.astype(BF16), b=b)

    def level_z(v, m):
        b, kk, qq = v["b"], v["kk"], v["qq"]
        if m >= SUBLANES:
            parts, srcs = [], []
            for p in range(0, C, 2 * m):
                bm = b[p + m - 1:p + m, :]
                parts += [bm - b[p:p + m], b[p + m:p + 2 * m] - bm]
                srcs += [kk[p:p + m], qq[p + m:p + 2 * m]]
            arg = jnp.concatenate(parts, axis=0)
            src = jnp.concatenate(srcs, axis=0)
        else:
            b3 = b.reshape(C // SUBLANES, SUBLANES, Dk)
            if m == 1:
                bm = jnp.where(sub % 2 == 1, roll8(b, 1), b3)
            else:
                bm = jnp.broadcast_to(b3[:, m - 1:m, :], b3.shape)
                for p in range(2 * m, SUBLANES, 2 * m):
                    bm = jnp.where(sub >= p, jnp.broadcast_to(b3[:, p + m - 1:p + m, :], b3.shape), bm)
            upper = (sub // m) % 2 == 1
            arg = ((b3 - bm) * jnp.where(upper, 1.0, -1.0)).reshape(C, Dk)
            src = jnp.where(upper, qq.reshape(b3.shape), kk.reshape(b3.shape)).reshape(C, Dk)
        return (src * jnp.exp2(arg)).astype(BF16)

    vs = [front(j) for j in range(n_steps)]
    a_lo = [None] * n_steps
    a_d0 = [jnp.zeros((H2, H2), F32)] * n_steps
    a_d1 = [jnp.zeros((H2, H2), F32)] * n_steps
    for li, m in enumerate(HG_LEVELS):
        for j, v in enumerate(vs):
            z = level_z(v, m)
            if li == 0:
                a_lo[j] = lax.dot_general(z[H2:], z[:H2], nt, preferred_element_type=F32)
            else:
                g0 = lax.dot_general(z[:H2], z[:H2], nt, preferred_element_type=F32)
                g1 = lax.dot_general(z[H2:], z[H2:], nt, preferred_element_type=F32)
                a_d0[j] = jnp.where(lvl_mask[m], g0, a_d0[j])
                a_d1[j] = jnp.where(lvl_mask[m], g1, a_d1[j])
    intra = []
    for j, v in enumerate(vs):
        o_top = jnp.dot(a_d0[j].astype(BF16), v["ii_bf"][:H2], preferred_element_type=F32)
        o_bot = jnp.dot(jnp.concatenate([a_lo[j], a_d1[j]], axis=1).astype(BF16), v["ii_bf"],
                        preferred_element_type=F32)
        o_diag = jnp.sum(v["qq"] * v["kk"], axis=-1, keepdims=True) * v["ii"]
        intra.append(jnp.concatenate([o_top, o_bot], axis=0) + o_diag)
    st = jnp.zeros((Dk, Dk), F32)
    for j, v in enumerate(vs):
        b = v["b"]
        b_last = b[C - 1:C, :]
        qe = (v["qq"] * jnp.exp2(b)).astype(BF16)
        o = intra[j] + lax.dot_general(qe, st.astype(BF16), nt, preferred_element_type=F32)
        ke = (v["kk"] * jnp.exp2(b_last - b)).astype(BF16)
        st = st * jnp.exp2(b_last) + lax.dot_general(
            v["ii_bf"], ke, (((0,), (0,)), ((), ())), preferred_element_type=F32)
        ms = jnp.mean(o * o, axis=-1, keepdims=True)
        y = o * lax.rsqrt(ms + EPS) * gn
        y = y * _silu_tanh(g_ref[0, v["rows"], :])
        o_ref[0, v["rows"], :] = y.astype(o_ref.dtype)
        _cast_some(pieces, j, n_steps)


def _hgrn(proj_b, lb_logits_h, gnorm_g, cast_weights):
    B, T, W4 = proj_b.shape
    W = W4 // 4
    H = W // REC_HEAD_DIM
    n_slots = lb_logits_h.shape[1]
    blk = lambda off: pl.BlockSpec((1, T, REC_HEAD_DIM), lambda b, h, off=off: (b, 0, off * H + h))
    cast_specs = _cast_specs(cast_weights, B * H, lambda b, h: b * H + h)
    outs = pl.pallas_call(
        functools.partial(_hgrn_kernel, n_steps=T // HG_CHUNK, n_cast=len(cast_weights)),
        out_shape=(jax.ShapeDtypeStruct((B, T, W), BF16),
                   *[jax.ShapeDtypeStruct(w.shape, BF16) for w in cast_weights]),
        grid=(B, H),
        in_specs=[blk(0), blk(1), blk(2), blk(3),
                  pl.BlockSpec((1, n_slots, REC_HEAD_DIM), lambda b, h: (h, 0, 0)),
                  pl.BlockSpec((1, REC_HEAD_DIM), lambda b, h: (0, 0)),
                  *cast_specs],
        out_specs=(pl.BlockSpec((1, T, REC_HEAD_DIM), lambda b, h: (b, 0, h)), *cast_specs),
        compiler_params=_cparams(("parallel", "parallel"), 40),
        name="hgrn2",
    )(proj_b, proj_b, proj_b, proj_b, lb_logits_h, gnorm_g, *cast_weights)
    return outs[0], outs[1:]


def _outproj_kernel(oa_ref, ob_ref, w_ref, x_ref, mod_ref, g_ref, b_ref, x1_ref, *, alpha, sub):
    wa = oa_ref.shape[-1]
    gate1 = _mod_row(mod_ref, pl.program_id(0), 2)
    for r0 in range(0, x_ref.shape[1], sub):
        rs = slice(r0, r0 + sub)
        mix = jnp.dot(oa_ref[0, rs, :], w_ref[0:wa, :], preferred_element_type=F32)
        mix = mix + jnp.dot(ob_ref[0, rs, :], w_ref[wa:, :], preferred_element_type=F32)
        x1_ref[0, rs, :] = _ln_rows(alpha * x_ref[0, rs, :] + gate1 * mix) * g_ref[...] + b_ref[...]


def _outproj(o_a, o_b, w_o, x, mod, ln_g, ln_b, alpha, tm=512, sub=128):
    B, T, D = x.shape
    Wa, Wb = o_a.shape[-1], o_b.shape[-1]
    return pl.pallas_call(
        functools.partial(_outproj_kernel, alpha=alpha, sub=sub),
        out_shape=jax.ShapeDtypeStruct((B, T, D), F32),
        grid=(B, T // tm),
        in_specs=[pl.BlockSpec((1, tm, Wa), lambda b, i: (b, i, 0)),
                  pl.BlockSpec((1, tm, Wb), lambda b, i: (b, i, 0)),
                  pl.BlockSpec((Wa + Wb, D), lambda b, i: (0, 0)),
                  pl.BlockSpec((1, tm, D), lambda b, i: (b, i, 0)),
                  pl.BlockSpec((B, N_MOD * D), lambda b, i: (0, 0)),
                  pl.BlockSpec((1, D), lambda b, i: (0, 0)),
                  pl.BlockSpec((1, D), lambda b, i: (0, 0))],
        out_specs=pl.BlockSpec((1, tm, D), lambda b, i: (b, i, 0)),
        compiler_params=_cparams(("parallel", "parallel"), 48),
        name="out_proj_ln1",
    )(o_a, o_b, w_o, x, mod, ln_g, ln_b)


def _ffn_kernel(wg_ref, wu_ref, wo_ref, x_ref, mod_ref, g_ref, b_ref, o_ref, h_sc, *, alpha, sub):
    f = pl.program_id(2)
    last = pl.num_programs(2) - 1

    def partial_out(rs):
        h = h_sc[rs, :]
        gate = jnp.dot(h, wg_ref[...], preferred_element_type=F32)
        up = jnp.dot(h, wu_ref[...], preferred_element_type=F32)
        act = (_silu_tanh(gate) * up).astype(BF16)
        return jnp.dot(act, wo_ref[...], preferred_element_type=F32)

    everything = slice(0, o_ref.shape[1])

    @pl.when(f == 0)
    def _():
        shift2 = _mod_row(mod_ref, pl.program_id(0), 3)
        scale2 = _mod_row(mod_ref, pl.program_id(0), 4)
        for r0 in range(0, o_ref.shape[1], sub):
            rs = slice(r0, r0 + sub)
            h_sc[rs, :] = (_ln_rows(x_ref[0, rs, :]) * (1.0 + scale2) + shift2).astype(BF16)
        o_ref[0] = partial_out(everything)

    @pl.when((f > 0) & (f < last))
    def _():
        o_ref[0] += partial_out(everything)

    @pl.when(f == last)
    def _():
        gate2 = _mod_row(mod_ref, pl.program_id(0), 5)
        for r0 in range(0, o_ref.shape[1], sub):
            rs = slice(r0, r0 + sub)
            y = o_ref[0, rs, :] + partial_out(rs)
            o_ref[0, rs, :] = _ln_rows(alpha * x_ref[0, rs, :] + gate2 * y) * g_ref[...] + b_ref[...]


def _ffn(w_in, w_out, x1, mod, ln_g, ln_b, alpha, tm=1024, tf=512, sub=256):
    B, T, D = x1.shape
    F = w_out.shape[0]
    nf = F // tf
    return pl.pallas_call(
        functools.partial(_ffn_kernel, alpha=alpha, sub=sub),
        out_shape=jax.ShapeDtypeStruct((B, T, D), F32),
        grid=(B, T // tm, nf),
        in_specs=[pl.BlockSpec((D, tf), lambda b, i, f: (0, f)),
                  pl.BlockSpec((D, tf), lambda b, i, f: (0, nf + f)),
                  pl.BlockSpec((tf, D), lambda b, i, f: (f, 0)),
                  pl.BlockSpec((1, tm, D), lambda b, i, f: (b, i, 0)),
                  pl.BlockSpec((B, N_MOD * D), lambda b, i, f: (0, 0)),
                  pl.BlockSpec((1, D), lambda b, i, f: (0, 0)),
                  pl.BlockSpec((1, D), lambda b, i, f: (0, 0))],
        out_specs=pl.BlockSpec((1, tm, D), lambda b, i, f: (b, i, 0)),
        scratch_shapes=[pltpu.VMEM((tm, D), BF16)],
        compiler_params=_cparams(("parallel", "parallel", "arbitrary"), 60),
        name="swiglu_ffn_ln2",
    )(w_in, w_in, w_out, x1, mod, ln_g, ln_b)


def _bias_vectors(rel_bias):
    H, n_rel = rel_bias.shape
    max_rel = (n_rel - 1) // 2
    u = jnp.arange(BIAS_W)
    idx = jnp.clip(KBLK - u, -max_rel, max_rel) + max_rel
    return rel_bias[:, idx]


def kernel(x, c, w_ada, b_ada, w_in, rel_bias, attn_norm_g, lb_logits, gnorm_g, w_o,
           ln1_g, ln1_b, w_ffn_in, w_ffn_out, ln2_g, ln2_b):
    B, T, D = x.shape
    depth = w_ada.shape[0]
    alpha = (2 * depth) ** 0.25
    attn_w = attn_norm_g.shape[1]
    rec_w = lb_logits.shape[1]
    n_slots = lb_logits.shape[0]
    rec_heads = rec_w // REC_HEAD_DIM
    assert depth == 1 and n_slots == depth + 1
    for layer in range(depth):
        mod = _mod(c, w_ada[layer], b_ada[layer])
        h1 = _ln_mod(x, mod, 0).reshape(B * T, D)
        q_scale = jnp.where(jnp.arange(w_in.shape[2]) < attn_w, ATTN_HEAD_DIM ** -0.5 * LOG2E, 1.0)
        q_scale = q_scale.astype(F32).reshape(1, -1)
        proj_a, _ = _matmul(h1, w_in[layer], q_scale, 0, 3 * attn_w, BF16)
        proj_b, (w_ffn_out_bf,) = _matmul(h1, w_in[layer], q_scale, 3 * attn_w, 4 * rec_w, F32, (w_ffn_out[layer],))
        proj_a = proj_a.reshape(B, T, 3 * attn_w)
        proj_b = proj_b.reshape(B, T, 4 * rec_w)
        bias_vec = (_bias_vectors(rel_bias[layer]) * LOG2E).reshape(-1, 2, BIAS_W)
        o_a, (w_o_bf,) = _attention(
            proj_a, bias_vec, attn_norm_g[layer].reshape(-1, 1, LANES), (w_o[layer],))
        lbl = lb_logits.reshape(n_slots, rec_heads, REC_HEAD_DIM).transpose(1, 0, 2)
        o_b, (w_ffn_in_bf,) = _hgrn(proj_b, lbl, gnorm_g[layer].reshape(1, REC_HEAD_DIM), (w_ffn_in[layer],))
        x = _outproj(o_a, o_b, w_o_bf, x, mod,
                     ln1_g[layer].reshape(1, D), ln1_b[layer].reshape(1, D), alpha)
        x = _ffn(w_ffn_in_bf, w_ffn_out_bf, x, mod,
                 ln2_g[layer].reshape(1, D), ln2_b[layer].reshape(1, D), alpha)
    return x
```

```python
import functools

import jax
import jax.numpy as jnp
from jax import lax
from jax.experimental import pallas as pl
from jax.experimental.pallas import tpu as pltpu

F32 = jnp.float32
BF16 = jnp.bfloat16

CHUNK = 64
N_PAST_CHUNKS = 8
BAND = (N_PAST_CHUNKS + 1) * CHUNK
ATTN_HEAD_DIM = 64
REC_HEAD_DIM = 128
N_MOD = 6
EPS = 1e-5
LANES = 128
SUBLANES = 8
BF16_ROWS = 16
QBLK = 2 * CHUNK
KBLK = BAND + CHUNK
BIAS_W = KBLK + QBLK

MIB = 1024 * 1024


def _cparams(sem, vmem_mib):
    return pltpu.CompilerParams(dimension_semantics=sem, vmem_limit_bytes=vmem_mib * MIB)


def _silu_tanh(x):
    h = 0.5 * x
    return h + h * jnp.tanh(h)


def _mod_row(mod_ref, b, r):
    d = mod_ref.shape[1] // N_MOD
    return mod_ref[pl.ds(b, 1), r * d:(r + 1) * d]


def _ln_rows(x):
    mu = jnp.mean(x, axis=-1, keepdims=True)
    xc = x - mu
    var = jnp.mean(xc * xc, axis=-1, keepdims=True)
    return xc * lax.rsqrt(var + EPS)


def _split3(x):
    hi = x.astype(BF16)
    r1 = x - hi.astype(F32)
    mid = r1.astype(BF16)
    lo = (r1 - mid.astype(F32)).astype(BF16)
    return hi, mid, lo


def _mod_kernel(c_ref, w_ref, b_ref, o_ref):
    act = jnp.concatenate(_split3(_silu_tanh(c_ref[...])), axis=0)
    w = w_ref[...]
    w_hi = w.astype(BF16)
    w_mid = (w - w_hi.astype(F32)).astype(BF16)
    acc = jnp.dot(act, w_hi, preferred_element_type=F32) + jnp.dot(act, w_mid, preferred_element_type=F32)
    out = acc[0:SUBLANES] + acc[SUBLANES:2 * SUBLANES] + acc[2 * SUBLANES:3 * SUBLANES]
    o_ref[...] = out[0:o_ref.shape[0]] + b_ref[...]


def _mod(c, w_ada, b_ada, tn=1024):
    B, D = c.shape
    N = w_ada.shape[1]
    assert B <= SUBLANES
    return pl.pallas_call(
        _mod_kernel,
        out_shape=jax.ShapeDtypeStruct((B, N), F32),
        grid=(N // tn,),
        in_specs=[pl.BlockSpec((SUBLANES, D), lambda j: (0, 0)),
                  pl.BlockSpec((D, tn), lambda j: (0, j)),
                  pl.BlockSpec((1, tn), lambda j: (0, j))],
        out_specs=pl.BlockSpec((B, tn), lambda j: (0, j)),
        compiler_params=_cparams(("parallel",), 40),
        name="adaln_mod",
    )(jnp.pad(c, ((0, SUBLANES - B), (0, 0))), w_ada, b_ada.reshape(1, N))


def _ln_mod_kernel(x_ref, mod_ref, o_ref, *, shift_row):
    y = _ln_rows(x_ref[0])
    b = pl.program_id(0)
    shift = _mod_row(mod_ref, b, shift_row)
    scale = _mod_row(mod_ref, b, shift_row + 1)
    o_ref[0] = (y * (1.0 + scale) + shift).astype(o_ref.dtype)


def _ln_mod(x, mod, shift_row, tm=1024):
    B, T, D = x.shape
    return pl.pallas_call(
        functools.partial(_ln_mod_kernel, shift_row=shift_row),
        out_shape=jax.ShapeDtypeStruct((B, T, D), BF16),
        grid=(B, T // tm),
        in_specs=[pl.BlockSpec((1, tm, D), lambda b, i: (b, i, 0)),
                  pl.BlockSpec((B, N_MOD * D), lambda b, i: (0, 0))],
        out_specs=pl.BlockSpec((1, tm, D), lambda b, i: (b, i, 0)),
        compiler_params=_cparams(("parallel", "parallel"), 32),
        name="ln_modulate",
    )(x, mod)


def _matmul_kernel(a_ref, w_ref, s_ref, o_ref, w_bf):
    @pl.when(pl.program_id(1) == 0)
    def _():
        w_bf[...] = (w_ref[...] * s_ref[...]).astype(BF16)

    o_ref[...] = jnp.dot(a_ref[...], w_bf[...], preferred_element_type=F32).astype(o_ref.dtype)


def _matmul(a, w, col_scale, col0, n_out, out_dtype, tm=1024, tn=1024):
    M, K = a.shape
    assert col0 % tn == 0 and n_out % tn == 0 and M % tm == 0
    col_block0 = col0 // tn
    return pl.pallas_call(
        _matmul_kernel,
        out_shape=jax.ShapeDtypeStruct((M, n_out), out_dtype),
        grid=(n_out // tn, M // tm),
        in_specs=[pl.BlockSpec((tm, K), lambda j, i: (i, 0)),
                  pl.BlockSpec((K, tn), lambda j, i: (0, j + col_block0)),
                  pl.BlockSpec((1, tn), lambda j, i: (0, j + col_block0))],
        out_specs=pl.BlockSpec((tm, tn), lambda j, i: (i, j)),
        scratch_shapes=[pltpu.VMEM((K, tn), BF16)],
        compiler_params=_cparams(("parallel", "arbitrary"), 48),
        name="in_proj",
    )(a, w, col_scale)


def _cast_specs(weights, n_grid, step_of):
    specs = []
    for w in weights:
        rows = w.shape[0] // n_grid
        assert w.shape[0] % n_grid == 0 and rows % BF16_ROWS == 0
        specs.append(pl.BlockSpec((rows, w.shape[1]), lambda *g: (step_of(*g), 0)))
    return specs


def _cast_pieces(cast_in, cast_out):
    return [(src_ref, dst_ref, r0) for src_ref, dst_ref in zip(cast_in, cast_out)
            for r0 in range(0, src_ref.shape[0], BF16_ROWS)]


def _cast_some(pieces, m, n):
    for src_ref, dst_ref, r0 in pieces[m * len(pieces) // n:(m + 1) * len(pieces) // n]:
        dst_ref[r0:r0 + BF16_ROWS, :] = src_ref[r0:r0 + BF16_ROWS, :].astype(BF16)


NEG_BIG = -1e30
LOG2E = 1.4426950408889634


def _attn_kernel(q_ref, k_ref, v_ref, bias_ref, gain_ref, *rest, n_chunks, n_cast):
    cast_in = rest[:n_cast]
    o_ref = rest[n_cast]
    cast_out = rest[n_cast + 1:2 * n_cast + 1]
    kta, ktb, vpa, vpb, tab, s_a, s_b, s_c = rest[2 * n_cast + 1:]
    T = n_chunks * CHUNK
    n_steps = T // QBLK
    head0 = lax.broadcasted_iota(jnp.int32, (QBLK, LANES), 1) < ATTN_HEAD_DIM
    m0 = jnp.where(head0, 1.0, 0.0).astype(BF16)
    m1 = jnp.where(head0, 0.0, 1.0).astype(BF16)

    head0_t = lax.broadcasted_iota(jnp.int32, (LANES, QBLK), 0) < ATTN_HEAD_DIM
    mt0 = jnp.where(head0_t, 1.0, 0.0).astype(BF16)
    mt1 = jnp.where(head0_t, 0.0, 1.0).astype(BF16)

    def prep(blk):
        rows = slice(blk * QBLK, (blk + 1) * QBLK)
        kt = k_ref[0, rows, :].T
        v = v_ref[0, rows, :]
        kta[:, rows] = kt * mt0
        ktb[:, rows] = kt * mt1
        vpa[rows, 0:LANES] = v * m0
        vpb[rows, 0:LANES] = v * m1
        vpa[rows, LANES:2 * LANES] = m0
        vpb[rows, LANES:2 * LANES] = m1

    @pl.when(pl.program_id(1) == 0)
    def _():
        qry = lax.broadcasted_iota(jnp.int32, (QBLK, KBLK), 0)
        key = lax.broadcasted_iota(jnp.int32, (QBLK, KBLK), 1)
        in_band = ((qry < CHUNK) & (key < BAND)) | ((qry >= CHUNK) & (key >= CHUNK))
        for hh in range(2):
            g = jnp.broadcast_to(bias_ref[0, hh:hh + 1, :], (QBLK, BIAS_W))
            t = pltpu.roll(g, BIAS_W - QBLK, 1, stride=1, stride_axis=0)[:, :KBLK]
            tab[hh] = jnp.where(in_band, t, NEG_BIG).astype(BF16)

    eye = (lax.broadcasted_iota(jnp.int32, (QBLK, QBLK), 0)
           == lax.broadcasted_iota(jnp.int32, (QBLK, QBLK), 1)).astype(BF16)
    gain = gain_ref[0]

    def band_of(m):
        hi = (m + 1) * QBLK
        lo = max(0, hi - KBLK)
        return slice(lo, hi), KBLK - (hi - lo)

    def scores(m, dst):
        prep(m)
        band, col0 = band_of(m)
        lhs = jnp.concatenate([q_ref[0, m * QBLK:(m + 1) * QBLK, :], eye], axis=1)
        rhs = jnp.concatenate([jnp.concatenate([kta[:, band], ktb[:, band]], axis=1),
                               jnp.concatenate([tab[0, :, col0:], tab[1, :, col0:]], axis=1)], axis=0)
        dst[:, 0:rhs.shape[1]] = jnp.dot(lhs, rhs, preferred_element_type=F32)

    def finish(m, src):
        band, col0 = band_of(m)
        w = KBLK - col0
        s = src[:, 0:2 * w]
        p = jnp.concatenate(
            [jnp.exp2(sh - jnp.max(sh, axis=-1, keepdims=True)) for sh in (s[:, 0:w], s[:, w:2 * w])],
            axis=1).astype(BF16)
        pv = jnp.dot(p, jnp.concatenate([vpa[band, :], vpb[band, :]], axis=0), preferred_element_type=F32)
        o = pv[:, 0:LANES] * (1.0 / pv[:, LANES:2 * LANES])
        o2 = o * o
        ms0 = jnp.sum(jnp.where(head0, o2, 0.0), axis=-1, keepdims=True) / ATTN_HEAD_DIM
        ms1 = jnp.sum(jnp.where(head0, 0.0, o2), axis=-1, keepdims=True) / ATTN_HEAD_DIM
        y = o * lax.rsqrt(jnp.where(head0, ms0, ms1) + EPS) * gain
        o_ref[0, m * QBLK:(m + 1) * QBLK, :] = y.astype(o_ref.dtype)

    pieces = _cast_pieces(cast_in, cast_out)
    bufs = (s_a, s_b, s_c)
    depth = len(bufs)
    ahead = depth - 1
    for m in range(ahead):
        scores(m, bufs[m % depth])
    for m in range(n_steps):
        if m + ahead < n_steps:
            scores(m + ahead, bufs[(m + ahead) % depth])
        finish(m, bufs[m % depth])
        _cast_some(pieces, m, n_steps)


def _attention(proj_a, bias_vec, attn_gain, cast_weights):
    B, T, W3 = proj_a.shape
    W = W3 // 3
    n_pairs = W // LANES
    n_grid = n_pairs * B
    cast_specs = _cast_specs(cast_weights, n_grid, lambda h, b: h * B + b)
    outs = pl.pallas_call(
        functools.partial(_attn_kernel, n_chunks=T // CHUNK, n_cast=len(cast_weights)),
        out_shape=(jax.ShapeDtypeStruct((B, T, W), BF16),
                   *[jax.ShapeDtypeStruct(w.shape, BF16) for w in cast_weights]),
        grid=(n_pairs, B),
        in_specs=[pl.BlockSpec((1, T, LANES), lambda h, b: (b, 0, h)),
                  pl.BlockSpec((1, T, LANES), lambda h, b: (b, 0, n_pairs + h)),
                  pl.BlockSpec((1, T, LANES), lambda h, b: (b, 0, 2 * n_pairs + h)),
                  pl.BlockSpec((1, 2, BIAS_W), lambda h, b: (h, 0, 0)),
                  pl.BlockSpec((1, 1, LANES), lambda h, b: (h, 0, 0)),
                  *cast_specs],
        out_specs=(pl.BlockSpec((1, T, LANES), lambda h, b: (b, 0, h)), *cast_specs),
        scratch_shapes=[pltpu.VMEM((LANES, T), BF16),
                        pltpu.VMEM((LANES, T), BF16),
                        pltpu.VMEM((T, 2 * LANES), BF16),
                        pltpu.VMEM((T, 2 * LANES), BF16),
                        pltpu.VMEM((2, QBLK, KBLK), BF16),
                        pltpu.VMEM((QBLK, 2 * KBLK), F32),
                        pltpu.VMEM((QBLK, 2 * KBLK), F32),
                        pltpu.VMEM((QBLK, 2 * KBLK), F32)],
        compiler_params=_cparams(("parallel", "arbitrary"), 48),
        name="chunk_attention",
    )(proj_a, proj_a, proj_a, bias_vec, attn_gain, *cast_weights)
    return outs[0], outs[1:]


HG_CHUNK = 256
HG_LEVELS = (128, 64, 32, 16, 8, 4, 2, 1)
assert HG_LEVELS[-1] == 1


def _hgrn_kernel(q_ref, f_ref, i_ref, g_ref, lbl_ref, gn_ref, *rest, n_steps, n_cast):
    cast_in, o_ref, cast_out = rest[:n_cast], rest[n_cast], rest[n_cast + 1:]
    pieces = _cast_pieces(cast_in, cast_out)
    C = HG_CHUNK
    H2 = C // 2
    Dk = REC_HEAD_DIM
    nt = (((1,), (1,)), ((), ()))
    lbl = lbl_ref[0]
    e = jnp.exp(lbl - jnp.max(lbl, axis=0, keepdims=True))
    lb = e[0:1, :] / jnp.sum(e, axis=0, keepdims=True)
    c1 = 0.5 * (1.0 - lb)
    gn = gn_ref[...]

    r = lax.broadcasted_iota(jnp.int32, (C, C), 0)
    s = lax.broadcasted_iota(jnp.int32, (C, C), 1)
    tril = (s <= r).astype(BF16)
    rh = lax.broadcasted_iota(jnp.int32, (H2, H2), 0)
    sh = lax.broadcasted_iota(jnp.int32, (H2, H2), 1)
    lvl_mask = {m: ((rh // (2 * m)) == (sh // (2 * m))) & (((rh // m) % 2) == 1) & (((sh // m) % 2) == 0)
                for m in HG_LEVELS[1:]}
    sub = lax.broadcasted_iota(jnp.int32, (C // SUBLANES, SUBLANES, Dk), 1)

    def roll8(x, d):
        return pltpu.roll(x.reshape(C // SUBLANES, SUBLANES, Dk), d, 1)

    def front(n):
        rows = slice(n * C, (n + 1) * C)
        c1t = c1 * jnp.tanh(0.5 * f_ref[0, rows, :])
        f = (1.0 - c1) + c1t
        kk = c1 - c1t
        qq = _silu_tanh(q_ref[0, rows, :])
        ii = i_ref[0, rows, :]
        hi, mid, lo = _split3(jnp.log2(f))
        bb = jnp.dot(tril, jnp.concatenate([hi, mid, lo], axis=1), preferred_element_type=F32)
        b = bb[:, 0:Dk] + bb[:, Dk:2 * Dk] + bb[:, 2 * Dk:3 * Dk]
        return dict(rows=rows, kk=kk, qq=qq, ii=ii, ii_bf=ii.astype(BF16), b=b)

    def level_z(v, m):
        b, kk, qq = v["b"], v["kk"], v["qq"]
        if m >= SUBLANES:
            parts, srcs = [], []
            for p in range(0, C, 2 * m):
                bm = b[p + m - 1:p + m, :]
                parts += [bm - b[p:p + m], b[p + m:p + 2 * m] - bm]
                srcs += [kk[p:p + m], qq[p + m:p + 2 * m]]
            arg = jnp.concatenate(parts, axis=0)
            src = jnp.concatenate(srcs, axis=0)
        else:
            b3 = b.reshape(C // SUBLANES, SUBLANES, Dk)
            if m == 1:
                bm = jnp.where(sub % 2 == 1, roll8(b, 1), b3)
            else:
                bm = jnp.broadcast_to(b3[:, m - 1:m, :], b3.shape)
                for p in range(2 * m, SUBLANES, 2 * m):
                    bm = jnp.where(sub >= p, jnp.broadcast_to(b3[:, p + m - 1:p + m, :], b3.shape), bm)
            upper = (sub // m) % 2 == 1
            arg = ((b3 - bm) * jnp.where(upper, 1.0, -1.0)).reshape(C, Dk)
            src = jnp.where(upper, qq.reshape(b3.shape), kk.reshape(b3.shape)).reshape(C, Dk)
        return (src * jnp.exp2(arg)).astype(BF16)

    vs = [front(j) for j in range(n_steps)]
    a_lo = [None] * n_steps
    a_d0 = [jnp.zeros((H2, H2), F32)] * n_steps
    a_d1 = [jnp.zeros((H2, H2), F32)] * n_steps
    for li, m in enumerate(HG_LEVELS):
        for j, v in enumerate(vs):
            z = level_z(v, m)
            if li == 0:
                a_lo[j] = lax.dot_general(z[H2:], z[:H2], nt, preferred_element_type=F32)
            else:
                g = lax.dot_general(z, z, nt, preferred_element_type=F32)
                a_d0[j] = jnp.where(lvl_mask[m], g[:H2, :H2], a_d0[j])
                a_d1[j] = jnp.where(lvl_mask[m], g[H2:, H2:], a_d1[j])
    intra = []
    for j, v in enumerate(vs):
        o_top = jnp.dot(a_d0[j].astype(BF16), v["ii_bf"][:H2], preferred_element_type=F32)
        o_bot = jnp.dot(jnp.concatenate([a_lo[j], a_d1[j]], axis=1).astype(BF16), v["ii_bf"],
                        preferred_element_type=F32)
        o_diag = jnp.sum(v["qq"] * v["kk"], axis=-1, keepdims=True) * v["ii"]
        intra.append(jnp.concatenate([o_top, o_bot], axis=0) + o_diag)
    st = jnp.zeros((Dk, Dk), F32)
    for j, v in enumerate(vs):
        b = v["b"]
        b_last = b[C - 1:C, :]
        qe = (v["qq"] * jnp.exp2(b)).astype(BF16)
        o = intra[j] + lax.dot_general(qe, st.astype(BF16), nt, preferred_element_type=F32)
        ke = (v["kk"] * jnp.exp2(b_last - b)).astype(BF16)
        st = st * jnp.exp2(b_last) + lax.dot_general(
            v["ii_bf"], ke, (((0,), (0,)), ((), ())), preferred_element_type=F32)
        ms = jnp.mean(o * o, axis=-1, keepdims=True)
        y = o * lax.rsqrt(ms + EPS) * gn
        y = y * _silu_tanh(g_ref[0, v["rows"], :])
        o_ref[0, v["rows"], :] = y.astype(o_ref.dtype)
        _cast_some(pieces, j, n_steps)


def _hgrn(proj_b, lb_logits_h, gnorm_g, cast_weights):
    B, T, W4 = proj_b.shape
    W = W4 // 4
    H = W // REC_HEAD_DIM
    n_slots = lb_logits_h.shape[1]
    blk = lambda off: pl.BlockSpec((1, T, REC_HEAD_DIM), lambda b, h, off=off: (b, 0, off * H + h))
    cast_specs = _cast_specs(cast_weights, B * H, lambda b, h: b * H + h)
    outs = pl.pallas_call(
        functools.partial(_hgrn_kernel, n_steps=T // HG_CHUNK, n_cast=len(cast_weights)),
        out_shape=(jax.ShapeDtypeStruct((B, T, W), BF16),
                   *[jax.ShapeDtypeStruct(w.shape, BF16) for w in cast_weights]),
        grid=(B, H),
        in_specs=[blk(0), blk(1), blk(2), blk(3),
                  pl.BlockSpec((1, n_slots, REC_HEAD_DIM), lambda b, h: (h, 0, 0)),
                  pl.BlockSpec((1, REC_HEAD_DIM), lambda b, h: (0, 0)),
                  *cast_specs],
        out_specs=(pl.BlockSpec((1, T, REC_HEAD_DIM), lambda b, h: (b, 0, h)), *cast_specs),
        compiler_params=_cparams(("parallel", "parallel"), 40),
        name="hgrn2",
    )(proj_b, proj_b, proj_b, proj_b, lb_logits_h, gnorm_g, *cast_weights)
    return outs[0], outs[1:]


def _outproj_kernel(oa_ref, ob_ref, w_ref, x_ref, mod_ref, g_ref, b_ref, x1_ref, *, alpha, sub):
    wa = oa_ref.shape[-1]
    gate1 = _mod_row(mod_ref, pl.program_id(0), 2)
    for r0 in range(0, x_ref.shape[1], sub):
        rs = slice(r0, r0 + sub)
        mix = jnp.dot(oa_ref[0, rs, :], w_ref[0:wa, :], preferred_element_type=F32)
        mix = mix + jnp.dot(ob_ref[0, rs, :], w_ref[wa:, :], preferred_element_type=F32)
        x1_ref[0, rs, :] = _ln_rows(alpha * x_ref[0, rs, :] + gate1 * mix) * g_ref[...] + b_ref[...]


def _outproj(o_a, o_b, w_o, x, mod, ln_g, ln_b, alpha, tm=512, sub=128):
    B, T, D = x.shape
    Wa, Wb = o_a.shape[-1], o_b.shape[-1]
    return pl.pallas_call(
        functools.partial(_outproj_kernel, alpha=alpha, sub=sub),
        out_shape=jax.ShapeDtypeStruct((B, T, D), F32),
        grid=(B, T // tm),
        in_specs=[pl.BlockSpec((1, tm, Wa), lambda b, i: (b, i, 0)),
                  pl.BlockSpec((1, tm, Wb), lambda b, i: (b, i, 0)),
                  pl.BlockSpec((Wa + Wb, D), lambda b, i: (0, 0)),
                  pl.BlockSpec((1, tm, D), lambda b, i: (b, i, 0)),
                  pl.BlockSpec((B, N_MOD * D), lambda b, i: (0, 0)),
                  pl.BlockSpec((1, D), lambda b, i: (0, 0)),
                  pl.BlockSpec((1, D), lambda b, i: (0, 0))],
        out_specs=pl.BlockSpec((1, tm, D), lambda b, i: (b, i, 0)),
        compiler_params=_cparams(("parallel", "parallel"), 48),
        name="out_proj_ln1",
    )(o_a, o_b, w_o, x, mod, ln_g, ln_b)


def _ffn_kernel(wg_ref, wu_ref, wo_ref, x_ref, mod_ref, g_ref, b_ref, o_ref, h_sc, *, alpha, sub):
    f = pl.program_id(2)
    last = pl.num_programs(2) - 1

    def partial_out(rs):
        h = h_sc[rs, :]
        gate = jnp.dot(h, wg_ref[...], preferred_element_type=F32)
        up = jnp.dot(h, wu_ref[...], preferred_element_type=F32)
        act = (_silu_tanh(gate) * up).astype(BF16)
        return jnp.dot(act, wo_ref[...], preferred_element_type=F32)

    everything = slice(0, o_ref.shape[1])

    @pl.when(f == 0)
    def _():
        shift2 = _mod_row(mod_ref, pl.program_id(0), 3)
        scale2 = _mod_row(mod_ref, pl.program_id(0), 4)
        for r0 in range(0, o_ref.shape[1], sub):
            rs = slice(r0, r0 + sub)
            h_sc[rs, :] = (_ln_rows(x_ref[0, rs, :]) * (1.0 + scale2) + shift2).astype(BF16)
        o_ref[0] = partial_out(everything)

    @pl.when((f > 0) & (f < last))
    def _():
        o_ref[0] += partial_out(everything)

    @pl.when(f == last)
    def _():
        gate2 = _mod_row(mod_ref, pl.program_id(0), 5)
        for r0 in range(0, o_ref.shape[1], sub):
            rs = slice(r0, r0 + sub)
            y = o_ref[0, rs, :] + partial_out(rs)
            o_ref[0, rs, :] = _ln_rows(alpha * x_ref[0, rs, :] + gate2 * y) * g_ref[...] + b_ref[...]


def _ffn(w_in, w_out, x1, mod, ln_g, ln_b, alpha, tm=1024, tf=512, sub=256):
    B, T, D = x1.shape
    F = w_out.shape[0]
    nf = F // tf
    return pl.pallas_call(
        functools.partial(_ffn_kernel, alpha=alpha, sub=sub),
        out_shape=jax.ShapeDtypeStruct((B, T, D), F32),
        grid=(B, T // tm, nf),
        in_specs=[pl.BlockSpec((D, tf), lambda b, i, f: (0, f)),
                  pl.BlockSpec((D, tf), lambda b, i, f: (0, nf + f)),
                  pl.BlockSpec((tf, D), lambda b, i, f: (f, 0)),
                  pl.BlockSpec((1, tm, D), lambda b, i, f: (b, i, 0)),
                  pl.BlockSpec((B, N_MOD * D), lambda b, i, f: (0, 0)),
                  pl.BlockSpec((1, D), lambda b, i, f: (0, 0)),
                  pl.BlockSpec((1, D), lambda b, i, f: (0, 0))],
        out_specs=pl.BlockSpec((1, tm, D), lambda b, i, f: (b, i, 0)),
        scratch_shapes=[pltpu.VMEM((tm, D), BF16)],
        compiler_params=_cparams(("parallel", "parallel", "arbitrary"), 60),
        name="swiglu_ffn_ln2",
    )(w_in, w_in, w_out, x1, mod, ln_g, ln_b)


def _bias_vectors(rel_bias):
    H, n_rel = rel_bias.shape
    max_rel = (n_rel - 1) // 2
    u = jnp.arange(BIAS_W)
    idx = jnp.clip(KBLK - u, -max_rel, max_rel) + max_rel
    return rel_bias[:, idx]


def kernel(x, c, w_ada, b_ada, w_in, rel_bias, attn_norm_g, lb_logits, gnorm_g, w_o,
           ln1_g, ln1_b, w_ffn_in, w_ffn_out, ln2_g, ln2_b):
    B, T, D = x.shape
    depth = w_ada.shape[0]
    alpha = (2 * depth) ** 0.25
    attn_w = attn_norm_g.shape[1]
    rec_w = lb_logits.shape[1]
    n_slots = lb_logits.shape[0]
    rec_heads = rec_w // REC_HEAD_DIM
    assert depth == 1 and n_slots == depth + 1
    for layer in range(depth):
        mod = _mod(c, w_ada[layer], b_ada[layer])
        h1 = _ln_mod(x, mod, 0).reshape(B * T, D)
        q_scale = jnp.where(jnp.arange(w_in.shape[2]) < attn_w, ATTN_HEAD_DIM ** -0.5 * LOG2E, 1.0)
        q_scale = q_scale.astype(F32).reshape(1, -1)
        proj_a = _matmul(h1, w_in[layer], q_scale, 0, 3 * attn_w, BF16).reshape(B, T, 3 * attn_w)
        proj_b = _matmul(h1, w_in[layer], q_scale, 3 * attn_w, 4 * rec_w, F32).reshape(B, T, 4 * rec_w)
        bias_vec = (_bias_vectors(rel_bias[layer]) * LOG2E).reshape(-1, 2, BIAS_W)
        o_a, (w_o_bf, w_ffn_out_bf) = _attention(
            proj_a, bias_vec, attn_norm_g[layer].reshape(-1, 1, LANES), (w_o[layer], w_ffn_out[layer]))
        lbl = lb_logits.reshape(n_slots, rec_heads, REC_HEAD_DIM).transpose(1, 0, 2)
        o_b, (w_ffn_in_bf,) = _hgrn(proj_b, lbl, gnorm_g[layer].reshape(1, REC_HEAD_DIM), (w_ffn_in[layer],))
        x = _outproj(o_a, o_b, w_o_bf, x, mod,
                     ln1_g[layer].reshape(1, D), ln1_b[layer].reshape(1, D), alpha)
        x = _ffn(w_ffn_in_bf, w_ffn_out_bf, x, mod,
                 ln2_g[layer].reshape(1, D), ln2_b[layer].reshape(1, D), alpha)
    return x
```

```python
import functools

import jax
import jax.numpy as jnp
from jax import lax
from jax.experimental import pallas as pl
from jax.experimental.pallas import tpu as pltpu

F32 = jnp.float32
BF16 = jnp.bfloat16

CHUNK = 64
N_PAST_CHUNKS = 8
BAND = (N_PAST_CHUNKS + 1) * CHUNK
ATTN_HEAD_DIM = 64
REC_HEAD_DIM = 128
N_MOD = 6
EPS = 1e-5
LANES = 128
SUBLANES = 8
BF16_ROWS = 16
QBLK = 2 * CHUNK
KBLK = BAND + CHUNK
BIAS_W = KBLK + QBLK

MIB = 1024 * 1024


def _cparams(sem, vmem_mib):
    return pltpu.CompilerParams(dimension_semantics=sem, vmem_limit_bytes=vmem_mib * MIB)


def _silu_tanh(x):
    h = 0.5 * x
    return h + h * jnp.tanh(h)


def _mod_row(mod_ref, b, r):
    d = mod_ref.shape[1] // N_MOD
    return mod_ref[pl.ds(b, 1), r * d:(r + 1) * d]


def _ln_rows(x):
    mu = jnp.mean(x, axis=-1, keepdims=True)
    xc = x - mu
    var = jnp.mean(xc * xc, axis=-1, keepdims=True)
    return xc * lax.rsqrt(var + EPS)


def _split3(x):
    hi = x.astype(BF16)
    r1 = x - hi.astype(F32)
    mid = r1.astype(BF16)
    lo = (r1 - mid.astype(F32)).astype(BF16)
    return hi, mid, lo


def _mod_kernel(c_ref, w_ref, b_ref, o_ref):
    act = jnp.concatenate(_split3(_silu_tanh(c_ref[...])), axis=0)
    w = w_ref[...]
    w_hi = w.astype(BF16)
    w_mid = (w - w_hi.astype(F32)).astype(BF16)
    acc = jnp.dot(act, w_hi, preferred_element_type=F32) + jnp.dot(act, w_mid, preferred_element_type=F32)
    out = acc[0:SUBLANES] + acc[SUBLANES:2 * SUBLANES] + acc[2 * SUBLANES:3 * SUBLANES]
    o_ref[...] = out[0:o_ref.shape[0]] + b_ref[...]


def _mod(c, w_ada, b_ada, tn=1024):
    B, D = c.shape
    N = w_ada.shape[1]
    assert B <= SUBLANES
    return pl.pallas_call(
        _mod_kernel,
        out_shape=jax.ShapeDtypeStruct((B, N), F32),
        grid=(N // tn,),
        in_specs=[pl.BlockSpec((SUBLANES, D), lambda j: (0, 0)),
                  pl.BlockSpec((D, tn), lambda j: (0, j)),
                  pl.BlockSpec((1, tn), lambda j: (0, j))],
        out_specs=pl.BlockSpec((B, tn), lambda j: (0, j)),
        compiler_params=_cparams(("parallel",), 40),
        name="adaln_mod",
    )(jnp.pad(c, ((0, SUBLANES - B), (0, 0))), w_ada, b_ada.reshape(1, N))


def _ln_mod_kernel(x_ref, mod_ref, o_ref, *, shift_row):
    y = _ln_rows(x_ref[0])
    b = pl.program_id(0)
    shift = _mod_row(mod_ref, b, shift_row)
    scale = _mod_row(mod_ref, b, shift_row + 1)
    o_ref[0] = (y * (1.0 + scale) + shift).astype(o_ref.dtype)


def _ln_mod(x, mod, shift_row, tm=1024):
    B, T, D = x.shape
    return pl.pallas_call(
        functools.partial(_ln_mod_kernel, shift_row=shift_row),
        out_shape=jax.ShapeDtypeStruct((B, T, D), BF16),
        grid=(B, T // tm),
        in_specs=[pl.BlockSpec((1, tm, D), lambda b, i: (b, i, 0)),
                  pl.BlockSpec((B, N_MOD * D), lambda b, i: (0, 0))],
        out_specs=pl.BlockSpec((1, tm, D), lambda b, i: (b, i, 0)),
        compiler_params=_cparams(("parallel", "parallel"), 32),
        name="ln_modulate",
    )(x, mod)


def _matmul_kernel(a_ref, w_ref, s_ref, o_ref, w_bf):
    @pl.when(pl.program_id(1) == 0)
    def _():
        w_bf[...] = (w_ref[...] * s_ref[...]).astype(BF16)

    o_ref[...] = jnp.dot(a_ref[...], w_bf[...], preferred_element_type=F32).astype(o_ref.dtype)


def _matmul(a, w, col_scale, col0, n_out, out_dtype, tm=1024, tn=1024):
    M, K = a.shape
    assert col0 % tn == 0 and n_out % tn == 0 and M % tm == 0
    col_block0 = col0 // tn
    return pl.pallas_call(
        _matmul_kernel,
        out_shape=jax.ShapeDtypeStruct((M, n_out), out_dtype),
        grid=(n_out // tn, M // tm),
        in_specs=[pl.BlockSpec((tm, K), lambda j, i: (i, 0)),
                  pl.BlockSpec((K, tn), lambda j, i: (0, j + col_block0)),
                  pl.BlockSpec((1, tn), lambda j, i: (0, j + col_block0))],
        out_specs=pl.BlockSpec((tm, tn), lambda j, i: (i, j)),
        scratch_shapes=[pltpu.VMEM((K, tn), BF16)],
        compiler_params=_cparams(("parallel", "arbitrary"), 48),
        name="in_proj",
    )(a, w, col_scale)


def _cast_specs(weights, n_grid, step_of):
    specs = []
    for w in weights:
        rows = w.shape[0] // n_grid
        assert w.shape[0] % n_grid == 0 and rows % BF16_ROWS == 0
        specs.append(pl.BlockSpec((rows, w.shape[1]), lambda *g: (step_of(*g), 0)))
    return specs


def _cast_pieces(cast_in, cast_out):
    return [(src_ref, dst_ref, r0) for src_ref, dst_ref in zip(cast_in, cast_out)
            for r0 in range(0, src_ref.shape[0], BF16_ROWS)]


def _cast_some(pieces, m, n):
    for src_ref, dst_ref, r0 in pieces[m * len(pieces) // n:(m + 1) * len(pieces) // n]:
        dst_ref[r0:r0 + BF16_ROWS, :] = src_ref[r0:r0 + BF16_ROWS, :].astype(BF16)


NEG_BIG = -1e30
LOG2E = 1.4426950408889634


def _attn_kernel(q_ref, k_ref, v_ref, bias_ref, gain_ref, *rest, n_chunks, n_cast):
    cast_in = rest[:n_cast]
    o_ref = rest[n_cast]
    cast_out = rest[n_cast + 1:2 * n_cast + 1]
    kta, ktb, vpa, vpb, tab, s_a, s_b, s_c = rest[2 * n_cast + 1:]
    T = n_chunks * CHUNK
    n_steps = T // QBLK
    head0 = lax.broadcasted_iota(jnp.int32, (QBLK, LANES), 1) < ATTN_HEAD_DIM
    m0 = jnp.where(head0, 1.0, 0.0).astype(BF16)
    m1 = jnp.where(head0, 0.0, 1.0).astype(BF16)

    head0_t = lax.broadcasted_iota(jnp.int32, (LANES, QBLK), 0) < ATTN_HEAD_DIM
    mt0 = jnp.where(head0_t, 1.0, 0.0).astype(BF16)
    mt1 = jnp.where(head0_t, 0.0, 1.0).astype(BF16)

    def prep(blk):
        rows = slice(blk * QBLK, (blk + 1) * QBLK)
        kt = k_ref[0, rows, :].T
        v = v_ref[0, rows, :]
        kta[:, rows] = kt * mt0
        ktb[:, rows] = kt * mt1
        vpa[rows, 0:LANES] = v * m0
        vpb[rows, 0:LANES] = v * m1
        vpa[rows, LANES:2 * LANES] = m0
        vpb[rows, LANES:2 * LANES] = m1

    @pl.when(pl.program_id(1) == 0)
    def _():
        qry = lax.broadcasted_iota(jnp.int32, (QBLK, KBLK), 0)
        key = lax.broadcasted_iota(jnp.int32, (QBLK, KBLK), 1)
        in_band = ((qry < CHUNK) & (key < BAND)) | ((qry >= CHUNK) & (key >= CHUNK))
        for hh in range(2):
            g = jnp.broadcast_to(bias_ref[0, hh:hh + 1, :], (QBLK, BIAS_W))
            t = pltpu.roll(g, BIAS_W - QBLK, 1, stride=1, stride_axis=0)[:, :KBLK]
            tab[hh] = jnp.where(in_band, t, NEG_BIG).astype(BF16)

    eye = (lax.broadcasted_iota(jnp.int32, (QBLK, QBLK), 0)
           == lax.broadcasted_iota(jnp.int32, (QBLK, QBLK), 1)).astype(BF16)
    gain = gain_ref[0]

    def band_of(m):
        hi = (m + 1) * QBLK
        lo = max(0, hi - KBLK)
        return slice(lo, hi), KBLK - (hi - lo)

    def scores(m, dst):
        prep(m)
        band, col0 = band_of(m)
        lhs = jnp.concatenate([q_ref[0, m * QBLK:(m + 1) * QBLK, :], eye], axis=1)
        rhs = jnp.concatenate([jnp.concatenate([kta[:, band], ktb[:, band]], axis=1),
                               jnp.concatenate([tab[0, :, col0:], tab[1, :, col0:]], axis=1)], axis=0)
        dst[:, 0:rhs.shape[1]] = jnp.dot(lhs, rhs, preferred_element_type=F32)

    def finish(m, src):
        band, col0 = band_of(m)
        w = KBLK - col0
        s = src[:, 0:2 * w]
        p = jnp.concatenate(
            [jnp.exp2(sh - jnp.max(sh, axis=-1, keepdims=True)) for sh in (s[:, 0:w], s[:, w:2 * w])],
            axis=1).astype(BF16)
        pv = jnp.dot(p, jnp.concatenate([vpa[band, :], vpb[band, :]], axis=0), preferred_element_type=F32)
        o = pv[:, 0:LANES] * (1.0 / pv[:, LANES:2 * LANES])
        o2 = o * o
        ms0 = jnp.sum(jnp.where(head0, o2, 0.0), axis=-1, keepdims=True) / ATTN_HEAD_DIM
        ms1 = jnp.sum(jnp.where(head0, 0.0, o2), axis=-1, keepdims=True) / ATTN_HEAD_DIM
        y = o * lax.rsqrt(jnp.where(head0, ms0, ms1) + EPS) * gain
        o_ref[0, m * QBLK:(m + 1) * QBLK, :] = y.astype(o_ref.dtype)

    pieces = _cast_pieces(cast_in, cast_out)
    bufs = (s_a, s_b, s_c)
    depth = len(bufs)
    ahead = depth - 1
    for m in range(ahead):
        scores(m, bufs[m % depth])
    for m in range(n_steps):
        if m + ahead < n_steps:
            scores(m + ahead, bufs[(m + ahead) % depth])
        finish(m, bufs[m % depth])
        _cast_some(pieces, m, n_steps)


def _attention(proj_a, bias_vec, attn_gain, cast_weights):
    B, T, W3 = proj_a.shape
    W = W3 // 3
    n_pairs = W // LANES
    n_grid = n_pairs * B
    cast_specs = _cast_specs(cast_weights, n_grid, lambda h, b: h * B + b)
    outs = pl.pallas_call(
        functools.partial(_attn_kernel, n_chunks=T // CHUNK, n_cast=len(cast_weights)),
        out_shape=(jax.ShapeDtypeStruct((B, T, W), BF16),
                   *[jax.ShapeDtypeStruct(w.shape, BF16) for w in cast_weights]),
        grid=(n_pairs, B),
        in_specs=[pl.BlockSpec((1, T, LANES), lambda h, b: (b, 0, h)),
                  pl.BlockSpec((1, T, LANES), lambda h, b: (b, 0, n_pairs + h)),
                  pl.BlockSpec((1, T, LANES), lambda h, b: (b, 0, 2 * n_pairs + h)),
                  pl.BlockSpec((1, 2, BIAS_W), lambda h, b: (h, 0, 0)),
                  pl.BlockSpec((1, 1, LANES), lambda h, b: (h, 0, 0)),
                  *cast_specs],
        out_specs=(pl.BlockSpec((1, T, LANES), lambda h, b: (b, 0, h)), *cast_specs),
        scratch_shapes=[pltpu.VMEM((LANES, T), BF16),
                        pltpu.VMEM((LANES, T), BF16),
                        pltpu.VMEM((T, 2 * LANES), BF16),
                        pltpu.VMEM((T, 2 * LANES), BF16),
                        pltpu.VMEM((2, QBLK, KBLK), BF16),
                        pltpu.VMEM((QBLK, 2 * KBLK), F32),
                        pltpu.VMEM((QBLK, 2 * KBLK), F32),
                        pltpu.VMEM((QBLK, 2 * KBLK), F32)],
        compiler_params=_cparams(("parallel", "arbitrary"), 48),
        name="chunk_attention",
    )(proj_a, proj_a, proj_a, bias_vec, attn_gain, *cast_weights)
    return outs[0], outs[1:]


HG_CHUNK = 256
HG_LEVELS = (128, 64, 32, 16, 8, 4, 2, 1)
assert HG_LEVELS[-1] == 1


def _hgrn_kernel(q_ref, f_ref, i_ref, g_ref, lbl_ref, gn_ref, *rest, n_steps, n_cast):
    cast_in, o_ref, cast_out = rest[:n_cast], rest[n_cast], rest[n_cast + 1:]
    pieces = _cast_pieces(cast_in, cast_out)
    C = HG_CHUNK
    H2 = C // 2
    Dk = REC_HEAD_DIM
    nt = (((1,), (1,)), ((), ()))
    lbl = lbl_ref[0]
    e = jnp.exp(lbl - jnp.max(lbl, axis=0, keepdims=True))
    lb = e[0:1, :] / jnp.sum(e, axis=0, keepdims=True)
    c1 = 0.5 * (1.0 - lb)
    gn = gn_ref[...]

    r = lax.broadcasted_iota(jnp.int32, (C, C), 0)
    s = lax.broadcasted_iota(jnp.int32, (C, C), 1)
    tril = (s <= r).astype(BF16)
    rh = lax.broadcasted_iota(jnp.int32, (H2, H2), 0)
    sh = lax.broadcasted_iota(jnp.int32, (H2, H2), 1)
    lvl_mask = {m: ((rh // (2 * m)) == (sh // (2 * m))) & (((rh // m) % 2) == 1) & (((sh // m) % 2) == 0)
                for m in HG_LEVELS[1:]}
    sub = lax.broadcasted_iota(jnp.int32, (C // SUBLANES, SUBLANES, Dk), 1)

    def roll8(x, d):
        return pltpu.roll(x.reshape(C // SUBLANES, SUBLANES, Dk), d, 1)

    def front(n):
        rows = slice(n * C, (n + 1) * C)
        c1t = c1 * jnp.tanh(0.5 * f_ref[0, rows, :])
        f = (1.0 - c1) + c1t
        kk = c1 - c1t
        qq = _silu_tanh(q_ref[0, rows, :])
        ii = i_ref[0, rows, :]
        cat = jnp.concatenate(_split3(jnp.log2(f)), axis=1)
        return dict(rows=rows, kk=kk, qq=qq, ii=ii, ii_bf=ii.astype(BF16), cat=cat)

    def level_z(v, m):
        b, kk, qq = v["b"], v["kk"], v["qq"]
        if m >= SUBLANES:
            parts, srcs = [], []
            for p in range(0, C, 2 * m):
                bm = b[p + m - 1:p + m, :]
                parts += [bm - b[p:p + m], b[p + m:p + 2 * m] - bm]
                srcs += [kk[p:p + m], qq[p + m:p + 2 * m]]
            arg = jnp.concatenate(parts, axis=0)
            src = jnp.concatenate(srcs, axis=0)
        else:
            b3 = b.reshape(C // SUBLANES, SUBLANES, Dk)
            if m == 1:
                bm = jnp.where(sub % 2 == 1, roll8(b, 1), b3)
            else:
                bm = jnp.broadcast_to(b3[:, m - 1:m, :], b3.shape)
                for p in range(2 * m, SUBLANES, 2 * m):
                    bm = jnp.where(sub >= p, jnp.broadcast_to(b3[:, p + m - 1:p + m, :], b3.shape), bm)
            upper = (sub // m) % 2 == 1
            arg = ((b3 - bm) * jnp.where(upper, 1.0, -1.0)).reshape(C, Dk)
            src = jnp.where(upper, qq.reshape(b3.shape), kk.reshape(b3.shape)).reshape(C, Dk)
        return (src * jnp.exp2(arg)).astype(BF16)

    vs = [front(j) for j in range(n_steps)]
    bb = jnp.dot(tril, jnp.concatenate([v["cat"] for v in vs], axis=1), preferred_element_type=F32)
    for j, v in enumerate(vs):
        c0 = 3 * Dk * j
        v["b"] = bb[:, c0:c0 + Dk] + bb[:, c0 + Dk:c0 + 2 * Dk] + bb[:, c0 + 2 * Dk:c0 + 3 * Dk]
    a_lo = [None] * n_steps
    a_d0 = [jnp.zeros((H2, H2), F32)] * n_steps
    a_d1 = [jnp.zeros((H2, H2), F32)] * n_steps
    for li, m in enumerate(HG_LEVELS):
        for j, v in enumerate(vs):
            z = level_z(v, m)
            if li == 0:
                a_lo[j] = lax.dot_general(z[H2:], z[:H2], nt, preferred_element_type=F32)
            else:
                g = lax.dot_general(z, z, nt, preferred_element_type=F32)
                a_d0[j] = jnp.where(lvl_mask[m], g[:H2, :H2], a_d0[j])
                a_d1[j] = jnp.where(lvl_mask[m], g[H2:, H2:], a_d1[j])
    intra = []
    for j, v in enumerate(vs):
        a = jnp.concatenate([jnp.concatenate([a_d0[j], jnp.zeros((H2, H2), F32)], axis=1),
                             jnp.concatenate([a_lo[j], a_d1[j]], axis=1)], axis=0).astype(BF16)
        o_diag = jnp.sum(v["qq"] * v["kk"], axis=-1, keepdims=True) * v["ii"]
        intra.append(jnp.dot(a, v["ii_bf"], preferred_element_type=F32) + o_diag)
    st = jnp.zeros((Dk, Dk), F32)
    for j, v in enumerate(vs):
        b = v["b"]
        b_last = b[C - 1:C, :]
        qe = (v["qq"] * jnp.exp2(b)).astype(BF16)
        o = intra[j] + lax.dot_general(qe, st.astype(BF16), nt, preferred_element_type=F32)
        ke = (v["kk"] * jnp.exp2(b_last - b)).astype(BF16)
        st = st * jnp.exp2(b_last) + lax.dot_general(
            v["ii_bf"], ke, (((0,), (0,)), ((), ())), preferred_element_type=F32)
        ms = jnp.mean(o * o, axis=-1, keepdims=True)
        y = o * lax.rsqrt(ms + EPS) * gn
        y = y * _silu_tanh(g_ref[0, v["rows"], :])
        o_ref[0, v["rows"], :] = y.astype(o_ref.dtype)
        _cast_some(pieces, j, n_steps)


def _hgrn(proj_b, lb_logits_h, gnorm_g, cast_weights):
    B, T, W4 = proj_b.shape
    W = W4 // 4
    H = W // REC_HEAD_DIM
    n_slots = lb_logits_h.shape[1]
    blk = lambda off: pl.BlockSpec((1, T, REC_HEAD_DIM), lambda b, h, off=off: (b, 0, off * H + h))
    cast_specs = _cast_specs(cast_weights, B * H, lambda b, h: b * H + h)
    outs = pl.pallas_call(
        functools.partial(_hgrn_kernel, n_steps=T // HG_CHUNK, n_cast=len(cast_weights)),
        out_shape=(jax.ShapeDtypeStruct((B, T, W), BF16),
                   *[jax.ShapeDtypeStruct(w.shape, BF16) for w in cast_weights]),
        grid=(B, H),
        in_specs=[blk(0), blk(1), blk(2), blk(3),
                  pl.BlockSpec((1, n_slots, REC_HEAD_DIM), lambda b, h: (h, 0, 0)),
                  pl.BlockSpec((1, REC_HEAD_DIM), lambda b, h: (0, 0)),
                  *cast_specs],
        out_specs=(pl.BlockSpec((1, T, REC_HEAD_DIM), lambda b, h: (b, 0, h)), *cast_specs),
        compiler_params=_cparams(("parallel", "parallel"), 40),
        name="hgrn2",
    )(proj_b, proj_b, proj_b, proj_b, lb_logits_h, gnorm_g, *cast_weights)
    return outs[0], outs[1:]


def _outproj_kernel(oa_ref, ob_ref, w_ref, x_ref, mod_ref, g_ref, b_ref, x1_ref, *, alpha, sub):
    wa = oa_ref.shape[-1]
    gate1 = _mod_row(mod_ref, pl.program_id(0), 2)
    for r0 in range(0, x_ref.shape[1], sub):
        rs = slice(r0, r0 + sub)
        mix = jnp.dot(oa_ref[0, rs, :], w_ref[0:wa, :], preferred_element_type=F32)
        mix = mix + jnp.dot(ob_ref[0, rs, :], w_ref[wa:, :], preferred_element_type=F32)
        x1_ref[0, rs, :] = _ln_rows(alpha * x_ref[0, rs, :] + gate1 * mix) * g_ref[...] + b_ref[...]


def _outproj(o_a, o_b, w_o, x, mod, ln_g, ln_b, alpha, tm=512, sub=128):
    B, T, D = x.shape
    Wa, Wb = o_a.shape[-1], o_b.shape[-1]
    return pl.pallas_call(
        functools.partial(_outproj_kernel, alpha=alpha, sub=sub),
        out_shape=jax.ShapeDtypeStruct((B, T, D), F32),
        grid=(B, T // tm),
        in_specs=[pl.BlockSpec((1, tm, Wa), lambda b, i: (b, i, 0)),
                  pl.BlockSpec((1, tm, Wb), lambda b, i: (b, i, 0)),
                  pl.BlockSpec((Wa + Wb, D), lambda b, i: (0, 0)),
                  pl.BlockSpec((1, tm, D), lambda b, i: (b, i, 0)),
                  pl.BlockSpec((B, N_MOD * D), lambda b, i: (0, 0)),
                  pl.BlockSpec((1, D), lambda b, i: (0, 0)),
                  pl.BlockSpec((1, D), lambda b, i: (0, 0))],
        out_specs=pl.BlockSpec((1, tm, D), lambda b, i: (b, i, 0)),
        compiler_params=_cparams(("parallel", "parallel"), 48),
        name="out_proj_ln1",
    )(o_a, o_b, w_o, x, mod, ln_g, ln_b)


def _ffn_kernel(wg_ref, wu_ref, wo_ref, x_ref, mod_ref, g_ref, b_ref, o_ref, h_sc, *, alpha, sub):
    f = pl.program_id(2)
    last = pl.num_programs(2) - 1

    def partial_out(rs):
        h = h_sc[rs, :]
        gate = jnp.dot(h, wg_ref[...], preferred_element_type=F32)
        up = jnp.dot(h, wu_ref[...], preferred_element_type=F32)
        act = (_silu_tanh(gate) * up).astype(BF16)
        return jnp.dot(act, wo_ref[...], preferred_element_type=F32)

    everything = slice(0, o_ref.shape[1])

    @pl.when(f == 0)
    def _():
        shift2 = _mod_row(mod_ref, pl.program_id(0), 3)
        scale2 = _mod_row(mod_ref, pl.program_id(0), 4)
        for r0 in range(0, o_ref.shape[1], sub):
            rs = slice(r0, r0 + sub)
            h_sc[rs, :] = (_ln_rows(x_ref[0, rs, :]) * (1.0 + scale2) + shift2).astype(BF16)
        o_ref[0] = partial_out(everything)

    @pl.when((f > 0) & (f < last))
    def _():
        o_ref[0] += partial_out(everything)

    @pl.when(f == last)
    def _():
        gate2 = _mod_row(mod_ref, pl.program_id(0), 5)
        for r0 in range(0, o_ref.shape[1], sub):
            rs = slice(r0, r0 + sub)
            y = o_ref[0, rs, :] + partial_out(rs)
            o_ref[0, rs, :] = _ln_rows(alpha * x_ref[0, rs, :] + gate2 * y) * g_ref[...] + b_ref[...]


def _ffn(w_in, w_out, x1, mod, ln_g, ln_b, alpha, tm=1024, tf=512, sub=256):
    B, T, D = x1.shape
    F = w_out.shape[0]
    nf = F // tf
    return pl.pallas_call(
        functools.partial(_ffn_kernel, alpha=alpha, sub=sub),
        out_shape=jax.ShapeDtypeStruct((B, T, D), F32),
        grid=(B, T // tm, nf),
        in_specs=[pl.BlockSpec((D, tf), lambda b, i, f: (0, f)),
                  pl.BlockSpec((D, tf), lambda b, i, f: (0, nf + f)),
                  pl.BlockSpec((tf, D), lambda b, i, f: (f, 0)),
                  pl.BlockSpec((1, tm, D), lambda b, i, f: (b, i, 0)),
                  pl.BlockSpec((B, N_MOD * D), lambda b, i, f: (0, 0)),
                  pl.BlockSpec((1, D), lambda b, i, f: (0, 0)),
                  pl.BlockSpec((1, D), lambda b, i, f: (0, 0))],
        out_specs=pl.BlockSpec((1, tm, D), lambda b, i, f: (b, i, 0)),
        scratch_shapes=[pltpu.VMEM((tm, D), BF16)],
        compiler_params=_cparams(("parallel", "parallel", "arbitrary"), 60),
        name="swiglu_ffn_ln2",
    )(w_in, w_in, w_out, x1, mod, ln_g, ln_b)


def _bias_vectors(rel_bias):
    H, n_rel = rel_bias.shape
    max_rel = (n_rel - 1) // 2
    u = jnp.arange(BIAS_W)
    idx = jnp.clip(KBLK - u, -max_rel, max_rel) + max_rel
    return rel_bias[:, idx]


def kernel(x, c, w_ada, b_ada, w_in, rel_bias, attn_norm_g, lb_logits, gnorm_g, w_o,
           ln1_g, ln1_b, w_ffn_in, w_ffn_out, ln2_g, ln2_b):
    B, T, D = x.shape
    depth = w_ada.shape[0]
    alpha = (2 * depth) ** 0.25
    attn_w = attn_norm_g.shape[1]
    rec_w = lb_logits.shape[1]
    n_slots = lb_logits.shape[0]
    rec_heads = rec_w // REC_HEAD_DIM
    assert depth == 1 and n_slots == depth + 1
    for layer in range(depth):
        mod = _mod(c, w_ada[layer], b_ada[layer])
        h1 = _ln_mod(x, mod, 0).reshape(B * T, D)
        q_scale = jnp.where(jnp.arange(w_in.shape[2]) < attn_w, ATTN_HEAD_DIM ** -0.5 * LOG2E, 1.0)
        q_scale = q_scale.astype(F32).reshape(1, -1)
        proj_a = _matmul(h1, w_in[layer], q_scale, 0, 3 * attn_w, BF16).reshape(B, T, 3 * attn_w)
        proj_b = _matmul(h1, w_in[layer], q_scale, 3 * attn_w, 4 * rec_w, F32).reshape(B, T, 4 * rec_w)
        bias_vec = (_bias_vectors(rel_bias[layer]) * LOG2E).reshape(-1, 2, BIAS_W)
        o_a, (w_o_bf, w_ffn_out_bf) = _attention(
            proj_a, bias_vec, attn_norm_g[layer].reshape(-1, 1, LANES), (w_o[layer], w_ffn_out[layer]))
        lbl = lb_logits.reshape(n_slots, rec_heads, REC_HEAD_DIM).transpose(1, 0, 2)
        o_b, (w_ffn_in_bf,) = _hgrn(proj_b, lbl, gnorm_g[layer].reshape(1, REC_HEAD_DIM), (w_ffn_in[layer],))
        x = _outproj(o_a, o_b, w_o_bf, x, mod,
                     ln1_g[layer].reshape(1, D), ln1_b[layer].reshape(1, D), alpha)
        x = _ffn(w_ffn_in_bf, w_ffn_out_bf, x, mod,
                 ln2_g[layer].reshape(1, D), ln2_b[layer].reshape(1, D), alpha)
    return x
```

```python
import functools

import jax
import jax.numpy as jnp
from jax import lax
from jax.experimental import pallas as pl
from jax.experimental.pallas import tpu as pltpu

F32 = jnp.float32
BF16 = jnp.bfloat16

CHUNK = 64
N_PAST_CHUNKS = 8
BAND = (N_PAST_CHUNKS + 1) * CHUNK
ATTN_HEAD_DIM = 64
REC_HEAD_DIM = 128
N_MOD = 6
EPS = 1e-5
LANES = 128
SUBLANES = 8
BF16_ROWS = 16
QBLK = 2 * CHUNK
KBLK = BAND + CHUNK
BIAS_W = KBLK + QBLK

MIB = 1024 * 1024


def _cparams(sem, vmem_mib):
    return pltpu.CompilerParams(dimension_semantics=sem, vmem_limit_bytes=vmem_mib * MIB)


def _silu_tanh(x):
    h = 0.5 * x
    return h + h * jnp.tanh(h)


def _mod_row(mod_ref, b, r):
    d = mod_ref.shape[1] // N_MOD
    return mod_ref[pl.ds(b, 1), r * d:(r + 1) * d]


def _ln_rows(x):
    mu = jnp.mean(x, axis=-1, keepdims=True)
    xc = x - mu
    var = jnp.mean(xc * xc, axis=-1, keepdims=True)
    return xc * lax.rsqrt(var + EPS)


def _split3(x):
    hi = x.astype(BF16)
    r1 = x - hi.astype(F32)
    mid = r1.astype(BF16)
    lo = (r1 - mid.astype(F32)).astype(BF16)
    return hi, mid, lo


def _mod_kernel(c_ref, w_ref, b_ref, o_ref):
    act = jnp.concatenate(_split3(_silu_tanh(c_ref[...])), axis=0)
    w = w_ref[...]
    w_hi = w.astype(BF16)
    w_mid = (w - w_hi.astype(F32)).astype(BF16)
    acc = jnp.dot(act, w_hi, preferred_element_type=F32) + jnp.dot(act, w_mid, preferred_element_type=F32)
    out = acc[0:SUBLANES] + acc[SUBLANES:2 * SUBLANES] + acc[2 * SUBLANES:3 * SUBLANES]
    o_ref[...] = out[0:o_ref.shape[0]] + b_ref[...]


def _mod(c, w_ada, b_ada, tn=1024):
    B, D = c.shape
    N = w_ada.shape[1]
    assert B <= SUBLANES
    return pl.pallas_call(
        _mod_kernel,
        out_shape=jax.ShapeDtypeStruct((B, N), F32),
        grid=(N // tn,),
        in_specs=[pl.BlockSpec((SUBLANES, D), lambda j: (0, 0)),
                  pl.BlockSpec((D, tn), lambda j: (0, j)),
                  pl.BlockSpec((1, tn), lambda j: (0, j))],
        out_specs=pl.BlockSpec((B, tn), lambda j: (0, j)),
        compiler_params=_cparams(("parallel",), 40),
        name="adaln_mod",
    )(jnp.pad(c, ((0, SUBLANES - B), (0, 0))), w_ada, b_ada.reshape(1, N))


def _ln_mod_kernel(x_ref, mod_ref, o_ref, *, shift_row):
    y = _ln_rows(x_ref[0])
    b = pl.program_id(0)
    shift = _mod_row(mod_ref, b, shift_row)
    scale = _mod_row(mod_ref, b, shift_row + 1)
    o_ref[0] = (y * (1.0 + scale) + shift).astype(o_ref.dtype)


def _ln_mod(x, mod, shift_row, tm=1024):
    B, T, D = x.shape
    return pl.pallas_call(
        functools.partial(_ln_mod_kernel, shift_row=shift_row),
        out_shape=jax.ShapeDtypeStruct((B, T, D), BF16),
        grid=(B, T // tm),
        in_specs=[pl.BlockSpec((1, tm, D), lambda b, i: (b, i, 0)),
                  pl.BlockSpec((B, N_MOD * D), lambda b, i: (0, 0))],
        out_specs=pl.BlockSpec((1, tm, D), lambda b, i: (b, i, 0)),
        compiler_params=_cparams(("parallel", "parallel"), 32),
        name="ln_modulate",
    )(x, mod)


def _matmul_kernel(a_ref, w_ref, s_ref, o_ref, w_bf):
    @pl.when(pl.program_id(1) == 0)
    def _():
        w_bf[...] = (w_ref[...] * s_ref[...]).astype(BF16)

    o_ref[...] = jnp.dot(a_ref[...], w_bf[...], preferred_element_type=F32).astype(o_ref.dtype)


def _matmul(a, w, col_scale, col0, n_out, out_dtype, tm=1024, tn=1024):
    M, K = a.shape
    assert col0 % tn == 0 and n_out % tn == 0 and M % tm == 0
    col_block0 = col0 // tn
    return pl.pallas_call(
        _matmul_kernel,
        out_shape=jax.ShapeDtypeStruct((M, n_out), out_dtype),
        grid=(n_out // tn, M // tm),
        in_specs=[pl.BlockSpec((tm, K), lambda j, i: (i, 0)),
                  pl.BlockSpec((K, tn), lambda j, i: (0, j + col_block0)),
                  pl.BlockSpec((1, tn), lambda j, i: (0, j + col_block0))],
        out_specs=pl.BlockSpec((tm, tn), lambda j, i: (i, j)),
        scratch_shapes=[pltpu.VMEM((K, tn), BF16)],
        compiler_params=_cparams(("parallel", "arbitrary"), 48),
        name="in_proj",
    )(a, w, col_scale)


def _cast_specs(weights, n_grid, step_of):
    specs = []
    for w in weights:
        rows = w.shape[0] // n_grid
        assert w.shape[0] % n_grid == 0 and rows % BF16_ROWS == 0
        specs.append(pl.BlockSpec((rows, w.shape[1]), lambda *g: (step_of(*g), 0)))
    return specs


def _cast_pieces(cast_in, cast_out):
    return [(src_ref, dst_ref, r0) for src_ref, dst_ref in zip(cast_in, cast_out)
            for r0 in range(0, src_ref.shape[0], BF16_ROWS)]


def _cast_some(pieces, m, n):
    for src_ref, dst_ref, r0 in pieces[m * len(pieces) // n:(m + 1) * len(pieces) // n]:
        dst_ref[r0:r0 + BF16_ROWS, :] = src_ref[r0:r0 + BF16_ROWS, :].astype(BF16)


NEG_BIG = -1e30
LOG2E = 1.4426950408889634


def _attn_kernel(q_ref, k_ref, v_ref, bias_ref, gain_ref, *rest, n_chunks, n_cast):
    cast_in = rest[:n_cast]
    o_ref = rest[n_cast]
    cast_out = rest[n_cast + 1:2 * n_cast + 1]
    kta, ktb, vpa, vpb, tab, s_a, s_b, s_c = rest[2 * n_cast + 1:]
    T = n_chunks * CHUNK
    n_steps = T // QBLK
    head0 = lax.broadcasted_iota(jnp.int32, (QBLK, LANES), 1) < ATTN_HEAD_DIM
    m0 = jnp.where(head0, 1.0, 0.0).astype(BF16)
    m1 = jnp.where(head0, 0.0, 1.0).astype(BF16)

    head0_t = lax.broadcasted_iota(jnp.int32, (LANES, QBLK), 0) < ATTN_HEAD_DIM
    mt0 = jnp.where(head0_t, 1.0, 0.0).astype(BF16)
    mt1 = jnp.where(head0_t, 0.0, 1.0).astype(BF16)

    def prep(blk):
        rows = slice(blk * QBLK, (blk + 1) * QBLK)
        kt = k_ref[0, rows, :].T
        v = v_ref[0, rows, :]
        kta[:, rows] = kt * mt0
        ktb[:, rows] = kt * mt1
        vpa[rows, 0:LANES] = v * m0
        vpb[rows, 0:LANES] = v * m1
        vpa[rows, LANES:2 * LANES] = m0
        vpb[rows, LANES:2 * LANES] = m1

    @pl.when(pl.program_id(1) == 0)
    def _():
        qry = lax.broadcasted_iota(jnp.int32, (QBLK, KBLK), 0)
        key = lax.broadcasted_iota(jnp.int32, (QBLK, KBLK), 1)
        in_band = ((qry < CHUNK) & (key < BAND)) | ((qry >= CHUNK) & (key >= CHUNK))
        for hh in range(2):
            g = jnp.broadcast_to(bias_ref[0, hh:hh + 1, :], (QBLK, BIAS_W))
            t = pltpu.roll(g, BIAS_W - QBLK, 1, stride=1, stride_axis=0)[:, :KBLK]
            tab[hh] = jnp.where(in_band, t, NEG_BIG).astype(BF16)

    eye = (lax.broadcasted_iota(jnp.int32, (QBLK, QBLK), 0)
           == lax.broadcasted_iota(jnp.int32, (QBLK, QBLK), 1)).astype(BF16)
    gain = gain_ref[0]

    def band_of(m):
        hi = (m + 1) * QBLK
        lo = max(0, hi - KBLK)
        return slice(lo, hi), KBLK - (hi - lo)

    def scores(m, dst):
        prep(m)
        band, col0 = band_of(m)
        lhs = jnp.concatenate([q_ref[0, m * QBLK:(m + 1) * QBLK, :], eye], axis=1)
        rhs = jnp.concatenate([jnp.concatenate([kta[:, band], ktb[:, band]], axis=1),
                               jnp.concatenate([tab[0, :, col0:], tab[1, :, col0:]], axis=1)], axis=0)
        dst[:, 0:rhs.shape[1]] = jnp.dot(lhs, rhs, preferred_element_type=F32)

    def finish(m, src):
        band, col0 = band_of(m)
        w = KBLK - col0
        s = src[:, 0:2 * w]
        p = jnp.concatenate(
            [jnp.exp2(sh - jnp.max(sh, axis=-1, keepdims=True)) for sh in (s[:, 0:w], s[:, w:2 * w])],
            axis=1).astype(BF16)
        pv = jnp.dot(p, jnp.concatenate([vpa[band, :], vpb[band, :]], axis=0), preferred_element_type=F32)
        o = pv[:, 0:LANES] * (1.0 / pv[:, LANES:2 * LANES])
        o2 = o * o
        ms0 = jnp.sum(jnp.where(head0, o2, 0.0), axis=-1, keepdims=True) / ATTN_HEAD_DIM
        ms1 = jnp.sum(jnp.where(head0, 0.0, o2), axis=-1, keepdims=True) / ATTN_HEAD_DIM
        y = o * lax.rsqrt(jnp.where(head0, ms0, ms1) + EPS) * gain
        o_ref[0, m * QBLK:(m + 1) * QBLK, :] = y.astype(o_ref.dtype)

    pieces = _cast_pieces(cast_in, cast_out)
    bufs = (s_a, s_b, s_c)
    depth = len(bufs)
    ahead = depth - 1
    for m in range(ahead):
        scores(m, bufs[m % depth])
    for m in range(n_steps):
        if m + ahead < n_steps:
            scores(m + ahead, bufs[(m + ahead) % depth])
        finish(m, bufs[m % depth])
        _cast_some(pieces, m, n_steps)


def _attention(proj_a, bias_vec, attn_gain, cast_weights):
    B, T, W3 = proj_a.shape
    W = W3 // 3
    n_pairs = W // LANES
    n_grid = n_pairs * B
    cast_specs = _cast_specs(cast_weights, n_grid, lambda h, b: h * B + b)
    outs = pl.pallas_call(
        functools.partial(_attn_kernel, n_chunks=T // CHUNK, n_cast=len(cast_weights)),
        out_shape=(jax.ShapeDtypeStruct((B, T, W), BF16),
                   *[jax.ShapeDtypeStruct(w.shape, BF16) for w in cast_weights]),
        grid=(n_pairs, B),
        in_specs=[pl.BlockSpec((1, T, LANES), lambda h, b: (b, 0, h)),
                  pl.BlockSpec((1, T, LANES), lambda h, b: (b, 0, n_pairs + h)),
                  pl.BlockSpec((1, T, LANES), lambda h, b: (b, 0, 2 * n_pairs + h)),
                  pl.BlockSpec((1, 2, BIAS_W), lambda h, b: (h, 0, 0)),
                  pl.BlockSpec((1, 1, LANES), lambda h, b: (h, 0, 0)),
                  *cast_specs],
        out_specs=(pl.BlockSpec((1, T, LANES), lambda h, b: (b, 0, h)), *cast_specs),
        scratch_shapes=[pltpu.VMEM((LANES, T), BF16),
                        pltpu.VMEM((LANES, T), BF16),
                        pltpu.VMEM((T, 2 * LANES), BF16),
                        pltpu.VMEM((T, 2 * LANES), BF16),
                        pltpu.VMEM((2, QBLK, KBLK), BF16),
                        pltpu.VMEM((QBLK, 2 * KBLK), F32),
                        pltpu.VMEM((QBLK, 2 * KBLK), F32),
                        pltpu.VMEM((QBLK, 2 * KBLK), F32)],
        compiler_params=_cparams(("parallel", "arbitrary"), 48),
        name="chunk_attention",
    )(proj_a, proj_a, proj_a, bias_vec, attn_gain, *cast_weights)
    return outs[0], outs[1:]


HG_CHUNK = 256
HG_LEVELS = (128, 64, 32, 16, 8, 4, 2, 1)
assert HG_LEVELS[-1] == 1


def _hgrn_kernel(q_ref, f_ref, i_ref, g_ref, lbl_ref, gn_ref, *rest, n_steps, n_cast):
    cast_in, o_ref, cast_out = rest[:n_cast], rest[n_cast], rest[n_cast + 1:]
    pieces = _cast_pieces(cast_in, cast_out)
    C = HG_CHUNK
    H2 = C // 2
    Dk = REC_HEAD_DIM
    nt = (((1,), (1,)), ((), ()))
    lbl = lbl_ref[0]
    e = jnp.exp(lbl - jnp.max(lbl, axis=0, keepdims=True))
    lb = e[0:1, :] / jnp.sum(e, axis=0, keepdims=True)
    c1 = 0.5 * (1.0 - lb)
    gn = gn_ref[...]

    r = lax.broadcasted_iota(jnp.int32, (C, C), 0)
    s = lax.broadcasted_iota(jnp.int32, (C, C), 1)
    tril = (s <= r).astype(BF16)
    rh = lax.broadcasted_iota(jnp.int32, (H2, H2), 0)
    sh = lax.broadcasted_iota(jnp.int32, (H2, H2), 1)
    lvl_mask = {m: ((rh // (2 * m)) == (sh // (2 * m))) & (((rh // m) % 2) == 1) & (((sh // m) % 2) == 0)
                for m in HG_LEVELS[1:]}
    sub = lax.broadcasted_iota(jnp.int32, (C // SUBLANES, SUBLANES, Dk), 1)

    def roll8(x, d):
        return pltpu.roll(x.reshape(C // SUBLANES, SUBLANES, Dk), d, 1)

    def front(n):
        rows = slice(n * C, (n + 1) * C)
        c1t = c1 * jnp.tanh(0.5 * f_ref[0, rows, :])
        f = (1.0 - c1) + c1t
        kk = c1 - c1t
        qq = _silu_tanh(q_ref[0, rows, :])
        ii = i_ref[0, rows, :]
        cat = jnp.concatenate(_split3(jnp.log2(f)), axis=1)
        return dict(rows=rows, kk=kk, qq=qq, ii=ii, ii_bf=ii.astype(BF16), cat=cat)

    def level_z(v, m):
        b, kk, qq = v["b"], v["kk"], v["qq"]
        if m >= SUBLANES:
            parts, srcs = [], []
            for p in range(0, C, 2 * m):
                bm = b[p + m - 1:p + m, :]
                parts += [bm - b[p:p + m], b[p + m:p + 2 * m] - bm]
                srcs += [kk[p:p + m], qq[p + m:p + 2 * m]]
            arg = jnp.concatenate(parts, axis=0)
            src = jnp.concatenate(srcs, axis=0)
        else:
            b3 = b.reshape(C // SUBLANES, SUBLANES, Dk)
            if m == 1:
                bm = jnp.where(sub % 2 == 1, roll8(b, 1), b3)
            else:
                bm = jnp.broadcast_to(b3[:, m - 1:m, :], b3.shape)
                for p in range(2 * m, SUBLANES, 2 * m):
                    bm = jnp.where(sub >= p, jnp.broadcast_to(b3[:, p + m - 1:p + m, :], b3.shape), bm)
            upper = (sub // m) % 2 == 1
            arg = ((b3 - bm) * jnp.where(upper, 1.0, -1.0)).reshape(C, Dk)
            src = jnp.where(upper, qq.reshape(b3.shape), kk.reshape(b3.shape)).reshape(C, Dk)
        return (src * jnp.exp2(arg)).astype(BF16)

    vs = [front(j) for j in range(n_steps)]
    bb = jnp.dot(tril, jnp.concatenate([v["cat"] for v in vs], axis=1), preferred_element_type=F32)
    for j, v in enumerate(vs):
        c0 = 3 * Dk * j
        v["b"] = bb[:, c0:c0 + Dk] + bb[:, c0 + Dk:c0 + 2 * Dk] + bb[:, c0 + 2 * Dk:c0 + 3 * Dk]
    a_lo = [None] * n_steps
    a_d0 = [jnp.zeros((H2, H2), F32)] * n_steps
    a_d1 = [jnp.zeros((H2, H2), F32)] * n_steps
    for li, m in enumerate(HG_LEVELS):
        for j, v in enumerate(vs):
            z = level_z(v, m)
            if li == 0:
                a_lo[j] = lax.dot_general(z[H2:], z[:H2], nt, preferred_element_type=F32)
            else:
                g = lax.dot_general(z, z, nt, preferred_element_type=F32)
                a_d0[j] = jnp.where(lvl_mask[m], g[:H2, :H2], a_d0[j])
                a_d1[j] = jnp.where(lvl_mask[m], g[H2:, H2:], a_d1[j])
    intra = []
    for j, v in enumerate(vs):
        a = jnp.concatenate([jnp.concatenate([a_d0[j], jnp.zeros((H2, H2), F32)], axis=1),
                             jnp.concatenate([a_lo[j], a_d1[j]], axis=1)], axis=0).astype(BF16)
        o_diag = jnp.sum(v["qq"] * v["kk"], axis=-1, keepdims=True) * v["ii"]
        intra.append(jnp.dot(a, v["ii_bf"], preferred_element_type=F32) + o_diag)
    st = jnp.zeros((Dk, Dk), F32)
    for j, v in enumerate(vs):
        b = v["b"]
        b_last = b[C - 1:C, :]
        qe = (v["qq"] * jnp.exp2(b)).astype(BF16)
        o = intra[j] + lax.dot_general(qe, st.astype(BF16), nt, preferred_element_type=F32)
        ke = (v["kk"] * jnp.exp2(b_last - b)).astype(BF16)
        st = st * jnp.exp2(b_last) + lax.dot_general(
            v["ii_bf"], ke, (((0,), (0,)), ((), ())), preferred_element_type=F32)
        ms = jnp.mean(o * o, axis=-1, keepdims=True)
        y = o * lax.rsqrt(ms + EPS) * gn
        y = y * _silu_tanh(g_ref[0, v["rows"], :])
        o_ref[0, v["rows"], :] = y.astype(o_ref.dtype)
        _cast_some(pieces, j, n_steps)


def _hgrn(proj_b, lb_logits_h, gnorm_g, cast_weights):
    B, T, W4 = proj_b.shape
    W = W4 // 4
    H = W // REC_HEAD_DIM
    n_slots = lb_logits_h.shape[1]
    blk = lambda off: pl.BlockSpec((1, T, REC_HEAD_DIM), lambda b, h, off=off: (b, 0, off * H + h))
    cast_specs = _cast_specs(cast_weights, B * H, lambda b, h: b * H + h)
    outs = pl.pallas_call(
        functools.partial(_hgrn_kernel, n_steps=T // HG_CHUNK, n_cast=len(cast_weights)),
        out_shape=(jax.ShapeDtypeStruct((B, T, W), BF16),
                   *[jax.ShapeDtypeStruct(w.shape, BF16) for w in cast_weights]),
        grid=(B, H),
        in_specs=[blk(0), blk(1), blk(2), blk(3),
                  pl.BlockSpec((1, n_slots, REC_HEAD_DIM), lambda b, h: (h, 0, 0)),
                  pl.BlockSpec((1, REC_HEAD_DIM), lambda b, h: (0, 0)),
                  *cast_specs],
        out_specs=(pl.BlockSpec((1, T, REC_HEAD_DIM), lambda b, h: (b, 0, h)), *cast_specs),
        compiler_params=_cparams(("parallel", "parallel"), 40),
        name="hgrn2",
    )(proj_b, proj_b, proj_b, proj_b, lb_logits_h, gnorm_g, *cast_weights)
    return outs[0], outs[1:]


def _outproj_kernel(oa_ref, ob_ref, w_ref, x_ref, mod_ref, g_ref, b_ref, x1_ref, *, alpha, sub):
    gate1 = _mod_row(mod_ref, pl.program_id(0), 2)
    for r0 in range(0, x_ref.shape[1], sub):
        rs = slice(r0, r0 + sub)
        o_cat = jnp.concatenate([oa_ref[0, rs, :], ob_ref[0, rs, :]], axis=1)
        mix = jnp.dot(o_cat, w_ref[...], preferred_element_type=F32)
        x1_ref[0, rs, :] = _ln_rows(alpha * x_ref[0, rs, :] + gate1 * mix) * g_ref[...] + b_ref[...]


def _outproj(o_a, o_b, w_o, x, mod, ln_g, ln_b, alpha, tm=512, sub=128):
    B, T, D = x.shape
    Wa, Wb = o_a.shape[-1], o_b.shape[-1]
    return pl.pallas_call(
        functools.partial(_outproj_kernel, alpha=alpha, sub=sub),
        out_shape=jax.ShapeDtypeStruct((B, T, D), F32),
        grid=(B, T // tm),
        in_specs=[pl.BlockSpec((1, tm, Wa), lambda b, i: (b, i, 0)),
                  pl.BlockSpec((1, tm, Wb), lambda b, i: (b, i, 0)),
                  pl.BlockSpec((Wa + Wb, D), lambda b, i: (0, 0)),
                  pl.BlockSpec((1, tm, D), lambda b, i: (b, i, 0)),
                  pl.BlockSpec((B, N_MOD * D), lambda b, i: (0, 0)),
                  pl.BlockSpec((1, D), lambda b, i: (0, 0)),
                  pl.BlockSpec((1, D), lambda b, i: (0, 0))],
        out_specs=pl.BlockSpec((1, tm, D), lambda b, i: (b, i, 0)),
        compiler_params=_cparams(("parallel", "parallel"), 48),
        name="out_proj_ln1",
    )(o_a, o_b, w_o, x, mod, ln_g, ln_b)


def _ffn_kernel(wg_ref, wu_ref, wo_ref, x_ref, mod_ref, g_ref, b_ref, o_ref, h_sc, *, alpha, sub):
    f = pl.program_id(2)
    last = pl.num_programs(2) - 1

    def partial_out(rs):
        h = h_sc[rs, :]
        gate = jnp.dot(h, wg_ref[...], preferred_element_type=F32)
        up = jnp.dot(h, wu_ref[...], preferred_element_type=F32)
        act = (_silu_tanh(gate) * up).astype(BF16)
        return jnp.dot(act, wo_ref[...], preferred_element_type=F32)

    everything = slice(0, o_ref.shape[1])

    @pl.when(f == 0)
    def _():
        shift2 = _mod_row(mod_ref, pl.program_id(0), 3)
        scale2 = _mod_row(mod_ref, pl.program_id(0), 4)
        for r0 in range(0, o_ref.shape[1], sub):
            rs = slice(r0, r0 + sub)
            h_sc[rs, :] = (_ln_rows(x_ref[0, rs, :]) * (1.0 + scale2) + shift2).astype(BF16)
        o_ref[0] = partial_out(everything)

    @pl.when((f > 0) & (f < last))
    def _():
        o_ref[0] += partial_out(everything)

    @pl.when(f == last)
    def _():
        gate2 = _mod_row(mod_ref, pl.program_id(0), 5)
        for r0 in range(0, o_ref.shape[1], sub):
            rs = slice(r0, r0 + sub)
            y = o_ref[0, rs, :] + partial_out(rs)
            o_ref[0, rs, :] = _ln_rows(alpha * x_ref[0, rs, :] + gate2 * y) * g_ref[...] + b_ref[...]


def _ffn(w_in, w_out, x1, mod, ln_g, ln_b, alpha, tm=1024, tf=512, sub=256):
    B, T, D = x1.shape
    F = w_out.shape[0]
    nf = F // tf
    return pl.pallas_call(
        functools.partial(_ffn_kernel, alpha=alpha, sub=sub),
        out_shape=jax.ShapeDtypeStruct((B, T, D), F32),
        grid=(B, T // tm, nf),
        in_specs=[pl.BlockSpec((D, tf), lambda b, i, f: (0, f)),
                  pl.BlockSpec((D, tf), lambda b, i, f: (0, nf + f)),
                  pl.BlockSpec((tf, D), lambda b, i, f: (f, 0)),
                  pl.BlockSpec((1, tm, D), lambda b, i, f: (b, i, 0)),
                  pl.BlockSpec((B, N_MOD * D), lambda b, i, f: (0, 0)),
                  pl.BlockSpec((1, D), lambda b, i, f: (0, 0)),
                  pl.BlockSpec((1, D), lambda b, i, f: (0, 0))],
        out_specs=pl.BlockSpec((1, tm, D), lambda b, i, f: (b, i, 0)),
        scratch_shapes=[pltpu.VMEM((tm, D), BF16)],
        compiler_params=_cparams(("parallel", "parallel", "arbitrary"), 60),
        name="swiglu_ffn_ln2",
    )(w_in, w_in, w_out, x1, mod, ln_g, ln_b)


def _bias_vectors(rel_bias):
    H, n_rel = rel_bias.shape
    max_rel = (n_rel - 1) // 2
    u = jnp.arange(BIAS_W)
    idx = jnp.clip(KBLK - u, -max_rel, max_rel) + max_rel
    return rel_bias[:, idx]


def kernel(x, c, w_ada, b_ada, w_in, rel_bias, attn_norm_g, lb_logits, gnorm_g, w_o,
           ln1_g, ln1_b, w_ffn_in, w_ffn_out, ln2_g, ln2_b):
    B, T, D = x.shape
    depth = w_ada.shape[0]
    alpha = (2 * depth) ** 0.25
    attn_w = attn_norm_g.shape[1]
    rec_w = lb_logits.shape[1]
    n_slots = lb_logits.shape[0]
    rec_heads = rec_w // REC_HEAD_DIM
    assert depth == 1 and n_slots == depth + 1
    for layer in range(depth):
        mod = _mod(c, w_ada[layer], b_ada[layer])
        h1 = _ln_mod(x, mod, 0).reshape(B * T, D)
        q_scale = jnp.where(jnp.arange(w_in.shape[2]) < attn_w, ATTN_HEAD_DIM ** -0.5 * LOG2E, 1.0)
        q_scale = q_scale.astype(F32).reshape(1, -1)
        proj_a = _matmul(h1, w_in[layer], q_scale, 0, 3 * attn_w, BF16).reshape(B, T, 3 * attn_w)
        proj_b = _matmul(h1, w_in[layer], q_scale, 3 * attn_w, 4 * rec_w, F32).reshape(B, T, 4 * rec_w)
        bias_vec = (_bias_vectors(rel_bias[layer]) * LOG2E).reshape(-1, 2, BIAS_W)
        o_a, (w_o_bf, w_ffn_out_bf) = _attention(
            proj_a, bias_vec, attn_norm_g[layer].reshape(-1, 1, LANES), (w_o[layer], w_ffn_out[layer]))
        lbl = lb_logits.reshape(n_slots, rec_heads, REC_HEAD_DIM).transpose(1, 0, 2)
        o_b, (w_ffn_in_bf,) = _hgrn(proj_b, lbl, gnorm_g[layer].reshape(1, REC_HEAD_DIM), (w_ffn_in[layer],))
        x = _outproj(o_a, o_b, w_o_bf, x, mod,
                     ln1_g[layer].reshape(1, D), ln1_b[layer].reshape(1, D), alpha)
        x = _ffn(w_ffn_in_bf, w_ffn_out_bf, x, mod,
                 ln2_g[layer].reshape(1, D), ln2_b[layer].reshape(1, D), alpha)
    return x
```

```python
import functools

import jax
import jax.numpy as jnp
from jax import lax
from jax.experimental import pallas as pl
from jax.experimental.pallas import tpu as pltpu

F32 = jnp.float32
BF16 = jnp.bfloat16

CHUNK = 64
N_PAST_CHUNKS = 8
BAND = (N_PAST_CHUNKS + 1) * CHUNK
ATTN_HEAD_DIM = 64
REC_HEAD_DIM = 128
N_MOD = 6
EPS = 1e-5
LANES = 128
SUBLANES = 8
BF16_ROWS = 16
QBLK = 2 * CHUNK
KBLK = BAND + CHUNK
BIAS_W = KBLK + QBLK

MIB = 1024 * 1024


def _cparams(sem, vmem_mib):
    return pltpu.CompilerParams(dimension_semantics=sem, vmem_limit_bytes=vmem_mib * MIB)


def _silu_tanh(x):
    h = 0.5 * x
    return h + h * jnp.tanh(h)


def _mod_row(mod_ref, b, r):
    d = mod_ref.shape[1] // N_MOD
    return mod_ref[pl.ds(b, 1), r * d:(r + 1) * d]


def _ln_rows(x):
    mu = jnp.mean(x, axis=-1, keepdims=True)
    xc = x - mu
    var = jnp.mean(xc * xc, axis=-1, keepdims=True)
    return xc * lax.rsqrt(var + EPS)


def _split3(x):
    hi = x.astype(BF16)
    r1 = x - hi.astype(F32)
    mid = r1.astype(BF16)
    lo = (r1 - mid.astype(F32)).astype(BF16)
    return hi, mid, lo


def _mod_kernel(c_ref, w_ref, b_ref, o_ref):
    act = jnp.concatenate(_split3(_silu_tanh(c_ref[...])), axis=0)
    w = w_ref[...]
    w_hi = w.astype(BF16)
    w_mid = (w - w_hi.astype(F32)).astype(BF16)
    acc = jnp.dot(act, w_hi, preferred_element_type=F32) + jnp.dot(act, w_mid, preferred_element_type=F32)
    out = acc[0:SUBLANES] + acc[SUBLANES:2 * SUBLANES] + acc[2 * SUBLANES:3 * SUBLANES]
    o_ref[...] = out[0:o_ref.shape[0]] + b_ref[...]


def _mod(c, w_ada, b_ada, tn=1024):
    B, D = c.shape
    N = w_ada.shape[1]
    assert B <= SUBLANES
    return pl.pallas_call(
        _mod_kernel,
        out_shape=jax.ShapeDtypeStruct((B, N), F32),
        grid=(N // tn,),
        in_specs=[pl.BlockSpec((SUBLANES, D), lambda j: (0, 0)),
                  pl.BlockSpec((D, tn), lambda j: (0, j)),
                  pl.BlockSpec((1, tn), lambda j: (0, j))],
        out_specs=pl.BlockSpec((B, tn), lambda j: (0, j)),
        compiler_params=_cparams(("parallel",), 40),
        name="adaln_mod",
    )(jnp.pad(c, ((0, SUBLANES - B), (0, 0))), w_ada, b_ada.reshape(1, N))


def _ln_proj_kernel(x_ref, mod_ref, w_ref, s_ref, h_ref, o_ref, w_bf, *, sub):
    b = pl.program_id(0)

    @pl.when((b == 0) & (pl.program_id(1) == 0))
    def _():
        w_bf[...] = (w_ref[...] * s_ref[...]).astype(BF16)

    shift = _mod_row(mod_ref, b, 0)
    scale = _mod_row(mod_ref, b, 1)
    for r0 in range(0, x_ref.shape[1], sub):
        rs = slice(r0, r0 + sub)
        h_ref[0, rs, :] = (_ln_rows(x_ref[0, rs, :]) * (1.0 + scale) + shift).astype(BF16)
    o_ref[0] = jnp.dot(h_ref[0], w_bf[...], preferred_element_type=F32).astype(o_ref.dtype)


def _ln_proj(x, mod, w, col_scale, tm=1024, tn=1024, sub=256):
    B, T, D = x.shape
    return pl.pallas_call(
        functools.partial(_ln_proj_kernel, sub=sub),
        out_shape=(jax.ShapeDtypeStruct((B, T, D), BF16), jax.ShapeDtypeStruct((B, T, tn), BF16)),
        grid=(B, T // tm),
        in_specs=[pl.BlockSpec((1, tm, D), lambda b, i: (b, i, 0)),
                  pl.BlockSpec((B, N_MOD * D), lambda b, i: (0, 0)),
                  pl.BlockSpec((D, tn), lambda b, i: (0, 0)),
                  pl.BlockSpec((1, tn), lambda b, i: (0, 0))],
        out_specs=(pl.BlockSpec((1, tm, D), lambda b, i: (b, i, 0)),
                   pl.BlockSpec((1, tm, tn), lambda b, i: (b, i, 0))),
        scratch_shapes=[pltpu.VMEM((D, tn), BF16)],
        compiler_params=_cparams(("arbitrary", "arbitrary"), 56),
        name="ln_in_proj",
    )(x, mod, w, col_scale)


def _matmul_kernel(a_ref, w_ref, s_ref, o_ref, w_bf):
    @pl.when(pl.program_id(1) == 0)
    def _():
        w_bf[...] = (w_ref[...] * s_ref[...]).astype(BF16)

    o_ref[...] = jnp.dot(a_ref[...], w_bf[...], preferred_element_type=F32).astype(o_ref.dtype)


def _matmul(a, w, col_scale, col0, n_out, out_dtype, tm=1024, tn=1024):
    M, K = a.shape
    assert col0 % tn == 0 and n_out % tn == 0 and M % tm == 0
    col_block0 = col0 // tn
    return pl.pallas_call(
        _matmul_kernel,
        out_shape=jax.ShapeDtypeStruct((M, n_out), out_dtype),
        grid=(n_out // tn, M // tm),
        in_specs=[pl.BlockSpec((tm, K), lambda j, i: (i, 0)),
                  pl.BlockSpec((K, tn), lambda j, i: (0, j + col_block0)),
                  pl.BlockSpec((1, tn), lambda j, i: (0, j + col_block0))],
        out_specs=pl.BlockSpec((tm, tn), lambda j, i: (i, j)),
        scratch_shapes=[pltpu.VMEM((K, tn), BF16)],
        compiler_params=_cparams(("parallel", "arbitrary"), 48),
        name="in_proj",
    )(a, w, col_scale)


def _cast_specs(weights, n_grid, step_of):
    specs = []
    for w in weights:
        rows = w.shape[0] // n_grid
        assert w.shape[0] % n_grid == 0 and rows % BF16_ROWS == 0
        specs.append(pl.BlockSpec((rows, w.shape[1]), lambda *g: (step_of(*g), 0)))
    return specs


def _cast_pieces(cast_in, cast_out):
    return [(src_ref, dst_ref, r0) for src_ref, dst_ref in zip(cast_in, cast_out)
            for r0 in range(0, src_ref.shape[0], BF16_ROWS)]


def _cast_some(pieces, m, n):
    for src_ref, dst_ref, r0 in pieces[m * len(pieces) // n:(m + 1) * len(pieces) // n]:
        dst_ref[r0:r0 + BF16_ROWS, :] = src_ref[r0:r0 + BF16_ROWS, :].astype(BF16)


NEG_BIG = -1e30
LOG2E = 1.4426950408889634


def _attn_kernel(q_ref, k_ref, v_ref, bias_ref, gain_ref, *rest, n_chunks, n_cast):
    cast_in = rest[:n_cast]
    o_ref = rest[n_cast]
    cast_out = rest[n_cast + 1:2 * n_cast + 1]
    kta, ktb, vpa, vpb, tab, s_a, s_b, s_c = rest[2 * n_cast + 1:]
    T = n_chunks * CHUNK
    n_steps = T // QBLK
    head0 = lax.broadcasted_iota(jnp.int32, (QBLK, LANES), 1) < ATTN_HEAD_DIM
    m0 = jnp.where(head0, 1.0, 0.0).astype(BF16)
    m1 = jnp.where(head0, 0.0, 1.0).astype(BF16)

    head0_t = lax.broadcasted_iota(jnp.int32, (LANES, QBLK), 0) < ATTN_HEAD_DIM
    mt0 = jnp.where(head0_t, 1.0, 0.0).astype(BF16)
    mt1 = jnp.where(head0_t, 0.0, 1.0).astype(BF16)

    def prep(blk):
        rows = slice(blk * QBLK, (blk + 1) * QBLK)
        kt = k_ref[0, rows, :].T
        v = v_ref[0, rows, :]
        kta[:, rows] = kt * mt0
        ktb[:, rows] = kt * mt1
        vpa[rows, 0:LANES] = v * m0
        vpb[rows, 0:LANES] = v * m1
        vpa[rows, LANES:2 * LANES] = m0
        vpb[rows, LANES:2 * LANES] = m1

    @pl.when(pl.program_id(1) == 0)
    def _():
        qry = lax.broadcasted_iota(jnp.int32, (QBLK, KBLK), 0)
        key = lax.broadcasted_iota(jnp.int32, (QBLK, KBLK), 1)
        in_band = ((qry < CHUNK) & (key < BAND)) | ((qry >= CHUNK) & (key >= CHUNK))
        for hh in range(2):
            g = jnp.broadcast_to(bias_ref[0, hh:hh + 1, :], (QBLK, BIAS_W))
            t = pltpu.roll(g, BIAS_W - QBLK, 1, stride=1, stride_axis=0)[:, :KBLK]
            tab[hh] = jnp.where(in_band, t, NEG_BIG).astype(BF16)

    eye = (lax.broadcasted_iota(jnp.int32, (QBLK, QBLK), 0)
           == lax.broadcasted_iota(jnp.int32, (QBLK, QBLK), 1)).astype(BF16)
    gain = gain_ref[0]

    def band_of(m):
        hi = (m + 1) * QBLK
        lo = max(0, hi - KBLK)
        return slice(lo, hi), KBLK - (hi - lo)

    def scores(m, dst):
        prep(m)
        band, col0 = band_of(m)
        lhs = jnp.concatenate([q_ref[0, m * QBLK:(m + 1) * QBLK, :], eye], axis=1)
        rhs = jnp.concatenate([jnp.concatenate([kta[:, band], ktb[:, band]], axis=1),
                               jnp.concatenate([tab[0, :, col0:], tab[1, :, col0:]], axis=1)], axis=0)
        dst[:, 0:rhs.shape[1]] = jnp.dot(lhs, rhs, preferred_element_type=F32)

    def finish(m, src):
        band, col0 = band_of(m)
        w = KBLK - col0
        s = src[:, 0:2 * w]
        p = jnp.concatenate(
            [jnp.exp2(sh - jnp.max(sh, axis=-1, keepdims=True)) for sh in (s[:, 0:w], s[:, w:2 * w])],
            axis=1).astype(BF16)
        pv = jnp.dot(p, jnp.concatenate([vpa[band, :], vpb[band, :]], axis=0), preferred_element_type=F32)
        o = pv[:, 0:LANES] * (1.0 / pv[:, LANES:2 * LANES])
        o2 = o * o
        ms0 = jnp.sum(jnp.where(head0, o2, 0.0), axis=-1, keepdims=True) / ATTN_HEAD_DIM
        ms1 = jnp.sum(jnp.where(head0, 0.0, o2), axis=-1, keepdims=True) / ATTN_HEAD_DIM
        y = o * lax.rsqrt(jnp.where(head0, ms0, ms1) + EPS) * gain
        o_ref[0, m * QBLK:(m + 1) * QBLK, :] = y.astype(o_ref.dtype)

    pieces = _cast_pieces(cast_in, cast_out)
    bufs = (s_a, s_b, s_c)
    depth = len(bufs)
    ahead = depth - 1
    for m in range(ahead):
        scores(m, bufs[m % depth])
    for m in range(n_steps):
        if m + ahead < n_steps:
            scores(m + ahead, bufs[(m + ahead) % depth])
        finish(m, bufs[m % depth])
        _cast_some(pieces, m, n_steps)


def _attention(q_arr, kv_arr, bias_vec, attn_gain, cast_weights):
    B, T, W = q_arr.shape
    n_pairs = W // LANES
    n_grid = n_pairs * B
    cast_specs = _cast_specs(cast_weights, n_grid, lambda h, b: h * B + b)
    outs = pl.pallas_call(
        functools.partial(_attn_kernel, n_chunks=T // CHUNK, n_cast=len(cast_weights)),
        out_shape=(jax.ShapeDtypeStruct((B, T, W), BF16),
                   *[jax.ShapeDtypeStruct(w.shape, BF16) for w in cast_weights]),
        grid=(n_pairs, B),
        in_specs=[pl.BlockSpec((1, T, LANES), lambda h, b: (b, 0, h)),
                  pl.BlockSpec((1, T, LANES), lambda h, b: (b, 0, h)),
                  pl.BlockSpec((1, T, LANES), lambda h, b: (b, 0, n_pairs + h)),
                  pl.BlockSpec((1, 2, BIAS_W), lambda h, b: (h, 0, 0)),
                  pl.BlockSpec((1, 1, LANES), lambda h, b: (h, 0, 0)),
                  *cast_specs],
        out_specs=(pl.BlockSpec((1, T, LANES), lambda h, b: (b, 0, h)), *cast_specs),
        scratch_shapes=[pltpu.VMEM((LANES, T), BF16),
                        pltpu.VMEM((LANES, T), BF16),
                        pltpu.VMEM((T, 2 * LANES), BF16),
                        pltpu.VMEM((T, 2 * LANES), BF16),
                        pltpu.VMEM((2, QBLK, KBLK), BF16),
                        pltpu.VMEM((QBLK, 2 * KBLK), F32),
                        pltpu.VMEM((QBLK, 2 * KBLK), F32),
                        pltpu.VMEM((QBLK, 2 * KBLK), F32)],
        compiler_params=_cparams(("parallel", "arbitrary"), 48),
        name="chunk_attention",
    )(q_arr, kv_arr, kv_arr, bias_vec, attn_gain, *cast_weights)
    return outs[0], outs[1:]


HG_CHUNK = 256
HG_LEVELS = (128, 64, 32, 16, 8, 4, 2, 1)
assert HG_LEVELS[-1] == 1


def _hgrn_kernel(q_ref, f_ref, i_ref, g_ref, lbl_ref, gn_ref, *rest, n_steps, n_cast):
    cast_in, o_ref, cast_out = rest[:n_cast], rest[n_cast], rest[n_cast + 1:]
    pieces = _cast_pieces(cast_in, cast_out)
    C = HG_CHUNK
    H2 = C // 2
    Dk = REC_HEAD_DIM
    nt = (((1,), (1,)), ((), ()))
    lbl = lbl_ref[0]
    e = jnp.exp(lbl - jnp.max(lbl, axis=0, keepdims=True))
    lb = e[0:1, :] / jnp.sum(e, axis=0, keepdims=True)
    c1 = 0.5 * (1.0 - lb)
    gn = gn_ref[...]

    r = lax.broadcasted_iota(jnp.int32, (C, C), 0)
    s = lax.broadcasted_iota(jnp.int32, (C, C), 1)
    tril = (s <= r).astype(BF16)
    rh = lax.broadcasted_iota(jnp.int32, (H2, H2), 0)
    sh = lax.broadcasted_iota(jnp.int32, (H2, H2), 1)
    lvl_mask = {m: ((rh // (2 * m)) == (sh // (2 * m))) & (((rh // m) % 2) == 1) & (((sh // m) % 2) == 0)
                for m in HG_LEVELS[1:]}
    sub = lax.broadcasted_iota(jnp.int32, (C // SUBLANES, SUBLANES, Dk), 1)

    def roll8(x, d):
        return pltpu.roll(x.reshape(C // SUBLANES, SUBLANES, Dk), d, 1)

    def front(n):
        rows = slice(n * C, (n + 1) * C)
        c1t = c1 * jnp.tanh(0.5 * f_ref[0, rows, :])
        f = (1.0 - c1) + c1t
        kk = c1 - c1t
        qq = _silu_tanh(q_ref[0, rows, :])
        ii = i_ref[0, rows, :]
        cat = jnp.concatenate(_split3(jnp.log2(f)), axis=1)
        return dict(rows=rows, kk=kk, qq=qq, ii=ii, ii_bf=ii.astype(BF16), cat=cat)

    def level_z(v, m):
        b, kk, qq = v["b"], v["kk"], v["qq"]
        if m >= SUBLANES:
            parts, srcs = [], []
            for p in range(0, C, 2 * m):
                bm = b[p + m - 1:p + m, :]
                parts += [bm - b[p:p + m], b[p + m:p + 2 * m] - bm]
                srcs += [kk[p:p + m], qq[p + m:p + 2 * m]]
            arg = jnp.concatenate(parts, axis=0)
            src = jnp.concatenate(srcs, axis=0)
        else:
            b3 = b.reshape(C // SUBLANES, SUBLANES, Dk)
            if m == 1:
                bm = jnp.where(sub % 2 == 1, roll8(b, 1), b3)
            else:
                bm = jnp.broadcast_to(b3[:, m - 1:m, :], b3.shape)
                for p in range(2 * m, SUBLANES, 2 * m):
                    bm = jnp.where(sub >= p, jnp.broadcast_to(b3[:, p + m - 1:p + m, :], b3.shape), bm)
            upper = (sub // m) % 2 == 1
            arg = ((b3 - bm) * jnp.where(upper, 1.0, -1.0)).reshape(C, Dk)
            src = jnp.where(upper, qq.reshape(b3.shape), kk.reshape(b3.shape)).reshape(C, Dk)
        return (src * jnp.exp2(arg)).astype(BF16)

    vs = [front(j) for j in range(n_steps)]
    bb = jnp.dot(tril, jnp.concatenate([v["cat"] for v in vs], axis=1), preferred_element_type=F32)
    for j, v in enumerate(vs):
        c0 = 3 * Dk * j
        v["b"] = bb[:, c0:c0 + Dk] + bb[:, c0 + Dk:c0 + 2 * Dk] + bb[:, c0 + 2 * Dk:c0 + 3 * Dk]
    a_lo = [None] * n_steps
    a_d0 = [jnp.zeros((H2, H2), F32)] * n_steps
    a_d1 = [jnp.zeros((H2, H2), F32)] * n_steps
    for li, m in enumerate(HG_LEVELS):
        for j, v in enumerate(vs):
            z = level_z(v, m)
            if li == 0:
                a_lo[j] = lax.dot_general(z[H2:], z[:H2], nt, preferred_element_type=F32)
            else:
                g = lax.dot_general(z, z, nt, preferred_element_type=F32)
                a_d0[j] = jnp.where(lvl_mask[m], g[:H2, :H2], a_d0[j])
                a_d1[j] = jnp.where(lvl_mask[m], g[H2:, H2:], a_d1[j])
    intra = []
    for j, v in enumerate(vs):
        a = jnp.concatenate([jnp.concatenate([a_d0[j], jnp.zeros((H2, H2), F32)], axis=1),
                             jnp.concatenate([a_lo[j], a_d1[j]], axis=1)], axis=0).astype(BF16)
        o_diag = jnp.sum(v["qq"] * v["kk"], axis=-1, keepdims=True) * v["ii"]
        intra.append(jnp.dot(a, v["ii_bf"], preferred_element_type=F32) + o_diag)
    st = jnp.zeros((Dk, Dk), F32)
    for j, v in enumerate(vs):
        b = v["b"]
        b_last = b[C - 1:C, :]
        qe = (v["qq"] * jnp.exp2(b)).astype(BF16)
        o = intra[j] + lax.dot_general(qe, st.astype(BF16), nt, preferred_element_type=F32)
        ke = (v["kk"] * jnp.exp2(b_last - b)).astype(BF16)
        st = st * jnp.exp2(b_last) + lax.dot_general(
            v["ii_bf"], ke, (((0,), (0,)), ((), ())), preferred_element_type=F32)
        ms = jnp.mean(o * o, axis=-1, keepdims=True)
        y = o * lax.rsqrt(ms + EPS) * gn
        y = y * _silu_tanh(g_ref[0, v["rows"], :])
        o_ref[0, v["rows"], :] = y.astype(o_ref.dtype)
        _cast_some(pieces, j, n_steps)


def _hgrn(proj_b, lb_logits_h, gnorm_g, cast_weights):
    B, T, W4 = proj_b.shape
    W = W4 // 4
    H = W // REC_HEAD_DIM
    n_slots = lb_logits_h.shape[1]
    blk = lambda off: pl.BlockSpec((1, T, REC_HEAD_DIM), lambda b, h, off=off: (b, 0, off * H + h))
    cast_specs = _cast_specs(cast_weights, B * H, lambda b, h: b * H + h)
    outs = pl.pallas_call(
        functools.partial(_hgrn_kernel, n_steps=T // HG_CHUNK, n_cast=len(cast_weights)),
        out_shape=(jax.ShapeDtypeStruct((B, T, W), BF16),
                   *[jax.ShapeDtypeStruct(w.shape, BF16) for w in cast_weights]),
        grid=(B, H),
        in_specs=[blk(0), blk(1), blk(2), blk(3),
                  pl.BlockSpec((1, n_slots, REC_HEAD_DIM), lambda b, h: (h, 0, 0)),
                  pl.BlockSpec((1, REC_HEAD_DIM), lambda b, h: (0, 0)),
                  *cast_specs],
        out_specs=(pl.BlockSpec((1, T, REC_HEAD_DIM), lambda b, h: (b, 0, h)), *cast_specs),
        compiler_params=_cparams(("parallel", "parallel"), 40),
        name="hgrn2",
    )(proj_b, proj_b, proj_b, proj_b, lb_logits_h, gnorm_g, *cast_weights)
    return outs[0], outs[1:]


def _outproj_kernel(oa_ref, ob_ref, w_ref, x_ref, mod_ref, g_ref, b_ref, x1_ref, *, alpha, sub):
    gate1 = _mod_row(mod_ref, pl.program_id(0), 2)
    for r0 in range(0, x_ref.shape[1], sub):
        rs = slice(r0, r0 + sub)
        o_cat = jnp.concatenate([oa_ref[0, rs, :], ob_ref[0, rs, :]], axis=1)
        mix = jnp.dot(o_cat, w_ref[...], preferred_element_type=F32)
        x1_ref[0, rs, :] = _ln_rows(alpha * x_ref[0, rs, :] + gate1 * mix) * g_ref[...] + b_ref[...]


def _outproj(o_a, o_b, w_o, x, mod, ln_g, ln_b, alpha, tm=512, sub=128):
    B, T, D = x.shape
    Wa, Wb = o_a.shape[-1], o_b.shape[-1]
    return pl.pallas_call(
        functools.partial(_outproj_kernel, alpha=alpha, sub=sub),
        out_shape=jax.ShapeDtypeStruct((B, T, D), F32),
        grid=(B, T // tm),
        in_specs=[pl.BlockSpec((1, tm, Wa), lambda b, i: (b, i, 0)),
                  pl.BlockSpec((1, tm, Wb), lambda b, i: (b, i, 0)),
                  pl.BlockSpec((Wa + Wb, D), lambda b, i: (0, 0)),
                  pl.BlockSpec((1, tm, D), lambda b, i: (b, i, 0)),
                  pl.BlockSpec((B, N_MOD * D), lambda b, i: (0, 0)),
                  pl.BlockSpec((1, D), lambda b, i: (0, 0)),
                  pl.BlockSpec((1, D), lambda b, i: (0, 0))],
        out_specs=pl.BlockSpec((1, tm, D), lambda b, i: (b, i, 0)),
        compiler_params=_cparams(("parallel", "parallel"), 48),
        name="out_proj_ln1",
    )(o_a, o_b, w_o, x, mod, ln_g, ln_b)


def _ffn_kernel(wg_ref, wu_ref, wo_ref, x_ref, mod_ref, g_ref, b_ref, o_ref, h_sc, *, alpha, sub):
    f = pl.program_id(2)
    last = pl.num_programs(2) - 1

    def partial_out(rs):
        h = h_sc[rs, :]
        gate = jnp.dot(h, wg_ref[...], preferred_element_type=F32)
        up = jnp.dot(h, wu_ref[...], preferred_element_type=F32)
        act = (_silu_tanh(gate) * up).astype(BF16)
        return jnp.dot(act, wo_ref[...], preferred_element_type=F32)

    everything = slice(0, o_ref.shape[1])

    @pl.when(f == 0)
    def _():
        shift2 = _mod_row(mod_ref, pl.program_id(0), 3)
        scale2 = _mod_row(mod_ref, pl.program_id(0), 4)
        for r0 in range(0, o_ref.shape[1], sub):
            rs = slice(r0, r0 + sub)
            h_sc[rs, :] = (_ln_rows(x_ref[0, rs, :]) * (1.0 + scale2) + shift2).astype(BF16)
        o_ref[0] = partial_out(everything)

    @pl.when((f > 0) & (f < last))
    def _():
        o_ref[0] += partial_out(everything)

    @pl.when(f == last)
    def _():
        gate2 = _mod_row(mod_ref, pl.program_id(0), 5)
        for r0 in range(0, o_ref.shape[1], sub):
            rs = slice(r0, r0 + sub)
            y = o_ref[0, rs, :] + partial_out(rs)
            o_ref[0, rs, :] = _ln_rows(alpha * x_ref[0, rs, :] + gate2 * y) * g_ref[...] + b_ref[...]


def _ffn(w_in, w_out, x1, mod, ln_g, ln_b, alpha, tm=1024, tf=512, sub=256):
    B, T, D = x1.shape
    F = w_out.shape[0]
    nf = F // tf
    return pl.pallas_call(
        functools.partial(_ffn_kernel, alpha=alpha, sub=sub),
        out_shape=jax.ShapeDtypeStruct((B, T, D), F32),
        grid=(B, T // tm, nf),
        in_specs=[pl.BlockSpec((D, tf), lambda b, i, f: (0, f)),
                  pl.BlockSpec((D, tf), lambda b, i, f: (0, nf + f)),
                  pl.BlockSpec((tf, D), lambda b, i, f: (f, 0)),
                  pl.BlockSpec((1, tm, D), lambda b, i, f: (b, i, 0)),
                  pl.BlockSpec((B, N_MOD * D), lambda b, i, f: (0, 0)),
                  pl.BlockSpec((1, D), lambda b, i, f: (0, 0)),
                  pl.BlockSpec((1, D), lambda b, i, f: (0, 0))],
        out_specs=pl.BlockSpec((1, tm, D), lambda b, i, f: (b, i, 0)),
        scratch_shapes=[pltpu.VMEM((tm, D), BF16)],
        compiler_params=_cparams(("parallel", "parallel", "arbitrary"), 60),
        name="swiglu_ffn_ln2",
    )(w_in, w_in, w_out, x1, mod, ln_g, ln_b)


def _bias_vectors(rel_bias):
    H, n_rel = rel_bias.shape
    max_rel = (n_rel - 1) // 2
    u = jnp.arange(BIAS_W)
    idx = jnp.clip(KBLK - u, -max_rel, max_rel) + max_rel
    return rel_bias[:, idx]


def kernel(x, c, w_ada, b_ada, w_in, rel_bias, attn_norm_g, lb_logits, gnorm_g, w_o,
           ln1_g, ln1_b, w_ffn_in, w_ffn_out, ln2_g, ln2_b):
    B, T, D = x.shape
    depth = w_ada.shape[0]
    alpha = (2 * depth) ** 0.25
    attn_w = attn_norm_g.shape[1]
    rec_w = lb_logits.shape[1]
    n_slots = lb_logits.shape[0]
    rec_heads = rec_w // REC_HEAD_DIM
    assert depth == 1 and n_slots == depth + 1
    for layer in range(depth):
        mod = _mod(c, w_ada[layer], b_ada[layer])
        q_scale = jnp.where(jnp.arange(w_in.shape[2]) < attn_w, ATTN_HEAD_DIM ** -0.5 * LOG2E, 1.0)
        q_scale = q_scale.astype(F32).reshape(1, -1)
        h1, q_a = _ln_proj(x, mod, w_in[layer], q_scale, tn=attn_w)
        h1 = h1.reshape(B * T, D)
        kv_a = _matmul(h1, w_in[layer], q_scale, attn_w, 2 * attn_w, BF16).reshape(B, T, 2 * attn_w)
        proj_b = _matmul(h1, w_in[layer], q_scale, 3 * attn_w, 4 * rec_w, F32).reshape(B, T, 4 * rec_w)
        bias_vec = (_bias_vectors(rel_bias[layer]) * LOG2E).reshape(-1, 2, BIAS_W)
        o_a, (w_o_bf, w_ffn_out_bf) = _attention(
            q_a, kv_a, bias_vec, attn_norm_g[layer].reshape(-1, 1, LANES), (w_o[layer], w_ffn_out[layer]))
        lbl = lb_logits.reshape(n_slots, rec_heads, REC_HEAD_DIM).transpose(1, 0, 2)
        o_b, (w_ffn_in_bf,) = _hgrn(proj_b, lbl, gnorm_g[layer].reshape(1, REC_HEAD_DIM), (w_ffn_in[layer],))
        x = _outproj(o_a, o_b, w_o_bf, x, mod,
                     ln1_g[layer].reshape(1, D), ln1_b[layer].reshape(1, D), alpha)
        x = _ffn(w_ffn_in_bf, w_ffn_out_bf, x, mod,
                 ln2_g[layer].reshape(1, D), ln2_b[layer].reshape(1, D), alpha)
    return x
```

```python
import functools

import jax
import jax.numpy as jnp
from jax import lax
from jax.experimental import pallas as pl
from jax.experimental.pallas import tpu as pltpu

F32 = jnp.float32
BF16 = jnp.bfloat16

CHUNK = 64
N_PAST_CHUNKS = 8
BAND = (N_PAST_CHUNKS + 1) * CHUNK
ATTN_HEAD_DIM = 64
REC_HEAD_DIM = 128
N_MOD = 6
N_MOD_PRE = 2
EPS = 1e-5
LANES = 128
SUBLANES = 8
BF16_ROWS = 16
QBLK = 2 * CHUNK
KBLK = BAND + CHUNK
BIAS_W = KBLK + QBLK

MIB = 1024 * 1024


def _cparams(sem, vmem_mib):
    return pltpu.CompilerParams(dimension_semantics=sem, vmem_limit_bytes=vmem_mib * MIB)


def _silu_tanh(x):
    h = 0.5 * x
    return h + h * jnp.tanh(h)


def _mod_row(mod_ref, b, r, d):
    return mod_ref[pl.ds(b, 1), r * d:(r + 1) * d]


def _ln_rows(x):
    mu = jnp.mean(x, axis=-1, keepdims=True)
    xc = x - mu
    var = jnp.mean(xc * xc, axis=-1, keepdims=True)
    return xc * lax.rsqrt(var + EPS)


def _split3(x):
    hi = x.astype(BF16)
    r1 = x - hi.astype(F32)
    mid = r1.astype(BF16)
    lo = (r1 - mid.astype(F32)).astype(BF16)
    return hi, mid, lo


def _mod_block(c_ref, w_ref, b_ref, o_ref):
    act = jnp.concatenate(_split3(_silu_tanh(c_ref[...])), axis=0)
    w = w_ref[...]
    w_hi = w.astype(BF16)
    w_mid = (w - w_hi.astype(F32)).astype(BF16)
    acc = jnp.dot(act, w_hi, preferred_element_type=F32) + jnp.dot(act, w_mid, preferred_element_type=F32)
    out = acc[0:SUBLANES] + acc[SUBLANES:2 * SUBLANES] + acc[2 * SUBLANES:3 * SUBLANES]
    o_ref[...] = out[0:o_ref.shape[0]] + b_ref[...]


def _mod(c8, w_ada, b_ada, n_cols, n_rows, tn=1024):
    D = c8.shape[1]
    return pl.pallas_call(
        _mod_block,
        out_shape=jax.ShapeDtypeStruct((n_rows, n_cols), F32),
        grid=(n_cols // tn,),
        in_specs=[pl.BlockSpec((SUBLANES, D), lambda j: (0, 0)),
                  pl.BlockSpec((D, tn), lambda j: (0, j)),
                  pl.BlockSpec((1, tn), lambda j: (0, j))],
        out_specs=pl.BlockSpec((n_rows, tn), lambda j: (0, j)),
        compiler_params=_cparams(("parallel",), 40),
        name="adaln_mod",
    )(c8, w_ada, b_ada)


def _ln_proj_kernel(x_ref, mod_ref, w_ref, s_ref, h_ref, o_ref, w_bf, *, sub):
    b = pl.program_id(0)

    @pl.when((b == 0) & (pl.program_id(1) == 0))
    def _():
        w_bf[...] = (w_ref[...] * s_ref[...]).astype(BF16)

    d = x_ref.shape[-1]
    shift = _mod_row(mod_ref, b, 0, d)
    scale = _mod_row(mod_ref, b, 1, d)
    for r0 in range(0, x_ref.shape[1], sub):
        rs = slice(r0, r0 + sub)
        h_ref[0, rs, :] = (_ln_rows(x_ref[0, rs, :]) * (1.0 + scale) + shift).astype(BF16)
    o_ref[0] = jnp.dot(h_ref[0], w_bf[...], preferred_element_type=F32).astype(o_ref.dtype)


def _ln_proj(x, mod, w, col_scale, tm=1024, tn=1024, sub=256):
    B, T, D = x.shape
    return pl.pallas_call(
        functools.partial(_ln_proj_kernel, sub=sub),
        out_shape=(jax.ShapeDtypeStruct((B, T, D), BF16), jax.ShapeDtypeStruct((B, T, tn), BF16)),
        grid=(B, T // tm),
        in_specs=[pl.BlockSpec((1, tm, D), lambda b, i: (b, i, 0)),
                  pl.BlockSpec((B, N_MOD_PRE * D), lambda b, i: (0, 0)),
                  pl.BlockSpec((D, tn), lambda b, i: (0, 0)),
                  pl.BlockSpec((1, tn), lambda b, i: (0, 0))],
        out_specs=(pl.BlockSpec((1, tm, D), lambda b, i: (b, i, 0)),
                   pl.BlockSpec((1, tm, tn), lambda b, i: (b, i, 0))),
        scratch_shapes=[pltpu.VMEM((D, tn), BF16)],
        compiler_params=_cparams(("arbitrary", "arbitrary"), 56),
        name="ln_in_proj",
    )(x, mod, w, col_scale)


def _matmul_kernel(a_ref, w_ref, s_ref, o_ref, w_bf):
    @pl.when(pl.program_id(1) == 0)
    def _():
        w_bf[...] = (w_ref[...] * s_ref[...]).astype(BF16)

    o_ref[...] = jnp.dot(a_ref[...], w_bf[...], preferred_element_type=F32).astype(o_ref.dtype)


def _matmul(a, w, col_scale, col0, n_out, out_dtype, tm=1024, tn=1024):
    M, K = a.shape
    assert col0 % tn == 0 and n_out % tn == 0 and M % tm == 0
    col_block0 = col0 // tn
    return pl.pallas_call(
        _matmul_kernel,
        out_shape=jax.ShapeDtypeStruct((M, n_out), out_dtype),
        grid=(n_out // tn, M // tm),
        in_specs=[pl.BlockSpec((tm, K), lambda j, i: (i, 0)),
                  pl.BlockSpec((K, tn), lambda j, i: (0, j + col_block0)),
                  pl.BlockSpec((1, tn), lambda j, i: (0, j + col_block0))],
        out_specs=pl.BlockSpec((tm, tn), lambda j, i: (i, j)),
        scratch_shapes=[pltpu.VMEM((K, tn), BF16)],
        compiler_params=_cparams(("parallel", "arbitrary"), 48),
        name="in_proj",
    )(a, w, col_scale)


def _cast_specs(weights, n_grid, step_of):
    specs = []
    for w in weights:
        rows = w.shape[0] // n_grid
        assert w.shape[0] % n_grid == 0 and rows % BF16_ROWS == 0
        specs.append(pl.BlockSpec((rows, w.shape[1]), lambda *g: (step_of(*g), 0)))
    return specs


def _cast_pieces(cast_in, cast_out):
    return [(src_ref, dst_ref, r0) for src_ref, dst_ref in zip(cast_in, cast_out)
            for r0 in range(0, src_ref.shape[0], BF16_ROWS)]


def _cast_some(pieces, m, n):
    for src_ref, dst_ref, r0 in pieces[m * len(pieces) // n:(m + 1) * len(pieces) // n]:
        dst_ref[r0:r0 + BF16_ROWS, :] = src_ref[r0:r0 + BF16_ROWS, :].astype(BF16)


NEG_BIG = -1e30
LOG2E = 1.4426950408889634


def _attn_kernel(q_ref, k_ref, v_ref, bias_ref, gain_ref, *rest, n_chunks, n_cast):
    cast_in, mod_in = rest[:n_cast], rest[n_cast:n_cast + 3]
    o_ref = rest[n_cast + 3]
    cast_out, mod_out = rest[n_cast + 4:2 * n_cast + 4], rest[2 * n_cast + 4]
    kta, ktb, vpa, vpb, tab, s_a, s_b, s_c = rest[2 * n_cast + 5:]
    T = n_chunks * CHUNK
    n_steps = T // QBLK
    head0 = lax.broadcasted_iota(jnp.int32, (QBLK, LANES), 1) < ATTN_HEAD_DIM
    m0 = jnp.where(head0, 1.0, 0.0).astype(BF16)
    m1 = jnp.where(head0, 0.0, 1.0).astype(BF16)

    head0_t = lax.broadcasted_iota(jnp.int32, (LANES, QBLK), 0) < ATTN_HEAD_DIM
    mt0 = jnp.where(head0_t, 1.0, 0.0).astype(BF16)
    mt1 = jnp.where(head0_t, 0.0, 1.0).astype(BF16)

    def prep(blk):
        rows = slice(blk * QBLK, (blk + 1) * QBLK)
        kt = k_ref[0, rows, :].T
        v = v_ref[0, rows, :]
        kta[:, rows] = kt * mt0
        ktb[:, rows] = kt * mt1
        vpa[rows, 0:LANES] = v * m0
        vpb[rows, 0:LANES] = v * m1
        vpa[rows, LANES:2 * LANES] = m0
        vpb[rows, LANES:2 * LANES] = m1

    @pl.when(pl.program_id(1) == 0)
    def _():
        qry = lax.broadcasted_iota(jnp.int32, (QBLK, KBLK), 0)
        key = lax.broadcasted_iota(jnp.int32, (QBLK, KBLK), 1)
        in_band = ((qry < CHUNK) & (key < BAND)) | ((qry >= CHUNK) & (key >= CHUNK))
        for hh in range(2):
            g = jnp.broadcast_to(bias_ref[0, hh:hh + 1, :], (QBLK, BIAS_W))
            t = pltpu.roll(g, BIAS_W - QBLK, 1, stride=1, stride_axis=0)[:, :KBLK]
            tab[hh] = jnp.where(in_band, t, NEG_BIG).astype(BF16)

    eye = (lax.broadcasted_iota(jnp.int32, (QBLK, QBLK), 0)
           == lax.broadcasted_iota(jnp.int32, (QBLK, QBLK), 1)).astype(BF16)
    gain = gain_ref[0]

    def band_of(m):
        hi = (m + 1) * QBLK
        lo = max(0, hi - KBLK)
        return slice(lo, hi), KBLK - (hi - lo)

    def scores(m, dst):
        prep(m)
        band, col0 = band_of(m)
        lhs = jnp.concatenate([q_ref[0, m * QBLK:(m + 1) * QBLK, :], eye], axis=1)
        rhs = jnp.concatenate([jnp.concatenate([kta[:, band], ktb[:, band]], axis=1),
                               jnp.concatenate([tab[0, :, col0:], tab[1, :, col0:]], axis=1)], axis=0)
        dst[:, 0:rhs.shape[1]] = jnp.dot(lhs, rhs, preferred_element_type=F32)

    def finish(m, src):
        band, col0 = band_of(m)
        w = KBLK - col0
        s = src[:, 0:2 * w]
        p = jnp.concatenate(
            [jnp.exp2(sh - jnp.max(sh, axis=-1, keepdims=True)) for sh in (s[:, 0:w], s[:, w:2 * w])],
            axis=1).astype(BF16)
        pv = jnp.dot(p, jnp.concatenate([vpa[band, :], vpb[band, :]], axis=0), preferred_element_type=F32)
        o = pv[:, 0:LANES] * (1.0 / pv[:, LANES:2 * LANES])
        o2 = o * o
        ms0 = jnp.sum(jnp.where(head0, o2, 0.0), axis=-1, keepdims=True) / ATTN_HEAD_DIM
        ms1 = jnp.sum(jnp.where(head0, 0.0, o2), axis=-1, keepdims=True) / ATTN_HEAD_DIM
        y = o * lax.rsqrt(jnp.where(head0, ms0, ms1) + EPS) * gain
        o_ref[0, m * QBLK:(m + 1) * QBLK, :] = y.astype(o_ref.dtype)

    pieces = _cast_pieces(cast_in, cast_out)
    bufs = (s_a, s_b, s_c)
    depth = len(bufs)
    ahead = depth - 1
    for m in range(ahead):
        scores(m, bufs[m % depth])
    for m in range(n_steps):
        if m + ahead < n_steps:
            scores(m + ahead, bufs[(m + ahead) % depth])
        finish(m, bufs[m % depth])
        _cast_some(pieces, m, n_steps)
    _mod_block(*mod_in, mod_out)


def _attention(q_arr, kv_arr, bias_vec, attn_gain, cast_weights, c8, w_ada, b_ada, mod_col0):
    B, T, W = q_arr.shape
    n_pairs = W // LANES
    n_grid = n_pairs * B
    cast_specs = _cast_specs(cast_weights, n_grid, lambda h, b: h * B + b)
    D = c8.shape[1]
    n_mod = w_ada.shape[1] - mod_col0
    tn_mod = n_mod // n_grid
    assert n_mod % n_grid == 0 and tn_mod % LANES == 0 and mod_col0 % tn_mod == 0
    mod_blk0 = mod_col0 // tn_mod
    outs = pl.pallas_call(
        functools.partial(_attn_kernel, n_chunks=T // CHUNK, n_cast=len(cast_weights)),
        out_shape=(jax.ShapeDtypeStruct((B, T, W), BF16),
                   *[jax.ShapeDtypeStruct(w.shape, BF16) for w in cast_weights],
                   jax.ShapeDtypeStruct((B, n_mod), F32)),
        grid=(n_pairs, B),
        in_specs=[pl.BlockSpec((1, T, LANES), lambda h, b: (b, 0, h)),
                  pl.BlockSpec((1, T, LANES), lambda h, b: (b, 0, h)),
                  pl.BlockSpec((1, T, LANES), lambda h, b: (b, 0, n_pairs + h)),
                  pl.BlockSpec((1, 2, BIAS_W), lambda h, b: (h, 0, 0)),
                  pl.BlockSpec((1, 1, LANES), lambda h, b: (h, 0, 0)),
                  *cast_specs,
                  pl.BlockSpec((SUBLANES, D), lambda h, b: (0, 0)),
                  pl.BlockSpec((D, tn_mod), lambda h, b: (0, mod_blk0 + h * B + b)),
                  pl.BlockSpec((1, tn_mod), lambda h, b: (0, mod_blk0 + h * B + b))],
        out_specs=(pl.BlockSpec((1, T, LANES), lambda h, b: (b, 0, h)), *cast_specs,
                   pl.BlockSpec((B, tn_mod), lambda h, b: (0, h * B + b))),
        scratch_shapes=[pltpu.VMEM((LANES, T), BF16),
                        pltpu.VMEM((LANES, T), BF16),
                        pltpu.VMEM((T, 2 * LANES), BF16),
                        pltpu.VMEM((T, 2 * LANES), BF16),
                        pltpu.VMEM((2, QBLK, KBLK), BF16),
                        pltpu.VMEM((QBLK, 2 * KBLK), F32),
                        pltpu.VMEM((QBLK, 2 * KBLK), F32),
                        pltpu.VMEM((QBLK, 2 * KBLK), F32)],
        compiler_params=_cparams(("parallel", "arbitrary"), 48),
        name="chunk_attention",
    )(q_arr, kv_arr, kv_arr, bias_vec, attn_gain, *cast_weights, c8, w_ada, b_ada)
    return outs[0], outs[-1], outs[1:-1]


HG_CHUNK = 256
HG_LEVELS = (128, 64, 32, 16, 8, 4, 2, 1)
assert HG_LEVELS[-1] == 1


def _hgrn_kernel(q_ref, f_ref, i_ref, g_ref, lbl_ref, gn_ref, *rest, n_steps, n_cast):
    cast_in, o_ref, cast_out = rest[:n_cast], rest[n_cast], rest[n_cast + 1:]
    pieces = _cast_pieces(cast_in, cast_out)
    C = HG_CHUNK
    H2 = C // 2
    Dk = REC_HEAD_DIM
    nt = (((1,), (1,)), ((), ()))
    lbl = lbl_ref[0]
    e = jnp.exp(lbl - jnp.max(lbl, axis=0, keepdims=True))
    lb = e[0:1, :] / jnp.sum(e, axis=0, keepdims=True)
    c1 = 0.5 * (1.0 - lb)
    gn = gn_ref[...]

    r = lax.broadcasted_iota(jnp.int32, (C, C), 0)
    s = lax.broadcasted_iota(jnp.int32, (C, C), 1)
    tril = (s <= r).astype(BF16)
    rh = lax.broadcasted_iota(jnp.int32, (H2, H2), 0)
    sh = lax.broadcasted_iota(jnp.int32, (H2, H2), 1)
    lvl_mask = {m: ((rh // (2 * m)) == (sh // (2 * m))) & (((rh // m) % 2) == 1) & (((sh // m) % 2) == 0)
                for m in HG_LEVELS[1:]}
    sub = lax.broadcasted_iota(jnp.int32, (C // SUBLANES, SUBLANES, Dk), 1)

    def roll8(x, d):
        return pltpu.roll(x.reshape(C // SUBLANES, SUBLANES, Dk), d, 1)

    def front(n):
        rows = slice(n * C, (n + 1) * C)
        c1t = c1 * jnp.tanh(0.5 * f_ref[0, rows, :])
        f = (1.0 - c1) + c1t
        kk = c1 - c1t
        qq = _silu_tanh(q_ref[0, rows, :])
        ii = i_ref[0, rows, :]
        cat = jnp.concatenate(_split3(jnp.log2(f)), axis=1)
        return dict(rows=rows, kk=kk, qq=qq, ii=ii, ii_bf=ii.astype(BF16), cat=cat)

    def level_z(v, m):
        b, kk, qq = v["b"], v["kk"], v["qq"]
        if m >= SUBLANES:
            parts, srcs = [], []
            for p in range(0, C, 2 * m):
                bm = b[p + m - 1:p + m, :]
                parts += [bm - b[p:p + m], b[p + m:p + 2 * m] - bm]
                srcs += [kk[p:p + m], qq[p + m:p + 2 * m]]
            arg = jnp.concatenate(parts, axis=0)
            src = jnp.concatenate(srcs, axis=0)
        else:
            b3 = b.reshape(C // SUBLANES, SUBLANES, Dk)
            if m == 1:
                bm = jnp.where(sub % 2 == 1, roll8(b, 1), b3)
            else:
                bm = jnp.broadcast_to(b3[:, m - 1:m, :], b3.shape)
                for p in range(2 * m, SUBLANES, 2 * m):
                    bm = jnp.where(sub >= p, jnp.broadcast_to(b3[:, p + m - 1:p + m, :], b3.shape), bm)
            upper = (sub // m) % 2 == 1
            arg = ((b3 - bm) * jnp.where(upper, 1.0, -1.0)).reshape(C, Dk)
            src = jnp.where(upper, qq.reshape(b3.shape), kk.reshape(b3.shape)).reshape(C, Dk)
        return (src * jnp.exp2(arg)).astype(BF16)

    vs = [front(j) for j in range(n_steps)]
    bb = jnp.dot(tril, jnp.concatenate([v["cat"] for v in vs], axis=1), preferred_element_type=F32)
    for j, v in enumerate(vs):
        c0 = 3 * Dk * j
        v["b"] = bb[:, c0:c0 + Dk] + bb[:, c0 + Dk:c0 + 2 * Dk] + bb[:, c0 + 2 * Dk:c0 + 3 * Dk]
    a_lo = [None] * n_steps
    a_d0 = [jnp.zeros((H2, H2), F32)] * n_steps
    a_d1 = [jnp.zeros((H2, H2), F32)] * n_steps
    for li, m in enumerate(HG_LEVELS):
        for j, v in enumerate(vs):
            z = level_z(v, m)
            if li == 0:
                a_lo[j] = lax.dot_general(z[H2:], z[:H2], nt, preferred_element_type=F32)
            else:
                g = lax.dot_general(z, z, nt, preferred_element_type=F32)
                a_d0[j] = jnp.where(lvl_mask[m], g[:H2, :H2], a_d0[j])
                a_d1[j] = jnp.where(lvl_mask[m], g[H2:, H2:], a_d1[j])
    intra = []
    for j, v in enumerate(vs):
        a = jnp.concatenate([jnp.concatenate([a_d0[j], jnp.zeros((H2, H2), F32)], axis=1),
                             jnp.concatenate([a_lo[j], a_d1[j]], axis=1)], axis=0).astype(BF16)
        o_diag = jnp.sum(v["qq"] * v["kk"], axis=-1, keepdims=True) * v["ii"]
        intra.append(jnp.dot(a, v["ii_bf"], preferred_element_type=F32) + o_diag)
    st = jnp.zeros((Dk, Dk), F32)
    for j, v in enumerate(vs):
        b = v["b"]
        b_last = b[C - 1:C, :]
        qe = (v["qq"] * jnp.exp2(b)).astype(BF16)
        o = intra[j] + lax.dot_general(qe, st.astype(BF16), nt, preferred_element_type=F32)
        ke = (v["kk"] * jnp.exp2(b_last - b)).astype(BF16)
        st = st * jnp.exp2(b_last) + lax.dot_general(
            v["ii_bf"], ke, (((0,), (0,)), ((), ())), preferred_element_type=F32)
        ms = jnp.mean(o * o, axis=-1, keepdims=True)
        y = o * lax.rsqrt(ms + EPS) * gn
        y = y * _silu_tanh(g_ref[0, v["rows"], :])
        o_ref[0, v["rows"], :] = y.astype(o_ref.dtype)
        _cast_some(pieces, j, n_steps)


def _hgrn(proj_b, lb_logits_h, gnorm_g, cast_weights):
    B, T, W4 = proj_b.shape
    W = W4 // 4
    H = W // REC_HEAD_DIM
    n_slots = lb_logits_h.shape[1]
    blk = lambda off: pl.BlockSpec((1, T, REC_HEAD_DIM), lambda b, h, off=off: (b, 0, off * H + h))
    cast_specs = _cast_specs(cast_weights, B * H, lambda b, h: b * H + h)
    outs = pl.pallas_call(
        functools.partial(_hgrn_kernel, n_steps=T // HG_CHUNK, n_cast=len(cast_weights)),
        out_shape=(jax.ShapeDtypeStruct((B, T, W), BF16),
                   *[jax.ShapeDtypeStruct(w.shape, BF16) for w in cast_weights]),
        grid=(B, H),
        in_specs=[blk(0), blk(1), blk(2), blk(3),
                  pl.BlockSpec((1, n_slots, REC_HEAD_DIM), lambda b, h: (h, 0, 0)),
                  pl.BlockSpec((1, REC_HEAD_DIM), lambda b, h: (0, 0)),
                  *cast_specs],
        out_specs=(pl.BlockSpec((1, T, REC_HEAD_DIM), lambda b, h: (b, 0, h)), *cast_specs),
        compiler_params=_cparams(("parallel", "parallel"), 40),
        name="hgrn2",
    )(proj_b, proj_b, proj_b, proj_b, lb_logits_h, gnorm_g, *cast_weights)
    return outs[0], outs[1:]


def _outproj_kernel(oa_ref, ob_ref, w_ref, x_ref, mod_ref, g_ref, b_ref, x1_ref, *, alpha, sub):
    gate1 = _mod_row(mod_ref, pl.program_id(0), 0, x_ref.shape[-1])
    for r0 in range(0, x_ref.shape[1], sub):
        rs = slice(r0, r0 + sub)
        o_cat = jnp.concatenate([oa_ref[0, rs, :], ob_ref[0, rs, :]], axis=1)
        mix = jnp.dot(o_cat, w_ref[...], preferred_element_type=F32)
        x1_ref[0, rs, :] = _ln_rows(alpha * x_ref[0, rs, :] + gate1 * mix) * g_ref[...] + b_ref[...]


def _outproj(o_a, o_b, w_o, x, mod, ln_g, ln_b, alpha, tm=512, sub=128):
    B, T, D = x.shape
    Wa, Wb = o_a.shape[-1], o_b.shape[-1]
    return pl.pallas_call(
        functools.partial(_outproj_kernel, alpha=alpha, sub=sub),
        out_shape=jax.ShapeDtypeStruct((B, T, D), F32),
        grid=(B, T // tm),
        in_specs=[pl.BlockSpec((1, tm, Wa), lambda b, i: (b, i, 0)),
                  pl.BlockSpec((1, tm, Wb), lambda b, i: (b, i, 0)),
                  pl.BlockSpec((Wa + Wb, D), lambda b, i: (0, 0)),
                  pl.BlockSpec((1, tm, D), lambda b, i: (b, i, 0)),
                  pl.BlockSpec((B, (N_MOD - N_MOD_PRE) * D), lambda b, i: (0, 0)),
                  pl.BlockSpec((1, D), lambda b, i: (0, 0)),
                  pl.BlockSpec((1, D), lambda b, i: (0, 0))],
        out_specs=pl.BlockSpec((1, tm, D), lambda b, i: (b, i, 0)),
        compiler_params=_cparams(("parallel", "parallel"), 48),
        name="out_proj_ln1",
    )(o_a, o_b, w_o, x, mod, ln_g, ln_b)


def _ffn_kernel(wg_ref, wu_ref, wo_ref, x_ref, mod_ref, g_ref, b_ref, o_ref, h_sc, *, alpha, sub):
    f = pl.program_id(2)
    last = pl.num_programs(2) - 1

    def partial_out(rs):
        h = h_sc[rs, :]
        gate = jnp.dot(h, wg_ref[...], preferred_element_type=F32)
        up = jnp.dot(h, wu_ref[...], preferred_element_type=F32)
        act = (_silu_tanh(gate) * up).astype(BF16)
        return jnp.dot(act, wo_ref[...], preferred_element_type=F32)

    everything = slice(0, o_ref.shape[1])

    @pl.when(f == 0)
    def _():
        shift2 = _mod_row(mod_ref, pl.program_id(0), 1, x_ref.shape[-1])
        scale2 = _mod_row(mod_ref, pl.program_id(0), 2, x_ref.shape[-1])
        for r0 in range(0, o_ref.shape[1], sub):
            rs = slice(r0, r0 + sub)
            h_sc[rs, :] = (_ln_rows(x_ref[0, rs, :]) * (1.0 + scale2) + shift2).astype(BF16)
        o_ref[0] = partial_out(everything)

    @pl.when((f > 0) & (f < last))
    def _():
        o_ref[0] += partial_out(everything)

    @pl.when(f == last)
    def _():
        gate2 = _mod_row(mod_ref, pl.program_id(0), 3, x_ref.shape[-1])
        for r0 in range(0, o_ref.shape[1], sub):
            rs = slice(r0, r0 + sub)
            y = o_ref[0, rs, :] + partial_out(rs)
            o_ref[0, rs, :] = _ln_rows(alpha * x_ref[0, rs, :] + gate2 * y) * g_ref[...] + b_ref[...]


def _ffn(w_in, w_out, x1, mod, ln_g, ln_b, alpha, tm=1024, tf=512, sub=256):
    B, T, D = x1.shape
    F = w_out.shape[0]
    nf = F // tf
    return pl.pallas_call(
        functools.partial(_ffn_kernel, alpha=alpha, sub=sub),
        out_shape=jax.ShapeDtypeStruct((B, T, D), F32),
        grid=(B, T // tm, nf),
        in_specs=[pl.BlockSpec((D, tf), lambda b, i, f: (0, f)),
                  pl.BlockSpec((D, tf), lambda b, i, f: (0, nf + f)),
                  pl.BlockSpec((tf, D), lambda b, i, f: (f, 0)),
                  pl.BlockSpec((1, tm, D), lambda b, i, f: (b, i, 0)),
                  pl.BlockSpec((B, (N_MOD - N_MOD_PRE) * D), lambda b, i, f: (0, 0)),
                  pl.BlockSpec((1, D), lambda b, i, f: (0, 0)),
                  pl.BlockSpec((1, D), lambda b, i, f: (0, 0))],
        out_specs=pl.BlockSpec((1, tm, D), lambda b, i, f: (b, i, 0)),
        scratch_shapes=[pltpu.VMEM((tm, D), BF16)],
        compiler_params=_cparams(("parallel", "parallel", "arbitrary"), 60),
        name="swiglu_ffn_ln2",
    )(w_in, w_in, w_out, x1, mod, ln_g, ln_b)


def _bias_vectors(rel_bias):
    H, n_rel = rel_bias.shape
    max_rel = (n_rel - 1) // 2
    u = jnp.arange(BIAS_W)
    idx = jnp.clip(KBLK - u, -max_rel, max_rel) + max_rel
    return rel_bias[:, idx]


def kernel(x, c, w_ada, b_ada, w_in, rel_bias, attn_norm_g, lb_logits, gnorm_g, w_o,
           ln1_g, ln1_b, w_ffn_in, w_ffn_out, ln2_g, ln2_b):
    B, T, D = x.shape
    depth = w_ada.shape[0]
    alpha = (2 * depth) ** 0.25
    attn_w = attn_norm_g.shape[1]
    rec_w = lb_logits.shape[1]
    n_slots = lb_logits.shape[0]
    rec_heads = rec_w // REC_HEAD_DIM
    assert depth == 1 and n_slots == depth + 1
    for layer in range(depth):
        c8 = jnp.pad(c, ((0, SUBLANES - B), (0, 0)))
        b_ada2 = b_ada[layer].reshape(1, -1)
        mod_pre = _mod(c8, w_ada[layer], b_ada2, N_MOD_PRE * D, B)
        q_scale = jnp.where(jnp.arange(w_in.shape[2]) < attn_w, ATTN_HEAD_DIM ** -0.5 * LOG2E, 1.0)
        q_scale = q_scale.astype(F32).reshape(1, -1)
        h1, q_a = _ln_proj(x, mod_pre, w_in[layer], q_scale, tn=attn_w)
        h1 = h1.reshape(B * T, D)
        kv_a = _matmul(h1, w_in[layer], q_scale, attn_w, 2 * attn_w, BF16).reshape(B, T, 2 * attn_w)
        proj_b = _matmul(h1, w_in[layer], q_scale, 3 * attn_w, 4 * rec_w, F32).reshape(B, T, 4 * rec_w)
        bias_vec = (_bias_vectors(rel_bias[layer]) * LOG2E).reshape(-1, 2, BIAS_W)
        o_a, mod, (w_o_bf, w_ffn_out_bf) = _attention(
            q_a, kv_a, bias_vec, attn_norm_g[layer].reshape(-1, 1, LANES), (w_o[layer], w_ffn_out[layer]),
            c8, w_ada[layer], b_ada2, N_MOD_PRE * D)
        lbl = lb_logits.reshape(n_slots, rec_heads, REC_HEAD_DIM).transpose(1, 0, 2)
        o_b, (w_ffn_in_bf,) = _hgrn(proj_b, lbl, gnorm_g[layer].reshape(1, REC_HEAD_DIM), (w_ffn_in[layer],))
        x = _outproj(o_a, o_b, w_o_bf, x, mod,
                     ln1_g[layer].reshape(1, D), ln1_b[layer].reshape(1, D), alpha)
        x = _ffn(w_ffn_in_bf, w_ffn_out_bf, x, mod,
                 ln2_g[layer].reshape(1, D), ln2_b[layer].reshape(1, D), alpha)
    return x
```

```python
import functools

import jax
import jax.numpy as jnp
from jax import lax
from jax.experimental import pallas as pl
from jax.experimental.pallas import tpu as pltpu

F32 = jnp.float32
BF16 = jnp.bfloat16

CHUNK = 64
N_PAST_CHUNKS = 8
BAND = (N_PAST_CHUNKS + 1) * CHUNK
ATTN_HEAD_DIM = 64
REC_HEAD_DIM = 128
N_MOD = 6
N_MOD_PRE = 2
EPS = 1e-5
LANES = 128
SUBLANES = 8
BF16_ROWS = 16
QBLK = 2 * CHUNK
KBLK = BAND + CHUNK
BIAS_W = KBLK + QBLK

MIB = 1024 * 1024


def _cparams(sem, vmem_mib):
    return pltpu.CompilerParams(dimension_semantics=sem, vmem_limit_bytes=vmem_mib * MIB)


def _silu_tanh(x):
    h = 0.5 * x
    return h + h * jnp.tanh(h)


def _mod_row(mod_ref, b, r, d):
    return mod_ref[pl.ds(b, 1), r * d:(r + 1) * d]


def _ln_rows(x):
    mu = jnp.mean(x, axis=-1, keepdims=True)
    xc = x - mu
    var = jnp.mean(xc * xc, axis=-1, keepdims=True)
    return xc * lax.rsqrt(var + EPS)


def _split3(x):
    hi = x.astype(BF16)
    r1 = x - hi.astype(F32)
    mid = r1.astype(BF16)
    lo = (r1 - mid.astype(F32)).astype(BF16)
    return hi, mid, lo


def _mod_block(c_ref, w_ref, b_ref, o_ref):
    act = jnp.concatenate(_split3(_silu_tanh(c_ref[...])), axis=0)
    w = w_ref[...]
    w_hi = w.astype(BF16)
    w_mid = (w - w_hi.astype(F32)).astype(BF16)
    acc = jnp.dot(act, w_hi, preferred_element_type=F32) + jnp.dot(act, w_mid, preferred_element_type=F32)
    out = acc[0:SUBLANES] + acc[SUBLANES:2 * SUBLANES] + acc[2 * SUBLANES:3 * SUBLANES]
    o_ref[...] = out[0:o_ref.shape[0]] + b_ref[...]


def _mod(c8, w_ada, b_ada, n_cols, n_rows, tn=1024):
    D = c8.shape[1]
    return pl.pallas_call(
        _mod_block,
        out_shape=jax.ShapeDtypeStruct((n_rows, n_cols), F32),
        grid=(n_cols // tn,),
        in_specs=[pl.BlockSpec((SUBLANES, D), lambda j: (0, 0)),
                  pl.BlockSpec((D, tn), lambda j: (0, j)),
                  pl.BlockSpec((1, tn), lambda j: (0, j))],
        out_specs=pl.BlockSpec((n_rows, tn), lambda j: (0, j)),
        compiler_params=_cparams(("parallel",), 40),
        name="adaln_mod",
    )(c8, w_ada, b_ada)


def _ln_proj_kernel(x_ref, mod_ref, w_ref, s_ref, h_ref, o_ref, w_bf, *, sub):
    b = pl.program_id(0)

    @pl.when((b == 0) & (pl.program_id(1) == 0))
    def _():
        w_bf[...] = (w_ref[...] * s_ref[...]).astype(BF16)

    d = x_ref.shape[-1]
    shift = _mod_row(mod_ref, b, 0, d)
    scale = _mod_row(mod_ref, b, 1, d)
    for r0 in range(0, x_ref.shape[1], sub):
        rs = slice(r0, r0 + sub)
        h_ref[0, rs, :] = (_ln_rows(x_ref[0, rs, :]) * (1.0 + scale) + shift).astype(BF16)
    o_ref[0] = jnp.dot(h_ref[0], w_bf[...], preferred_element_type=F32).astype(o_ref.dtype)


def _ln_proj(x, mod, w, col_scale, tm=1024, tn=1024, sub=256):
    B, T, D = x.shape
    return pl.pallas_call(
        functools.partial(_ln_proj_kernel, sub=sub),
        out_shape=(jax.ShapeDtypeStruct((B, T, D), BF16), jax.ShapeDtypeStruct((B, T, tn), BF16)),
        grid=(B, T // tm),
        in_specs=[pl.BlockSpec((1, tm, D), lambda b, i: (b, i, 0)),
                  pl.BlockSpec((B, N_MOD_PRE * D), lambda b, i: (0, 0)),
                  pl.BlockSpec((D, tn), lambda b, i: (0, 0)),
                  pl.BlockSpec((1, tn), lambda b, i: (0, 0))],
        out_specs=(pl.BlockSpec((1, tm, D), lambda b, i: (b, i, 0)),
                   pl.BlockSpec((1, tm, tn), lambda b, i: (b, i, 0))),
        scratch_shapes=[pltpu.VMEM((D, tn), BF16)],
        compiler_params=_cparams(("arbitrary", "arbitrary"), 56),
        name="ln_in_proj",
    )(x, mod, w, col_scale)


def _matmul_kernel(a_ref, w_ref, s_ref, o_ref, w_bf):
    @pl.when(pl.program_id(1) == 0)
    def _():
        w_bf[...] = (w_ref[...] * s_ref[...]).astype(BF16)

    o_ref[...] = jnp.dot(a_ref[...], w_bf[...], preferred_element_type=F32).astype(o_ref.dtype)


def _matmul(a, w, col_scale, col0, n_out, out_dtype, tm=1024, tn=1024):
    M, K = a.shape
    assert col0 % tn == 0 and n_out % tn == 0 and M % tm == 0
    col_block0 = col0 // tn
    return pl.pallas_call(
        _matmul_kernel,
        out_shape=jax.ShapeDtypeStruct((M, n_out), out_dtype),
        grid=(n_out // tn, M // tm),
        in_specs=[pl.BlockSpec((tm, K), lambda j, i: (i, 0)),
                  pl.BlockSpec((K, tn), lambda j, i: (0, j + col_block0)),
                  pl.BlockSpec((1, tn), lambda j, i: (0, j + col_block0))],
        out_specs=pl.BlockSpec((tm, tn), lambda j, i: (i, j)),
        scratch_shapes=[pltpu.VMEM((K, tn), BF16)],
        compiler_params=_cparams(("parallel", "arbitrary"), 48),
        name="in_proj",
    )(a, w, col_scale)


def _cast_specs(weights, n_grid, step_of):
    specs = []
    for w in weights:
        rows = w.shape[0] // n_grid
        assert w.shape[0] % n_grid == 0 and rows % BF16_ROWS == 0
        specs.append(pl.BlockSpec((rows, w.shape[1]), lambda *g: (step_of(*g), 0)))
    return specs


def _cast_pieces(cast_in, cast_out):
    return [(src_ref, dst_ref, r0) for src_ref, dst_ref in zip(cast_in, cast_out)
            for r0 in range(0, src_ref.shape[0], BF16_ROWS)]


def _cast_some(pieces, m, n):
    for src_ref, dst_ref, r0 in pieces[m * len(pieces) // n:(m + 1) * len(pieces) // n]:
        dst_ref[r0:r0 + BF16_ROWS, :] = src_ref[r0:r0 + BF16_ROWS, :].astype(BF16)


NEG_BIG = -1e30
LOG2E = 1.4426950408889634


def _attn_kernel(q_ref, k_ref, v_ref, bias_ref, gain_ref, *rest, n_chunks, n_cast):
    cast_in, mod_in = rest[:n_cast], rest[n_cast:n_cast + 3]
    o_ref = rest[n_cast + 3]
    cast_out, mod_out = rest[n_cast + 4:2 * n_cast + 4], rest[2 * n_cast + 4]
    kta, ktb, vpa, vpb, tab, s_a, s_b, s_c = rest[2 * n_cast + 5:]
    T = n_chunks * CHUNK
    n_steps = T // QBLK
    head0 = lax.broadcasted_iota(jnp.int32, (QBLK, LANES), 1) < ATTN_HEAD_DIM
    m0 = jnp.where(head0, 1.0, 0.0).astype(BF16)
    m1 = jnp.where(head0, 0.0, 1.0).astype(BF16)

    head0_t = lax.broadcasted_iota(jnp.int32, (LANES, QBLK), 0) < ATTN_HEAD_DIM
    mt0 = jnp.where(head0_t, 1.0, 0.0).astype(BF16)
    mt1 = jnp.where(head0_t, 0.0, 1.0).astype(BF16)

    def prep(blk):
        rows = slice(blk * QBLK, (blk + 1) * QBLK)
        kt = k_ref[0, rows, :].T
        v = v_ref[0, rows, :]
        kta[:, rows] = kt * mt0
        ktb[:, rows] = kt * mt1
        vpa[rows, 0:LANES] = v * m0
        vpb[rows, 0:LANES] = v * m1
        vpa[rows, LANES:2 * LANES] = m0
        vpb[rows, LANES:2 * LANES] = m1

    @pl.when(pl.program_id(1) == 0)
    def _():
        qry = lax.broadcasted_iota(jnp.int32, (QBLK, KBLK), 0)
        key = lax.broadcasted_iota(jnp.int32, (QBLK, KBLK), 1)
        in_band = ((qry < CHUNK) & (key < BAND)) | ((qry >= CHUNK) & (key >= CHUNK))
        for hh in range(2):
            g = jnp.broadcast_to(bias_ref[0, hh:hh + 1, :], (QBLK, BIAS_W))
            t = pltpu.roll(g, BIAS_W - QBLK, 1, stride=1, stride_axis=0)[:, :KBLK]
            tab[hh] = jnp.where(in_band, t, NEG_BIG).astype(BF16)

    _mod_block(*mod_in, mod_out)

    eye = (lax.broadcasted_iota(jnp.int32, (QBLK, QBLK), 0)
           == lax.broadcasted_iota(jnp.int32, (QBLK, QBLK), 1)).astype(BF16)
    gain = gain_ref[0]

    def band_of(m):
        hi = (m + 1) * QBLK
        lo = max(0, hi - KBLK)
        return slice(lo, hi), KBLK - (hi - lo)

    def scores(m, dst):
        prep(m)
        band, col0 = band_of(m)
        lhs = jnp.concatenate([q_ref[0, m * QBLK:(m + 1) * QBLK, :], eye], axis=1)
        rhs = jnp.concatenate([jnp.concatenate([kta[:, band], ktb[:, band]], axis=1),
                               jnp.concatenate([tab[0, :, col0:], tab[1, :, col0:]], axis=1)], axis=0)
        dst[:, 0:rhs.shape[1]] = jnp.dot(lhs, rhs, preferred_element_type=F32)

    def finish(m, src):
        band, col0 = band_of(m)
        w = KBLK - col0
        s = src[:, 0:2 * w]
        p = jnp.concatenate(
            [jnp.exp2(sh - jnp.max(sh, axis=-1, keepdims=True)) for sh in (s[:, 0:w], s[:, w:2 * w])],
            axis=1).astype(BF16)
        pv = jnp.dot(p, jnp.concatenate([vpa[band, :], vpb[band, :]], axis=0), preferred_element_type=F32)
        o = pv[:, 0:LANES] * (1.0 / pv[:, LANES:2 * LANES])
        o2 = o * o
        ms0 = jnp.sum(jnp.where(head0, o2, 0.0), axis=-1, keepdims=True) / ATTN_HEAD_DIM
        ms1 = jnp.sum(jnp.where(head0, 0.0, o2), axis=-1, keepdims=True) / ATTN_HEAD_DIM
        y = o * lax.rsqrt(jnp.where(head0, ms0, ms1) + EPS) * gain
        o_ref[0, m * QBLK:(m + 1) * QBLK, :] = y.astype(o_ref.dtype)

    pieces = _cast_pieces(cast_in, cast_out)
    bufs = (s_a, s_b, s_c)
    depth = len(bufs)
    ahead = depth - 1
    for m in range(ahead):
        scores(m, bufs[m % depth])
    for m in range(n_steps):
        if m + ahead < n_steps:
            scores(m + ahead, bufs[(m + ahead) % depth])
        finish(m, bufs[m % depth])
        _cast_some(pieces, m, n_steps)


def _attention(q_arr, kv_arr, bias_vec, attn_gain, cast_weights, c8, w_ada, b_ada, mod_col0):
    B, T, W = q_arr.shape
    n_pairs = W // LANES
    n_grid = n_pairs * B
    cast_specs = _cast_specs(cast_weights, n_grid, lambda h, b: h * B + b)
    D = c8.shape[1]
    n_mod = w_ada.shape[1] - mod_col0
    tn_mod = n_mod // n_grid
    assert n_mod % n_grid == 0 and tn_mod % LANES == 0 and mod_col0 % tn_mod == 0
    mod_blk0 = mod_col0 // tn_mod
    outs = pl.pallas_call(
        functools.partial(_attn_kernel, n_chunks=T // CHUNK, n_cast=len(cast_weights)),
        out_shape=(jax.ShapeDtypeStruct((B, T, W), BF16),
                   *[jax.ShapeDtypeStruct(w.shape, BF16) for w in cast_weights],
                   jax.ShapeDtypeStruct((B, n_mod), F32)),
        grid=(n_pairs, B),
        in_specs=[pl.BlockSpec((1, T, LANES), lambda h, b: (b, 0, h)),
                  pl.BlockSpec((1, T, LANES), lambda h, b: (b, 0, h)),
                  pl.BlockSpec((1, T, LANES), lambda h, b: (b, 0, n_pairs + h)),
                  pl.BlockSpec((1, 2, BIAS_W), lambda h, b: (h, 0, 0)),
                  pl.BlockSpec((1, 1, LANES), lambda h, b: (h, 0, 0)),
                  *cast_specs,
                  pl.BlockSpec((SUBLANES, D), lambda h, b: (0, 0)),
                  pl.BlockSpec((D, tn_mod), lambda h, b: (0, mod_blk0 + h * B + b)),
                  pl.BlockSpec((1, tn_mod), lambda h, b: (0, mod_blk0 + h * B + b))],
        out_specs=(pl.BlockSpec((1, T, LANES), lambda h, b: (b, 0, h)), *cast_specs,
                   pl.BlockSpec((B, tn_mod), lambda h, b: (0, h * B + b))),
        scratch_shapes=[pltpu.VMEM((LANES, T), BF16),
                        pltpu.VMEM((LANES, T), BF16),
                        pltpu.VMEM((T, 2 * LANES), BF16),
                        pltpu.VMEM((T, 2 * LANES), BF16),
                        pltpu.VMEM((2, QBLK, KBLK), BF16),
                        pltpu.VMEM((QBLK, 2 * KBLK), F32),
                        pltpu.VMEM((QBLK, 2 * KBLK), F32),
                        pltpu.VMEM((QBLK, 2 * KBLK), F32)],
        compiler_params=_cparams(("parallel", "arbitrary"), 48),
        name="chunk_attention",
    )(q_arr, kv_arr, kv_arr, bias_vec, attn_gain, *cast_weights, c8, w_ada, b_ada)
    return outs[0], outs[-1], outs[1:-1]


HG_CHUNK = 256
HG_LEVELS = (128, 64, 32, 16, 8, 4, 2, 1)
assert HG_LEVELS[-1] == 1


def _hgrn_kernel(q_ref, f_ref, i_ref, g_ref, lbl_ref, gn_ref, *rest, n_steps, n_cast):
    cast_in, o_ref, cast_out = rest[:n_cast], rest[n_cast], rest[n_cast + 1:]
    pieces = _cast_pieces(cast_in, cast_out)
    C = HG_CHUNK
    H2 = C // 2
    Dk = REC_HEAD_DIM
    nt = (((1,), (1,)), ((), ()))
    lbl = lbl_ref[0]
    e = jnp.exp(lbl - jnp.max(lbl, axis=0, keepdims=True))
    lb = e[0:1, :] / jnp.sum(e, axis=0, keepdims=True)
    c1 = 0.5 * (1.0 - lb)
    gn = gn_ref[...]

    r = lax.broadcasted_iota(jnp.int32, (C, C), 0)
    s = lax.broadcasted_iota(jnp.int32, (C, C), 1)
    tril = (s <= r).astype(BF16)
    rh = lax.broadcasted_iota(jnp.int32, (H2, H2), 0)
    sh = lax.broadcasted_iota(jnp.int32, (H2, H2), 1)
    lvl_mask = {m: ((rh // (2 * m)) == (sh // (2 * m))) & (((rh // m) % 2) == 1) & (((sh // m) % 2) == 0)
                for m in HG_LEVELS[1:]}
    sub = lax.broadcasted_iota(jnp.int32, (C // SUBLANES, SUBLANES, Dk), 1)

    def roll8(x, d):
        return pltpu.roll(x.reshape(C // SUBLANES, SUBLANES, Dk), d, 1)

    def front(n):
        rows = slice(n * C, (n + 1) * C)
        c1t = c1 * jnp.tanh(0.5 * f_ref[0, rows, :])
        f = (1.0 - c1) + c1t
        kk = c1 - c1t
        qq = _silu_tanh(q_ref[0, rows, :])
        ii = i_ref[0, rows, :]
        cat = jnp.concatenate(_split3(jnp.log2(f)), axis=1)
        return dict(rows=rows, kk=kk, qq=qq, ii=ii, ii_bf=ii.astype(BF16), cat=cat)

    def level_z(v, m):
        b, kk, qq = v["b"], v["kk"], v["qq"]
        if m >= SUBLANES:
            parts, srcs = [], []
            for p in range(0, C, 2 * m):
                bm = b[p + m - 1:p + m, :]
                parts += [bm - b[p:p + m], b[p + m:p + 2 * m] - bm]
                srcs += [kk[p:p + m], qq[p + m:p + 2 * m]]
            arg = jnp.concatenate(parts, axis=0)
            src = jnp.concatenate(srcs, axis=0)
        else:
            b3 = b.reshape(C // SUBLANES, SUBLANES, Dk)
            if m == 1:
                bm = jnp.where(sub % 2 == 1, roll8(b, 1), b3)
            else:
                bm = jnp.broadcast_to(b3[:, m - 1:m, :], b3.shape)
                for p in range(2 * m, SUBLANES, 2 * m):
                    bm = jnp.where(sub >= p, jnp.broadcast_to(b3[:, p + m - 1:p + m, :], b3.shape), bm)
            upper = (sub // m) % 2 == 1
            arg = ((b3 - bm) * jnp.where(upper, 1.0, -1.0)).reshape(C, Dk)
            src = jnp.where(upper, qq.reshape(b3.shape), kk.reshape(b3.shape)).reshape(C, Dk)
        return (src * jnp.exp2(arg)).astype(BF16)

    vs = [front(j) for j in range(n_steps)]
    bb = jnp.dot(tril, jnp.concatenate([v["cat"] for v in vs], axis=1), preferred_element_type=F32)
    for j, v in enumerate(vs):
        c0 = 3 * Dk * j
        v["b"] = bb[:, c0:c0 + Dk] + bb[:, c0 + Dk:c0 + 2 * Dk] + bb[:, c0 + 2 * Dk:c0 + 3 * Dk]
    a_lo = [None] * n_steps
    a_d0 = [jnp.zeros((H2, H2), F32)] * n_steps
    a_d1 = [jnp.zeros((H2, H2), F32)] * n_steps
    for li, m in enumerate(HG_LEVELS):
        for j, v in enumerate(vs):
            z = level_z(v, m)
            if li == 0:
                a_lo[j] = lax.dot_general(z[H2:], z[:H2], nt, preferred_element_type=F32)
            else:
                g = lax.dot_general(z, z, nt, preferred_element_type=F32)
                a_d0[j] = jnp.where(lvl_mask[m], g[:H2, :H2], a_d0[j])
                a_d1[j] = jnp.where(lvl_mask[m], g[H2:, H2:], a_d1[j])
    intra = []
    for j, v in enumerate(vs):
        a = jnp.concatenate([jnp.concatenate([a_d0[j], jnp.zeros((H2, H2), F32)], axis=1),
                             jnp.concatenate([a_lo[j], a_d1[j]], axis=1)], axis=0).astype(BF16)
        o_diag = jnp.sum(v["qq"] * v["kk"], axis=-1, keepdims=True) * v["ii"]
        intra.append(jnp.dot(a, v["ii_bf"], preferred_element_type=F32) + o_diag)
    st = jnp.zeros((Dk, Dk), F32)
    for j, v in enumerate(vs):
        b = v["b"]
        b_last = b[C - 1:C, :]
        qe = (v["qq"] * jnp.exp2(b)).astype(BF16)
        o = intra[j] + lax.dot_general(qe, st.astype(BF16), nt, preferred_element_type=F32)
        ke = (v["kk"] * jnp.exp2(b_last - b)).astype(BF16)
        st = st * jnp.exp2(b_last) + lax.dot_general(
            v["ii_bf"], ke, (((0,), (0,)), ((), ())), preferred_element_type=F32)
        ms = jnp.mean(o * o, axis=-1, keepdims=True)
        y = o * lax.rsqrt(ms + EPS) * gn
        y = y * _silu_tanh(g_ref[0, v["rows"], :])
        o_ref[0, v["rows"], :] = y.astype(o_ref.dtype)
        _cast_some(pieces, j, n_steps)


def _hgrn(proj_b, lb_logits_h, gnorm_g, cast_weights):
    B, T, W4 = proj_b.shape
    W = W4 // 4
    H = W // REC_HEAD_DIM
    n_slots = lb_logits_h.shape[1]
    blk = lambda off: pl.BlockSpec((1, T, REC_HEAD_DIM), lambda b, h, off=off: (b, 0, off * H + h))
    cast_specs = _cast_specs(cast_weights, B * H, lambda b, h: b * H + h)
    outs = pl.pallas_call(
        functools.partial(_hgrn_kernel, n_steps=T // HG_CHUNK, n_cast=len(cast_weights)),
        out_shape=(jax.ShapeDtypeStruct((B, T, W), BF16),
                   *[jax.ShapeDtypeStruct(w.shape, BF16) for w in cast_weights]),
        grid=(B, H),
        in_specs=[blk(0), blk(1), blk(2), blk(3),
                  pl.BlockSpec((1, n_slots, REC_HEAD_DIM), lambda b, h: (h, 0, 0)),
                  pl.BlockSpec((1, REC_HEAD_DIM), lambda b, h: (0, 0)),
                  *cast_specs],
        out_specs=(pl.BlockSpec((1, T, REC_HEAD_DIM), lambda b, h: (b, 0, h)), *cast_specs),
        compiler_params=_cparams(("parallel", "parallel"), 40),
        name="hgrn2",
    )(proj_b, proj_b, proj_b, proj_b, lb_logits_h, gnorm_g, *cast_weights)
    return outs[0], outs[1:]


def _outproj_kernel(oa_ref, ob_ref, w_ref, x_ref, mod_ref, g_ref, b_ref, x1_ref, *, alpha, sub):
    gate1 = _mod_row(mod_ref, pl.program_id(0), 0, x_ref.shape[-1])
    for r0 in range(0, x_ref.shape[1], sub):
        rs = slice(r0, r0 + sub)
        o_cat = jnp.concatenate([oa_ref[0, rs, :], ob_ref[0, rs, :]], axis=1)
        mix = jnp.dot(o_cat, w_ref[...], preferred_element_type=F32)
        x1_ref[0, rs, :] = _ln_rows(alpha * x_ref[0, rs, :] + gate1 * mix) * g_ref[...] + b_ref[...]


def _outproj(o_a, o_b, w_o, x, mod, ln_g, ln_b, alpha, tm=512, sub=128):
    B, T, D = x.shape
    Wa, Wb = o_a.shape[-1], o_b.shape[-1]
    return pl.pallas_call(
        functools.partial(_outproj_kernel, alpha=alpha, sub=sub),
        out_shape=jax.ShapeDtypeStruct((B, T, D), F32),
        grid=(B, T // tm),
        in_specs=[pl.BlockSpec((1, tm, Wa), lambda b, i: (b, i, 0)),
                  pl.BlockSpec((1, tm, Wb), lambda b, i: (b, i, 0)),
                  pl.BlockSpec((Wa + Wb, D), lambda b, i: (0, 0)),
                  pl.BlockSpec((1, tm, D), lambda b, i: (b, i, 0)),
                  pl.BlockSpec((B, (N_MOD - N_MOD_PRE) * D), lambda b, i: (0, 0)),
                  pl.BlockSpec((1, D), lambda b, i: (0, 0)),
                  pl.BlockSpec((1, D), lambda b, i: (0, 0))],
        out_specs=pl.BlockSpec((1, tm, D), lambda b, i: (b, i, 0)),
        compiler_params=_cparams(("parallel", "parallel"), 48),
        name="out_proj_ln1",
    )(o_a, o_b, w_o, x, mod, ln_g, ln_b)


def _ffn_kernel(wg_ref, wu_ref, wo_ref, x_ref, mod_ref, g_ref, b_ref, o_ref, h_sc, *, alpha, sub):
    f = pl.program_id(2)
    last = pl.num_programs(2) - 1

    def partial_out(rs):
        h = h_sc[rs, :]
        gate = jnp.dot(h, wg_ref[...], preferred_element_type=F32)
        up = jnp.dot(h, wu_ref[...], preferred_element_type=F32)
        act = (_silu_tanh(gate) * up).astype(BF16)
        return jnp.dot(act, wo_ref[...], preferred_element_type=F32)

    everything = slice(0, o_ref.shape[1])

    @pl.when(f == 0)
    def _():
        shift2 = _mod_row(mod_ref, pl.program_id(0), 1, x_ref.shape[-1])
        scale2 = _mod_row(mod_ref, pl.program_id(0), 2, x_ref.shape[-1])
        for r0 in range(0, o_ref.shape[1], sub):
            rs = slice(r0, r0 + sub)
            h_sc[rs, :] = (_ln_rows(x_ref[0, rs, :]) * (1.0 + scale2) + shift2).astype(BF16)
        o_ref[0] = partial_out(everything)

    @pl.when((f > 0) & (f < last))
    def _():
        o_ref[0] += partial_out(everything)

    @pl.when(f == last)
    def _():
        gate2 = _mod_row(mod_ref, pl.program_id(0), 3, x_ref.shape[-1])
        for r0 in range(0, o_ref.shape[1], sub):
            rs = slice(r0, r0 + sub)
            y = o_ref[0, rs, :] + partial_out(rs)
            o_ref[0, rs, :] = _ln_rows(alpha * x_ref[0, rs, :] + gate2 * y) * g_ref[...] + b_ref[...]


def _ffn(w_in, w_out, x1, mod, ln_g, ln_b, alpha, tm=1024, tf=512, sub=256):
    B, T, D = x1.shape
    F = w_out.shape[0]
    nf = F // tf
    return pl.pallas_call(
        functools.partial(_ffn_kernel, alpha=alpha, sub=sub),
        out_shape=jax.ShapeDtypeStruct((B, T, D), F32),
        grid=(B, T // tm, nf),
        in_specs=[pl.BlockSpec((D, tf), lambda b, i, f: (0, f)),
                  pl.BlockSpec((D, tf), lambda b, i, f: (0, nf + f)),
                  pl.BlockSpec((tf, D), lambda b, i, f: (f, 0)),
                  pl.BlockSpec((1, tm, D), lambda b, i, f: (b, i, 0)),
                  pl.BlockSpec((B, (N_MOD - N_MOD_PRE) * D), lambda b, i, f: (0, 0)),
                  pl.BlockSpec((1, D), lambda b, i, f: (0, 0)),
                  pl.BlockSpec((1, D), lambda b, i, f: (0, 0))],
        out_specs=pl.BlockSpec((1, tm, D), lambda b, i, f: (b, i, 0)),
        scratch_shapes=[pltpu.VMEM((tm, D), BF16)],
        compiler_params=_cparams(("parallel", "parallel", "arbitrary"), 60),
        name="swiglu_ffn_ln2",
    )(w_in, w_in, w_out, x1, mod, ln_g, ln_b)


def _bias_vectors(rel_bias):
    H, n_rel = rel_bias.shape
    max_rel = (n_rel - 1) // 2
    u = jnp.arange(BIAS_W)
    idx = jnp.clip(KBLK - u, -max_rel, max_rel) + max_rel
    return rel_bias[:, idx]


def kernel(x, c, w_ada, b_ada, w_in, rel_bias, attn_norm_g, lb_logits, gnorm_g, w_o,
           ln1_g, ln1_b, w_ffn_in, w_ffn_out, ln2_g, ln2_b):
    B, T, D = x.shape
    depth = w_ada.shape[0]
    alpha = (2 * depth) ** 0.25
    attn_w = attn_norm_g.shape[1]
    rec_w = lb_logits.shape[1]
    n_slots = lb_logits.shape[0]
    rec_heads = rec_w // REC_HEAD_DIM
    assert depth == 1 and n_slots == depth + 1
    for layer in range(depth):
        c8 = jnp.pad(c, ((0, SUBLANES - B), (0, 0)))
        b_ada2 = b_ada[layer].reshape(1, -1)
        mod_pre = _mod(c8, w_ada[layer], b_ada2, N_MOD_PRE * D, B)
        q_scale = jnp.where(jnp.arange(w_in.shape[2]) < attn_w, ATTN_HEAD_DIM ** -0.5 * LOG2E, 1.0)
        q_scale = q_scale.astype(F32).reshape(1, -1)
        h1, q_a = _ln_proj(x, mod_pre, w_in[layer], q_scale, tn=attn_w)
        h1 = h1.reshape(B * T, D)
        kv_a = _matmul(h1, w_in[layer], q_scale, attn_w, 2 * attn_w, BF16).reshape(B, T, 2 * attn_w)
        proj_b = _matmul(h1, w_in[layer], q_scale, 3 * attn_w, 4 * rec_w, F32).reshape(B, T, 4 * rec_w)
        bias_vec = (_bias_vectors(rel_bias[layer]) * LOG2E).reshape(-1, 2, BIAS_W)
        o_a, mod, (w_o_bf, w_ffn_out_bf) = _attention(
            q_a, kv_a, bias_vec, attn_norm_g[layer].reshape(-1, 1, LANES), (w_o[layer], w_ffn_out[layer]),
            c8, w_ada[layer], b_ada2, N_MOD_PRE * D)
        lbl = lb_logits.reshape(n_slots, rec_heads, REC_HEAD_DIM).transpose(1, 0, 2)
        o_b, (w_ffn_in_bf,) = _hgrn(proj_b, lbl, gnorm_g[layer].reshape(1, REC_HEAD_DIM), (w_ffn_in[layer],))
        x = _outproj(o_a, o_b, w_o_bf, x, mod,
                     ln1_g[layer].reshape(1, D), ln1_b[layer].reshape(1, D), alpha)
        x = _ffn(w_ffn_in_bf, w_ffn_out_bf, x, mod,
                 ln2_g[layer].reshape(1, D), ln2_b[layer].reshape(1, D), alpha)
    return x
```

```python
import functools

import jax
import jax.numpy as jnp
from jax import lax
from jax.experimental import pallas as pl
from jax.experimental.pallas import tpu as pltpu

F32 = jnp.float32
BF16 = jnp.bfloat16

CHUNK = 64
N_PAST_CHUNKS = 8
BAND = (N_PAST_CHUNKS + 1) * CHUNK
ATTN_HEAD_DIM = 64
REC_HEAD_DIM = 128
N_MOD = 6
N_MOD_PRE = 2
EPS = 1e-5
LANES = 128
SUBLANES = 8
BF16_ROWS = 16
QBLK = 2 * CHUNK
KBLK = BAND + CHUNK
BIAS_W = KBLK + QBLK

MIB = 1024 * 1024


def _cparams(sem, vmem_mib):
    return pltpu.CompilerParams(dimension_semantics=sem, vmem_limit_bytes=vmem_mib * MIB)


def _silu_tanh(x):
    h = 0.5 * x
    return h + h * jnp.tanh(h)


def _mod_row(mod_ref, b, r, d):
    return mod_ref[pl.ds(b, 1), r * d:(r + 1) * d]


def _ln_rows(x):
    mu = jnp.mean(x, axis=-1, keepdims=True)
    xc = x - mu
    var = jnp.mean(xc * xc, axis=-1, keepdims=True)
    return xc * lax.rsqrt(var + EPS)


def _split3(x):
    hi = x.astype(BF16)
    r1 = x - hi.astype(F32)
    mid = r1.astype(BF16)
    lo = (r1 - mid.astype(F32)).astype(BF16)
    return hi, mid, lo


def _mod_block(c_ref, w_ref, b_ref, o_ref):
    act = jnp.concatenate(_split3(_silu_tanh(c_ref[...])), axis=0)
    w = w_ref[...]
    w_hi = w.astype(BF16)
    w_mid = (w - w_hi.astype(F32)).astype(BF16)
    acc = jnp.dot(act, w_hi, preferred_element_type=F32) + jnp.dot(act, w_mid, preferred_element_type=F32)
    out = acc[0:SUBLANES] + acc[SUBLANES:2 * SUBLANES] + acc[2 * SUBLANES:3 * SUBLANES]
    o_ref[...] = out[0:o_ref.shape[0]] + b_ref[...]


def _mod(c8, w_ada, b_ada, n_cols, n_rows, tn=1024):
    D = c8.shape[1]
    return pl.pallas_call(
        _mod_block,
        out_shape=jax.ShapeDtypeStruct((n_rows, n_cols), F32),
        grid=(n_cols // tn,),
        in_specs=[pl.BlockSpec((SUBLANES, D), lambda j: (0, 0)),
                  pl.BlockSpec((D, tn), lambda j: (0, j)),
                  pl.BlockSpec((1, tn), lambda j: (0, j))],
        out_specs=pl.BlockSpec((n_rows, tn), lambda j: (0, j)),
        compiler_params=_cparams(("parallel",), 40),
        name="adaln_mod",
    )(c8, w_ada, b_ada)


def _ln_proj_kernel(x_ref, mod_ref, w_ref, s_ref, h_ref, o_ref, w_bf, *, sub):
    b = pl.program_id(0)

    @pl.when((b == 0) & (pl.program_id(1) == 0))
    def _():
        w_bf[...] = (w_ref[...] * s_ref[...]).astype(BF16)

    d = x_ref.shape[-1]
    shift = _mod_row(mod_ref, b, 0, d)
    scale = _mod_row(mod_ref, b, 1, d)
    for r0 in range(0, x_ref.shape[1], sub):
        rs = slice(r0, r0 + sub)
        h_ref[0, rs, :] = (_ln_rows(x_ref[0, rs, :]) * (1.0 + scale) + shift).astype(BF16)
    o_ref[0] = jnp.dot(h_ref[0], w_bf[...], preferred_element_type=F32).astype(o_ref.dtype)


def _ln_proj(x, mod, w, col_scale, tm=1024, tn=1024, sub=256):
    B, T, D = x.shape
    return pl.pallas_call(
        functools.partial(_ln_proj_kernel, sub=sub),
        out_shape=(jax.ShapeDtypeStruct((B, T, D), BF16), jax.ShapeDtypeStruct((B, T, tn), BF16)),
        grid=(B, T // tm),
        in_specs=[pl.BlockSpec((1, tm, D), lambda b, i: (b, i, 0)),
                  pl.BlockSpec((B, N_MOD_PRE * D), lambda b, i: (0, 0)),
                  pl.BlockSpec((D, tn), lambda b, i: (0, 0)),
                  pl.BlockSpec((1, tn), lambda b, i: (0, 0))],
        out_specs=(pl.BlockSpec((1, tm, D), lambda b, i: (b, i, 0)),
                   pl.BlockSpec((1, tm, tn), lambda b, i: (b, i, 0))),
        scratch_shapes=[pltpu.VMEM((D, tn), BF16)],
        compiler_params=_cparams(("arbitrary", "arbitrary"), 56),
        name="ln_in_proj",
    )(x, mod, w, col_scale)


def _matmul_kernel(a_ref, w_ref, s_ref, o_ref, w_bf):
    @pl.when(pl.program_id(1) == 0)
    def _():
        w_bf[...] = (w_ref[...] * s_ref[...]).astype(BF16)

    o_ref[...] = jnp.dot(a_ref[...], w_bf[...], preferred_element_type=F32).astype(o_ref.dtype)


def _matmul(a, w, col_scale, col0, n_out, out_dtype, tm=1024, tn=1024):
    M, K = a.shape
    assert col0 % tn == 0 and n_out % tn == 0 and M % tm == 0
    col_block0 = col0 // tn
    return pl.pallas_call(
        _matmul_kernel,
        out_shape=jax.ShapeDtypeStruct((M, n_out), out_dtype),
        grid=(n_out // tn, M // tm),
        in_specs=[pl.BlockSpec((tm, K), lambda j, i: (i, 0)),
                  pl.BlockSpec((K, tn), lambda j, i: (0, j + col_block0)),
                  pl.BlockSpec((1, tn), lambda j, i: (0, j + col_block0))],
        out_specs=pl.BlockSpec((tm, tn), lambda j, i: (i, j)),
        scratch_shapes=[pltpu.VMEM((K, tn), BF16)],
        compiler_params=_cparams(("parallel", "arbitrary"), 48),
        name="in_proj",
    )(a, w, col_scale)


def _cast_specs(weights, n_grid, step_of):
    specs = []
    for w in weights:
        rows = w.shape[0] // n_grid
        assert w.shape[0] % n_grid == 0 and rows % BF16_ROWS == 0
        specs.append(pl.BlockSpec((rows, w.shape[1]), lambda *g: (step_of(*g), 0)))
    return specs


def _cast_pieces(cast_in, cast_out):
    return [(src_ref, dst_ref, r0) for src_ref, dst_ref in zip(cast_in, cast_out)
            for r0 in range(0, src_ref.shape[0], BF16_ROWS)]


def _cast_some(pieces, m, n):
    for src_ref, dst_ref, r0 in pieces[m * len(pieces) // n:(m + 1) * len(pieces) // n]:
        dst_ref[r0:r0 + BF16_ROWS, :] = src_ref[r0:r0 + BF16_ROWS, :].astype(BF16)


NEG_BIG = -1e30
LOG2E = 1.4426950408889634


def _attn_kernel(q_ref, k_ref, v_ref, bias_ref, gain_ref, *rest, n_chunks, n_cast):
    cast_in, mod_in = rest[:n_cast], rest[n_cast:n_cast + 3]
    o_ref = rest[n_cast + 3]
    cast_out, mod_out = rest[n_cast + 4:2 * n_cast + 4], rest[2 * n_cast + 4]
    kta, ktb, vpa, vpb, tab, s_a, s_b, s_c = rest[2 * n_cast + 5:]
    T = n_chunks * CHUNK
    n_steps = T // QBLK
    head0 = lax.broadcasted_iota(jnp.int32, (QBLK, LANES), 1) < ATTN_HEAD_DIM
    m0 = jnp.where(head0, 1.0, 0.0).astype(BF16)
    m1 = jnp.where(head0, 0.0, 1.0).astype(BF16)

    head0_t = lax.broadcasted_iota(jnp.int32, (LANES, QBLK), 0) < ATTN_HEAD_DIM
    mt0 = jnp.where(head0_t, 1.0, 0.0).astype(BF16)
    mt1 = jnp.where(head0_t, 0.0, 1.0).astype(BF16)

    def prep(blk):
        rows = slice(blk * QBLK, (blk + 1) * QBLK)
        kt = k_ref[0, rows, :].T
        v = v_ref[0, rows, :]
        kta[:, rows] = kt * mt0
        ktb[:, rows] = kt * mt1
        vpa[rows, 0:LANES] = v * m0
        vpb[rows, 0:LANES] = v * m1
        vpa[rows, LANES:2 * LANES] = m0
        vpb[rows, LANES:2 * LANES] = m1

    @pl.when(pl.program_id(1) == 0)
    def _():
        qry = lax.broadcasted_iota(jnp.int32, (QBLK, KBLK), 0)
        key = lax.broadcasted_iota(jnp.int32, (QBLK, KBLK), 1)
        in_band = ((qry < CHUNK) & (key < BAND)) | ((qry >= CHUNK) & (key >= CHUNK))
        for hh in range(2):
            g = jnp.broadcast_to(bias_ref[0, hh:hh + 1, :], (QBLK, BIAS_W))
            t = pltpu.roll(g, BIAS_W - QBLK, 1, stride=1, stride_axis=0)[:, :KBLK]
            tab[hh] = jnp.where(in_band, t, NEG_BIG).astype(BF16)

    eye = (lax.broadcasted_iota(jnp.int32, (QBLK, QBLK), 0)
           == lax.broadcasted_iota(jnp.int32, (QBLK, QBLK), 1)).astype(BF16)
    gain = gain_ref[0]

    def band_of(m):
        hi = (m + 1) * QBLK
        lo = max(0, hi - KBLK)
        return slice(lo, hi), KBLK - (hi - lo)

    def scores(m, dst):
        prep(m)
        band, col0 = band_of(m)
        lhs = jnp.concatenate([q_ref[0, m * QBLK:(m + 1) * QBLK, :], eye], axis=1)
        rhs = jnp.concatenate([jnp.concatenate([kta[:, band], ktb[:, band]], axis=1),
                               jnp.concatenate([tab[0, :, col0:], tab[1, :, col0:]], axis=1)], axis=0)
        dst[:, 0:rhs.shape[1]] = jnp.dot(lhs, rhs, preferred_element_type=F32)

    def finish(m, src):
        band, col0 = band_of(m)
        w = KBLK - col0
        s = src[:, 0:2 * w]
        p = jnp.concatenate(
            [jnp.exp2(sh - jnp.max(sh, axis=-1, keepdims=True)) for sh in (s[:, 0:w], s[:, w:2 * w])],
            axis=1).astype(BF16)
        pv = jnp.dot(p, jnp.concatenate([vpa[band, :], vpb[band, :]], axis=0), preferred_element_type=F32)
        o = pv[:, 0:LANES] * (1.0 / pv[:, LANES:2 * LANES])
        o2 = o * o
        ms0 = jnp.sum(jnp.where(head0, o2, 0.0), axis=-1, keepdims=True) / ATTN_HEAD_DIM
        ms1 = jnp.sum(jnp.where(head0, 0.0, o2), axis=-1, keepdims=True) / ATTN_HEAD_DIM
        y = o * lax.rsqrt(jnp.where(head0, ms0, ms1) + EPS) * gain
        o_ref[0, m * QBLK:(m + 1) * QBLK, :] = y.astype(o_ref.dtype)

    pieces = _cast_pieces(cast_in, cast_out)
    bufs = (s_a, s_b, s_c)
    depth = len(bufs)
    ahead = depth - 1
    for m in range(ahead):
        scores(m, bufs[m % depth])
    for m in range(n_steps):
        if m + ahead < n_steps:
            scores(m + ahead, bufs[(m + ahead) % depth])
        finish(m, bufs[m % depth])
        _cast_some(pieces, m, n_steps)
    _mod_block(*mod_in, mod_out)


def _attention(q_arr, kv_arr, bias_vec, attn_gain, cast_weights, c8, w_ada, b_ada, mod_col0):
    B, T, W = q_arr.shape
    n_pairs = W // LANES
    n_grid = n_pairs * B
    cast_specs = _cast_specs(cast_weights, n_grid, lambda h, b: h * B + b)
    D = c8.shape[1]
    n_mod = w_ada.shape[1] - mod_col0
    tn_mod = n_mod // n_grid
    assert n_mod % n_grid == 0 and tn_mod % LANES == 0 and mod_col0 % tn_mod == 0
    mod_blk0 = mod_col0 // tn_mod
    outs = pl.pallas_call(
        functools.partial(_attn_kernel, n_chunks=T // CHUNK, n_cast=len(cast_weights)),
        out_shape=(jax.ShapeDtypeStruct((B, T, W), BF16),
                   *[jax.ShapeDtypeStruct(w.shape, BF16) for w in cast_weights],
                   jax.ShapeDtypeStruct((B, n_mod), F32)),
        grid=(n_pairs, B),
        in_specs=[pl.BlockSpec((1, T, LANES), lambda h, b: (b, 0, h)),
                  pl.BlockSpec((1, T, LANES), lambda h, b: (b, 0, h)),
                  pl.BlockSpec((1, T, LANES), lambda h, b: (b, 0, n_pairs + h)),
                  pl.BlockSpec((1, 2, BIAS_W), lambda h, b: (h, 0, 0)),
                  pl.BlockSpec((1, 1, LANES), lambda h, b: (h, 0, 0)),
                  *cast_specs,
                  pl.BlockSpec((SUBLANES, D), lambda h, b: (0, 0)),
                  pl.BlockSpec((D, tn_mod), lambda h, b: (0, mod_blk0 + h * B + b)),
                  pl.BlockSpec((1, tn_mod), lambda h, b: (0, mod_blk0 + h * B + b))],
        out_specs=(pl.BlockSpec((1, T, LANES), lambda h, b: (b, 0, h)), *cast_specs,
                   pl.BlockSpec((B, tn_mod), lambda h, b: (0, h * B + b))),
        scratch_shapes=[pltpu.VMEM((LANES, T), BF16),
                        pltpu.VMEM((LANES, T), BF16),
                        pltpu.VMEM((T, 2 * LANES), BF16),
                        pltpu.VMEM((T, 2 * LANES), BF16),
                        pltpu.VMEM((2, QBLK, KBLK), BF16),
                        pltpu.VMEM((QBLK, 2 * KBLK), F32),
                        pltpu.VMEM((QBLK, 2 * KBLK), F32),
                        pltpu.VMEM((QBLK, 2 * KBLK), F32)],
        compiler_params=_cparams(("parallel", "arbitrary"), 48),
        name="chunk_attention",
    )(q_arr, kv_arr, kv_arr, bias_vec, attn_gain, *cast_weights, c8, w_ada, b_ada)
    return outs[0], outs[-1], outs[1:-1]


HG_CHUNK = 256
HG_LEVELS = (128, 64, 32, 16, 8, 4, 2, 1)
assert HG_LEVELS[-1] == 1


def _hgrn_kernel(q_ref, f_ref, i_ref, g_ref, lbl_ref, gn_ref, *rest, n_steps, n_cast):
    cast_in, o_ref, cast_out = rest[:n_cast], rest[n_cast], rest[n_cast + 1:]
    pieces = _cast_pieces(cast_in, cast_out)
    C = HG_CHUNK
    H2 = C // 2
    Dk = REC_HEAD_DIM
    nt = (((1,), (1,)), ((), ()))
    lbl = lbl_ref[0]
    e = jnp.exp(lbl - jnp.max(lbl, axis=0, keepdims=True))
    lb = e[0:1, :] / jnp.sum(e, axis=0, keepdims=True)
    c1 = 0.5 * (1.0 - lb)
    gn = gn_ref[...]

    r = lax.broadcasted_iota(jnp.int32, (C, C), 0)
    s = lax.broadcasted_iota(jnp.int32, (C, C), 1)
    tril = (s <= r).astype(BF16)
    rh = lax.broadcasted_iota(jnp.int32, (H2, H2), 0)
    sh = lax.broadcasted_iota(jnp.int32, (H2, H2), 1)
    lvl_mask = {m: ((rh // (2 * m)) == (sh // (2 * m))) & (((rh // m) % 2) == 1) & (((sh // m) % 2) == 0)
                for m in HG_LEVELS[1:]}
    sub = lax.broadcasted_iota(jnp.int32, (C // SUBLANES, SUBLANES, Dk), 1)

    def roll8(x, d):
        return pltpu.roll(x.reshape(C // SUBLANES, SUBLANES, Dk), d, 1)

    def front(n):
        rows = slice(n * C, (n + 1) * C)
        c1t = c1 * jnp.tanh(0.5 * f_ref[0, rows, :])
        f = (1.0 - c1) + c1t
        kk = c1 - c1t
        qq = _silu_tanh(q_ref[0, rows, :])
        ii = i_ref[0, rows, :]
        cat = jnp.concatenate(_split3(jnp.log2(f)), axis=1)
        return dict(rows=rows, kk=kk, qq=qq, ii=ii, ii_bf=ii.astype(BF16), cat=cat)

    def level_z(v, m):
        b, kk, qq = v["b"], v["kk"], v["qq"]
        if m >= SUBLANES:
            parts, srcs = [], []
            for p in range(0, C, 2 * m):
                bm = b[p + m - 1:p + m, :]
                parts += [bm - b[p:p + m], b[p + m:p + 2 * m] - bm]
                srcs += [kk[p:p + m], qq[p + m:p + 2 * m]]
            arg = jnp.concatenate(parts, axis=0)
            src = jnp.concatenate(srcs, axis=0)
        else:
            b3 = b.reshape(C // SUBLANES, SUBLANES, Dk)
            if m == 1:
                bm = jnp.where(sub % 2 == 1, roll8(b, 1), b3)
            else:
                bm = jnp.broadcast_to(b3[:, m - 1:m, :], b3.shape)
                for p in range(2 * m, SUBLANES, 2 * m):
                    bm = jnp.where(sub >= p, jnp.broadcast_to(b3[:, p + m - 1:p + m, :], b3.shape), bm)
            upper = (sub // m) % 2 == 1
            arg = ((b3 - bm) * jnp.where(upper, 1.0, -1.0)).reshape(C, Dk)
            src = jnp.where(upper, qq.reshape(b3.shape), kk.reshape(b3.shape)).reshape(C, Dk)
        return (src * jnp.exp2(arg)).astype(BF16)

    vs = [front(j) for j in range(n_steps)]
    bb = jnp.dot(tril, jnp.concatenate([v["cat"] for v in vs], axis=1), preferred_element_type=F32)
    for j, v in enumerate(vs):
        c0 = 3 * Dk * j
        v["b"] = bb[:, c0:c0 + Dk] + bb[:, c0 + Dk:c0 + 2 * Dk] + bb[:, c0 + 2 * Dk:c0 + 3 * Dk]
    a_lo = [None] * n_steps
    a_d0 = [jnp.zeros((H2, H2), F32)] * n_steps
    a_d1 = [jnp.zeros((H2, H2), F32)] * n_steps
    for li, m in enumerate(HG_LEVELS):
        for j, v in enumerate(vs):
            z = level_z(v, m)
            if li == 0:
                a_lo[j] = lax.dot_general(z[H2:], z[:H2], nt, preferred_element_type=F32)
            else:
                g = lax.dot_general(z, z, nt, preferred_element_type=F32)
                a_d0[j] = jnp.where(lvl_mask[m], g[:H2, :H2], a_d0[j])
                a_d1[j] = jnp.where(lvl_mask[m], g[H2:, H2:], a_d1[j])
    intra = []
    for j, v in enumerate(vs):
        a = jnp.concatenate([jnp.concatenate([a_d0[j], jnp.zeros((H2, H2), F32)], axis=1),
                             jnp.concatenate([a_lo[j], a_d1[j]], axis=1)], axis=0).astype(BF16)
        o_diag = jnp.sum(v["qq"] * v["kk"], axis=-1, keepdims=True) * v["ii"]
        intra.append(jnp.dot(a, v["ii_bf"], preferred_element_type=F32) + o_diag)
    st = jnp.zeros((Dk, Dk), F32)
    for j, v in enumerate(vs):
        b = v["b"]
        b_last = b[C - 1:C, :]
        qe = (v["qq"] * jnp.exp2(b)).astype(BF16)
        o = intra[j] + lax.dot_general(qe, st.astype(BF16), nt, preferred_element_type=F32)
        ke = (v["kk"] * jnp.exp2(b_last - b)).astype(BF16)
        st = st * jnp.exp2(b_last) + lax.dot_general(
            v["ii_bf"], ke, (((0,), (0,)), ((), ())), preferred_element_type=F32)
        ms = jnp.mean(o * o, axis=-1, keepdims=True)
        y = o * lax.rsqrt(ms + EPS) * gn
        y = y * _silu_tanh(g_ref[0, v["rows"], :])
        o_ref[0, v["rows"], :] = y.astype(o_ref.dtype)
        _cast_some(pieces, j, n_steps)


def _hgrn(proj_b, lb_logits_h, gnorm_g, cast_weights):
    B, T, W4 = proj_b.shape
    W = W4 // 4
    H = W // REC_HEAD_DIM
    n_slots = lb_logits_h.shape[1]
    blk = lambda off: pl.BlockSpec((1, T, REC_HEAD_DIM), lambda b, h, off=off: (b, 0, off * H + h))
    cast_specs = _cast_specs(cast_weights, B * H, lambda b, h: b * H + h)
    outs = pl.pallas_call(
        functools.partial(_hgrn_kernel, n_steps=T // HG_CHUNK, n_cast=len(cast_weights)),
        out_shape=(jax.ShapeDtypeStruct((B, T, W), BF16),
                   *[jax.ShapeDtypeStruct(w.shape, BF16) for w in cast_weights]),
        grid=(B, H),
        in_specs=[blk(0), blk(1), blk(2), blk(3),
                  pl.BlockSpec((1, n_slots, REC_HEAD_DIM), lambda b, h: (h, 0, 0)),
                  pl.BlockSpec((1, REC_HEAD_DIM), lambda b, h: (0, 0)),
                  *cast_specs],
        out_specs=(pl.BlockSpec((1, T, REC_HEAD_DIM), lambda b, h: (b, 0, h)), *cast_specs),
        compiler_params=_cparams(("parallel", "parallel"), 40),
        name="hgrn2",
    )(proj_b, proj_b, proj_b, proj_b, lb_logits_h, gnorm_g, *cast_weights)
    return outs[0], outs[1:]


def _outproj_kernel(oa_ref, ob_ref, w_ref, x_ref, mod_ref, g_ref, b_ref, x1_ref, *, alpha, sub):
    gate1 = _mod_row(mod_ref, pl.program_id(0), 0, x_ref.shape[-1])
    for r0 in range(0, x_ref.shape[1], sub):
        rs = slice(r0, r0 + sub)
        o_cat = jnp.concatenate([oa_ref[0, rs, :], ob_ref[0, rs, :]], axis=1)
        mix = jnp.dot(o_cat, w_ref[...], preferred_element_type=F32)
        x1_ref[0, rs, :] = _ln_rows(alpha * x_ref[0, rs, :] + gate1 * mix) * g_ref[...] + b_ref[...]


def _outproj(o_a, o_b, w_o, x, mod, ln_g, ln_b, alpha, tm=512, sub=128):
    B, T, D = x.shape
    Wa, Wb = o_a.shape[-1], o_b.shape[-1]
    return pl.pallas_call(
        functools.partial(_outproj_kernel, alpha=alpha, sub=sub),
        out_shape=jax.ShapeDtypeStruct((B, T, D), F32),
        grid=(B, T // tm),
        in_specs=[pl.BlockSpec((1, tm, Wa), lambda b, i: (b, i, 0)),
                  pl.BlockSpec((1, tm, Wb), lambda b, i: (b, i, 0)),
                  pl.BlockSpec((Wa + Wb, D), lambda b, i: (0, 0)),
                  pl.BlockSpec((1, tm, D), lambda b, i: (b, i, 0)),
                  pl.BlockSpec((B, (N_MOD - N_MOD_PRE) * D), lambda b, i: (0, 0)),
                  pl.BlockSpec((1, D), lambda b, i: (0, 0)),
                  pl.BlockSpec((1, D), lambda b, i: (0, 0))],
        out_specs=pl.BlockSpec((1, tm, D), lambda b, i: (b, i, 0)),
        compiler_params=_cparams(("parallel", "parallel"), 48),
        name="out_proj_ln1",
    )(o_a, o_b, w_o, x, mod, ln_g, ln_b)


def _ffn_kernel(wg_ref, wu_ref, wo_ref, x_ref, mod_ref, g_ref, b_ref, o_ref, h_sc, *, alpha, sub):
    f = pl.program_id(2)
    last = pl.num_programs(2) - 1

    def partial_out(rs):
        h = h_sc[rs, :]
        gate = jnp.dot(h, wg_ref[...], preferred_element_type=F32)
        up = jnp.dot(h, wu_ref[...], preferred_element_type=F32)
        act = (_silu_tanh(gate) * up).astype(BF16)
        return jnp.dot(act, wo_ref[...], preferred_element_type=F32)

    everything = slice(0, o_ref.shape[1])

    @pl.when(f == 0)
    def _():
        shift2 = _mod_row(mod_ref, pl.program_id(0), 1, x_ref.shape[-1])
        scale2 = _mod_row(mod_ref, pl.program_id(0), 2, x_ref.shape[-1])
        for r0 in range(0, o_ref.shape[1], sub):
            rs = slice(r0, r0 + sub)
            h_sc[rs, :] = (_ln_rows(x_ref[0, rs, :]) * (1.0 + scale2) + shift2).astype(BF16)
        o_ref[0] = partial_out(everything)

    @pl.when((f > 0) & (f < last))
    def _():
        o_ref[0] += partial_out(everything)

    @pl.when(f == last)
    def _():
        gate2 = _mod_row(mod_ref, pl.program_id(0), 3, x_ref.shape[-1])
        for r0 in range(0, o_ref.shape[1], sub):
            rs = slice(r0, r0 + sub)
            y = o_ref[0, rs, :] + partial_out(rs)
            o_ref[0, rs, :] = _ln_rows(alpha * x_ref[0, rs, :] + gate2 * y) * g_ref[...] + b_ref[...]


def _ffn(w_in, w_out, x1, mod, ln_g, ln_b, alpha, tm=1024, tf=512, sub=256):
    B, T, D = x1.shape
    F = w_out.shape[0]
    nf = F // tf
    return pl.pallas_call(
        functools.partial(_ffn_kernel, alpha=alpha, sub=sub),
        out_shape=jax.ShapeDtypeStruct((B, T, D), F32),
        grid=(B, T // tm, nf),
        in_specs=[pl.BlockSpec((D, tf), lambda b, i, f: (0, f)),
                  pl.BlockSpec((D, tf), lambda b, i, f: (0, nf + f)),
                  pl.BlockSpec((tf, D), lambda b, i, f: (f, 0)),
                  pl.BlockSpec((1, tm, D), lambda b, i, f: (b, i, 0)),
                  pl.BlockSpec((B, (N_MOD - N_MOD_PRE) * D), lambda b, i, f: (0, 0)),
                  pl.BlockSpec((1, D), lambda b, i, f: (0, 0)),
                  pl.BlockSpec((1, D), lambda b, i, f: (0, 0))],
        out_specs=pl.BlockSpec((1, tm, D), lambda b, i, f: (b, i, 0)),
        scratch_shapes=[pltpu.VMEM((tm, D), BF16)],
        compiler_params=_cparams(("parallel", "parallel", "arbitrary"), 60),
        name="swiglu_ffn_ln2",
    )(w_in, w_in, w_out, x1, mod, ln_g, ln_b)


def _bias_vectors(rel_bias):
    H, n_rel = rel_bias.shape
    max_rel = (n_rel - 1) // 2
    u = jnp.arange(BIAS_W)
    idx = jnp.clip(KBLK - u, -max_rel, max_rel) + max_rel
    return rel_bias[:, idx]


def kernel(x, c, w_ada, b_ada, w_in, rel_bias, attn_norm_g, lb_logits, gnorm_g, w_o,
           ln1_g, ln1_b, w_ffn_in, w_ffn_out, ln2_g, ln2_b):
    B, T, D = x.shape
    depth = w_ada.shape[0]
    alpha = (2 * depth) ** 0.25
    attn_w = attn_norm_g.shape[1]
    rec_w = lb_logits.shape[1]
    n_slots = lb_logits.shape[0]
    rec_heads = rec_w // REC_HEAD_DIM
    assert depth == 1 and n_slots == depth + 1
    for layer in range(depth):
        c8 = jnp.pad(c, ((0, SUBLANES - B), (0, 0)))
        b_ada2 = b_ada[layer].reshape(1, -1)
        mod_pre = _mod(c8, w_ada[layer], b_ada2, N_MOD_PRE * D, B)
        q_scale = jnp.where(jnp.arange(w_in.shape[2]) < attn_w, ATTN_HEAD_DIM ** -0.5 * LOG2E, 1.0)
        q_scale = q_scale.astype(F32).reshape(1, -1)
        h1, q_a = _ln_proj(x, mod_pre, w_in[layer], q_scale, tn=attn_w)
        h1 = h1.reshape(B * T, D)
        kv_a = _matmul(h1, w_in[layer], q_scale, attn_w, 2 * attn_w, BF16).reshape(B, T, 2 * attn_w)
        proj_b = _matmul(h1, w_in[layer], q_scale, 3 * attn_w, 4 * rec_w, F32).reshape(B, T, 4 * rec_w)
        bias_vec = (_bias_vectors(rel_bias[layer]) * LOG2E).reshape(-1, 2, BIAS_W)
        o_a, mod, (w_o_bf, w_ffn_out_bf) = _attention(
            q_a, kv_a, bias_vec, attn_norm_g[layer].reshape(-1, 1, LANES), (w_o[layer], w_ffn_out[layer]),
            c8, w_ada[layer], b_ada2, N_MOD_PRE * D)
        lbl = lb_logits.reshape(n_slots, rec_heads, REC_HEAD_DIM).transpose(1, 0, 2)
        o_b, (w_ffn_in_bf,) = _hgrn(proj_b, lbl, gnorm_g[layer].reshape(1, REC_HEAD_DIM), (w_ffn_in[layer],))
        x = _outproj(o_a, o_b, w_o_bf, x, mod,
                     ln1_g[layer].reshape(1, D), ln1_b[layer].reshape(1, D), alpha)
        x = _ffn(w_ffn_in_bf, w_ffn_out_bf, x, mod,
                 ln2_g[layer].reshape(1, D), ln2_b[layer].reshape(1, D), alpha)
    return x
```

```python
import functools

import jax
import jax.numpy as jnp
from jax import lax
from jax.experimental import pallas as pl
from jax.experimental.pallas import tpu as pltpu

F32 = jnp.float32
BF16 = jnp.bfloat16

CHUNK = 64
N_PAST_CHUNKS = 8
BAND = (N_PAST_CHUNKS + 1) * CHUNK
ATTN_HEAD_DIM = 64
REC_HEAD_DIM = 128
N_MOD = 6
N_MOD_PRE = 2
EPS = 1e-5
LANES = 128
SUBLANES = 8
BF16_ROWS = 16
QBLK = 2 * CHUNK
KBLK = BAND + CHUNK
BIAS_W = KBLK + QBLK

MIB = 1024 * 1024


def _cparams(sem, vmem_mib):
    return pltpu.CompilerParams(dimension_semantics=sem, vmem_limit_bytes=vmem_mib * MIB)


def _silu_tanh(x):
    h = 0.5 * x
    return h + h * jnp.tanh(h)


def _mod_row(mod_ref, b, r, d):
    return mod_ref[pl.ds(b, 1), r * d:(r + 1) * d]


def _ln_rows(x):
    mu = jnp.mean(x, axis=-1, keepdims=True)
    xc = x - mu
    var = jnp.mean(xc * xc, axis=-1, keepdims=True)
    return xc * lax.rsqrt(var + EPS)


def _split3(x):
    hi = x.astype(BF16)
    r1 = x - hi.astype(F32)
    mid = r1.astype(BF16)
    lo = (r1 - mid.astype(F32)).astype(BF16)
    return hi, mid, lo


def _mod_block(c_ref, w_ref, b_ref, o_ref):
    act = jnp.concatenate(_split3(_silu_tanh(c_ref[...])), axis=0)
    w = w_ref[...]
    w_hi = w.astype(BF16)
    w_mid = (w - w_hi.astype(F32)).astype(BF16)
    acc = jnp.dot(act, w_hi, preferred_element_type=F32) + jnp.dot(act, w_mid, preferred_element_type=F32)
    out = acc[0:SUBLANES] + acc[SUBLANES:2 * SUBLANES] + acc[2 * SUBLANES:3 * SUBLANES]
    o_ref[...] = out[0:o_ref.shape[0]] + b_ref[...]


def _mod(c8, w_ada, b_ada, n_cols, n_rows, tn=1024):
    D = c8.shape[1]
    return pl.pallas_call(
        _mod_block,
        out_shape=jax.ShapeDtypeStruct((n_rows, n_cols), F32),
        grid=(n_cols // tn,),
        in_specs=[pl.BlockSpec((SUBLANES, D), lambda j: (0, 0)),
                  pl.BlockSpec((D, tn), lambda j: (0, j)),
                  pl.BlockSpec((1, tn), lambda j: (0, j))],
        out_specs=pl.BlockSpec((n_rows, tn), lambda j: (0, j)),
        compiler_params=_cparams(("parallel",), 40),
        name="adaln_mod",
    )(c8, w_ada, b_ada)


def _ln_proj_kernel(x_ref, mod_ref, w_ref, s_ref, h_ref, o_ref, w_bf, *, sub):
    b = pl.program_id(0)

    @pl.when((b == 0) & (pl.program_id(1) == 0))
    def _():
        w_bf[...] = (w_ref[...] * s_ref[...]).astype(BF16)

    d = x_ref.shape[-1]
    shift = _mod_row(mod_ref, b, 0, d)
    scale = _mod_row(mod_ref, b, 1, d)
    for r0 in range(0, x_ref.shape[1], sub):
        rs = slice(r0, r0 + sub)
        h_ref[0, rs, :] = (_ln_rows(x_ref[0, rs, :]) * (1.0 + scale) + shift).astype(BF16)
    o_ref[0] = jnp.dot(h_ref[0], w_bf[...], preferred_element_type=F32).astype(o_ref.dtype)


def _ln_proj(x, mod, w, col_scale, tm=1024, tn=1024, sub=256):
    B, T, D = x.shape
    return pl.pallas_call(
        functools.partial(_ln_proj_kernel, sub=sub),
        out_shape=(jax.ShapeDtypeStruct((B, T, D), BF16), jax.ShapeDtypeStruct((B, T, tn), BF16)),
        grid=(B, T // tm),
        in_specs=[pl.BlockSpec((1, tm, D), lambda b, i: (b, i, 0)),
                  pl.BlockSpec((B, N_MOD_PRE * D), lambda b, i: (0, 0)),
                  pl.BlockSpec((D, tn), lambda b, i: (0, 0)),
                  pl.BlockSpec((1, tn), lambda b, i: (0, 0))],
        out_specs=(pl.BlockSpec((1, tm, D), lambda b, i: (b, i, 0)),
                   pl.BlockSpec((1, tm, tn), lambda b, i: (b, i, 0))),
        scratch_shapes=[pltpu.VMEM((D, tn), BF16)],
        compiler_params=_cparams(("arbitrary", "arbitrary"), 56),
        name="ln_in_proj",
    )(x, mod, w, col_scale)


def _matmul_kernel(a_ref, w_ref, s_ref, o_ref, w_bf):
    @pl.when(pl.program_id(1) == 0)
    def _():
        w_bf[...] = (w_ref[...] * s_ref[...]).astype(BF16)

    o_ref[...] = jnp.dot(a_ref[...], w_bf[...], preferred_element_type=F32).astype(o_ref.dtype)


def _matmul(a, w, col_scale, col0, n_out, out_dtype, tm=1024, tn=1024):
    M, K = a.shape
    assert col0 % tn == 0 and n_out % tn == 0 and M % tm == 0
    col_block0 = col0 // tn
    return pl.pallas_call(
        _matmul_kernel,
        out_shape=jax.ShapeDtypeStruct((M, n_out), out_dtype),
        grid=(n_out // tn, M // tm),
        in_specs=[pl.BlockSpec((tm, K), lambda j, i: (i, 0)),
                  pl.BlockSpec((K, tn), lambda j, i: (0, j + col_block0)),
                  pl.BlockSpec((1, tn), lambda j, i: (0, j + col_block0))],
        out_specs=pl.BlockSpec((tm, tn), lambda j, i: (i, j)),
        scratch_shapes=[pltpu.VMEM((K, tn), BF16)],
        compiler_params=_cparams(("parallel", "arbitrary"), 48),
        name="in_proj",
    )(a, w, col_scale)


def _cast_specs(weights, n_grid, step_of):
    specs = []
    for w in weights:
        rows = w.shape[0] // n_grid
        assert w.shape[0] % n_grid == 0 and rows % BF16_ROWS == 0
        specs.append(pl.BlockSpec((rows, w.shape[1]), lambda *g: (step_of(*g), 0)))
    return specs


def _cast_pieces(cast_in, cast_out):
    return [(src_ref, dst_ref, r0) for src_ref, dst_ref in zip(cast_in, cast_out)
            for r0 in range(0, src_ref.shape[0], BF16_ROWS)]


def _cast_some(pieces, m, n):
    for src_ref, dst_ref, r0 in pieces[m * len(pieces) // n:(m + 1) * len(pieces) // n]:
        dst_ref[r0:r0 + BF16_ROWS, :] = src_ref[r0:r0 + BF16_ROWS, :].astype(BF16)


NEG_BIG = -1e30
LOG2E = 1.4426950408889634


def _attn_kernel(q_ref, k_ref, v_ref, bias_ref, gain_ref, *rest, n_chunks, n_cast):
    cast_in, mod_in = rest[:n_cast], rest[n_cast:n_cast + 3]
    o_ref = rest[n_cast + 3]
    cast_out, mod_out = rest[n_cast + 4:2 * n_cast + 4], rest[2 * n_cast + 4]
    kta, ktb, vpa, vpb, tab, s_a, s_b, s_c = rest[2 * n_cast + 5:]
    T = n_chunks * CHUNK
    n_steps = T // QBLK
    head0 = lax.broadcasted_iota(jnp.int32, (QBLK, LANES), 1) < ATTN_HEAD_DIM
    m0 = jnp.where(head0, 1.0, 0.0).astype(BF16)
    m1 = jnp.where(head0, 0.0, 1.0).astype(BF16)

    head0_t = lax.broadcasted_iota(jnp.int32, (LANES, QBLK), 0) < ATTN_HEAD_DIM
    mt0 = jnp.where(head0_t, 1.0, 0.0).astype(BF16)
    mt1 = jnp.where(head0_t, 0.0, 1.0).astype(BF16)

    def prep(blk):
        rows = slice(blk * QBLK, (blk + 1) * QBLK)
        kt = k_ref[0, rows, :].T
        v = v_ref[0, rows, :]
        kta[:, rows] = kt * mt0
        ktb[:, rows] = kt * mt1
        vpa[rows, 0:LANES] = v * m0
        vpb[rows, 0:LANES] = v * m1
        vpa[rows, LANES:2 * LANES] = m0
        vpb[rows, LANES:2 * LANES] = m1

    @pl.when(pl.program_id(1) == 0)
    def _():
        qry = lax.broadcasted_iota(jnp.int32, (QBLK, KBLK), 0)
        key = lax.broadcasted_iota(jnp.int32, (QBLK, KBLK), 1)
        in_band = ((qry < CHUNK) & (key < BAND)) | ((qry >= CHUNK) & (key >= CHUNK))
        for hh in range(2):
            g = jnp.broadcast_to(bias_ref[0, hh:hh + 1, :], (QBLK, BIAS_W))
            t = pltpu.roll(g, BIAS_W - QBLK, 1, stride=1, stride_axis=0)[:, :KBLK]
            tab[hh] = jnp.where(in_band, t, NEG_BIG).astype(BF16)

    eye = (lax.broadcasted_iota(jnp.int32, (QBLK, QBLK), 0)
           == lax.broadcasted_iota(jnp.int32, (QBLK, QBLK), 1)).astype(BF16)
    gain = gain_ref[0]

    def band_of(m):
        hi = (m + 1) * QBLK
        lo = max(0, hi - KBLK)
        return slice(lo, hi), KBLK - (hi - lo)

    def scores(m, dst):
        prep(m)
        band, col0 = band_of(m)
        lhs = jnp.concatenate([q_ref[0, m * QBLK:(m + 1) * QBLK, :], eye], axis=1)
        rhs = jnp.concatenate([jnp.concatenate([kta[:, band], ktb[:, band]], axis=1),
                               jnp.concatenate([tab[0, :, col0:], tab[1, :, col0:]], axis=1)], axis=0)
        dst[:, 0:rhs.shape[1]] = jnp.dot(lhs, rhs, preferred_element_type=F32)

    def finish(m, src):
        band, col0 = band_of(m)
        w = KBLK - col0
        s = src[:, 0:2 * w]
        p = jnp.concatenate(
            [jnp.exp2(sh - jnp.max(sh, axis=-1, keepdims=True)) for sh in (s[:, 0:w], s[:, w:2 * w])],
            axis=1).astype(BF16)
        pv = jnp.dot(p, jnp.concatenate([vpa[band, :], vpb[band, :]], axis=0), preferred_element_type=F32)
        o = pv[:, 0:LANES] * (1.0 / pv[:, LANES:2 * LANES])
        o2 = o * o
        ms0 = jnp.sum(jnp.where(head0, o2, 0.0), axis=-1, keepdims=True) / ATTN_HEAD_DIM
        ms1 = jnp.sum(jnp.where(head0, 0.0, o2), axis=-1, keepdims=True) / ATTN_HEAD_DIM
        y = o * lax.rsqrt(jnp.where(head0, ms0, ms1) + EPS) * gain
        o_ref[0, m * QBLK:(m + 1) * QBLK, :] = y.astype(o_ref.dtype)

    pieces = _cast_pieces(cast_in, cast_out)
    bufs = (s_a, s_b, s_c)
    depth = len(bufs)
    ahead = depth - 1
    for m in range(ahead):
        scores(m, bufs[m % depth])
    for m in range(n_steps):
        if m + ahead < n_steps:
            scores(m + ahead, bufs[(m + ahead) % depth])
        finish(m, bufs[m % depth])
        _cast_some(pieces, m, n_steps)
    _mod_block(*mod_in, mod_out)


def _attention(q_arr, kv_arr, bias_vec, attn_gain, cast_weights, c8, w_ada, b_ada, mod_col0):
    B, T, W = q_arr.shape
    n_pairs = W // LANES
    n_grid = n_pairs * B
    cast_specs = _cast_specs(cast_weights, n_grid, lambda h, b: h * B + b)
    D = c8.shape[1]
    n_mod = w_ada.shape[1] - mod_col0
    tn_mod = n_mod // n_grid
    assert n_mod % n_grid == 0 and tn_mod % LANES == 0 and mod_col0 % tn_mod == 0
    mod_blk0 = mod_col0 // tn_mod
    outs = pl.pallas_call(
        functools.partial(_attn_kernel, n_chunks=T // CHUNK, n_cast=len(cast_weights)),
        out_shape=(jax.ShapeDtypeStruct((B, T, W), BF16),
                   *[jax.ShapeDtypeStruct(w.shape, BF16) for w in cast_weights],
                   jax.ShapeDtypeStruct((B, n_mod), F32)),
        grid=(n_pairs, B),
        in_specs=[pl.BlockSpec((1, T, LANES), lambda h, b: (b, 0, h)),
                  pl.BlockSpec((1, T, LANES), lambda h, b: (b, 0, h)),
                  pl.BlockSpec((1, T, LANES), lambda h, b: (b, 0, n_pairs + h)),
                  pl.BlockSpec((1, 2, BIAS_W), lambda h, b: (h, 0, 0)),
                  pl.BlockSpec((1, 1, LANES), lambda h, b: (h, 0, 0)),
                  *cast_specs,
                  pl.BlockSpec((SUBLANES, D), lambda h, b: (0, 0)),
                  pl.BlockSpec((D, tn_mod), lambda h, b: (0, mod_blk0 + h * B + b)),
                  pl.BlockSpec((1, tn_mod), lambda h, b: (0, mod_blk0 + h * B + b))],
        out_specs=(pl.BlockSpec((1, T, LANES), lambda h, b: (b, 0, h)), *cast_specs,
                   pl.BlockSpec((B, tn_mod), lambda h, b: (0, h * B + b))),
        scratch_shapes=[pltpu.VMEM((LANES, T), BF16),
                        pltpu.VMEM((LANES, T), BF16),
                        pltpu.VMEM((T, 2 * LANES), BF16),
                        pltpu.VMEM((T, 2 * LANES), BF16),
                        pltpu.VMEM((2, QBLK, KBLK), BF16),
                        pltpu.VMEM((QBLK, 2 * KBLK), F32),
                        pltpu.VMEM((QBLK, 2 * KBLK), F32),
                        pltpu.VMEM((QBLK, 2 * KBLK), F32)],
        compiler_params=_cparams(("parallel", "arbitrary"), 48),
        name="chunk_attention",
    )(q_arr, kv_arr, kv_arr, bias_vec, attn_gain, *cast_weights, c8, w_ada, b_ada)
    return outs[0], outs[-1], outs[1:-1]


HG_CHUNK = 256
HG_LEVELS = (128, 64, 32, 16, 8, 4, 2, 1)
assert HG_LEVELS[-1] == 1


def _hgrn_kernel(q_ref, f_ref, i_ref, g_ref, lbl_ref, gn_ref, *rest, n_steps, n_cast):
    cast_in, o_ref, cast_out = rest[:n_cast], rest[n_cast], rest[n_cast + 1:]
    pieces = _cast_pieces(cast_in, cast_out)
    C = HG_CHUNK
    H2 = C // 2
    Dk = REC_HEAD_DIM
    nt = (((1,), (1,)), ((), ()))
    lbl = lbl_ref[0]
    e = jnp.exp(lbl - jnp.max(lbl, axis=0, keepdims=True))
    lb = e[0:1, :] / jnp.sum(e, axis=0, keepdims=True)
    c1 = 0.5 * (1.0 - lb)
    gn = gn_ref[...]

    r = lax.broadcasted_iota(jnp.int32, (C, C), 0)
    s = lax.broadcasted_iota(jnp.int32, (C, C), 1)
    tril = (s <= r).astype(BF16)
    rh = lax.broadcasted_iota(jnp.int32, (H2, H2), 0)
    sh = lax.broadcasted_iota(jnp.int32, (H2, H2), 1)
    lvl_mask = {m: ((rh // (2 * m)) == (sh // (2 * m))) & (((rh // m) % 2) == 1) & (((sh // m) % 2) == 0)
                for m in HG_LEVELS[1:]}
    sub = lax.broadcasted_iota(jnp.int32, (C // SUBLANES, SUBLANES, Dk), 1)

    def roll8(x, d):
        return pltpu.roll(x.reshape(C // SUBLANES, SUBLANES, Dk), d, 1)

    def front(n):
        rows = slice(n * C, (n + 1) * C)
        c1t = c1 * jnp.tanh(0.5 * f_ref[0, rows, :])
        f = (1.0 - c1) + c1t
        kk = c1 - c1t
        qq = _silu_tanh(q_ref[0, rows, :])
        ii = i_ref[0, rows, :]
        cat = jnp.concatenate(_split3(jnp.log2(f)), axis=1)
        return dict(rows=rows, kk=kk, qq=qq, ii=ii, ii_bf=ii.astype(BF16), cat=cat)

    def level_z(v, m):
        b, kk, qq = v["b"], v["kk"], v["qq"]
        if m >= SUBLANES:
            parts, srcs = [], []
            for p in range(0, C, 2 * m):
                bm = b[p + m - 1:p + m, :]
                parts += [bm - b[p:p + m], b[p + m:p + 2 * m] - bm]
                srcs += [kk[p:p + m], qq[p + m:p + 2 * m]]
            arg = jnp.concatenate(parts, axis=0)
            src = jnp.concatenate(srcs, axis=0)
        else:
            b3 = b.reshape(C // SUBLANES, SUBLANES, Dk)
            if m == 1:
                bm = jnp.where(sub % 2 == 1, roll8(b, 1), b3)
            else:
                bm = jnp.broadcast_to(b3[:, m - 1:m, :], b3.shape)
                for p in range(2 * m, SUBLANES, 2 * m):
                    bm = jnp.where(sub >= p, jnp.broadcast_to(b3[:, p + m - 1:p + m, :], b3.shape), bm)
            upper = (sub // m) % 2 == 1
            arg = ((b3 - bm) * jnp.where(upper, 1.0, -1.0)).reshape(C, Dk)
            src = jnp.where(upper, qq.reshape(b3.shape), kk.reshape(b3.shape)).reshape(C, Dk)
        return (src * jnp.exp2(arg)).astype(BF16)

    vs = [front(j) for j in range(n_steps)]
    bb = jnp.dot(tril, jnp.concatenate([v["cat"] for v in vs], axis=1), preferred_element_type=F32)
    for j, v in enumerate(vs):
        c0 = 3 * Dk * j
        v["b"] = bb[:, c0:c0 + Dk] + bb[:, c0 + Dk:c0 + 2 * Dk] + bb[:, c0 + 2 * Dk:c0 + 3 * Dk]
    a_lo = [None] * n_steps
    a_d0 = [jnp.zeros((H2, H2), F32)] * n_steps
    a_d1 = [jnp.zeros((H2, H2), F32)] * n_steps
    for li, m in enumerate(HG_LEVELS):
        for j, v in enumerate(vs):
            z = level_z(v, m)
            if li == 0:
                a_lo[j] = lax.dot_general(z[H2:], z[:H2], nt, preferred_element_type=F32)
            else:
                g = lax.dot_general(z, z, nt, preferred_element_type=F32)
                a_d0[j] = jnp.where(lvl_mask[m], g[:H2, :H2], a_d0[j])
                a_d1[j] = jnp.where(lvl_mask[m], g[H2:, H2:], a_d1[j])
    intra = []
    for j, v in enumerate(vs):
        a = jnp.concatenate([jnp.concatenate([a_d0[j], jnp.zeros((H2, H2), F32)], axis=1),
                             jnp.concatenate([a_lo[j], a_d1[j]], axis=1)], axis=0).astype(BF16)
        o_diag = jnp.sum(v["qq"] * v["kk"], axis=-1, keepdims=True) * v["ii"]
        intra.append(jnp.dot(a, v["ii_bf"], preferred_element_type=F32) + o_diag)
    st = jnp.zeros((Dk, Dk), F32)
    for j, v in enumerate(vs):
        b = v["b"]
        b_last = b[C - 1:C, :]
        qe = (v["qq"] * jnp.exp2(b)).astype(BF16)
        o = intra[j] + lax.dot_general(qe, st.astype(BF16), nt, preferred_element_type=F32)
        ke = (v["kk"] * jnp.exp2(b_last - b)).astype(BF16)
        st = st * jnp.exp2(b_last) + lax.dot_general(
            v["ii_bf"], ke, (((0,), (0,)), ((), ())), preferred_element_type=F32)
        ms = jnp.mean(o * o, axis=-1, keepdims=True)
        y = o * lax.rsqrt(ms + EPS) * gn
        y = y * _silu_tanh(g_ref[0, v["rows"], :])
        o_ref[0, v["rows"], :] = y.astype(o_ref.dtype)
        _cast_some(pieces, j, n_steps)


def _hgrn(proj_b, lb_logits_h, gnorm_g, cast_weights):
    B, T, W4 = proj_b.shape
    W = W4 // 4
    H = W // REC_HEAD_DIM
    n_slots = lb_logits_h.shape[1]
    blk = lambda off: pl.BlockSpec((1, T, REC_HEAD_DIM), lambda b, h, off=off: (b, 0, off * H + h))
    cast_specs = _cast_specs(cast_weights, B * H, lambda b, h: b * H + h)
    outs = pl.pallas_call(
        functools.partial(_hgrn_kernel, n_steps=T // HG_CHUNK, n_cast=len(cast_weights)),
        out_shape=(jax.ShapeDtypeStruct((B, T, W), BF16),
                   *[jax.ShapeDtypeStruct(w.shape, BF16) for w in cast_weights]),
        grid=(B, H),
        in_specs=[blk(0), blk(1), blk(2), blk(3),
                  pl.BlockSpec((1, n_slots, REC_HEAD_DIM), lambda b, h: (h, 0, 0)),
                  pl.BlockSpec((1, REC_HEAD_DIM), lambda b, h: (0, 0)),
                  *cast_specs],
        out_specs=(pl.BlockSpec((1, T, REC_HEAD_DIM), lambda b, h: (b, 0, h)), *cast_specs),
        compiler_params=_cparams(("parallel", "parallel"), 40),
        name="hgrn2",
    )(proj_b, proj_b, proj_b, proj_b, lb_logits_h, gnorm_g, *cast_weights)
    return outs[0], outs[1:]


def _outproj_kernel(oa_ref, ob_ref, w_ref, x_ref, mod_ref, g_ref, b_ref, x1_ref, *, alpha, sub):
    gate1 = _mod_row(mod_ref, pl.program_id(0), 0, x_ref.shape[-1])
    for r0 in range(0, x_ref.shape[1], sub):
        rs = slice(r0, r0 + sub)
        o_cat = jnp.concatenate([oa_ref[0, rs, :], ob_ref[0, rs, :]], axis=1)
        mix = jnp.dot(o_cat, w_ref[...], preferred_element_type=F32)
        x1_ref[0, rs, :] = _ln_rows(alpha * x_ref[0, rs, :] + gate1 * mix) * g_ref[...] + b_ref[...]


def _outproj(o_a, o_b, w_o, x, mod, ln_g, ln_b, alpha, tm=512, sub=128):
    B, T, D = x.shape
    Wa, Wb = o_a.shape[-1], o_b.shape[-1]
    return pl.pallas_call(
        functools.partial(_outproj_kernel, alpha=alpha, sub=sub),
        out_shape=jax.ShapeDtypeStruct((B, T, D), F32),
        grid=(B, T // tm),
        in_specs=[pl.BlockSpec((1, tm, Wa), lambda b, i: (b, i, 0)),
                  pl.BlockSpec((1, tm, Wb), lambda b, i: (b, i, 0)),
                  pl.BlockSpec((Wa + Wb, D), lambda b, i: (0, 0)),
                  pl.BlockSpec((1, tm, D), lambda b, i: (b, i, 0)),
                  pl.BlockSpec((B, (N_MOD - N_MOD_PRE) * D), lambda b, i: (0, 0)),
                  pl.BlockSpec((1, D), lambda b, i: (0, 0)),
                  pl.BlockSpec((1, D), lambda b, i: (0, 0))],
        out_specs=pl.BlockSpec((1, tm, D), lambda b, i: (b, i, 0)),
        compiler_params=_cparams(("parallel", "parallel"), 48),
        name="out_proj_ln1",
    )(o_a, o_b, w_o, x, mod, ln_g, ln_b)


def _ffn_kernel(wg_ref, wu_ref, wo_ref, x_ref, mod_ref, g_ref, b_ref, o_ref, h_sc, *, alpha, sub):
    f = pl.program_id(2)
    last = pl.num_programs(2) - 1

    def partial_out(rs):
        h = h_sc[rs, :]
        tf = wg_ref.shape[1]
        gu = jnp.dot(h, jnp.concatenate([wg_ref[...], wu_ref[...]], axis=1), preferred_element_type=F32)
        act = (_silu_tanh(gu[:, :tf]) * gu[:, tf:]).astype(BF16)
        return jnp.dot(act, wo_ref[...], preferred_element_type=F32)

    everything = slice(0, o_ref.shape[1])

    @pl.when(f == 0)
    def _():
        shift2 = _mod_row(mod_ref, pl.program_id(0), 1, x_ref.shape[-1])
        scale2 = _mod_row(mod_ref, pl.program_id(0), 2, x_ref.shape[-1])
        for r0 in range(0, o_ref.shape[1], sub):
            rs = slice(r0, r0 + sub)
            h_sc[rs, :] = (_ln_rows(x_ref[0, rs, :]) * (1.0 + scale2) + shift2).astype(BF16)
        o_ref[0] = partial_out(everything)

    @pl.when((f > 0) & (f < last))
    def _():
        o_ref[0] += partial_out(everything)

    @pl.when(f == last)
    def _():
        gate2 = _mod_row(mod_ref, pl.program_id(0), 3, x_ref.shape[-1])
        for r0 in range(0, o_ref.shape[1], sub):
            rs = slice(r0, r0 + sub)
            y = o_ref[0, rs, :] + partial_out(rs)
            o_ref[0, rs, :] = _ln_rows(alpha * x_ref[0, rs, :] + gate2 * y) * g_ref[...] + b_ref[...]


def _ffn(w_in, w_out, x1, mod, ln_g, ln_b, alpha, tm=1024, tf=512, sub=256):
    B, T, D = x1.shape
    F = w_out.shape[0]
    nf = F // tf
    return pl.pallas_call(
        functools.partial(_ffn_kernel, alpha=alpha, sub=sub),
        out_shape=jax.ShapeDtypeStruct((B, T, D), F32),
        grid=(B, T // tm, nf),
        in_specs=[pl.BlockSpec((D, tf), lambda b, i, f: (0, f)),
                  pl.BlockSpec((D, tf), lambda b, i, f: (0, nf + f)),
                  pl.BlockSpec((tf, D), lambda b, i, f: (f, 0)),
                  pl.BlockSpec((1, tm, D), lambda b, i, f: (b, i, 0)),
                  pl.BlockSpec((B, (N_MOD - N_MOD_PRE) * D), lambda b, i, f: (0, 0)),
                  pl.BlockSpec((1, D), lambda b, i, f: (0, 0)),
                  pl.BlockSpec((1, D), lambda b, i, f: (0, 0))],
        out_specs=pl.BlockSpec((1, tm, D), lambda b, i, f: (b, i, 0)),
        scratch_shapes=[pltpu.VMEM((tm, D), BF16)],
        compiler_params=_cparams(("parallel", "parallel", "arbitrary"), 60),
        name="swiglu_ffn_ln2",
    )(w_in, w_in, w_out, x1, mod, ln_g, ln_b)


def _bias_vectors(rel_bias):
    H, n_rel = rel_bias.shape
    max_rel = (n_rel - 1) // 2
    u = jnp.arange(BIAS_W)
    idx = jnp.clip(KBLK - u, -max_rel, max_rel) + max_rel
    return rel_bias[:, idx]


def kernel(x, c, w_ada, b_ada, w_in, rel_bias, attn_norm_g, lb_logits, gnorm_g, w_o,
           ln1_g, ln1_b, w_ffn_in, w_ffn_out, ln2_g, ln2_b):
    B, T, D = x.shape
    depth = w_ada.shape[0]
    alpha = (2 * depth) ** 0.25
    attn_w = attn_norm_g.shape[1]
    rec_w = lb_logits.shape[1]
    n_slots = lb_logits.shape[0]
    rec_heads = rec_w // REC_HEAD_DIM
    assert depth == 1 and n_slots == depth + 1
    for layer in range(depth):
        c8 = jnp.pad(c, ((0, SUBLANES - B), (0, 0)))
        b_ada2 = b_ada[layer].reshape(1, -1)
        mod_pre = _mod(c8, w_ada[layer], b_ada2, N_MOD_PRE * D, B)
        q_scale = jnp.where(jnp.arange(w_in.shape[2]) < attn_w, ATTN_HEAD_DIM ** -0.5 * LOG2E, 1.0)
        q_scale = q_scale.astype(F32).reshape(1, -1)
        h1, q_a = _ln_proj(x, mod_pre, w_in[layer], q_scale, tn=attn_w)
        h1 = h1.reshape(B * T, D)
        kv_a = _matmul(h1, w_in[layer], q_scale, attn_w, 2 * attn_w, BF16).reshape(B, T, 2 * attn_w)
        proj_b = _matmul(h1, w_in[layer], q_scale, 3 * attn_w, 4 * rec_w, F32).reshape(B, T, 4 * rec_w)
        bias_vec = (_bias_vectors(rel_bias[layer]) * LOG2E).reshape(-1, 2, BIAS_W)
        o_a, mod, (w_o_bf, w_ffn_out_bf) = _attention(
            q_a, kv_a, bias_vec, attn_norm_g[layer].reshape(-1, 1, LANES), (w_o[layer], w_ffn_out[layer]),
            c8, w_ada[layer], b_ada2, N_MOD_PRE * D)
        lbl = lb_logits.reshape(n_slots, rec_heads, REC_HEAD_DIM).transpose(1, 0, 2)
        o_b, (w_ffn_in_bf,) = _hgrn(proj_b, lbl, gnorm_g[layer].reshape(1, REC_HEAD_DIM), (w_ffn_in[layer],))
        x = _outproj(o_a, o_b, w_o_bf, x, mod,
                     ln1_g[layer].reshape(1, D), ln1_b[layer].reshape(1, D), alpha)
        x = _ffn(w_ffn_in_bf, w_ffn_out_bf, x, mod,
                 ln2_g[layer].reshape(1, D), ln2_b[layer].reshape(1, D), alpha)
    return x
```

```python
import functools

import jax
import jax.numpy as jnp
from jax import lax
from jax.experimental import pallas as pl
from jax.experimental.pallas import tpu as pltpu

F32 = jnp.float32
BF16 = jnp.bfloat16

CHUNK = 64
N_PAST_CHUNKS = 8
BAND = (N_PAST_CHUNKS + 1) * CHUNK
ATTN_HEAD_DIM = 64
REC_HEAD_DIM = 128
N_MOD = 6
N_MOD_PRE = 2
EPS = 1e-5
LANES = 128
SUBLANES = 8
BF16_ROWS = 16
QBLK = 2 * CHUNK
KBLK = BAND + CHUNK
BIAS_W = KBLK + QBLK

MIB = 1024 * 1024


def _cparams(sem, vmem_mib):
    return pltpu.CompilerParams(dimension_semantics=sem, vmem_limit_bytes=vmem_mib * MIB)


def _silu_tanh(x):
    h = 0.5 * x
    return h + h * jnp.tanh(h)


def _mod_row(mod_ref, b, r, d):
    return mod_ref[pl.ds(b, 1), r * d:(r + 1) * d]


def _ln_rows(x):
    mu = jnp.mean(x, axis=-1, keepdims=True)
    xc = x - mu
    var = jnp.mean(xc * xc, axis=-1, keepdims=True)
    return xc * lax.rsqrt(var + EPS)


def _split3(x):
    hi = x.astype(BF16)
    r1 = x - hi.astype(F32)
    mid = r1.astype(BF16)
    lo = (r1 - mid.astype(F32)).astype(BF16)
    return hi, mid, lo


def _mod_block(c_ref, w_ref, b_ref, o_ref):
    act = jnp.concatenate(_split3(_silu_tanh(c_ref[...])), axis=0)
    w = w_ref[...]
    w_hi = w.astype(BF16)
    w_mid = (w - w_hi.astype(F32)).astype(BF16)
    acc = jnp.dot(act, w_hi, preferred_element_type=F32) + jnp.dot(act, w_mid, preferred_element_type=F32)
    out = acc[0:SUBLANES] + acc[SUBLANES:2 * SUBLANES] + acc[2 * SUBLANES:3 * SUBLANES]
    o_ref[...] = out[0:o_ref.shape[0]] + b_ref[...]


def _mod(c8, w_ada, b_ada, n_cols, n_rows, tn=1024):
    D = c8.shape[1]
    return pl.pallas_call(
        _mod_block,
        out_shape=jax.ShapeDtypeStruct((n_rows, n_cols), F32),
        grid=(n_cols // tn,),
        in_specs=[pl.BlockSpec((SUBLANES, D), lambda j: (0, 0)),
                  pl.BlockSpec((D, tn), lambda j: (0, j)),
                  pl.BlockSpec((1, tn), lambda j: (0, j))],
        out_specs=pl.BlockSpec((n_rows, tn), lambda j: (0, j)),
        compiler_params=_cparams(("parallel",), 40),
        name="adaln_mod",
    )(c8, w_ada, b_ada)


def _ln_proj_kernel(x_ref, mod_ref, w_ref, s_ref, h_ref, o_ref, w_bf, *, sub):
    b = pl.program_id(0)

    @pl.when((b == 0) & (pl.program_id(1) == 0))
    def _():
        w_bf[...] = (w_ref[...] * s_ref[...]).astype(BF16)

    d = x_ref.shape[-1]
    shift = _mod_row(mod_ref, b, 0, d)
    scale = _mod_row(mod_ref, b, 1, d)
    for r0 in range(0, x_ref.shape[1], sub):
        rs = slice(r0, r0 + sub)
        h_ref[0, rs, :] = (_ln_rows(x_ref[0, rs, :]) * (1.0 + scale) + shift).astype(BF16)
    o_ref[0] = jnp.dot(h_ref[0], w_bf[...], preferred_element_type=F32).astype(o_ref.dtype)


def _ln_proj(x, mod, w, col_scale, tm=1024, tn=1024, sub=256):
    B, T, D = x.shape
    return pl.pallas_call(
        functools.partial(_ln_proj_kernel, sub=sub),
        out_shape=(jax.ShapeDtypeStruct((B, T, D), BF16), jax.ShapeDtypeStruct((B, T, tn), BF16)),
        grid=(B, T // tm),
        in_specs=[pl.BlockSpec((1, tm, D), lambda b, i: (b, i, 0)),
                  pl.BlockSpec((B, N_MOD_PRE * D), lambda b, i: (0, 0)),
                  pl.BlockSpec((D, tn), lambda b, i: (0, 0)),
                  pl.BlockSpec((1, tn), lambda b, i: (0, 0))],
        out_specs=(pl.BlockSpec((1, tm, D), lambda b, i: (b, i, 0)),
                   pl.BlockSpec((1, tm, tn), lambda b, i: (b, i, 0))),
        scratch_shapes=[pltpu.VMEM((D, tn), BF16)],
        compiler_params=_cparams(("arbitrary", "arbitrary"), 56),
        name="ln_in_proj",
    )(x, mod, w, col_scale)


def _matmul_kernel(a_ref, w_ref, s_ref, o_ref, w_bf):
    @pl.when(pl.program_id(1) == 0)
    def _():
        w_bf[...] = (w_ref[...] * s_ref[...]).astype(BF16)

    o_ref[...] = jnp.dot(a_ref[...], w_bf[...], preferred_element_type=F32).astype(o_ref.dtype)


def _matmul(a, w, col_scale, col0, n_out, out_dtype, tm=1024, tn=1024):
    M, K = a.shape
    assert col0 % tn == 0 and n_out % tn == 0 and M % tm == 0
    col_block0 = col0 // tn
    return pl.pallas_call(
        _matmul_kernel,
        out_shape=jax.ShapeDtypeStruct((M, n_out), out_dtype),
        grid=(n_out // tn, M // tm),
        in_specs=[pl.BlockSpec((tm, K), lambda j, i: (i, 0)),
                  pl.BlockSpec((K, tn), lambda j, i: (0, j + col_block0)),
                  pl.BlockSpec((1, tn), lambda j, i: (0, j + col_block0))],
        out_specs=pl.BlockSpec((tm, tn), lambda j, i: (i, j)),
        scratch_shapes=[pltpu.VMEM((K, tn), BF16)],
        compiler_params=_cparams(("parallel", "arbitrary"), 48),
        name="in_proj",
    )(a, w, col_scale)


def _cast_specs(weights, n_grid, step_of):
    specs = []
    for w in weights:
        rows = w.shape[0] // n_grid
        assert w.shape[0] % n_grid == 0 and rows % BF16_ROWS == 0
        specs.append(pl.BlockSpec((rows, w.shape[1]), lambda *g: (step_of(*g), 0)))
    return specs


def _cast_pieces(cast_in, cast_out):
    return [(src_ref, dst_ref, r0) for src_ref, dst_ref in zip(cast_in, cast_out)
            for r0 in range(0, src_ref.shape[0], BF16_ROWS)]


def _cast_some(pieces, m, n):
    for src_ref, dst_ref, r0 in pieces[m * len(pieces) // n:(m + 1) * len(pieces) // n]:
        dst_ref[r0:r0 + BF16_ROWS, :] = src_ref[r0:r0 + BF16_ROWS, :].astype(BF16)


NEG_BIG = -1e30
LOG2E = 1.4426950408889634


def _attn_kernel(q_ref, k_ref, v_ref, bias_ref, gain_ref, *rest, n_chunks, n_cast):
    cast_in, mod_in = rest[:n_cast], rest[n_cast:n_cast + 3]
    o_ref = rest[n_cast + 3]
    cast_out, mod_out = rest[n_cast + 4:2 * n_cast + 4], rest[2 * n_cast + 4]
    kta, ktb, vpa, vpb, tab, s_a, s_b, s_c = rest[2 * n_cast + 5:]
    T = n_chunks * CHUNK
    n_steps = T // QBLK
    head0 = lax.broadcasted_iota(jnp.int32, (QBLK, LANES), 1) < ATTN_HEAD_DIM
    m0 = jnp.where(head0, 1.0, 0.0).astype(BF16)
    m1 = jnp.where(head0, 0.0, 1.0).astype(BF16)

    head0_t = lax.broadcasted_iota(jnp.int32, (LANES, QBLK), 0) < ATTN_HEAD_DIM
    mt0 = jnp.where(head0_t, 1.0, 0.0).astype(BF16)
    mt1 = jnp.where(head0_t, 0.0, 1.0).astype(BF16)

    def prep(blk):
        rows = slice(blk * QBLK, (blk + 1) * QBLK)
        kt = k_ref[0, rows, :].T
        v = v_ref[0, rows, :]
        kta[:, rows] = kt * mt0
        ktb[:, rows] = kt * mt1
        vpa[rows, 0:LANES] = v * m0
        vpb[rows, 0:LANES] = v * m1
        vpa[rows, LANES:2 * LANES] = m0
        vpb[rows, LANES:2 * LANES] = m1

    @pl.when(pl.program_id(1) == 0)
    def _():
        qry = lax.broadcasted_iota(jnp.int32, (QBLK, KBLK), 0)
        key = lax.broadcasted_iota(jnp.int32, (QBLK, KBLK), 1)
        in_band = ((qry < CHUNK) & (key < BAND)) | ((qry >= CHUNK) & (key >= CHUNK))
        for hh in range(2):
            g = jnp.broadcast_to(bias_ref[0, hh:hh + 1, :], (QBLK, BIAS_W))
            t = pltpu.roll(g, BIAS_W - QBLK, 1, stride=1, stride_axis=0)[:, :KBLK]
            tab[hh] = jnp.where(in_band, t, NEG_BIG).astype(BF16)

    eye = (lax.broadcasted_iota(jnp.int32, (QBLK, QBLK), 0)
           == lax.broadcasted_iota(jnp.int32, (QBLK, QBLK), 1)).astype(BF16)
    gain = gain_ref[0]

    def band_of(m):
        hi = (m + 1) * QBLK
        lo = max(0, hi - KBLK)
        return slice(lo, hi), KBLK - (hi - lo)

    def scores(m, dst):
        prep(m)
        band, col0 = band_of(m)
        lhs = jnp.concatenate([q_ref[0, m * QBLK:(m + 1) * QBLK, :], eye], axis=1)
        rhs = jnp.concatenate([jnp.concatenate([kta[:, band], ktb[:, band]], axis=1),
                               jnp.concatenate([tab[0, :, col0:], tab[1, :, col0:]], axis=1)], axis=0)
        dst[:, 0:rhs.shape[1]] = jnp.dot(lhs, rhs, preferred_element_type=F32)

    def finish(m, src):
        band, col0 = band_of(m)
        w = KBLK - col0
        s = src[:, 0:2 * w]
        p = jnp.concatenate(
            [jnp.exp2(sh - jnp.max(sh, axis=-1, keepdims=True)) for sh in (s[:, 0:w], s[:, w:2 * w])],
            axis=1).astype(BF16)
        pv = jnp.dot(p, jnp.concatenate([vpa[band, :], vpb[band, :]], axis=0), preferred_element_type=F32)
        o = pv[:, 0:LANES] * (1.0 / pv[:, LANES:2 * LANES])
        o2 = o * o
        ms0 = jnp.sum(jnp.where(head0, o2, 0.0), axis=-1, keepdims=True) / ATTN_HEAD_DIM
        ms1 = jnp.sum(jnp.where(head0, 0.0, o2), axis=-1, keepdims=True) / ATTN_HEAD_DIM
        y = o * lax.rsqrt(jnp.where(head0, ms0, ms1) + EPS) * gain
        o_ref[0, m * QBLK:(m + 1) * QBLK, :] = y.astype(o_ref.dtype)

    pieces = _cast_pieces(cast_in, cast_out)
    bufs = (s_a, s_b, s_c)
    depth = len(bufs)
    ahead = depth - 1
    for m in range(ahead):
        scores(m, bufs[m % depth])
    for m in range(n_steps):
        if m + ahead < n_steps:
            scores(m + ahead, bufs[(m + ahead) % depth])
        finish(m, bufs[m % depth])
        _cast_some(pieces, m, n_steps)
    _mod_block(*mod_in, mod_out)


def _attention(q_arr, kv_arr, bias_vec, attn_gain, cast_weights, c8, w_ada, b_ada, mod_col0):
    B, T, W = q_arr.shape
    n_pairs = W // LANES
    n_grid = n_pairs * B
    cast_specs = _cast_specs(cast_weights, n_grid, lambda h, b: h * B + b)
    D = c8.shape[1]
    n_mod = w_ada.shape[1] - mod_col0
    tn_mod = n_mod // n_grid
    assert n_mod % n_grid == 0 and tn_mod % LANES == 0 and mod_col0 % tn_mod == 0
    mod_blk0 = mod_col0 // tn_mod
    outs = pl.pallas_call(
        functools.partial(_attn_kernel, n_chunks=T // CHUNK, n_cast=len(cast_weights)),
        out_shape=(jax.ShapeDtypeStruct((B, T, W), BF16),
                   *[jax.ShapeDtypeStruct(w.shape, BF16) for w in cast_weights],
                   jax.ShapeDtypeStruct((B, n_mod), F32)),
        grid=(n_pairs, B),
        in_specs=[pl.BlockSpec((1, T, LANES), lambda h, b: (b, 0, h)),
                  pl.BlockSpec((1, T, LANES), lambda h, b: (b, 0, h)),
                  pl.BlockSpec((1, T, LANES), lambda h, b: (b, 0, n_pairs + h)),
                  pl.BlockSpec((1, 2, BIAS_W), lambda h, b: (h, 0, 0)),
                  pl.BlockSpec((1, 1, LANES), lambda h, b: (h, 0, 0)),
                  *cast_specs,
                  pl.BlockSpec((SUBLANES, D), lambda h, b: (0, 0)),
                  pl.BlockSpec((D, tn_mod), lambda h, b: (0, mod_blk0 + h * B + b)),
                  pl.BlockSpec((1, tn_mod), lambda h, b: (0, mod_blk0 + h * B + b))],
        out_specs=(pl.BlockSpec((1, T, LANES), lambda h, b: (b, 0, h)), *cast_specs,
                   pl.BlockSpec((B, tn_mod), lambda h, b: (0, h * B + b))),
        scratch_shapes=[pltpu.VMEM((LANES, T), BF16),
                        pltpu.VMEM((LANES, T), BF16),
                        pltpu.VMEM((T, 2 * LANES), BF16),
                        pltpu.VMEM((T, 2 * LANES), BF16),
                        pltpu.VMEM((2, QBLK, KBLK), BF16),
                        pltpu.VMEM((QBLK, 2 * KBLK), F32),
                        pltpu.VMEM((QBLK, 2 * KBLK), F32),
                        pltpu.VMEM((QBLK, 2 * KBLK), F32)],
        compiler_params=_cparams(("parallel", "arbitrary"), 48),
        name="chunk_attention",
    )(q_arr, kv_arr, kv_arr, bias_vec, attn_gain, *cast_weights, c8, w_ada, b_ada)
    return outs[0], outs[-1], outs[1:-1]


HG_CHUNK = 256
HG_LEVELS = (128, 64, 32, 16, 8, 4, 2, 1)
assert HG_LEVELS[-1] == 1


def _hgrn_kernel(q_ref, f_ref, i_ref, g_ref, lbl_ref, gn_ref, *rest, n_steps, n_cast):
    cast_in, o_ref, cast_out = rest[:n_cast], rest[n_cast], rest[n_cast + 1:]
    pieces = _cast_pieces(cast_in, cast_out)
    C = HG_CHUNK
    H2 = C // 2
    Dk = REC_HEAD_DIM
    nt = (((1,), (1,)), ((), ()))
    lbl = lbl_ref[0]
    e = jnp.exp(lbl - jnp.max(lbl, axis=0, keepdims=True))
    lb = e[0:1, :] / jnp.sum(e, axis=0, keepdims=True)
    c1 = 0.5 * (1.0 - lb)
    gn = gn_ref[...]

    r = lax.broadcasted_iota(jnp.int32, (C, C), 0)
    s = lax.broadcasted_iota(jnp.int32, (C, C), 1)
    tril = (s <= r).astype(BF16)
    rh = lax.broadcasted_iota(jnp.int32, (H2, H2), 0)
    sh = lax.broadcasted_iota(jnp.int32, (H2, H2), 1)
    lvl_mask = {m: ((rh // (2 * m)) == (sh // (2 * m))) & (((rh // m) % 2) == 1) & (((sh // m) % 2) == 0)
                for m in HG_LEVELS[1:]}
    sub = lax.broadcasted_iota(jnp.int32, (C // SUBLANES, SUBLANES, Dk), 1)

    def roll8(x, d):
        return pltpu.roll(x.reshape(C // SUBLANES, SUBLANES, Dk), d, 1)

    def front(n):
        rows = slice(n * C, (n + 1) * C)
        c1t = c1 * jnp.tanh(0.5 * f_ref[0, rows, :])
        f = (1.0 - c1) + c1t
        kk = c1 - c1t
        qq = _silu_tanh(q_ref[0, rows, :])
        ii = i_ref[0, rows, :]
        cat = jnp.concatenate(_split3(jnp.log2(f)), axis=1)
        return dict(rows=rows, kk=kk, qq=qq, ii=ii, ii_bf=ii.astype(BF16), cat=cat)

    def level_z(v, m):
        b, kk, qq = v["b"], v["kk"], v["qq"]
        if m >= SUBLANES:
            parts, srcs = [], []
            for p in range(0, C, 2 * m):
                bm = b[p + m - 1:p + m, :]
                parts += [bm - b[p:p + m], b[p + m:p + 2 * m] - bm]
                srcs += [kk[p:p + m], qq[p + m:p + 2 * m]]
            arg = jnp.concatenate(parts, axis=0)
            src = jnp.concatenate(srcs, axis=0)
        else:
            b3 = b.reshape(C // SUBLANES, SUBLANES, Dk)
            if m == 1:
                bm = jnp.where(sub % 2 == 1, roll8(b, 1), b3)
            else:
                bm = jnp.broadcast_to(b3[:, m - 1:m, :], b3.shape)
                for p in range(2 * m, SUBLANES, 2 * m):
                    bm = jnp.where(sub >= p, jnp.broadcast_to(b3[:, p + m - 1:p + m, :], b3.shape), bm)
            upper = (sub // m) % 2 == 1
            arg = ((b3 - bm) * jnp.where(upper, 1.0, -1.0)).reshape(C, Dk)
            src = jnp.where(upper, qq.reshape(b3.shape), kk.reshape(b3.shape)).reshape(C, Dk)
        return (src * jnp.exp2(arg)).astype(BF16)

    vs = [front(j) for j in range(n_steps)]
    bb = jnp.dot(tril, jnp.concatenate([v["cat"] for v in vs], axis=1), preferred_element_type=F32)
    for j, v in enumerate(vs):
        c0 = 3 * Dk * j
        v["b"] = bb[:, c0:c0 + Dk] + bb[:, c0 + Dk:c0 + 2 * Dk] + bb[:, c0 + 2 * Dk:c0 + 3 * Dk]
    a_lo = [None] * n_steps
    a_d0 = [jnp.zeros((H2, H2), F32)] * n_steps
    a_d1 = [jnp.zeros((H2, H2), F32)] * n_steps
    for li, m in enumerate(HG_LEVELS):
        for j, v in enumerate(vs):
            z = level_z(v, m)
            if li == 0:
                a_lo[j] = lax.dot_general(z[H2:], z[:H2], nt, preferred_element_type=F32)
            else:
                g = lax.dot_general(z, z, nt, preferred_element_type=F32)
                a_d0[j] = jnp.where(lvl_mask[m], g[:H2, :H2], a_d0[j])
                a_d1[j] = jnp.where(lvl_mask[m], g[H2:, H2:], a_d1[j])
    intra = []
    for j, v in enumerate(vs):
        a = jnp.concatenate([jnp.concatenate([a_d0[j], jnp.zeros((H2, H2), F32)], axis=1),
                             jnp.concatenate([a_lo[j], a_d1[j]], axis=1)], axis=0).astype(BF16)
        o_diag = jnp.sum(v["qq"] * v["kk"], axis=-1, keepdims=True) * v["ii"]
        intra.append(jnp.dot(a, v["ii_bf"], preferred_element_type=F32) + o_diag)
    st = jnp.zeros((Dk, Dk), F32)
    for j, v in enumerate(vs):
        b = v["b"]
        b_last = b[C - 1:C, :]
        qe = (v["qq"] * jnp.exp2(b)).astype(BF16)
        o = intra[j] + lax.dot_general(qe, st.astype(BF16), nt, preferred_element_type=F32)
        ke = (v["kk"] * jnp.exp2(b_last - b)).astype(BF16)
        st = st * jnp.exp2(b_last) + lax.dot_general(
            v["ii_bf"], ke, (((0,), (0,)), ((), ())), preferred_element_type=F32)
        ms = jnp.mean(o * o, axis=-1, keepdims=True)
        y = o * lax.rsqrt(ms + EPS) * gn
        y = y * _silu_tanh(g_ref[0, v["rows"], :])
        o_ref[0, v["rows"], :] = y.astype(o_ref.dtype)
        _cast_some(pieces, j, n_steps)


def _hgrn(proj_b, lb_logits_h, gnorm_g, cast_weights):
    B, T, W4 = proj_b.shape
    W = W4 // 4
    H = W // REC_HEAD_DIM
    n_slots = lb_logits_h.shape[1]
    blk = lambda off: pl.BlockSpec((1, T, REC_HEAD_DIM), lambda b, h, off=off: (b, 0, off * H + h))
    cast_specs = _cast_specs(cast_weights, B * H, lambda b, h: b * H + h)
    outs = pl.pallas_call(
        functools.partial(_hgrn_kernel, n_steps=T // HG_CHUNK, n_cast=len(cast_weights)),
        out_shape=(jax.ShapeDtypeStruct((B, T, W), BF16),
                   *[jax.ShapeDtypeStruct(w.shape, BF16) for w in cast_weights]),
        grid=(B, H),
        in_specs=[blk(0), blk(1), blk(2), blk(3),
                  pl.BlockSpec((1, n_slots, REC_HEAD_DIM), lambda b, h: (h, 0, 0)),
                  pl.BlockSpec((1, REC_HEAD_DIM), lambda b, h: (0, 0)),
                  *cast_specs],
        out_specs=(pl.BlockSpec((1, T, REC_HEAD_DIM), lambda b, h: (b, 0, h)), *cast_specs),
        compiler_params=_cparams(("parallel", "parallel"), 40),
        name="hgrn2",
    )(proj_b, proj_b, proj_b, proj_b, lb_logits_h, gnorm_g, *cast_weights)
    return outs[0], outs[1:]


def _outproj_kernel(oa_ref, ob_ref, w_ref, x_ref, mod_ref, g_ref, b_ref, x1_ref, *, alpha, sub):
    gate1 = _mod_row(mod_ref, pl.program_id(0), 0, x_ref.shape[-1])
    for r0 in range(0, x_ref.shape[1], sub):
        rs = slice(r0, r0 + sub)
        o_cat = jnp.concatenate([oa_ref[0, rs, :], ob_ref[0, rs, :]], axis=1)
        mix = jnp.dot(o_cat, w_ref[...], preferred_element_type=F32)
        x1_ref[0, rs, :] = _ln_rows(alpha * x_ref[0, rs, :] + gate1 * mix) * g_ref[...] + b_ref[...]


def _outproj(o_a, o_b, w_o, x, mod, ln_g, ln_b, alpha, tm=512, sub=256):
    B, T, D = x.shape
    Wa, Wb = o_a.shape[-1], o_b.shape[-1]
    return pl.pallas_call(
        functools.partial(_outproj_kernel, alpha=alpha, sub=sub),
        out_shape=jax.ShapeDtypeStruct((B, T, D), F32),
        grid=(B, T // tm),
        in_specs=[pl.BlockSpec((1, tm, Wa), lambda b, i: (b, i, 0)),
                  pl.BlockSpec((1, tm, Wb), lambda b, i: (b, i, 0)),
                  pl.BlockSpec((Wa + Wb, D), lambda b, i: (0, 0)),
                  pl.BlockSpec((1, tm, D), lambda b, i: (b, i, 0)),
                  pl.BlockSpec((B, (N_MOD - N_MOD_PRE) * D), lambda b, i: (0, 0)),
                  pl.BlockSpec((1, D), lambda b, i: (0, 0)),
                  pl.BlockSpec((1, D), lambda b, i: (0, 0))],
        out_specs=pl.BlockSpec((1, tm, D), lambda b, i: (b, i, 0)),
        compiler_params=_cparams(("parallel", "parallel"), 48),
        name="out_proj_ln1",
    )(o_a, o_b, w_o, x, mod, ln_g, ln_b)


def _ffn_kernel(wg_ref, wu_ref, wo_ref, x_ref, mod_ref, g_ref, b_ref, o_ref, h_sc, *, alpha, sub):
    f = pl.program_id(2)
    last = pl.num_programs(2) - 1

    def partial_out(rs):
        h = h_sc[rs, :]
        tf = wg_ref.shape[1]
        gu = jnp.dot(h, jnp.concatenate([wg_ref[...], wu_ref[...]], axis=1), preferred_element_type=F32)
        act = (_silu_tanh(gu[:, :tf]) * gu[:, tf:]).astype(BF16)
        return jnp.dot(act, wo_ref[...], preferred_element_type=F32)

    everything = slice(0, o_ref.shape[1])

    @pl.when(f == 0)
    def _():
        shift2 = _mod_row(mod_ref, pl.program_id(0), 1, x_ref.shape[-1])
        scale2 = _mod_row(mod_ref, pl.program_id(0), 2, x_ref.shape[-1])
        for r0 in range(0, o_ref.shape[1], sub):
            rs = slice(r0, r0 + sub)
            h_sc[rs, :] = (_ln_rows(x_ref[0, rs, :]) * (1.0 + scale2) + shift2).astype(BF16)
        o_ref[0] = partial_out(everything)

    @pl.when((f > 0) & (f < last))
    def _():
        o_ref[0] += partial_out(everything)

    @pl.when(f == last)
    def _():
        gate2 = _mod_row(mod_ref, pl.program_id(0), 3, x_ref.shape[-1])
        for r0 in range(0, o_ref.shape[1], sub):
            rs = slice(r0, r0 + sub)
            y = o_ref[0, rs, :] + partial_out(rs)
            o_ref[0, rs, :] = _ln_rows(alpha * x_ref[0, rs, :] + gate2 * y) * g_ref[...] + b_ref[...]


def _ffn(w_in, w_out, x1, mod, ln_g, ln_b, alpha, tm=1024, tf=512, sub=256):
    B, T, D = x1.shape
    F = w_out.shape[0]
    nf = F // tf
    return pl.pallas_call(
        functools.partial(_ffn_kernel, alpha=alpha, sub=sub),
        out_shape=jax.ShapeDtypeStruct((B, T, D), F32),
        grid=(B, T // tm, nf),
        in_specs=[pl.BlockSpec((D, tf), lambda b, i, f: (0, f)),
                  pl.BlockSpec((D, tf), lambda b, i, f: (0, nf + f)),
                  pl.BlockSpec((tf, D), lambda b, i, f: (f, 0)),
                  pl.BlockSpec((1, tm, D), lambda b, i, f: (b, i, 0)),
                  pl.BlockSpec((B, (N_MOD - N_MOD_PRE) * D), lambda b, i, f: (0, 0)),
                  pl.BlockSpec((1, D), lambda b, i, f: (0, 0)),
                  pl.BlockSpec((1, D), lambda b, i, f: (0, 0))],
        out_specs=pl.BlockSpec((1, tm, D), lambda b, i, f: (b, i, 0)),
        scratch_shapes=[pltpu.VMEM((tm, D), BF16)],
        compiler_params=_cparams(("parallel", "parallel", "arbitrary"), 60),
        name="swiglu_ffn_ln2",
    )(w_in, w_in, w_out, x1, mod, ln_g, ln_b)


def _bias_vectors(rel_bias):
    H, n_rel = rel_bias.shape
    max_rel = (n_rel - 1) // 2
    u = jnp.arange(BIAS_W)
    idx = jnp.clip(KBLK - u, -max_rel, max_rel) + max_rel
    return rel_bias[:, idx]


def kernel(x, c, w_ada, b_ada, w_in, rel_bias, attn_norm_g, lb_logits, gnorm_g, w_o,
           ln1_g, ln1_b, w_ffn_in, w_ffn_out, ln2_g, ln2_b):
    B, T, D = x.shape
    depth = w_ada.shape[0]
    alpha = (2 * depth) ** 0.25
    attn_w = attn_norm_g.shape[1]
    rec_w = lb_logits.shape[1]
    n_slots = lb_logits.shape[0]
    rec_heads = rec_w // REC_HEAD_DIM
    assert depth == 1 and n_slots == depth + 1
    for layer in range(depth):
        c8 = jnp.pad(c, ((0, SUBLANES - B), (0, 0)))
        b_ada2 = b_ada[layer].reshape(1, -1)
        mod_pre = _mod(c8, w_ada[layer], b_ada2, N_MOD_PRE * D, B)
        q_scale = jnp.where(jnp.arange(w_in.shape[2]) < attn_w, ATTN_HEAD_DIM ** -0.5 * LOG2E, 1.0)
        q_scale = q_scale.astype(F32).reshape(1, -1)
        h1, q_a = _ln_proj(x, mod_pre, w_in[layer], q_scale, tn=attn_w)
        h1 = h1.reshape(B * T, D)
        kv_a = _matmul(h1, w_in[layer], q_scale, attn_w, 2 * attn_w, BF16).reshape(B, T, 2 * attn_w)
        proj_b = _matmul(h1, w_in[layer], q_scale, 3 * attn_w, 4 * rec_w, F32).reshape(B, T, 4 * rec_w)
        bias_vec = (_bias_vectors(rel_bias[layer]) * LOG2E).reshape(-1, 2, BIAS_W)
        o_a, mod, (w_o_bf, w_ffn_out_bf) = _attention(
            q_a, kv_a, bias_vec, attn_norm_g[layer].reshape(-1, 1, LANES), (w_o[layer], w_ffn_out[layer]),
            c8, w_ada[layer], b_ada2, N_MOD_PRE * D)
        lbl = lb_logits.reshape(n_slots, rec_heads, REC_HEAD_DIM).transpose(1, 0, 2)
        o_b, (w_ffn_in_bf,) = _hgrn(proj_b, lbl, gnorm_g[layer].reshape(1, REC_HEAD_DIM), (w_ffn_in[layer],))
        x = _outproj(o_a, o_b, w_o_bf, x, mod,
                     ln1_g[layer].reshape(1, D), ln1_b[layer].reshape(1, D), alpha)
        x = _ffn(w_ffn_in_bf, w_ffn_out_bf, x, mod,
                 ln2_g[layer].reshape(1, D), ln2_b[layer].reshape(1, D), alpha)
    return x
```
